```python
import math
import jax, jax.numpy as jnp
from jax import lax
import numpy as np

D_MODEL = 1024
BATCH = 4
SEQ = 4096
DEPTH = 4

HEAD_DIM = 64
GLA_WIDTH = D_MODEL // 4
GLA_HEADS = GLA_WIDTH // HEAD_DIM
GLA_GATE_RANK = 16
GLA_GATE_TAU = 16.0
GLA_CHUNK = 16
SSD_WIDTH = D_MODEL // 2
SSD_HEADS = SSD_WIDTH // HEAD_DIM
SSD_GROUPS = 2
SSD_HEADS_PER_GROUP = SSD_HEADS // SSD_GROUPS
SSD_STATE = 64
SSD_CONV = 4
SSD_CHUNK = 64
SSD_XBC = SSD_WIDTH + 2 * SSD_GROUPS * SSD_STATE
RWKV_WIDTH = D_MODEL // 4
RWKV_HEADS = RWKV_WIDTH // HEAD_DIM
RWKV_DECAY_RANK = 64
RWKV_A_RANK = 64
RWKV_GATE_RANK = 128
RWKV_GN_EPS = 64e-5
MIX_WIDTH = GLA_WIDTH + SSD_WIDTH + RWKV_WIDTH
GLA_SPLITS = (GLA_WIDTH, GLA_WIDTH, GLA_WIDTH, GLA_WIDTH, GLA_GATE_RANK)
SSD_SPLITS = (SSD_WIDTH, SSD_XBC, SSD_HEADS)
RWKV_SPLITS = (RWKV_WIDTH, RWKV_WIDTH, RWKV_WIDTH, RWKV_DECAY_RANK, RWKV_A_RANK, RWKV_GATE_RANK)
GLA_COLS = sum(GLA_SPLITS)
SSD_COLS = sum(SSD_SPLITS)
RWKV_COLS = sum(RWKV_SPLITS)
N_IN = GLA_COLS + SSD_COLS + RWKV_COLS
N_EXPERTS = 16
N_EXPERT_GROUPS = 4
EXPERTS_PER_GROUP = N_EXPERTS // N_EXPERT_GROUPS
TOP_K = 2
EXPERT_FF = D_MODEL // 2
DEEPNORM_ALPHA = (2 * DEPTH) ** 0.25
DEEPNORM_BETA = (8 * DEPTH) ** -0.25
LN_EPS = 1e-5

kernel_name = "hymba_style_gla_ssd_rwkv7_grouped_moe_deepnorm"


def _offsets(sizes):
    out, acc = [], 0
    for s in sizes[:-1]:
        acc += s
        out.append(acc)
    return out


def layer_norm(x, g, b, eps=LN_EPS):
    xf = x.astype(jnp.float32)
    mu = jnp.mean(xf, axis=-1, keepdims=True)
    var = jnp.mean(jnp.square(xf - mu), axis=-1, keepdims=True)
    return ((xf - mu) * lax.rsqrt(var + eps) * g + b).astype(x.dtype)


def rms_norm(x, g, eps=LN_EPS):
    xf = x.astype(jnp.float32)
    return xf * lax.rsqrt(jnp.mean(jnp.square(xf), axis=-1, keepdims=True) + eps) * g


def token_shift(x):
    return jnp.pad(x, ((0, 0), (1, 0), (0, 0)))[:, :-1]


def causal_depthwise_conv(x, w, b):
    k_w, ch = w.shape
    y = lax.conv_general_dilated(x, w.astype(x.dtype)[:, None, :], window_strides=(1,),
                                 padding=[(k_w - 1, 0)], dimension_numbers=('NWC', 'WIO', 'NWC'),
                                 feature_group_count=ch)
    return y + b


def scan_chunk_states(d_state, decay):
    def step(s, inp):
        d, dec = inp
        return dec * s + d, s
    s0 = jnp.zeros_like(d_state[:, 0])
    _, s_in = lax.scan(step, s0, (jnp.moveaxis(d_state, 1, 0), jnp.moveaxis(decay, 1, 0)))
    return jnp.moveaxis(s_in, 0, 1)


def gla_mixer(q, k, v, g, a_lr, w_alpha, b_alpha, norm_g):
    f32 = jnp.float32
    bsz, seq, _ = q.shape
    n_chunks = seq // GLA_CHUNK
    shape = (bsz, n_chunks, GLA_CHUNK, GLA_HEADS, HEAD_DIM)
    log_a = jax.nn.log_sigmoid((a_lr @ w_alpha + b_alpha).astype(f32)) / GLA_GATE_TAU
    qc = q.astype(f32).reshape(shape) * HEAD_DIM ** -0.5
    kc = k.astype(f32).reshape(shape)
    vc = v.astype(f32).reshape(shape)
    cum = jnp.cumsum(log_a.reshape(shape), axis=2)
    causal = jnp.tril(jnp.ones((GLA_CHUNK, GLA_CHUNK), dtype=bool))[:, :, None, None]
    decay = jnp.exp(jnp.where(causal, cum[:, :, :, None] - cum[:, :, None, :], -jnp.inf))
    scores = jnp.einsum('bcthk,bcshk,bctshk->bchts', qc, kc, decay)
    o = jnp.einsum('bchts,bcshv->bcthv', scores, vc)
    cum_last = cum[:, :, -1]
    d_state = jnp.einsum('bcshk,bcshv->bchkv', kc * jnp.exp(cum_last[:, :, None] - cum), vc)
    s_in = scan_chunk_states(d_state, jnp.exp(cum_last)[..., None])
    o = o + jnp.einsum('bcthk,bchkv->bcthv', qc * jnp.exp(cum), s_in)
    o = rms_norm(o.reshape(bsz, seq, GLA_HEADS, HEAD_DIM), norm_g) * \
        jax.nn.silu(g.astype(f32).reshape(bsz, seq, GLA_HEADS, HEAD_DIM))
    return o.reshape(bsz, seq, GLA_WIDTH)


def ssd_mixer(z, xbc, dt_raw, conv_w, conv_b, dt_bias, a_log, d_skip, norm_g):
    f32 = jnp.float32
    bsz, seq, _ = z.shape
    n_chunks = seq // SSD_CHUNK
    G, E, P, M = SSD_GROUPS, SSD_HEADS_PER_GROUP, HEAD_DIM, SSD_STATE
    xbc = jax.nn.silu(causal_depthwise_conv(xbc, conv_w, conv_b).astype(f32))
    xs, bm, cm = jnp.split(xbc, [SSD_WIDTH, SSD_WIDTH + G * M], axis=-1)
    xs = xs.reshape(bsz, n_chunks, SSD_CHUNK, G, E, P)
    bm = bm.reshape(bsz, n_chunks, SSD_CHUNK, G, M)
    cm = cm.reshape(bsz, n_chunks, SSD_CHUNK, G, M)
    dt = jax.nn.softplus(dt_raw.astype(f32) + dt_bias).reshape(bsz, n_chunks, SSD_CHUNK, G, E)
    da = dt * (-jnp.exp(a_log.astype(f32))).reshape(G, E)
    cs = jnp.cumsum(da, axis=2)
    xdt = xs * dt[..., None]
    causal = jnp.tril(jnp.ones((SSD_CHUNK, SSD_CHUNK), dtype=bool))[:, :, None, None]
    seg = jnp.exp(jnp.where(causal, cs[:, :, :, None] - cs[:, :, None, :], -jnp.inf))
    cb = jnp.einsum('bctgm,bcsgm->bctsg', cm, bm)
    y = jnp.einsum('bctsg,bctsge,bcsgep->bctgep', cb, seg, xdt)
    cs_last = cs[:, :, -1]
    d_state = jnp.einsum('bcsgm,bcsge,bcsgep->bcgepm', bm, jnp.exp(cs_last[:, :, None] - cs), xdt)
    s_in = scan_chunk_states(d_state, jnp.exp(cs_last)[..., None, None])
    y = y + jnp.einsum('bctgm,bctge,bcgepm->bctgep', cm, jnp.exp(cs), s_in)
    y = y + d_skip.astype(f32).reshape(G, E, 1) * xs
    y = y.reshape(bsz, seq, SSD_WIDTH) * jax.nn.silu(z.astype(f32))
    y = rms_norm(y.reshape(bsz, seq, G, SSD_WIDTH // G), norm_g.reshape(G, SSD_WIDTH // G))
    return y.reshape(bsz, seq, SSD_WIDTH)


def rwkv7_mixer(cols, mu, w0, w2, a0, a2, g2, k_k, k_a, r_k, ln_g, ln_b):
    f32 = jnp.float32
    bsz, seq, _ = cols.shape
    H, N = RWKV_HEADS, HEAD_DIM
    cols = cols.astype(f32)
    cols = cols + (token_shift(cols) - cols) * mu
    r, k, v, w_lr, a_lr, g_lr = jnp.split(cols, _offsets(RWKV_SPLITS), axis=-1)
    w = -jax.nn.softplus(-(w0 + jnp.tanh(w_lr) @ w2)) - 0.5
    decay = jnp.exp(-jnp.exp(w.astype(f32)))
    a = jax.nn.sigmoid(a0 + a_lr @ a2)
    g = jax.nn.sigmoid(g_lr) @ g2
    heads = lambda t: t.astype(f32).reshape(bsz, seq, H, N)
    kk = heads(k * k_k)
    kk = kk / jnp.maximum(jnp.linalg.norm(kk, axis=-1, keepdims=True), 1e-12)
    k = heads(k * (1.0 + (a - 1.0) * k_a))
    r, v, decay, a = heads(r), heads(v), heads(decay), heads(a)

    def step(s, inp):
        r_t, w_t, k_t, v_t, kk_t, a_t = inp
        sa = jnp.einsum('bhvk,bhk->bhv', s, -kk_t)
        s = s * w_t[:, :, None, :] + sa[..., None] * (kk_t * a_t)[:, :, None, :] + v_t[..., None] * k_t[:, :, None, :]
        return s, jnp.einsum('bhvk,bhk->bhv', s, r_t)

    tm = lambda t: jnp.moveaxis(t, 1, 0)
    s0 = jnp.zeros((bsz, H, N, N), f32)
    _, y = lax.scan(step, s0, (tm(r), tm(decay), tm(k), tm(v), tm(kk), tm(a)))
    y = jnp.moveaxis(y, 0, 1)
    y = layer_norm(y, ln_g.reshape(H, N), ln_b.reshape(H, N), eps=RWKV_GN_EPS)
    y = y + jnp.sum(r * k * r_k, axis=-1, keepdims=True) * v
    return y.reshape(bsz, seq, RWKV_WIDTH) * g


def grouped_moe(x, router_w, router_b, w_gate, w_up, w_down):
    f32 = jnp.float32
    bsz, seq, dm = x.shape
    h = x.reshape(-1, dm)
    scores = jax.nn.sigmoid((h @ router_w).astype(f32))
    sel = (scores + router_b).reshape(-1, N_EXPERT_GROUPS, EXPERTS_PER_GROUP)
    group_score = jnp.sum(lax.top_k(sel, 2)[0], axis=-1)
    best_group = jnp.argmax(group_score, axis=-1)
    in_group = jax.nn.one_hot(best_group, N_EXPERT_GROUPS, dtype=f32)[..., None] > 0
    masked = jnp.where(in_group, sel, -jnp.inf).reshape(-1, N_EXPERTS)
    _, top_idx = lax.top_k(masked, TOP_K)
    top_w = jnp.take_along_axis(scores, top_idx, axis=-1)
    top_w = top_w / jnp.sum(top_w, axis=-1, keepdims=True)
    gate = jnp.sum(jax.nn.one_hot(top_idx, N_EXPERTS, dtype=f32) * top_w[..., None], axis=1)
    gate = gate.astype(h.dtype)
    out = jnp.zeros_like(h)
    for e in range(N_EXPERTS):
        he = jax.nn.silu(h @ w_gate[e]) * (h @ w_up[e])
        out = out + gate[:, e:e + 1] * (he @ w_down[e])
    return out.reshape(bsz, seq, dm)


def setup_inputs(seed: int = 0) -> dict:
    key = jax.random.key(seed)
    ks = iter(jax.random.split(key, 48))
    f32 = jnp.float32
    L = DEPTH
    nrm = lambda shape, scale: jax.random.normal(next(ks), shape, f32) * scale
    uni = lambda shape, lo, hi: jax.random.uniform(next(ks), shape, f32, lo, hi)
    ones_n = lambda shape: 1.0 + nrm(shape, 0.02)
    dt0 = jnp.exp(uni((L, SSD_HEADS), math.log(1e-3), math.log(1e-1)))
    ramp = (jnp.arange(RWKV_WIDTH, dtype=f32) / (RWKV_WIDTH - 1)) ** 0.85
    return {
        "x": nrm((BATCH, SEQ, D_MODEL), 1.0),
        "w_in": nrm((L, D_MODEL, N_IN), D_MODEL ** -0.5),
        "gla_w_alpha": nrm((L, GLA_GATE_RANK, GLA_WIDTH), GLA_GATE_RANK ** -0.5),
        "gla_b_alpha": nrm((L, GLA_WIDTH), 0.1),
        "gla_norm_g": ones_n((L, HEAD_DIM)),
        "ssd_conv_w": nrm((L, SSD_CONV, SSD_XBC), SSD_CONV ** -0.5),
        "ssd_conv_b": nrm((L, SSD_XBC), 0.02),
        "ssd_dt_bias": dt0 + jnp.log(-jnp.expm1(-dt0)),
        "ssd_A_log": jnp.log(uni((L, SSD_HEADS), 1.0, 16.0)),
        "ssd_D": ones_n((L, SSD_HEADS)),
        "ssd_norm_g": ones_n((L, SSD_WIDTH)),
        "rwkv_mu": uni((L, RWKV_COLS), 0.0, 1.0),
        "rwkv_w0": -6.0 + 5.0 * ramp + nrm((L, RWKV_WIDTH), 0.1),
        "rwkv_w2": nrm((L, RWKV_DECAY_RANK, RWKV_WIDTH), 0.1),
        "rwkv_a0": nrm((L, RWKV_WIDTH), 0.1),
        "rwkv_a2": nrm((L, RWKV_A_RANK, RWKV_WIDTH), 0.1),
        "rwkv_g2": nrm((L, RWKV_GATE_RANK, RWKV_WIDTH), RWKV_GATE_RANK ** -0.5),
        "rwkv_k_k": 0.85 + nrm((L, RWKV_WIDTH), 0.02),
        "rwkv_k_a": ones_n((L, RWKV_WIDTH)),
        "rwkv_r_k": nrm((L, RWKV_HEADS, HEAD_DIM), 0.1),
        "rwkv_ln_g": ones_n((L, RWKV_WIDTH)),
        "rwkv_ln_b": nrm((L, RWKV_WIDTH), 0.02),
        "w_out": nrm((L, MIX_WIDTH, D_MODEL), MIX_WIDTH ** -0.5 * DEEPNORM_BETA),
        "ln1_g": ones_n((L, D_MODEL)),
        "ln1_b": nrm((L, D_MODEL), 0.02),
        "router_w": nrm((D_MODEL, N_EXPERTS), D_MODEL ** -0.5),
        "router_b": nrm((N_EXPERTS,), 0.01),
        "exp_w_gate": nrm((L, N_EXPERTS, D_MODEL, EXPERT_FF), D_MODEL ** -0.5),
        "exp_w_up": nrm((L, N_EXPERTS, D_MODEL, EXPERT_FF), D_MODEL ** -0.5),
        "exp_w_down": nrm((L, N_EXPERTS, EXPERT_FF, D_MODEL), EXPERT_FF ** -0.5 * DEEPNORM_BETA),
        "ln2_g": ones_n((L, D_MODEL)),
        "ln2_b": nrm((L, D_MODEL), 0.02),
    }


def reference(x, w_in, gla_w_alpha, gla_b_alpha, gla_norm_g, ssd_conv_w, ssd_conv_b, ssd_dt_bias,
              ssd_A_log, ssd_D, ssd_norm_g, rwkv_mu, rwkv_w0, rwkv_w2, rwkv_a0, rwkv_a2, rwkv_g2,
              rwkv_k_k, rwkv_k_a, rwkv_r_k, rwkv_ln_g, rwkv_ln_b, w_out, ln1_g, ln1_b, router_w,
              router_b, exp_w_gate, exp_w_up, exp_w_down, ln2_g, ln2_b):
    for l in range(DEPTH):
        proj = x @ w_in[l]
        gla_cols, ssd_cols, rwkv_cols = jnp.split(proj, [GLA_COLS, GLA_COLS + SSD_COLS], axis=-1)
        gq, gk, gv, gg, ga = jnp.split(gla_cols, _offsets(GLA_SPLITS), axis=-1)
        sz, sxbc, sdt = jnp.split(ssd_cols, _offsets(SSD_SPLITS), axis=-1)
        o_gla = gla_mixer(gq, gk, gv, gg, ga, gla_w_alpha[l], gla_b_alpha[l], gla_norm_g[l])
        o_ssd = ssd_mixer(sz, sxbc, sdt, ssd_conv_w[l], ssd_conv_b[l], ssd_dt_bias[l], ssd_A_log[l],
                          ssd_D[l], ssd_norm_g[l])
        o_rwkv = rwkv7_mixer(rwkv_cols, rwkv_mu[l], rwkv_w0[l], rwkv_w2[l], rwkv_a0[l], rwkv_a2[l],
                             rwkv_g2[l], rwkv_k_k[l], rwkv_k_a[l], rwkv_r_k[l], rwkv_ln_g[l], rwkv_ln_b[l])
        heads_out = jnp.concatenate([o_gla.astype(x.dtype), o_ssd.astype(x.dtype), o_rwkv.astype(x.dtype)], axis=-1)
        x = layer_norm(DEEPNORM_ALPHA * x + heads_out @ w_out[l], ln1_g[l], ln1_b[l])
        ffn = grouped_moe(x, router_w, router_b, exp_w_gate[l], exp_w_up[l], exp_w_down[l])
        x = layer_norm(DEEPNORM_ALPHA * x + ffn, ln2_g[l], ln2_b[l])
    return x
```

```python
import functools
import math

import jax
import jax.numpy as jnp
from jax import lax
from jax.experimental import pallas as pl
from jax.experimental.pallas import tpu as pltpu

f32 = jnp.float32
bf16 = jnp.bfloat16

LANES = 128
HEAD_DIM = 64
GLA_WIDTH = 256
GLA_GATE_RANK = 16
GLA_GATE_TAU = 16.0
GLA_CHUNK = 16
GLA_BLOCK = 128
GLA_COLS_PAD = 4 * GLA_WIDTH + LANES
SSD_WIDTH = 512
SSD_HEADS = 8
SSD_GROUPS = 2
SSD_STATE = 64
SSD_CONV = 4
SSD_XBC = SSD_WIDTH + 2 * SSD_GROUPS * SSD_STATE
SSD_BLOCK = 128
SSD_COLS_PAD = SSD_WIDTH + SSD_XBC + LANES
RWKV_WIDTH = 256
RWKV_HEADS = 4
RWKV_CHUNK = 64
RWKV_GN_EPS = 64e-5
RWKV_COLS = 3 * RWKV_WIDTH + 64 + 64 + 128
N_EXPERTS = 16
N_EXPERT_GROUPS = 4
EXPERTS_PER_GROUP = 4
LN_EPS = 1e-5

PROJ_ROWS = 256
MOE_ROWS = 1024
VMEM_LIMIT = 56 * 1024 * 1024


def _bf(x):
    return x.astype(bf16)


def _dot(a, b):
    return jnp.dot(_bf(a), _bf(b), preferred_element_type=f32)


def _dot_nt(a, b):
    return lax.dot_general(_bf(a), _bf(b), (((1,), (1,)), ((), ())), preferred_element_type=f32)


def _dot_tn(a, b):
    return lax.dot_general(_bf(a), _bf(b), (((0,), (0,)), ((), ())), preferred_element_type=f32)


def _split(x, n):
    parts, r = [], x
    for _ in range(n):
        p = r.astype(bf16)
        parts.append(p)
        r = r - p.astype(f32)
    return parts


def _dot_sel_rhs(a, sel, n=3):
    out = None
    for p in _split(a, n):
        t = jnp.dot(p, sel, preferred_element_type=f32)
        out = t if out is None else out + t
    return out


def _dot_sel_lhs(sel, b, n=3):
    out = None
    for p in _split(b, n):
        t = jnp.dot(sel, p, preferred_element_type=f32)
        out = t if out is None else out + t
    return out


def _dot_hi(a, b):
    ah, al = _split(a, 2)
    bh, bl = _split(b, 2)
    d = functools.partial(jnp.dot, preferred_element_type=f32)
    return d(ah, bh) + d(al, bh) + d(ah, bl)


def _iota(shape, dim):
    return lax.broadcasted_iota(jnp.int32, shape, dim)


def _block_ones(n, blk):
    return (_iota((n, n), 0) // blk == _iota((n, n), 1) // blk)


def _silu(x):
    return x * jax.nn.sigmoid(x)


def _softplus(x):
    return jnp.maximum(x, 0.0) + jnp.log1p(jnp.exp(-jnp.abs(x)))


def _layer_norm(y, g, b, eps):
    mu = jnp.mean(y, axis=-1, keepdims=True)
    yc = y - mu
    var = jnp.mean(yc * yc, axis=-1, keepdims=True)
    return yc * lax.rsqrt(var + eps) * g + b


def _cparams(sem):
    return pltpu.CompilerParams(dimension_semantics=sem, vmem_limit_bytes=VMEM_LIMIT)


def _inproj_kernel(x_ref, wg_ref, ws_ref, wr_ref, og_ref, os_ref, or_ref):
    xb = _bf(x_ref[...])
    og_ref[...] = jnp.dot(xb, wg_ref[...], preferred_element_type=f32)
    os_ref[...] = jnp.dot(xb, ws_ref[...], preferred_element_type=f32)
    or_ref[...] = jnp.dot(xb, wr_ref[...], preferred_element_type=f32)


def _in_proj(x2, wg, ws, wr):
    n, d = x2.shape
    tm = PROJ_ROWS
    full = lambda w: pl.BlockSpec(w.shape, lambda i: (0, 0))
    rows = lambda c: pl.BlockSpec((tm, c), lambda i: (i, 0))
    return pl.pallas_call(
        _inproj_kernel,
        grid=(n // tm,),
        in_specs=[rows(d), full(wg), full(ws), full(wr)],
        out_specs=[rows(wg.shape[1]), rows(ws.shape[1]), rows(wr.shape[1])],
        out_shape=[jax.ShapeDtypeStruct((n, w.shape[1]), f32) for w in (wg, ws, wr)],
        compiler_params=_cparams(("arbitrary",)),
        name="in_proj",
    )(x2, wg, ws, wr)


def _gla_kernel(p_ref, wa_ref, ba_ref, ng_ref, o_ref, st_ref):
    nb, tb, _ = p_ref.shape
    w = GLA_WIDTH
    c = GLA_CHUNK
    nc = tb // c

    @pl.when(pl.program_id(0) == 0)
    def _():
        st_ref[...] = jnp.zeros_like(st_ref)

    head_blk = _block_ones(w, HEAD_DIM)
    bones = head_blk.astype(bf16)
    r_i, c_i = _iota((tb, tb), 0), _iota((tb, tb), 1)
    tri = ((r_i >= c_i) & (r_i // c == c_i // c)).astype(bf16)
    t_in_chunk = _iota((nc, c, w), 1)
    wa_h, wa_l = _split(wa_ref[...], 2)

    for b in range(nb):
        q = p_ref[b, :, 0:w] * (HEAD_DIM ** -0.5)
        k = p_ref[b, :, w:2 * w]
        v = p_ref[b, :, 2 * w:3 * w]
        g = p_ref[b, :, 3 * w:4 * w]
        a_h, a_l = _split(p_ref[b, :, 4 * w:4 * w + LANES], 2)
        d = functools.partial(jnp.dot, preferred_element_type=f32)
        z = d(a_h, wa_h) + d(a_l, wa_h) + d(a_h, wa_l) + ba_ref[...]
        log_a = -_softplus(-z) * (1.0 / GLA_GATE_TAU)
        cum = _dot_sel_lhs(tri, log_a, 3)

        q3, k3, v3, cum3 = (t.reshape(nc, c, w) for t in (q, k, v, cum))
        o_acc = jnp.zeros((nc, c, w), f32)
        for s in range(c):
            e = jnp.where(t_in_chunk >= s, jnp.exp(cum3 - cum3[:, s:s + 1, :]), 0.0)
            dmat = (q3 * k3[:, s:s + 1, :]) * e
            sx = _dot_sel_rhs(dmat.reshape(tb, w), bones, 2)
            o_acc = o_acc + sx.reshape(nc, c, w) * v3[:, s:s + 1, :]
        o_in = o_acc.reshape(tb, w)

        o_parts = []
        for ci in range(nc):
            sl = slice(ci * c, (ci + 1) * c)
            cum_c = cum[sl]
            cum_last = cum[(ci + 1) * c - 1:(ci + 1) * c]
            st = st_ref[b]
            o_parts.append(_dot_nt(q[sl] * jnp.exp(cum_c), st))
            upd = _dot_tn(v[sl], k[sl] * jnp.exp(cum_last - cum_c))
            st_ref[b] = st * jnp.exp(cum_last) + jnp.where(head_blk, upd, 0.0)
        o = o_in + jnp.concatenate(o_parts, axis=0)

        ms = _dot_sel_rhs(o * o, bones, 2) * (1.0 / HEAD_DIM)
        o_ref[b] = o * lax.rsqrt(ms + LN_EPS) * ng_ref[...] * _silu(g)


def _gla(pg, wa, ba, ng):
    nb, t, cols = pg.shape
    tb = GLA_BLOCK
    full2 = lambda a: pl.BlockSpec(a.shape, lambda i: (0, 0))
    return pl.pallas_call(
        _gla_kernel,
        grid=(t // tb,),
        in_specs=[pl.BlockSpec((nb, tb, cols), lambda i: (0, i, 0)), full2(wa), full2(ba), full2(ng)],
        out_specs=pl.BlockSpec((nb, tb, GLA_WIDTH), lambda i: (0, i, 0)),
        out_shape=jax.ShapeDtypeStruct((nb, t, GLA_WIDTH), f32),
        scratch_shapes=[pltpu.VMEM((nb, GLA_WIDTH, GLA_WIDTH), f32)],
        compiler_params=_cparams(("arbitrary",)),
        name="gla_mixer",
    )(pg, wa, ba, ng)


def _ssd_kernel(p_ref, prev_ref, cw_ref, cb_ref, dtb_ref, alog_ref, dsk_ref, ng_ref, o_ref, st_ref):
    _, L, _ = p_ref.shape
    W, M, G = SSD_WIDTH, SSD_STATE, SSD_GROUPS
    gw = W // G
    i = pl.program_id(1)

    @pl.when(i == 0)
    def _():
        st_ref[...] = jnp.zeros_like(st_ref)

    z = p_ref[0, :, 0:W]
    xbc = p_ref[0, :, W:W + SSD_XBC]
    dt_raw = p_ref[0, :, W + SSD_XBC:W + SSD_XBC + LANES]
    prev = jnp.where(i > 0, prev_ref[0, :, W:W + SSD_XBC], 0.0)
    xx = jnp.concatenate([prev, xbc], axis=0)
    conv = cb_ref[...]
    for j in range(SSD_CONV):
        off = 8 - (SSD_CONV - 1) + j
        conv = conv + cw_ref[j:j + 1, :] * xx[off:off + L]
    act = _silu(conv)
    xs = act[:, 0:W]
    bm = act[:, W:W + G * M]
    cm = act[:, W + G * M:W + 2 * G * M]

    dt = _softplus(dt_raw + dtb_ref[...])
    a_neg = jnp.where(_iota((1, LANES), 1) < SSD_HEADS, -jnp.exp(alog_ref[...]), 0.0)
    da = dt * a_neg
    r_i, c_i = _iota((L, L), 0), _iota((L, L), 1)
    causal = r_i >= c_i
    cs = _dot_sel_lhs(causal.astype(bf16), da, 3)
    expand = (_iota((LANES, W), 0) == _iota((LANES, W), 1) // HEAD_DIM).astype(bf16)
    dtx = _dot_sel_rhs(dt, expand, 3)
    csx = _dot_sel_rhs(cs, expand, 3)
    eye_h = (_iota((8, LANES), 0) == _iota((8, LANES), 1)).astype(bf16)
    cs_t = None
    for p in _split(cs, 3):
        t = lax.dot_general(eye_h, p, (((1,), (1,)), ((), ())), preferred_element_type=f32)
        cs_t = t if cs_t is None else cs_t + t
    xdt = xs * dtx

    st = st_ref[...]
    y_parts = []
    for gi in range(G):
        bm_g = bm[:, gi * M:(gi + 1) * M]
        cm_g = cm[:, gi * M:(gi + 1) * M]
        cb = _dot_nt(cm_g, bm_g)
        for e in range(SSD_HEADS // G):
            h = gi * (SSD_HEADS // G) + e
            seg = jnp.where(causal, jnp.exp(jnp.broadcast_to(cs[:, h:h + 1], (L, L)) - cs_t[h:h + 1, :]), 0.0)
            y_parts.append(_dot(cb * seg, xdt[:, h * HEAD_DIM:(h + 1) * HEAD_DIM]))
    y = jnp.concatenate(y_parts, axis=1)
    y_inter = jnp.concatenate(
        [_dot(cm[:, gi * M:(gi + 1) * M], st[:, gi * gw:(gi + 1) * gw]) for gi in range(G)], axis=1)
    y = y + y_inter * jnp.exp(csx)

    cs_last = csx[L - 1:L, :]
    xdtw = xdt * jnp.exp(cs_last - csx)
    d_state = jnp.concatenate(
        [_dot_tn(bm[:, gi * M:(gi + 1) * M], xdtw[:, gi * gw:(gi + 1) * gw]) for gi in range(G)], axis=1)
    st_ref[...] = st * jnp.exp(cs_last) + d_state

    y = (y + dsk_ref[...] * xs) * _silu(z)
    outs = []
    for gi in range(G):
        yg = y[:, gi * gw:(gi + 1) * gw]
        ms = jnp.mean(yg * yg, axis=-1, keepdims=True)
        outs.append(yg * lax.rsqrt(ms + LN_EPS))
    o_ref[0] = jnp.concatenate(outs, axis=1) * ng_ref[...]


def _ssd(ps, cw, cb, dtb, alog, dsk, ng):
    nb, t, cols = ps.shape
    L = SSD_BLOCK
    full2 = lambda a: pl.BlockSpec(a.shape, lambda b, i: (0, 0))
    return pl.pallas_call(
        _ssd_kernel,
        grid=(nb, t // L),
        in_specs=[
            pl.BlockSpec((1, L, cols), lambda b, i: (b, i, 0)),
            pl.BlockSpec((1, 8, cols), lambda b, i: (b, jnp.maximum(i * (L // 8) - 1, 0), 0)),
            full2(cw), full2(cb), full2(dtb), full2(alog), full2(dsk), full2(ng),
        ],
        out_specs=pl.BlockSpec((1, L, SSD_WIDTH), lambda b, i: (b, i, 0)),
        out_shape=jax.ShapeDtypeStruct((nb, t, SSD_WIDTH), f32),
        scratch_shapes=[pltpu.VMEM((SSD_STATE, SSD_WIDTH), f32)],
        compiler_params=_cparams(("arbitrary", "arbitrary")),
        name="ssd_mixer",
    )(ps, ps, cw, cb, dtb, alog, dsk, ng)


def _rwkv_kernel(p_ref, prev_ref, mu_ref, w0_ref, w2_ref, a0_ref, a2_ref, g2_ref, kk_ref, ka_ref, rk_ref,
                 lng_ref, lnb_ref, o_ref, zt_ref):
    _, C, _ = p_ref.shape
    W, H = RWKV_WIDTH, RWKV_HEADS
    i = pl.program_id(1)

    @pl.when(i == 0)
    def _():
        zt_ref[...] = jnp.zeros_like(zt_ref)

    cols = p_ref[0]
    prev_row = jnp.where(i > 0, prev_ref[0, 7:8, :], 0.0)
    shifted = jnp.concatenate([prev_row, cols[:C - 1]], axis=0)
    cols = cols + (shifted - cols) * mu_ref[...]
    r = cols[:, 0:W]
    k = cols[:, W:2 * W]
    v = cols[:, 2 * W:3 * W]
    w_lr = cols[:, 3 * W:3 * W + 64]
    a_lr = cols[:, 3 * W + 64:3 * W + 128]
    g_lr = cols[:, 3 * W + 128:3 * W + 256]

    wv = -_softplus(-(w0_ref[...] + _dot_hi(jnp.tanh(w_lr), w2_ref[...]))) - 0.5
    lw = -jnp.exp(wv)
    a = jax.nn.sigmoid(a0_ref[...] + _dot_hi(a_lr, a2_ref[...]))
    g = _dot_hi(jax.nn.sigmoid(g_lr), g2_ref[...])

    head_blk = _block_ones(W, HEAD_DIM)
    bones = head_blk.astype(bf16)
    kk = k * kk_ref[...]
    kk_norm = jnp.sqrt(_dot_sel_rhs(kk * kk, bones, 3))
    kk = kk / jnp.maximum(kk_norm, 1e-12)
    k2 = k * (1.0 + (a - 1.0) * ka_ref[...])
    bvec = kk * a

    r_i, c_i = _iota((C, C), 0), _iota((C, C), 1)
    cum = _dot_sel_lhs((r_i >= c_i).astype(bf16), lw, 3)
    cum_x = cum - lw
    cum_last = cum[C - 1:C, :]
    e_neg = jnp.exp(-cum)
    e_rem = jnp.exp(cum_last - cum)

    def stack(t):
        return jnp.where(head_blk, jnp.concatenate([t] * H, axis=0), 0.0)

    aw = stack(-kk * jnp.exp(cum_x))
    rw = stack(r * jnp.exp(cum))
    bn = stack(bvec * e_neg)
    kn = stack(k2 * e_neg)
    bt = stack(bvec * e_rem)
    kt = stack(k2 * e_rem)
    vs = stack(v)

    n = H * C
    rr, cc = _iota((n, n), 0), _iota((n, n), 1)
    same = rr // C == cc // C
    strict = same & (rr > cc)
    lower = same & (rr >= cc)
    a_ab = jnp.where(strict, _dot_nt(aw, bn), 0.0)
    a_ak = jnp.where(strict, _dot_nt(aw, kn), 0.0)
    a_rb = jnp.where(lower, _dot_nt(rw, bn), 0.0)
    a_rk = jnp.where(lower, _dot_nt(rw, kn), 0.0)

    pw = a_ab
    tinv = jnp.where(rr == cc, 1.0, 0.0) + a_ab
    for _ in range(int(math.log2(C)) - 1):
        pw = _dot(pw, pw)
        tinv = tinv + _dot(tinv, pw)

    zt = zt_ref[...]
    u = _dot(tinv, _dot_nt(aw, zt) + _dot(a_ak, vs))
    y_st = _dot_nt(rw, zt) + _dot(a_rb, u) + _dot(a_rk, vs)
    zt_ref[...] = zt * jnp.exp(cum_last) + _dot_tn(u, bt) + _dot_tn(vs, kt)
    y = y_st[0:C]
    for h in range(1, H):
        y = y + y_st[h * C:(h + 1) * C]

    inv_n = 1.0 / HEAD_DIM
    mu_h = _dot_sel_rhs(y, bones, 3) * inv_n
    yc = y - mu_h
    var_h = _dot_sel_rhs(yc * yc, bones, 3) * inv_n
    y = yc * lax.rsqrt(var_h + RWKV_GN_EPS) * lng_ref[...] + lnb_ref[...]
    bonus = _dot_sel_rhs(r * k2 * rk_ref[...], bones, 3)
    o_ref[0] = (y + bonus * v) * g


def _rwkv(pr, mu, w0, w2, a0, a2, g2, k_k, k_a, r_k, ln_g, ln_b):
    nb, t, cols = pr.shape
    C = RWKV_CHUNK
    full2 = lambda a: pl.BlockSpec(a.shape, lambda b, i: (0, 0))
    params = (mu, w0, w2, a0, a2, g2, k_k, k_a, r_k, ln_g, ln_b)
    return pl.pallas_call(
        _rwkv_kernel,
        grid=(nb, t // C),
        in_specs=[
            pl.BlockSpec((1, C, cols), lambda b, i: (b, i, 0)),
            pl.BlockSpec((1, 8, cols), lambda b, i: (b, jnp.maximum(i * (C // 8) - 1, 0), 0)),
        ] + [full2(p) for p in params],
        out_specs=pl.BlockSpec((1, C, RWKV_WIDTH), lambda b, i: (b, i, 0)),
        out_shape=jax.ShapeDtypeStruct((nb, t, RWKV_WIDTH), f32),
        scratch_shapes=[pltpu.VMEM((RWKV_WIDTH, RWKV_WIDTH), f32)],
        compiler_params=_cparams(("arbitrary", "arbitrary")),
        name="rwkv7_mixer",
    )(pr, pr, *params)


def _outproj_kernel(alpha, x_ref, og_ref, os_ref, or_ref, wo_ref, g_ref, b_ref, rwt_ref, rb_ref, x1_ref, gate_ref):
    tm = x_ref.shape[0]
    d = functools.partial(jnp.dot, preferred_element_type=f32)
    mix = (d(_bf(og_ref[...]), wo_ref[0:GLA_WIDTH, :])
           + d(_bf(os_ref[...]), wo_ref[GLA_WIDTH:GLA_WIDTH + SSD_WIDTH, :])
           + d(_bf(or_ref[...]), wo_ref[GLA_WIDTH + SSD_WIDTH:, :]))
    x1 = _layer_norm(alpha * x_ref[...] + mix, g_ref[...], b_ref[...], LN_EPS)
    x1_ref[...] = x1

    xh, xl = _split(x1, 2)
    wh, wl = _split(rwt_ref[...], 2)
    nt = lambda a, b: lax.dot_general(a, b, (((1,), (1,)), ((), ())), preferred_element_type=f32)
    logits = nt(wh, xh) + nt(wl, xh) + nt(wh, xl)
    scores = jax.nn.sigmoid(logits)
    sel = scores + rb_ref[...]
    srow = [sel[e:e + 1, :] for e in range(N_EXPERTS)]
    crow = [scores[e:e + 1, :] for e in range(N_EXPERTS)]

    def top2_sum(v0, v1, v2, v3):
        m01, n01 = jnp.maximum(v0, v1), jnp.minimum(v0, v1)
        m23, n23 = jnp.maximum(v2, v3), jnp.minimum(v2, v3)
        return jnp.maximum(m01, m23) + jnp.maximum(jnp.minimum(m01, m23), jnp.maximum(n01, n23))

    gscore = [top2_sum(*srow[4 * gi:4 * gi + 4]) for gi in range(N_EXPERT_GROUPS)]
    best = jnp.zeros((1, tm), jnp.int32)
    best_v = gscore[0]
    for gi in range(1, N_EXPERT_GROUPS):
        better = gscore[gi] > best_v
        best = jnp.where(better, gi, best)
        best_v = jnp.where(better, gscore[gi], best_v)
    sv, cv = [], []
    for j in range(EXPERTS_PER_GROUP):
        s_j, c_j = srow[j], crow[j]
        for gi in range(1, N_EXPERT_GROUPS):
            s_j = jnp.where(best == gi, srow[4 * gi + j], s_j)
            c_j = jnp.where(best == gi, crow[4 * gi + j], c_j)
        sv.append(s_j)
        cv.append(c_j)
    j1 = jnp.zeros((1, tm), jnp.int32)
    v1 = sv[0]
    for j in range(1, EXPERTS_PER_GROUP):
        better = sv[j] > v1
        j1 = jnp.where(better, j, j1)
        v1 = jnp.where(better, sv[j], v1)
    j2 = jnp.full((1, tm), -1, jnp.int32)
    v2 = jnp.full((1, tm), -jnp.inf, f32)
    for j in range(EXPERTS_PER_GROUP):
        better = (j1 != j) & ((j2 < 0) | (sv[j] > v2))
        j2 = jnp.where(better, j, j2)
        v2 = jnp.where(better, sv[j], v2)
    w1 = cv[0]
    w2 = cv[0]
    for j in range(1, EXPERTS_PER_GROUP):
        w1 = jnp.where(j1 == j, cv[j], w1)
        w2 = jnp.where(j2 == j, cv[j], w2)
    wsum = w1 + w2
    g1, g2 = w1 / wsum, w2 / wsum
    rows = []
    for e in range(N_EXPERTS):
        gi, j = divmod(e, EXPERTS_PER_GROUP)
        in_g = best == gi
        rows.append(jnp.where(in_g & (j1 == j), g1, jnp.where(in_g & (j2 == j), g2, 0.0)))
    gate_t = jnp.concatenate(rows, axis=0)
    eye = (_iota((N_EXPERTS, LANES), 0) == _iota((N_EXPERTS, LANES), 1)).astype(bf16)
    gate = None
    for p in _split(gate_t, 3):
        t = lax.dot_general(p, eye, (((0,), (0,)), ((), ())), preferred_element_type=f32)
        gate = t if gate is None else gate + t
    gate_ref[...] = gate


def _out_proj(alpha, x2, og, os_, or_, wo, g, b, rwt, rb):
    n, d = x2.shape
    tm = PROJ_ROWS
    full2 = lambda a: pl.BlockSpec(a.shape, lambda i: (0, 0))
    rows = lambda c: pl.BlockSpec((tm, c), lambda i: (i, 0))
    return pl.pallas_call(
        functools.partial(_outproj_kernel, alpha),
        grid=(n // tm,),
        in_specs=[rows(d), rows(og.shape[1]), rows(os_.shape[1]), rows(or_.shape[1]),
                  full2(wo), full2(g), full2(b), full2(rwt), full2(rb)],
        out_specs=[rows(d), rows(LANES)],
        out_shape=[jax.ShapeDtypeStruct((n, d), f32), jax.ShapeDtypeStruct((n, LANES), f32)],
        compiler_params=_cparams(("arbitrary",)),
        name="out_proj_router",
    )(x2, og, os_, or_, wo, g, b, rwt, rb)


def _moe_kernel(alpha, x_ref, gate_ref, wg_ref, wu_ref, wd_ref, g_ref, b_ref, o_ref, xb_ref, acc_ref):
    e = pl.program_id(1)

    @pl.when(e == 0)
    def _():
        xb_ref[...] = _bf(x_ref[...])
        acc_ref[...] = jnp.zeros_like(acc_ref)

    xb = xb_ref[...]
    hg = jnp.dot(xb, wg_ref[0], preferred_element_type=f32)
    hu = jnp.dot(xb, wu_ref[0], preferred_element_type=f32)
    gate = gate_ref[...]
    gcol = jnp.sum(jnp.where(_iota(gate.shape, 1) == e, gate, 0.0), axis=-1, keepdims=True)
    he = _silu(hg) * hu * gcol
    acc_ref[...] += jnp.dot(_bf(he), wd_ref[0], preferred_element_type=f32)

    @pl.when(e == pl.num_programs(1) - 1)
    def _():
        o_ref[...] = _layer_norm(alpha * x_ref[...] + acc_ref[...], g_ref[...], b_ref[...], LN_EPS)


def _moe(alpha, x1, gate, wg, wu, wd, g, b):
    n, d = x1.shape
    ne, _, ff = wg.shape
    tm = min(MOE_ROWS, n)
    full2 = lambda a: pl.BlockSpec(a.shape, lambda i, e: (0, 0))
    return pl.pallas_call(
        functools.partial(_moe_kernel, alpha),
        grid=(n // tm, ne),
        in_specs=[
            pl.BlockSpec((tm, d), lambda i, e: (i, 0)),
            pl.BlockSpec((tm, LANES), lambda i, e: (i, 0)),
            pl.BlockSpec((1, d, ff), lambda i, e: (e, 0, 0)),
            pl.BlockSpec((1, d, ff), lambda i, e: (e, 0, 0)),
            pl.BlockSpec((1, ff, d), lambda i, e: (e, 0, 0)),
            full2(g), full2(b),
        ],
        out_specs=pl.BlockSpec((tm, d), lambda i, e: (i, 0)),
        out_shape=jax.ShapeDtypeStruct((n, d), f32),
        scratch_shapes=[pltpu.VMEM((tm, d), bf16), pltpu.VMEM((tm, d), f32)],
        compiler_params=_cparams(("arbitrary", "arbitrary")),
        name="moe_experts",
    )(x1, gate, wg, wu, wd, g, b)


def _pad_cols(w, n):
    return jnp.pad(w, ((0, 0), (0, n - w.shape[1])))


def _row(v, n=None):
    v = v.reshape(1, -1)
    return v if n is None else _pad_cols(v, n)


def kernel(x, w_in, gla_w_alpha, gla_b_alpha, gla_norm_g, ssd_conv_w, ssd_conv_b, ssd_dt_bias, ssd_A_log, ssd_D, ssd_norm_g, rwkv_mu, rwkv_w0, rwkv_w2, rwkv_a0, rwkv_a2, rwkv_g2, rwkv_k_k, rwkv_k_a, rwkv_r_k, rwkv_ln_g, rwkv_ln_b, w_out, ln1_g, ln1_b, router_w, router_b, exp_w_gate, exp_w_up, exp_w_down, ln2_g, ln2_b):
    nb, t, d = x.shape
    depth = w_in.shape[0]
    alpha = float((2 * depth) ** 0.25)
    gla_cols = 4 * GLA_WIDTH + GLA_GATE_RANK
    ssd_cols = SSD_WIDTH + SSD_XBC + SSD_HEADS
    rwt = router_w.T
    rb = router_b.reshape(-1, 1)
    x2 = x.reshape(nb * t, d)
    for l in range(depth):
        wl = w_in[l]
        wg = _bf(_pad_cols(wl[:, :gla_cols], GLA_COLS_PAD))
        ws = _bf(_pad_cols(wl[:, gla_cols:gla_cols + ssd_cols], SSD_COLS_PAD))
        wr = _bf(wl[:, gla_cols + ssd_cols:])
        pg, ps, pr = _in_proj(x2, wg, ws, wr)
        og = _gla(pg.reshape(nb, t, -1),
                  jnp.pad(gla_w_alpha[l], ((0, LANES - GLA_GATE_RANK), (0, 0))),
                  _row(gla_b_alpha[l]), _row(jnp.tile(gla_norm_g[l], GLA_WIDTH // HEAD_DIM)))
        os_ = _ssd(ps.reshape(nb, t, -1), ssd_conv_w[l], _row(ssd_conv_b[l]), _row(ssd_dt_bias[l], LANES),
                   _row(ssd_A_log[l], LANES), _row(jnp.repeat(ssd_D[l], HEAD_DIM)), _row(ssd_norm_g[l]))
        or_ = _rwkv(pr.reshape(nb, t, -1), _row(rwkv_mu[l]), _row(rwkv_w0[l]), rwkv_w2[l], _row(rwkv_a0[l]),
                    rwkv_a2[l], rwkv_g2[l], _row(rwkv_k_k[l]), _row(rwkv_k_a[l]), _row(rwkv_r_k[l]),
                    _row(rwkv_ln_g[l]), _row(rwkv_ln_b[l]))
        x1, gate = _out_proj(alpha, x2, og.reshape(nb * t, -1), os_.reshape(nb * t, -1), or_.reshape(nb * t, -1),
                             _bf(w_out[l]), _row(ln1_g[l]), _row(ln1_b[l]), rwt, rb)
        x2 = _moe(alpha, x1, gate, _bf(exp_w_gate[l]), _bf(exp_w_up[l]), _bf(exp_w_down[l]),
                  _row(ln2_g[l]), _row(ln2_b[l]))
    return x2.reshape(nb, t, d)
```

```python
import functools
import math

import jax
import jax.numpy as jnp
from jax import lax
from jax.experimental import pallas as pl
from jax.experimental.pallas import tpu as pltpu

f32 = jnp.float32
bf16 = jnp.bfloat16

LANES = 128
HEAD_DIM = 64
GLA_WIDTH = 256
GLA_GATE_RANK = 16
GLA_GATE_TAU = 16.0
GLA_CHUNK = 16
GLA_BLOCK = 128
GLA_COLS_PAD = 4 * GLA_WIDTH + LANES
SSD_WIDTH = 512
SSD_HEADS = 8
SSD_GROUPS = 2
SSD_STATE = 64
SSD_CONV = 4
SSD_XBC = SSD_WIDTH + 2 * SSD_GROUPS * SSD_STATE
SSD_BLOCK = 128
SSD_COLS_PAD = SSD_WIDTH + SSD_XBC + LANES
RWKV_WIDTH = 256
RWKV_HEADS = 4
RWKV_CHUNK = 64
RWKV_GN_EPS = 64e-5
RWKV_COLS = 3 * RWKV_WIDTH + 64 + 64 + 128
N_EXPERTS = 16
N_EXPERT_GROUPS = 4
EXPERTS_PER_GROUP = 4
LN_EPS = 1e-5

IN_PROJ_ROWS = 512
OUT_PROJ_ROWS = 1024
MOE_ROWS = 1024
VMEM_LIMIT = 56 * 1024 * 1024


def _bf(x):
    return x.astype(bf16)


def _dot(a, b):
    return jnp.dot(_bf(a), _bf(b), preferred_element_type=f32)


def _dot_nt(a, b):
    return lax.dot_general(_bf(a), _bf(b), (((1,), (1,)), ((), ())), preferred_element_type=f32)


def _dot_tn(a, b):
    return lax.dot_general(_bf(a), _bf(b), (((0,), (0,)), ((), ())), preferred_element_type=f32)


def _split(x, n):
    parts, r = [], x
    for _ in range(n):
        p = r.astype(bf16)
        parts.append(p)
        r = r - p.astype(f32)
    return parts


def _dot_sel_rhs(a, sel, n=3):
    out = None
    for p in _split(a, n):
        t = jnp.dot(p, sel, preferred_element_type=f32)
        out = t if out is None else out + t
    return out


def _dot_sel_lhs(sel, b, n=3):
    out = None
    for p in _split(b, n):
        t = jnp.dot(sel, p, preferred_element_type=f32)
        out = t if out is None else out + t
    return out


def _dot_hi(a, b):
    ah, al = _split(a, 2)
    bh, bl = _split(b, 2)
    d = functools.partial(jnp.dot, preferred_element_type=f32)
    return d(ah, bh) + d(al, bh) + d(ah, bl)


def _iota(shape, dim):
    return lax.broadcasted_iota(jnp.int32, shape, dim)


def _block_ones(n, blk):
    return (_iota((n, n), 0) // blk == _iota((n, n), 1) // blk)


def _silu(x):
    return x * jax.nn.sigmoid(x)


def _softplus(x):
    return jnp.maximum(x, 0.0) + jnp.log1p(jnp.exp(-jnp.abs(x)))


def _layer_norm(y, g, b, eps):
    mu = jnp.mean(y, axis=-1, keepdims=True)
    yc = y - mu
    var = jnp.mean(yc * yc, axis=-1, keepdims=True)
    return yc * lax.rsqrt(var + eps) * g + b


def _cparams(sem):
    return pltpu.CompilerParams(dimension_semantics=sem, vmem_limit_bytes=VMEM_LIMIT)


def _inproj_kernel(x_ref, wg_ref, ws_ref, wr_ref, og_ref, os_ref, or_ref):
    xb = _bf(x_ref[...])
    og_ref[...] = jnp.dot(xb, wg_ref[...], preferred_element_type=f32)
    os_ref[...] = jnp.dot(xb, ws_ref[...], preferred_element_type=f32)
    or_ref[...] = jnp.dot(xb, wr_ref[...], preferred_element_type=f32)


def _in_proj(x2, wg, ws, wr):
    n, d = x2.shape
    tm = min(IN_PROJ_ROWS, n)
    full = lambda w: pl.BlockSpec(w.shape, lambda i: (0, 0))
    rows = lambda c: pl.BlockSpec((tm, c), lambda i: (i, 0))
    return pl.pallas_call(
        _inproj_kernel,
        grid=(n // tm,),
        in_specs=[rows(d), full(wg), full(ws), full(wr)],
        out_specs=[rows(wg.shape[1]), rows(ws.shape[1]), rows(wr.shape[1])],
        out_shape=[jax.ShapeDtypeStruct((n, w.shape[1]), f32) for w in (wg, ws, wr)],
        compiler_params=_cparams(("arbitrary",)),
        name="in_proj",
    )(x2, wg, ws, wr)


def _gla_kernel(p_ref, wa_ref, ba_ref, ng_ref, o_ref, st_ref, t_ref, d_ref, sx_ref):
    nb, tb, _ = p_ref.shape
    w, c = GLA_WIDTH, GLA_CHUNK
    nc = tb // c
    npt = w // LANES
    Q, K, V, CUM, OUT = range(5)

    @pl.when(pl.program_id(0) == 0)
    def _():
        st_ref[...] = jnp.zeros_like(st_ref)

    bones = _block_ones(w, HEAD_DIM).astype(bf16)
    pair_blk = _block_ones(LANES, HEAD_DIM)
    r_i, c_i = _iota((tb, tb), 0), _iota((tb, tb), 1)
    tri = ((r_i >= c_i) & (r_i // c == c_i // c)).astype(bf16)
    wa_h, wa_l = _split(wa_ref[...], 2)
    dd = functools.partial(jnp.dot, preferred_element_type=f32)

    for b in range(nb):
        a_h, a_l = _split(p_ref[b, :, 4 * w:4 * w + LANES], 2)
        z = dd(a_h, wa_h) + dd(a_l, wa_h) + dd(a_h, wa_l) + ba_ref[...]
        log_a = -_softplus(-z) * (1.0 / GLA_GATE_TAU)
        cum = _dot_sel_lhs(tri, log_a, 3)
        for lt in range(npt):
            lanes = slice(lt * LANES, (lt + 1) * LANES)
            t_ref[b, Q, lt] = p_ref[b, :, lanes] * (HEAD_DIM ** -0.5)
            t_ref[b, K, lt] = p_ref[b, :, w + lt * LANES:w + (lt + 1) * LANES]
            t_ref[b, V, lt] = p_ref[b, :, 2 * w + lt * LANES:2 * w + (lt + 1) * LANES]
            t_ref[b, CUM, lt] = cum[:, lanes]

    def ld(b, j, off):
        return jnp.concatenate([t_ref[b, j, lt, pl.ds(off, nc, stride=c), :] for lt in range(npt)], axis=1)

    for b in range(nb):
        qt = [ld(b, Q, o) for o in range(c)]
        kt = [ld(b, K, o) for o in range(c)]
        ct = [ld(b, CUM, o) for o in range(c)]
        pair = 0
        for t in range(c):
            for s in range(t + 1):
                qk = qt[t] * kt[s]
                d_ref[b, pair * nc:(pair + 1) * nc, :] = qk if s == t else qk * jnp.exp(ct[t] - ct[s])
                pair += 1
    for b in range(nb):
        sx_ref[b] = jnp.dot(_bf(d_ref[b]), bones, preferred_element_type=f32)
    for b in range(nb):
        vt = [ld(b, V, o) for o in range(c)]
        pair = 0
        for t in range(c):
            acc = None
            for s in range(t + 1):
                term = sx_ref[b, pair * nc:(pair + 1) * nc, :] * vt[s]
                acc = term if acc is None else acc + term
                pair += 1
            for lt in range(npt):
                t_ref[b, OUT, lt, pl.ds(t, nc, stride=c), :] = acc[:, lt * LANES:(lt + 1) * LANES]

    for ci in range(nc):
        sl = slice(ci * c, (ci + 1) * c)
        last = slice((ci + 1) * c - 1, (ci + 1) * c)
        for b in range(nb):
            for lt in range(npt):
                cum_c = t_ref[b, CUM, lt, sl, :]
                cum_last = t_ref[b, CUM, lt, last, :]
                st = st_ref[b, lt]
                t_ref[b, OUT, lt, sl, :] += _dot_nt(t_ref[b, Q, lt, sl, :] * jnp.exp(cum_c), st)
                upd = _dot_tn(t_ref[b, V, lt, sl, :], t_ref[b, K, lt, sl, :] * jnp.exp(cum_last - cum_c))
                st_ref[b, lt] = st * jnp.exp(cum_last) + jnp.where(pair_blk, upd, 0.0)

    o = jnp.concatenate(
        [jnp.concatenate([t_ref[b, OUT, lt] for lt in range(npt)], axis=1) for b in range(nb)], axis=0)
    ms = _dot_sel_rhs(o * o, bones, 2) * (1.0 / HEAD_DIM)
    o = o * lax.rsqrt(ms + LN_EPS) * ng_ref[...]
    for b in range(nb):
        o_ref[b] = o[b * tb:(b + 1) * tb] * _silu(p_ref[b, :, 3 * w:4 * w])


def _gla(pg, wa, ba, ng):
    nb, t, cols = pg.shape
    tb = GLA_BLOCK
    npt = GLA_WIDTH // LANES
    pair_rows = (GLA_CHUNK * (GLA_CHUNK + 1) // 2) * (tb // GLA_CHUNK)
    full2 = lambda a: pl.BlockSpec(a.shape, lambda i: (0, 0))
    return pl.pallas_call(
        _gla_kernel,
        grid=(t // tb,),
        in_specs=[pl.BlockSpec((nb, tb, cols), lambda i: (0, i, 0)), full2(wa), full2(ba), full2(ng)],
        out_specs=pl.BlockSpec((nb, tb, GLA_WIDTH), lambda i: (0, i, 0)),
        out_shape=jax.ShapeDtypeStruct((nb, t, GLA_WIDTH), f32),
        scratch_shapes=[
            pltpu.VMEM((nb, npt, LANES, LANES), f32),
            pltpu.VMEM((nb, 5, npt, tb, LANES), f32),
            pltpu.VMEM((nb, pair_rows, GLA_WIDTH), f32),
            pltpu.VMEM((nb, pair_rows, GLA_WIDTH), f32),
        ],
        compiler_params=_cparams(("arbitrary",)),
        name="gla_mixer",
    )(pg, wa, ba, ng)


def _ssd_kernel(p_ref, prev_ref, cw_ref, cb_ref, dtb_ref, alog_ref, dsk_ref, ng_ref, o_ref, st_ref):
    _, L, _ = p_ref.shape
    W, M, G = SSD_WIDTH, SSD_STATE, SSD_GROUPS
    gw = W // G
    i = pl.program_id(1)

    @pl.when(i == 0)
    def _():
        st_ref[...] = jnp.zeros_like(st_ref)

    z = p_ref[0, :, 0:W]
    xbc = p_ref[0, :, W:W + SSD_XBC]
    dt_raw = p_ref[0, :, W + SSD_XBC:W + SSD_XBC + LANES]
    prev = jnp.where(i > 0, prev_ref[0, :, W:W + SSD_XBC], 0.0)
    xx = jnp.concatenate([prev, xbc], axis=0)
    conv = cb_ref[...]
    for j in range(SSD_CONV):
        off = 8 - (SSD_CONV - 1) + j
        conv = conv + cw_ref[j:j + 1, :] * xx[off:off + L]
    act = _silu(conv)
    xs = act[:, 0:W]
    bm = act[:, W:W + G * M]
    cm = act[:, W + G * M:W + 2 * G * M]

    dt = _softplus(dt_raw + dtb_ref[...])
    a_neg = jnp.where(_iota((1, LANES), 1) < SSD_HEADS, -jnp.exp(alog_ref[...]), 0.0)
    da = dt * a_neg
    r_i, c_i = _iota((L, L), 0), _iota((L, L), 1)
    causal = r_i >= c_i
    cs = _dot_sel_lhs(causal.astype(bf16), da, 3)
    expand = (_iota((LANES, W), 0) == _iota((LANES, W), 1) // HEAD_DIM).astype(bf16)
    dtx = _dot_sel_rhs(dt, expand, 3)
    csx = _dot_sel_rhs(cs, expand, 3)
    eye_h = (_iota((8, LANES), 0) == _iota((8, LANES), 1)).astype(bf16)
    cs_t = None
    for p in _split(cs, 3):
        t = lax.dot_general(eye_h, p, (((1,), (1,)), ((), ())), preferred_element_type=f32)
        cs_t = t if cs_t is None else cs_t + t
    xdt = xs * dtx

    st = st_ref[...]
    y_parts = []
    for gi in range(G):
        bm_g = bm[:, gi * M:(gi + 1) * M]
        cm_g = cm[:, gi * M:(gi + 1) * M]
        cb = _dot_nt(cm_g, bm_g)
        for e in range(SSD_HEADS // G):
            h = gi * (SSD_HEADS // G) + e
            seg = jnp.where(causal, jnp.exp(jnp.broadcast_to(cs[:, h:h + 1], (L, L)) - cs_t[h:h + 1, :]), 0.0)
            y_parts.append(_dot(cb * seg, xdt[:, h * HEAD_DIM:(h + 1) * HEAD_DIM]))
    y = jnp.concatenate(y_parts, axis=1)
    y_inter = jnp.concatenate(
        [_dot(cm[:, gi * M:(gi + 1) * M], st[:, gi * gw:(gi + 1) * gw]) for gi in range(G)], axis=1)
    y = y + y_inter * jnp.exp(csx)

    cs_last = csx[L - 1:L, :]
    xdtw = xdt * jnp.exp(cs_last - csx)
    d_state = jnp.concatenate(
        [_dot_tn(bm[:, gi * M:(gi + 1) * M], xdtw[:, gi * gw:(gi + 1) * gw]) for gi in range(G)], axis=1)
    st_ref[...] = st * jnp.exp(cs_last) + d_state

    y = (y + dsk_ref[...] * xs) * _silu(z)
    outs = []
    for gi in range(G):
        yg = y[:, gi * gw:(gi + 1) * gw]
        ms = jnp.mean(yg * yg, axis=-1, keepdims=True)
        outs.append(yg * lax.rsqrt(ms + LN_EPS))
    o_ref[0] = jnp.concatenate(outs, axis=1) * ng_ref[...]


def _ssd(ps, cw, cb, dtb, alog, dsk, ng):
    nb, t, cols = ps.shape
    L = SSD_BLOCK
    full2 = lambda a: pl.BlockSpec(a.shape, lambda b, i: (0, 0))
    return pl.pallas_call(
        _ssd_kernel,
        grid=(nb, t // L),
        in_specs=[
            pl.BlockSpec((1, L, cols), lambda b, i: (b, i, 0)),
            pl.BlockSpec((1, 8, cols), lambda b, i: (b, jnp.maximum(i * (L // 8) - 1, 0), 0)),
            full2(cw), full2(cb), full2(dtb), full2(alog), full2(dsk), full2(ng),
        ],
        out_specs=pl.BlockSpec((1, L, SSD_WIDTH), lambda b, i: (b, i, 0)),
        out_shape=jax.ShapeDtypeStruct((nb, t, SSD_WIDTH), f32),
        scratch_shapes=[pltpu.VMEM((SSD_STATE, SSD_WIDTH), f32)],
        compiler_params=_cparams(("arbitrary", "arbitrary")),
        name="ssd_mixer",
    )(ps, ps, cw, cb, dtb, alog, dsk, ng)


def _rwkv_kernel(p_ref, prev_ref, mu_ref, w0_ref, w2_ref, a0_ref, a2_ref, g2_ref, kk_ref, ka_ref, rk_ref,
                 lng_ref, lnb_ref, o_ref, zt_ref):
    nb, C, _ = p_ref.shape
    W, H = RWKV_WIDTH, RWKV_HEADS
    i = pl.program_id(0)

    @pl.when(i == 0)
    def _():
        zt_ref[...] = jnp.zeros_like(zt_ref)

    head_blk = _block_ones(W, HEAD_DIM)
    bones = head_blk.astype(bf16)
    r_i, c_i = _iota((C, C), 0), _iota((C, C), 1)
    tri = (r_i >= c_i).astype(bf16)
    n = H * C
    rr, cc = _iota((n, n), 0), _iota((n, n), 1)
    same = rr // C == cc // C
    strict = same & (rr > cc)
    lower = same & (rr >= cc)
    eye = jnp.where(rr == cc, 1.0, 0.0)

    def stack(t):
        return jnp.where(head_blk, jnp.concatenate([t] * H, axis=0), 0.0)

    def prepare(b):
        cols = p_ref[b]
        prev_row = jnp.where(i > 0, prev_ref[b, 7:8, :], 0.0)
        shifted = jnp.concatenate([prev_row, cols[:C - 1]], axis=0)
        cols = cols + (shifted - cols) * mu_ref[...]
        r = cols[:, 0:W]
        k = cols[:, W:2 * W]
        v = cols[:, 2 * W:3 * W]
        w_lr = cols[:, 3 * W:3 * W + 64]
        a_lr = cols[:, 3 * W + 64:3 * W + 128]
        g_lr = cols[:, 3 * W + 128:3 * W + 256]

        wv = -_softplus(-(w0_ref[...] + _dot_hi(jnp.tanh(w_lr), w2_ref[...]))) - 0.5
        lw = -jnp.exp(wv)
        a = jax.nn.sigmoid(a0_ref[...] + _dot_hi(a_lr, a2_ref[...]))
        g = _dot_hi(jax.nn.sigmoid(g_lr), g2_ref[...])

        kk = k * kk_ref[...]
        kk_norm = jnp.sqrt(_dot_sel_rhs(kk * kk, bones, 3))
        kk = kk / jnp.maximum(kk_norm, 1e-12)
        k2 = k * (1.0 + (a - 1.0) * ka_ref[...])
        bvec = kk * a

        cum = _dot_sel_lhs(tri, lw, 3)
        cum_last = cum[C - 1:C, :]
        e_neg = jnp.exp(-cum)
        e_rem = jnp.exp(cum_last - cum)
        bonus = _dot_sel_rhs(r * k2 * rk_ref[...], bones, 3)
        return dict(
            aw=stack(-kk * jnp.exp(cum - lw)), rw=stack(r * jnp.exp(cum)), bn=stack(bvec * e_neg),
            kn=stack(k2 * e_neg), bt=stack(bvec * e_rem), kt=stack(k2 * e_rem), vs=stack(v),
            wc=jnp.exp(cum_last), tail=(bonus * v, g))

    rows = range(nb)
    q = [prepare(b) for b in rows]
    a_ab = [jnp.where(strict, _dot_nt(q[b]["aw"], q[b]["bn"]), 0.0) for b in rows]
    a_ak = [jnp.where(strict, _dot_nt(q[b]["aw"], q[b]["kn"]), 0.0) for b in rows]
    a_rb = [jnp.where(lower, _dot_nt(q[b]["rw"], q[b]["bn"]), 0.0) for b in rows]
    a_rk = [jnp.where(lower, _dot_nt(q[b]["rw"], q[b]["kn"]), 0.0) for b in rows]

    pw = a_ab
    tinv = [eye + a_ab[b] for b in rows]
    for _ in range(int(math.log2(C)) - 1):
        pw = [_dot(pw[b], pw[b]) for b in rows]
        tinv = [tinv[b] + _dot(tinv[b], pw[b]) for b in rows]

    zt = [zt_ref[b] for b in rows]
    rhs = [_dot_nt(q[b]["aw"], zt[b]) + _dot(a_ak[b], q[b]["vs"]) for b in rows]
    y0 = [_dot_nt(q[b]["rw"], zt[b]) + _dot(a_rk[b], q[b]["vs"]) for b in rows]
    u = [_dot(tinv[b], rhs[b]) for b in rows]
    y_st = [y0[b] + _dot(a_rb[b], u[b]) for b in rows]
    for b in rows:
        zt_ref[b] = zt[b] * q[b]["wc"] + _dot_tn(u[b], q[b]["bt"]) + _dot_tn(q[b]["vs"], q[b]["kt"])

    inv_n = 1.0 / HEAD_DIM
    for b in rows:
        y = y_st[b][0:C]
        for h in range(1, H):
            y = y + y_st[b][h * C:(h + 1) * C]
        mu_h = _dot_sel_rhs(y, bones, 3) * inv_n
        yc = y - mu_h
        var_h = _dot_sel_rhs(yc * yc, bones, 3) * inv_n
        y = yc * lax.rsqrt(var_h + RWKV_GN_EPS) * lng_ref[...] + lnb_ref[...]
        bonus_v, g = q[b]["tail"]
        o_ref[b] = (y + bonus_v) * g


def _rwkv(pr, mu, w0, w2, a0, a2, g2, k_k, k_a, r_k, ln_g, ln_b):
    nb, t, cols = pr.shape
    C = RWKV_CHUNK
    full2 = lambda a: pl.BlockSpec(a.shape, lambda i: (0, 0))
    params = (mu, w0, w2, a0, a2, g2, k_k, k_a, r_k, ln_g, ln_b)
    return pl.pallas_call(
        _rwkv_kernel,
        grid=(t // C,),
        in_specs=[
            pl.BlockSpec((nb, C, cols), lambda i: (0, i, 0)),
            pl.BlockSpec((nb, 8, cols), lambda i: (0, jnp.maximum(i * (C // 8) - 1, 0), 0)),
        ] + [full2(p) for p in params],
        out_specs=pl.BlockSpec((nb, C, RWKV_WIDTH), lambda i: (0, i, 0)),
        out_shape=jax.ShapeDtypeStruct((nb, t, RWKV_WIDTH), f32),
        scratch_shapes=[pltpu.VMEM((nb, RWKV_WIDTH, RWKV_WIDTH), f32)],
        compiler_params=_cparams(("arbitrary",)),
        name="rwkv7_mixer",
    )(pr, pr, *params)


def _outproj_kernel(alpha, x_ref, og_ref, os_ref, or_ref, wo_ref, g_ref, b_ref, rwt_ref, rb_ref, x1_ref, gate_ref):
    tm = x_ref.shape[0]
    d = functools.partial(jnp.dot, preferred_element_type=f32)
    mix = (d(_bf(og_ref[...]), wo_ref[0:GLA_WIDTH, :])
           + d(_bf(os_ref[...]), wo_ref[GLA_WIDTH:GLA_WIDTH + SSD_WIDTH, :])
           + d(_bf(or_ref[...]), wo_ref[GLA_WIDTH + SSD_WIDTH:, :]))
    x1 = _layer_norm(alpha * x_ref[...] + mix, g_ref[...], b_ref[...], LN_EPS)
    x1_ref[...] = x1

    xh, xl = _split(x1, 2)
    wh, wl = _split(rwt_ref[...], 2)
    nt = lambda a, b: lax.dot_general(a, b, (((1,), (1,)), ((), ())), preferred_element_type=f32)
    logits = nt(wh, xh) + nt(wl, xh) + nt(wh, xl)
    scores = jax.nn.sigmoid(logits)
    sel = scores + rb_ref[...]
    srow = [sel[e:e + 1, :] for e in range(N_EXPERTS)]
    crow = [scores[e:e + 1, :] for e in range(N_EXPERTS)]

    def top2_sum(v0, v1, v2, v3):
        m01, n01 = jnp.maximum(v0, v1), jnp.minimum(v0, v1)
        m23, n23 = jnp.maximum(v2, v3), jnp.minimum(v2, v3)
        return jnp.maximum(m01, m23) + jnp.maximum(jnp.minimum(m01, m23), jnp.maximum(n01, n23))

    gscore = [top2_sum(*srow[4 * gi:4 * gi + 4]) for gi in range(N_EXPERT_GROUPS)]
    best = jnp.zeros((1, tm), jnp.int32)
    best_v = gscore[0]
    for gi in range(1, N_EXPERT_GROUPS):
        better = gscore[gi] > best_v
        best = jnp.where(better, gi, best)
        best_v = jnp.where(better, gscore[gi], best_v)
    sv, cv = [], []
    for j in range(EXPERTS_PER_GROUP):
        s_j, c_j = srow[j], crow[j]
        for gi in range(1, N_EXPERT_GROUPS):
            s_j = jnp.where(best == gi, srow[4 * gi + j], s_j)
            c_j = jnp.where(best == gi, crow[4 * gi + j], c_j)
        sv.append(s_j)
        cv.append(c_j)
    j1 = jnp.zeros((1, tm), jnp.int32)
    v1 = sv[0]
    for j in range(1, EXPERTS_PER_GROUP):
        better = sv[j] > v1
        j1 = jnp.where(better, j, j1)
        v1 = jnp.where(better, sv[j], v1)
    j2 = jnp.full((1, tm), -1, jnp.int32)
    v2 = jnp.full((1, tm), -jnp.inf, f32)
    for j in range(EXPERTS_PER_GROUP):
        better = (j1 != j) & ((j2 < 0) | (sv[j] > v2))
        j2 = jnp.where(better, j, j2)
        v2 = jnp.where(better, sv[j], v2)
    w1 = cv[0]
    w2 = cv[0]
    for j in range(1, EXPERTS_PER_GROUP):
        w1 = jnp.where(j1 == j, cv[j], w1)
        w2 = jnp.where(j2 == j, cv[j], w2)
    wsum = w1 + w2
    g1, g2 = w1 / wsum, w2 / wsum
    rows = []
    for e in range(N_EXPERTS):
        gi, j = divmod(e, EXPERTS_PER_GROUP)
        in_g = best == gi
        rows.append(jnp.where(in_g & (j1 == j), g1, jnp.where(in_g & (j2 == j), g2, 0.0)))
    gate_t = jnp.concatenate(rows, axis=0)
    eye = (_iota((N_EXPERTS, LANES), 0) == _iota((N_EXPERTS, LANES), 1)).astype(bf16)
    gate = None
    for p in _split(gate_t, 3):
        t = lax.dot_general(p, eye, (((0,), (0,)), ((), ())), preferred_element_type=f32)
        gate = t if gate is None else gate + t
    gate_ref[...] = gate


def _out_proj(alpha, x2, og, os_, or_, wo, g, b, rwt, rb):
    n, d = x2.shape
    tm = min(OUT_PROJ_ROWS, n)
    full2 = lambda a: pl.BlockSpec(a.shape, lambda i: (0, 0))
    rows = lambda c: pl.BlockSpec((tm, c), lambda i: (i, 0))
    return pl.pallas_call(
        functools.partial(_outproj_kernel, alpha),
        grid=(n // tm,),
        in_specs=[rows(d), rows(og.shape[1]), rows(os_.shape[1]), rows(or_.shape[1]),
                  full2(wo), full2(g), full2(b), full2(rwt), full2(rb)],
        out_specs=[rows(d), rows(LANES)],
        out_shape=[jax.ShapeDtypeStruct((n, d), f32), jax.ShapeDtypeStruct((n, LANES), f32)],
        compiler_params=_cparams(("arbitrary",)),
        name="out_proj_router",
    )(x2, og, os_, or_, wo, g, b, rwt, rb)


def _moe_kernel(alpha, x_ref, gate_ref, wg_ref, wu_ref, wd_ref, g_ref, b_ref, o_ref, xb_ref, acc_ref):
    e = pl.program_id(1)

    @pl.when(e == 0)
    def _():
        xb_ref[...] = _bf(x_ref[...])
        acc_ref[...] = jnp.zeros_like(acc_ref)

    xb = xb_ref[...]
    hg = jnp.dot(xb, wg_ref[0], preferred_element_type=f32)
    hu = jnp.dot(xb, wu_ref[0], preferred_element_type=f32)
    gate = gate_ref[...]
    gcol = jnp.sum(jnp.where(_iota(gate.shape, 1) == e, gate, 0.0), axis=-1, keepdims=True)
    he = _silu(hg) * hu * gcol
    acc_ref[...] += jnp.dot(_bf(he), wd_ref[0], preferred_element_type=f32)

    @pl.when(e == pl.num_programs(1) - 1)
    def _():
        o_ref[...] = _layer_norm(alpha * x_ref[...] + acc_ref[...], g_ref[...], b_ref[...], LN_EPS)


def _moe(alpha, x1, gate, wg, wu, wd, g, b):
    n, d = x1.shape
    ne, _, ff = wg.shape
    tm = min(MOE_ROWS, n)
    full2 = lambda a: pl.BlockSpec(a.shape, lambda i, e: (0, 0))
    return pl.pallas_call(
        functools.partial(_moe_kernel, alpha),
        grid=(n // tm, ne),
        in_specs=[
            pl.BlockSpec((tm, d), lambda i, e: (i, 0)),
            pl.BlockSpec((tm, LANES), lambda i, e: (i, 0)),
            pl.BlockSpec((1, d, ff), lambda i, e: (e, 0, 0)),
            pl.BlockSpec((1, d, ff), lambda i, e: (e, 0, 0)),
            pl.BlockSpec((1, ff, d), lambda i, e: (e, 0, 0)),
            full2(g), full2(b),
        ],
        out_specs=pl.BlockSpec((tm, d), lambda i, e: (i, 0)),
        out_shape=jax.ShapeDtypeStruct((n, d), f32),
        scratch_shapes=[pltpu.VMEM((tm, d), bf16), pltpu.VMEM((tm, d), f32)],
        compiler_params=_cparams(("arbitrary", "arbitrary")),
        name="moe_experts",
    )(x1, gate, wg, wu, wd, g, b)


def _pad_cols(w, n):
    return jnp.pad(w, ((0, 0), (0, n - w.shape[1])))


def _row(v, n=None):
    v = v.reshape(1, -1)
    return v if n is None else _pad_cols(v, n)


def kernel(x, w_in, gla_w_alpha, gla_b_alpha, gla_norm_g, ssd_conv_w, ssd_conv_b, ssd_dt_bias, ssd_A_log, ssd_D, ssd_norm_g, rwkv_mu, rwkv_w0, rwkv_w2, rwkv_a0, rwkv_a2, rwkv_g2, rwkv_k_k, rwkv_k_a, rwkv_r_k, rwkv_ln_g, rwkv_ln_b, w_out, ln1_g, ln1_b, router_w, router_b, exp_w_gate, exp_w_up, exp_w_down, ln2_g, ln2_b):
    nb, t, d = x.shape
    depth = w_in.shape[0]
    alpha = float((2 * depth) ** 0.25)
    gla_cols = 4 * GLA_WIDTH + GLA_GATE_RANK
    ssd_cols = SSD_WIDTH + SSD_XBC + SSD_HEADS
    rwt = router_w.T
    rb = router_b.reshape(-1, 1)
    x2 = x.reshape(nb * t, d)
    for l in range(depth):
        wl = w_in[l]
        wg = _bf(_pad_cols(wl[:, :gla_cols], GLA_COLS_PAD))
        ws = _bf(_pad_cols(wl[:, gla_cols:gla_cols + ssd_cols], SSD_COLS_PAD))
        wr = _bf(wl[:, gla_cols + ssd_cols:])
        pg, ps, pr = _in_proj(x2, wg, ws, wr)
        og = _gla(pg.reshape(nb, t, -1),
                  jnp.pad(gla_w_alpha[l], ((0, LANES - GLA_GATE_RANK), (0, 0))),
                  _row(gla_b_alpha[l]), _row(jnp.tile(gla_norm_g[l], GLA_WIDTH // HEAD_DIM)))
        os_ = _ssd(ps.reshape(nb, t, -1), ssd_conv_w[l], _row(ssd_conv_b[l]), _row(ssd_dt_bias[l], LANES),
                   _row(ssd_A_log[l], LANES), _row(jnp.repeat(ssd_D[l], HEAD_DIM)), _row(ssd_norm_g[l]))
        or_ = _rwkv(pr.reshape(nb, t, -1), _row(rwkv_mu[l]), _row(rwkv_w0[l]), rwkv_w2[l], _row(rwkv_a0[l]),
                    rwkv_a2[l], rwkv_g2[l], _row(rwkv_k_k[l]), _row(rwkv_k_a[l]), _row(rwkv_r_k[l]),
                    _row(rwkv_ln_g[l]), _row(rwkv_ln_b[l]))
        x1, gate = _out_proj(alpha, x2, og.reshape(nb * t, -1), os_.reshape(nb * t, -1), or_.reshape(nb * t, -1),
                             _bf(w_out[l]), _row(ln1_g[l]), _row(ln1_b[l]), rwt, rb)
        x2 = _moe(alpha, x1, gate, _bf(exp_w_gate[l]), _bf(exp_w_up[l]), _bf(exp_w_down[l]),
                  _row(ln2_g[l]), _row(ln2_b[l]))
    return x2.reshape(nb, t, d)
```

```python
import functools
import math

import jax
import jax.numpy as jnp
from jax import lax
from jax.experimental import pallas as pl
from jax.experimental.pallas import tpu as pltpu

f32 = jnp.float32
bf16 = jnp.bfloat16

LANES = 128
HEAD_DIM = 64
GLA_WIDTH = 256
GLA_GATE_RANK = 16
GLA_GATE_TAU = 16.0
GLA_CHUNK = 16
GLA_BLOCK = 128
GLA_COLS_PAD = 4 * GLA_WIDTH + LANES
SSD_WIDTH = 512
SSD_HEADS = 8
SSD_GROUPS = 2
SSD_STATE = 64
SSD_CONV = 4
SSD_XBC = SSD_WIDTH + 2 * SSD_GROUPS * SSD_STATE
SSD_BLOCK = 128
SSD_COLS_PAD = SSD_WIDTH + SSD_XBC + LANES
RWKV_WIDTH = 256
RWKV_HEADS = 4
RWKV_CHUNK = 64
RWKV_GN_EPS = 64e-5
RWKV_COLS = 3 * RWKV_WIDTH + 64 + 64 + 128
N_EXPERTS = 16
N_EXPERT_GROUPS = 4
EXPERTS_PER_GROUP = 4
LN_EPS = 1e-5

IN_PROJ_ROWS = 512
OUT_PROJ_ROWS = 1024
MOE_TILE = 512
SEG_ALIGN = 8
VMEM_LIMIT = 56 * 1024 * 1024


def _bf(x):
    return x.astype(bf16)


def _dot(a, b):
    return jnp.dot(_bf(a), _bf(b), preferred_element_type=f32)


def _dot_nt(a, b):
    return lax.dot_general(_bf(a), _bf(b), (((1,), (1,)), ((), ())), preferred_element_type=f32)


def _dot_tn(a, b):
    return lax.dot_general(_bf(a), _bf(b), (((0,), (0,)), ((), ())), preferred_element_type=f32)


def _split(x, n):
    parts, r = [], x
    for _ in range(n):
        p = r.astype(bf16)
        parts.append(p)
        r = r - p.astype(f32)
    return parts


def _dot_sel_rhs(a, sel, n=3):
    out = None
    for p in _split(a, n):
        t = jnp.dot(p, sel, preferred_element_type=f32)
        out = t if out is None else out + t
    return out


def _dot_sel_lhs(sel, b, n=3):
    out = None
    for p in _split(b, n):
        t = jnp.dot(sel, p, preferred_element_type=f32)
        out = t if out is None else out + t
    return out


def _dot_hi(a, b):
    ah, al = _split(a, 2)
    bh, bl = _split(b, 2)
    d = functools.partial(jnp.dot, preferred_element_type=f32)
    return d(ah, bh) + d(al, bh) + d(ah, bl)


def _iota(shape, dim):
    return lax.broadcasted_iota(jnp.int32, shape, dim)


def _block_ones(n, blk):
    return (_iota((n, n), 0) // blk == _iota((n, n), 1) // blk)


def _silu(x):
    return x * jax.nn.sigmoid(x)


def _softplus(x):
    return jnp.maximum(x, 0.0) + jnp.log1p(jnp.exp(-jnp.abs(x)))


def _layer_norm(y, g, b, eps):
    mu = jnp.mean(y, axis=-1, keepdims=True)
    yc = y - mu
    var = jnp.mean(yc * yc, axis=-1, keepdims=True)
    return yc * lax.rsqrt(var + eps) * g + b


def _cparams(sem):
    return pltpu.CompilerParams(dimension_semantics=sem, vmem_limit_bytes=VMEM_LIMIT)


def _inproj_kernel(x_ref, wg_ref, ws_ref, wr_ref, og_ref, os_ref, or_ref):
    xb = _bf(x_ref[...])
    og_ref[...] = jnp.dot(xb, wg_ref[...], preferred_element_type=f32)
    os_ref[...] = jnp.dot(xb, ws_ref[...], preferred_element_type=f32)
    or_ref[...] = jnp.dot(xb, wr_ref[...], preferred_element_type=f32)


def _in_proj(x2, wg, ws, wr):
    n, d = x2.shape
    tm = min(IN_PROJ_ROWS, n)
    full = lambda w: pl.BlockSpec(w.shape, lambda i: (0, 0))
    rows = lambda c: pl.BlockSpec((tm, c), lambda i: (i, 0))
    return pl.pallas_call(
        _inproj_kernel,
        grid=(n // tm,),
        in_specs=[rows(d), full(wg), full(ws), full(wr)],
        out_specs=[rows(wg.shape[1]), rows(ws.shape[1]), rows(wr.shape[1])],
        out_shape=[jax.ShapeDtypeStruct((n, w.shape[1]), f32) for w in (wg, ws, wr)],
        compiler_params=_cparams(("arbitrary",)),
        name="in_proj",
    )(x2, wg, ws, wr)


def _gla_kernel(p_ref, wa_ref, ba_ref, ng_ref, o_ref, st_ref, t_ref, d_ref, sx_ref):
    nb, tb, _ = p_ref.shape
    w, c = GLA_WIDTH, GLA_CHUNK
    nc = tb // c
    npt = w // LANES
    Q, K, V, CUM, OUT = range(5)

    @pl.when(pl.program_id(0) == 0)
    def _():
        st_ref[...] = jnp.zeros_like(st_ref)

    bones = _block_ones(w, HEAD_DIM).astype(bf16)
    pair_blk = _block_ones(LANES, HEAD_DIM)
    r_i, c_i = _iota((tb, tb), 0), _iota((tb, tb), 1)
    tri = ((r_i >= c_i) & (r_i // c == c_i // c)).astype(bf16)
    wa_h, wa_l = _split(wa_ref[...], 2)
    dd = functools.partial(jnp.dot, preferred_element_type=f32)

    for b in range(nb):
        a_h, a_l = _split(p_ref[b, :, 4 * w:4 * w + LANES], 2)
        z = dd(a_h, wa_h) + dd(a_l, wa_h) + dd(a_h, wa_l) + ba_ref[...]
        log_a = -_softplus(-z) * (1.0 / GLA_GATE_TAU)
        cum = _dot_sel_lhs(tri, log_a, 3)
        for lt in range(npt):
            lanes = slice(lt * LANES, (lt + 1) * LANES)
            t_ref[b, Q, lt] = p_ref[b, :, lanes] * (HEAD_DIM ** -0.5)
            t_ref[b, K, lt] = p_ref[b, :, w + lt * LANES:w + (lt + 1) * LANES]
            t_ref[b, V, lt] = p_ref[b, :, 2 * w + lt * LANES:2 * w + (lt + 1) * LANES]
            t_ref[b, CUM, lt] = cum[:, lanes]

    def ld(b, j, off):
        return jnp.concatenate([t_ref[b, j, lt, pl.ds(off, nc, stride=c), :] for lt in range(npt)], axis=1)

    for b in range(nb):
        qt = [ld(b, Q, o) for o in range(c)]
        kt = [ld(b, K, o) for o in range(c)]
        ct = [ld(b, CUM, o) for o in range(c)]
        pair = 0
        for t in range(c):
            for s in range(t + 1):
                qk = qt[t] * kt[s]
                d_ref[b, pair * nc:(pair + 1) * nc, :] = qk if s == t else qk * jnp.exp(ct[t] - ct[s])
                pair += 1
    for b in range(nb):
        sx_ref[b] = jnp.dot(_bf(d_ref[b]), bones, preferred_element_type=f32)
    for b in range(nb):
        vt = [ld(b, V, o) for o in range(c)]
        pair = 0
        for t in range(c):
            acc = None
            for s in range(t + 1):
                term = sx_ref[b, pair * nc:(pair + 1) * nc, :] * vt[s]
                acc = term if acc is None else acc + term
                pair += 1
            for lt in range(npt):
                t_ref[b, OUT, lt, pl.ds(t, nc, stride=c), :] = acc[:, lt * LANES:(lt + 1) * LANES]

    for ci in range(nc):
        sl = slice(ci * c, (ci + 1) * c)
        last = slice((ci + 1) * c - 1, (ci + 1) * c)
        for b in range(nb):
            for lt in range(npt):
                cum_c = t_ref[b, CUM, lt, sl, :]
                cum_last = t_ref[b, CUM, lt, last, :]
                st = st_ref[b, lt]
                t_ref[b, OUT, lt, sl, :] += _dot_nt(t_ref[b, Q, lt, sl, :] * jnp.exp(cum_c), st)
                upd = _dot_tn(t_ref[b, V, lt, sl, :], t_ref[b, K, lt, sl, :] * jnp.exp(cum_last - cum_c))
                st_ref[b, lt] = st * jnp.exp(cum_last) + jnp.where(pair_blk, upd, 0.0)

    o = jnp.concatenate(
        [jnp.concatenate([t_ref[b, OUT, lt] for lt in range(npt)], axis=1) for b in range(nb)], axis=0)
    ms = _dot_sel_rhs(o * o, bones, 2) * (1.0 / HEAD_DIM)
    o = o * lax.rsqrt(ms + LN_EPS) * ng_ref[...]
    for b in range(nb):
        o_ref[b] = o[b * tb:(b + 1) * tb] * _silu(p_ref[b, :, 3 * w:4 * w])


def _gla(pg, wa, ba, ng):
    nb, t, cols = pg.shape
    tb = GLA_BLOCK
    npt = GLA_WIDTH // LANES
    pair_rows = (GLA_CHUNK * (GLA_CHUNK + 1) // 2) * (tb // GLA_CHUNK)
    full2 = lambda a: pl.BlockSpec(a.shape, lambda i: (0, 0))
    return pl.pallas_call(
        _gla_kernel,
        grid=(t // tb,),
        in_specs=[pl.BlockSpec((nb, tb, cols), lambda i: (0, i, 0)), full2(wa), full2(ba), full2(ng)],
        out_specs=pl.BlockSpec((nb, tb, GLA_WIDTH), lambda i: (0, i, 0)),
        out_shape=jax.ShapeDtypeStruct((nb, t, GLA_WIDTH), f32),
        scratch_shapes=[
            pltpu.VMEM((nb, npt, LANES, LANES), f32),
            pltpu.VMEM((nb, 5, npt, tb, LANES), f32),
            pltpu.VMEM((nb, pair_rows, GLA_WIDTH), f32),
            pltpu.VMEM((nb, pair_rows, GLA_WIDTH), f32),
        ],
        compiler_params=_cparams(("arbitrary",)),
        name="gla_mixer",
    )(pg, wa, ba, ng)


def _ssd_kernel(p_ref, prev_ref, cw_ref, cb_ref, dtb_ref, alog_ref, dsk_ref, ng_ref, o_ref, st_ref):
    _, L, _ = p_ref.shape
    W, M, G = SSD_WIDTH, SSD_STATE, SSD_GROUPS
    gw = W // G
    i = pl.program_id(1)

    @pl.when(i == 0)
    def _():
        st_ref[...] = jnp.zeros_like(st_ref)

    z = p_ref[0, :, 0:W]
    xbc = p_ref[0, :, W:W + SSD_XBC]
    dt_raw = p_ref[0, :, W + SSD_XBC:W + SSD_XBC + LANES]
    prev = jnp.where(i > 0, prev_ref[0, :, W:W + SSD_XBC], 0.0)
    xx = jnp.concatenate([prev, xbc], axis=0)
    conv = cb_ref[...]
    for j in range(SSD_CONV):
        off = 8 - (SSD_CONV - 1) + j
        conv = conv + cw_ref[j:j + 1, :] * xx[off:off + L]
    act = _silu(conv)
    xs = act[:, 0:W]
    bm = act[:, W:W + G * M]
    cm = act[:, W + G * M:W + 2 * G * M]

    dt = _softplus(dt_raw + dtb_ref[...])
    a_neg = jnp.where(_iota((1, LANES), 1) < SSD_HEADS, -jnp.exp(alog_ref[...]), 0.0)
    da = dt * a_neg
    r_i, c_i = _iota((L, L), 0), _iota((L, L), 1)
    causal = r_i >= c_i
    cs = _dot_sel_lhs(causal.astype(bf16), da, 3)
    expand = (_iota((LANES, W), 0) == _iota((LANES, W), 1) // HEAD_DIM).astype(bf16)
    dtx = _dot_sel_rhs(dt, expand, 3)
    csx = _dot_sel_rhs(cs, expand, 3)
    eye_h = (_iota((8, LANES), 0) == _iota((8, LANES), 1)).astype(bf16)
    cs_t = None
    for p in _split(cs, 3):
        t = lax.dot_general(eye_h, p, (((1,), (1,)), ((), ())), preferred_element_type=f32)
        cs_t = t if cs_t is None else cs_t + t
    xdt = xs * dtx

    st = st_ref[...]
    y_parts = []
    for gi in range(G):
        bm_g = bm[:, gi * M:(gi + 1) * M]
        cm_g = cm[:, gi * M:(gi + 1) * M]
        cb = _dot_nt(cm_g, bm_g)
        for e in range(SSD_HEADS // G):
            h = gi * (SSD_HEADS // G) + e
            seg = jnp.where(causal, jnp.exp(jnp.broadcast_to(cs[:, h:h + 1], (L, L)) - cs_t[h:h + 1, :]), 0.0)
            y_parts.append(_dot(cb * seg, xdt[:, h * HEAD_DIM:(h + 1) * HEAD_DIM]))
    y = jnp.concatenate(y_parts, axis=1)
    y_inter = jnp.concatenate(
        [_dot(cm[:, gi * M:(gi + 1) * M], st[:, gi * gw:(gi + 1) * gw]) for gi in range(G)], axis=1)
    y = y + y_inter * jnp.exp(csx)

    cs_last = csx[L - 1:L, :]
    xdtw = xdt * jnp.exp(cs_last - csx)
    d_state = jnp.concatenate(
        [_dot_tn(bm[:, gi * M:(gi + 1) * M], xdtw[:, gi * gw:(gi + 1) * gw]) for gi in range(G)], axis=1)
    st_ref[...] = st * jnp.exp(cs_last) + d_state

    y = (y + dsk_ref[...] * xs) * _silu(z)
    outs = []
    for gi in range(G):
        yg = y[:, gi * gw:(gi + 1) * gw]
        ms = jnp.mean(yg * yg, axis=-1, keepdims=True)
        outs.append(yg * lax.rsqrt(ms + LN_EPS))
    o_ref[0] = jnp.concatenate(outs, axis=1) * ng_ref[...]


def _ssd(ps, cw, cb, dtb, alog, dsk, ng):
    nb, t, cols = ps.shape
    L = SSD_BLOCK
    full2 = lambda a: pl.BlockSpec(a.shape, lambda b, i: (0, 0))
    return pl.pallas_call(
        _ssd_kernel,
        grid=(nb, t // L),
        in_specs=[
            pl.BlockSpec((1, L, cols), lambda b, i: (b, i, 0)),
            pl.BlockSpec((1, 8, cols), lambda b, i: (b, jnp.maximum(i * (L // 8) - 1, 0), 0)),
            full2(cw), full2(cb), full2(dtb), full2(alog), full2(dsk), full2(ng),
        ],
        out_specs=pl.BlockSpec((1, L, SSD_WIDTH), lambda b, i: (b, i, 0)),
        out_shape=jax.ShapeDtypeStruct((nb, t, SSD_WIDTH), f32),
        scratch_shapes=[pltpu.VMEM((SSD_STATE, SSD_WIDTH), f32)],
        compiler_params=_cparams(("arbitrary", "arbitrary")),
        name="ssd_mixer",
    )(ps, ps, cw, cb, dtb, alog, dsk, ng)


def _rwkv_kernel(p_ref, prev_ref, mu_ref, w0_ref, w2_ref, a0_ref, a2_ref, g2_ref, kk_ref, ka_ref, rk_ref,
                 lng_ref, lnb_ref, o_ref, zt_ref):
    nb, C, _ = p_ref.shape
    W, H = RWKV_WIDTH, RWKV_HEADS
    i = pl.program_id(0)

    @pl.when(i == 0)
    def _():
        zt_ref[...] = jnp.zeros_like(zt_ref)

    head_blk = _block_ones(W, HEAD_DIM)
    bones = head_blk.astype(bf16)
    r_i, c_i = _iota((C, C), 0), _iota((C, C), 1)
    tri = (r_i >= c_i).astype(bf16)
    n = H * C
    rr, cc = _iota((n, n), 0), _iota((n, n), 1)
    same = rr // C == cc // C
    strict = same & (rr > cc)
    lower = same & (rr >= cc)
    eye = jnp.where(rr == cc, 1.0, 0.0)

    def stack(t):
        return jnp.where(head_blk, jnp.concatenate([t] * H, axis=0), 0.0)

    def prepare(b):
        cols = p_ref[b]
        prev_row = jnp.where(i > 0, prev_ref[b, 7:8, :], 0.0)
        shifted = jnp.concatenate([prev_row, cols[:C - 1]], axis=0)
        cols = cols + (shifted - cols) * mu_ref[...]
        r = cols[:, 0:W]
        k = cols[:, W:2 * W]
        v = cols[:, 2 * W:3 * W]
        w_lr = cols[:, 3 * W:3 * W + 64]
        a_lr = cols[:, 3 * W + 64:3 * W + 128]
        g_lr = cols[:, 3 * W + 128:3 * W + 256]

        wv = -_softplus(-(w0_ref[...] + _dot_hi(jnp.tanh(w_lr), w2_ref[...]))) - 0.5
        lw = -jnp.exp(wv)
        a = jax.nn.sigmoid(a0_ref[...] + _dot_hi(a_lr, a2_ref[...]))
        g = _dot_hi(jax.nn.sigmoid(g_lr), g2_ref[...])

        kk = k * kk_ref[...]
        kk_norm = jnp.sqrt(_dot_sel_rhs(kk * kk, bones, 3))
        kk = kk / jnp.maximum(kk_norm, 1e-12)
        k2 = k * (1.0 + (a - 1.0) * ka_ref[...])
        bvec = kk * a

        cum = _dot_sel_lhs(tri, lw, 3)
        cum_last = cum[C - 1:C, :]
        e_neg = jnp.exp(-cum)
        e_rem = jnp.exp(cum_last - cum)
        bonus = _dot_sel_rhs(r * k2 * rk_ref[...], bones, 3)
        return dict(
            aw=stack(-kk * jnp.exp(cum - lw)), rw=stack(r * jnp.exp(cum)), bn=stack(bvec * e_neg),
            kn=stack(k2 * e_neg), bt=stack(bvec * e_rem), kt=stack(k2 * e_rem), vs=stack(v),
            wc=jnp.exp(cum_last), tail=(bonus * v, g))

    rows = range(nb)
    q = [prepare(b) for b in rows]
    a_ab = [jnp.where(strict, _dot_nt(q[b]["aw"], q[b]["bn"]), 0.0) for b in rows]
    a_ak = [jnp.where(strict, _dot_nt(q[b]["aw"], q[b]["kn"]), 0.0) for b in rows]
    a_rb = [jnp.where(lower, _dot_nt(q[b]["rw"], q[b]["bn"]), 0.0) for b in rows]
    a_rk = [jnp.where(lower, _dot_nt(q[b]["rw"], q[b]["kn"]), 0.0) for b in rows]

    pw = a_ab
    tinv = [eye + a_ab[b] for b in rows]
    for _ in range(int(math.log2(C)) - 1):
        pw = [_dot(pw[b], pw[b]) for b in rows]
        tinv = [tinv[b] + _dot(tinv[b], pw[b]) for b in rows]

    zt = [zt_ref[b] for b in rows]
    rhs = [_dot_nt(q[b]["aw"], zt[b]) + _dot(a_ak[b], q[b]["vs"]) for b in rows]
    y0 = [_dot_nt(q[b]["rw"], zt[b]) + _dot(a_rk[b], q[b]["vs"]) for b in rows]
    u = [_dot(tinv[b], rhs[b]) for b in rows]
    y_st = [y0[b] + _dot(a_rb[b], u[b]) for b in rows]
    for b in rows:
        zt_ref[b] = zt[b] * q[b]["wc"] + _dot_tn(u[b], q[b]["bt"]) + _dot_tn(q[b]["vs"], q[b]["kt"])

    inv_n = 1.0 / HEAD_DIM
    for b in rows:
        y = y_st[b][0:C]
        for h in range(1, H):
            y = y + y_st[b][h * C:(h + 1) * C]
        mu_h = _dot_sel_rhs(y, bones, 3) * inv_n
        yc = y - mu_h
        var_h = _dot_sel_rhs(yc * yc, bones, 3) * inv_n
        y = yc * lax.rsqrt(var_h + RWKV_GN_EPS) * lng_ref[...] + lnb_ref[...]
        bonus_v, g = q[b]["tail"]
        o_ref[b] = (y + bonus_v) * g


def _rwkv(pr, mu, w0, w2, a0, a2, g2, k_k, k_a, r_k, ln_g, ln_b):
    nb, t, cols = pr.shape
    C = RWKV_CHUNK
    full2 = lambda a: pl.BlockSpec(a.shape, lambda i: (0, 0))
    params = (mu, w0, w2, a0, a2, g2, k_k, k_a, r_k, ln_g, ln_b)
    return pl.pallas_call(
        _rwkv_kernel,
        grid=(t // C,),
        in_specs=[
            pl.BlockSpec((nb, C, cols), lambda i: (0, i, 0)),
            pl.BlockSpec((nb, 8, cols), lambda i: (0, jnp.maximum(i * (C // 8) - 1, 0), 0)),
        ] + [full2(p) for p in params],
        out_specs=pl.BlockSpec((nb, C, RWKV_WIDTH), lambda i: (0, i, 0)),
        out_shape=jax.ShapeDtypeStruct((nb, t, RWKV_WIDTH), f32),
        scratch_shapes=[pltpu.VMEM((nb, RWKV_WIDTH, RWKV_WIDTH), f32)],
        compiler_params=_cparams(("arbitrary",)),
        name="rwkv7_mixer",
    )(pr, pr, *params)


def _outproj_kernel(alpha, x_ref, og_ref, os_ref, or_ref, wo_ref, g_ref, b_ref, rwt_ref, rb_ref,
                    x1_ref, gate_ref, best_ref, cnt_ref):
    tm = x_ref.shape[0]
    d = functools.partial(jnp.dot, preferred_element_type=f32)
    mix = (d(_bf(og_ref[...]), wo_ref[0:GLA_WIDTH, :])
           + d(_bf(os_ref[...]), wo_ref[GLA_WIDTH:GLA_WIDTH + SSD_WIDTH, :])
           + d(_bf(or_ref[...]), wo_ref[GLA_WIDTH + SSD_WIDTH:, :]))
    x1 = _layer_norm(alpha * x_ref[...] + mix, g_ref[...], b_ref[...], LN_EPS)
    x1_ref[...] = x1

    xh, xl = _split(x1, 2)
    wh, wl = _split(rwt_ref[...], 2)
    nt = lambda a, b: lax.dot_general(a, b, (((1,), (1,)), ((), ())), preferred_element_type=f32)
    logits = nt(wh, xh) + nt(wl, xh) + nt(wh, xl)
    scores = jax.nn.sigmoid(logits)
    sel = scores + rb_ref[...]
    srow = [sel[e:e + 1, :] for e in range(N_EXPERTS)]
    crow = [scores[e:e + 1, :] for e in range(N_EXPERTS)]

    def top2_sum(v0, v1, v2, v3):
        m01, n01 = jnp.maximum(v0, v1), jnp.minimum(v0, v1)
        m23, n23 = jnp.maximum(v2, v3), jnp.minimum(v2, v3)
        return jnp.maximum(m01, m23) + jnp.maximum(jnp.minimum(m01, m23), jnp.maximum(n01, n23))

    gscore = [top2_sum(*srow[4 * gi:4 * gi + 4]) for gi in range(N_EXPERT_GROUPS)]
    best = jnp.zeros((1, tm), jnp.int32)
    best_v = gscore[0]
    for gi in range(1, N_EXPERT_GROUPS):
        better = gscore[gi] > best_v
        best = jnp.where(better, gi, best)
        best_v = jnp.where(better, gscore[gi], best_v)
    sv, cv = [], []
    for j in range(EXPERTS_PER_GROUP):
        s_j, c_j = srow[j], crow[j]
        for gi in range(1, N_EXPERT_GROUPS):
            s_j = jnp.where(best == gi, srow[4 * gi + j], s_j)
            c_j = jnp.where(best == gi, crow[4 * gi + j], c_j)
        sv.append(s_j)
        cv.append(c_j)
    j1 = jnp.zeros((1, tm), jnp.int32)
    v1 = sv[0]
    for j in range(1, EXPERTS_PER_GROUP):
        better = sv[j] > v1
        j1 = jnp.where(better, j, j1)
        v1 = jnp.where(better, sv[j], v1)
    j2 = jnp.full((1, tm), -1, jnp.int32)
    v2 = jnp.full((1, tm), -jnp.inf, f32)
    for j in range(EXPERTS_PER_GROUP):
        better = (j1 != j) & ((j2 < 0) | (sv[j] > v2))
        j2 = jnp.where(better, j, j2)
        v2 = jnp.where(better, sv[j], v2)
    w1 = cv[0]
    w2 = cv[0]
    for j in range(1, EXPERTS_PER_GROUP):
        w1 = jnp.where(j1 == j, cv[j], w1)
        w2 = jnp.where(j2 == j, cv[j], w2)
    wsum = w1 + w2
    g1, g2 = w1 / wsum, w2 / wsum
    rows = []
    for e in range(N_EXPERTS):
        gi, j = divmod(e, EXPERTS_PER_GROUP)
        in_g = best == gi
        rows.append(jnp.where(in_g & (j1 == j), g1, jnp.where(in_g & (j2 == j), g2, 0.0)))
    gate_t = jnp.concatenate(rows, axis=0)
    eye = (_iota((N_EXPERTS, LANES), 0) == _iota((N_EXPERTS, LANES), 1)).astype(bf16)
    gate = None
    for p in _split(gate_t, 3):
        t = lax.dot_general(p, eye, (((0,), (0,)), ((), ())), preferred_element_type=f32)
        gate = t if gate is None else gate + t
    gate_ref[...] = gate
    best_ref[...] = best
    lane = _iota((1, LANES), 1)
    cnt = jnp.zeros((1, LANES), jnp.int32)
    for gi in range(N_EXPERT_GROUPS):
        n_g = jnp.sum((best == gi).astype(jnp.int32), axis=-1, keepdims=True)
        cnt = jnp.where(lane == gi, n_g, cnt)
    cnt_ref[0] = cnt


def _out_proj(alpha, x2, og, os_, or_, wo, g, b, rwt, rb):
    n, d = x2.shape
    tm = min(OUT_PROJ_ROWS, n)
    full2 = lambda a: pl.BlockSpec(a.shape, lambda i: (0, 0))
    rows = lambda c: pl.BlockSpec((tm, c), lambda i: (i, 0))
    return pl.pallas_call(
        functools.partial(_outproj_kernel, alpha),
        grid=(n // tm,),
        in_specs=[rows(d), rows(og.shape[1]), rows(os_.shape[1]), rows(or_.shape[1]),
                  full2(wo), full2(g), full2(b), full2(rwt), full2(rb)],
        out_specs=[rows(d), rows(LANES), pl.BlockSpec((1, tm), lambda i: (0, i)),
                   pl.BlockSpec((1, 1, LANES), lambda i: (i, 0, 0))],
        out_shape=[jax.ShapeDtypeStruct((n, d), f32), jax.ShapeDtypeStruct((n, LANES), f32),
                   jax.ShapeDtypeStruct((1, n), jnp.int32), jax.ShapeDtypeStruct((n // tm, 1, LANES), jnp.int32)],
        compiler_params=_cparams(("arbitrary",)),
        name="out_proj_router",
    )(x2, og, os_, or_, wo, g, b, rwt, rb)


def _seg_copy(src, dst, sem, src_off, dst_off, rows, max_rows, wait):
    k = SEG_ALIGN
    sizes = []
    while k <= max_rows:
        sizes.append(k)
        k *= 2
    for k in reversed(sizes):
        shift = int(math.log2(k)) + 1

        @pl.when((rows & k) != 0)
        def _():
            done = (rows >> shift) << shift
            cp = pltpu.make_async_copy(
                src.at[pl.ds(pl.multiple_of(src_off + done, SEG_ALIGN), k)],
                dst.at[pl.ds(pl.multiple_of(dst_off + done, SEG_ALIGN), k)], sem)
            if wait:
                cp.wait()
            else:
                cp.start()


def _padded(count):
    return (count + (SEG_ALIGN - 1)) & (-SEG_ALIGN)


def _one_hot_rows(pos, n_rows):
    return jnp.where(_iota((n_rows, pos.shape[1]), 0) == pos, 1.0, 0.0).astype(bf16)


def _dispatch_kernel(cnt_sm, goff_sm, x_ref, gate_ref, best_ref, xs_in, gs_in, xs_out, gs_out, pos_ref,
                     xloc, gloc, sems):
    del xs_in, gs_in
    i = pl.program_id(0)
    tm = x_ref.shape[0]
    n_loc = xloc.shape[1]
    slot = i % 2

    def copies(blk, slot_, wait):
        loff = 0
        for gi in range(N_EXPERT_GROUPS):
            rows = _padded(cnt_sm[blk * N_EXPERT_GROUPS + gi])
            dst = goff_sm[blk * N_EXPERT_GROUPS + gi]
            _seg_copy(xloc.at[slot_], xs_out, sems.at[0, slot_], loff, dst, rows, tm, wait)
            _seg_copy(gloc.at[slot_], gs_out, sems.at[1, slot_], loff, dst, rows, tm, wait)
            loff = loff + rows

    best = best_ref[...]
    onehot = (_iota((8, tm), 0) == best).astype(bf16)
    earlier = (_iota((tm, tm), 0) < _iota((tm, tm), 1)).astype(bf16)
    rank = jnp.dot(onehot, earlier, preferred_element_type=f32)
    pos = jnp.zeros((1, tm), f32)
    loff = 0
    for gi in range(N_EXPERT_GROUPS):
        pos = jnp.where(best == gi, rank[gi:gi + 1, :] + loff.astype(f32) if gi else rank[gi:gi + 1, :], pos)
        loff = loff + _padded(cnt_sm[i * N_EXPERT_GROUPS + gi])
    pos = pos.astype(jnp.int32)
    pos_ref[...] = pos
    perm = _one_hot_rows(pos, n_loc)
    xloc[slot] = jnp.dot(perm, _bf(x_ref[...]), preferred_element_type=f32)
    gloc[slot] = _dot_sel_lhs(perm, gate_ref[...], 3)
    copies(i, slot, False)

    @pl.when(i > 0)
    def _():
        copies(i - 1, 1 - slot, True)

    @pl.when(i == pl.num_programs(0) - 1)
    def _():
        copies(i, slot, True)


def _experts_kernel(tg_sm, tv_sm, x_ref, gate_ref, wg_ref, wu_ref, wd_ref, y_ref):
    t = pl.program_id(0)

    @pl.when(tv_sm[t] == 0)
    def _():
        y_ref[...] = jnp.zeros_like(y_ref)

    @pl.when(tv_sm[t] > 0)
    def _():
        first = tg_sm[t] * EXPERTS_PER_GROUP
        xb = _bf(x_ref[...])
        gate = gate_ref[...]
        lane = _iota(gate.shape, 1)
        hs = []
        for j in range(EXPERTS_PER_GROUP):
            hg = jnp.dot(xb, wg_ref[j], preferred_element_type=f32)
            hu = jnp.dot(xb, wu_ref[j], preferred_element_type=f32)
            gcol = jnp.sum(jnp.where(lane == first + j, gate, 0.0), axis=-1, keepdims=True)
            hs.append(_bf(_silu(hg) * hu * gcol))
        ff = wd_ref.shape[1]
        y_ref[...] = jnp.dot(jnp.concatenate(hs, axis=1), wd_ref[...].reshape(EXPERTS_PER_GROUP * ff, -1),
                             preferred_element_type=f32)


def _combine_kernel(alpha, cnt_sm, goff_sm, x_ref, pos_ref, g_ref, b_ref, ys_ref, o_ref, yloc, sems):
    i = pl.program_id(0)
    tm = x_ref.shape[0]
    n_loc = yloc.shape[1]
    slot = i % 2

    def copies(blk, slot_, wait):
        loff = 0
        for gi in range(N_EXPERT_GROUPS):
            rows = _padded(cnt_sm[blk * N_EXPERT_GROUPS + gi])
            src = goff_sm[blk * N_EXPERT_GROUPS + gi]
            _seg_copy(ys_ref, yloc.at[slot_], sems.at[slot_], src, loff, rows, tm, wait)
            loff = loff + rows

    @pl.when(i == 0)
    def _():
        yloc[...] = jnp.zeros_like(yloc)
        copies(0, 0, False)

    @pl.when(i + 1 < pl.num_programs(0))
    def _():
        copies(i + 1, 1 - slot, False)

    copies(i, slot, True)
    perm = _one_hot_rows(pos_ref[...], n_loc)
    y = None
    for p in _split(yloc[slot], 2):
        t = lax.dot_general(perm, p, (((0,), (0,)), ((), ())), preferred_element_type=f32)
        y = t if y is None else y + t
    o_ref[...] = _layer_norm(alpha * x_ref[...] + y, g_ref[...], b_ref[...], LN_EPS)


def _moe_tables(cnt, n_tiles, tile):
    pc = (cnt + (SEG_ALIGN - 1)) // SEG_ALIGN * SEG_ALIGN
    gtot = jnp.sum(pc, axis=0)
    gcap = (gtot + tile - 1) // tile * tile
    gend = jnp.cumsum(gcap)
    gstart = gend - gcap
    goff = gstart[None, :] + jnp.cumsum(pc, axis=0) - pc
    tstart = jnp.arange(n_tiles, dtype=jnp.int32) * tile
    tg = jnp.minimum(jnp.sum((tstart[:, None] >= gend[None, :]).astype(jnp.int32), axis=1), N_EXPERT_GROUPS - 1)
    tv = jnp.clip(gtot[tg] - (tstart - gstart[tg]), 0, tile)
    return goff.reshape(-1).astype(jnp.int32), tg.astype(jnp.int32), tv.astype(jnp.int32)


def _moe(alpha, x1, gate, best, cnt, wg, wu, wd, g, b):
    n, d = x1.shape
    ne, _, ff = wg.shape
    nblk = cnt.shape[0]
    tm = n // nblk
    tile = MOE_TILE
    n_loc = tm + LANES
    n_sorted = -(-(n + nblk * N_EXPERT_GROUPS * SEG_ALIGN + N_EXPERT_GROUPS * tile) // tile) * tile
    n_tiles = n_sorted // tile
    cnt_flat = cnt[:, 0, :N_EXPERT_GROUPS].reshape(-1)
    goff, tg, tv = _moe_tables(cnt[:, 0, :N_EXPERT_GROUPS], n_tiles, tile)
    any_spec = pl.BlockSpec(memory_space=pl.ANY)

    xs, gs, pos = pl.pallas_call(
        _dispatch_kernel,
        grid_spec=pltpu.PrefetchScalarGridSpec(
            num_scalar_prefetch=2,
            grid=(nblk,),
            in_specs=[
                pl.BlockSpec((tm, d), lambda i, *_: (i, 0)),
                pl.BlockSpec((tm, LANES), lambda i, *_: (i, 0)),
                pl.BlockSpec((1, tm), lambda i, *_: (0, i)),
                any_spec, any_spec,
            ],
            out_specs=[any_spec, any_spec, pl.BlockSpec((1, tm), lambda i, *_: (0, i))],
            scratch_shapes=[pltpu.VMEM((2, n_loc, d), f32), pltpu.VMEM((2, n_loc, LANES), f32),
                            pltpu.SemaphoreType.DMA((2, 2))],
        ),
        out_shape=[jax.ShapeDtypeStruct((n_sorted, d), f32), jax.ShapeDtypeStruct((n_sorted, LANES), f32),
                   jax.ShapeDtypeStruct((1, n), jnp.int32)],
        input_output_aliases={5: 0, 6: 1},
        compiler_params=_cparams(("arbitrary",)),
        name="moe_dispatch",
    )(cnt_flat, goff, x1, gate, best, jnp.zeros((n_sorted, d), f32), jnp.zeros((n_sorted, LANES), f32))

    ys = pl.pallas_call(
        _experts_kernel,
        grid_spec=pltpu.PrefetchScalarGridSpec(
            num_scalar_prefetch=2,
            grid=(n_tiles,),
            in_specs=[
                pl.BlockSpec((tile, d), lambda t, *_: (t, 0)),
                pl.BlockSpec((tile, LANES), lambda t, *_: (t, 0)),
                pl.BlockSpec((EXPERTS_PER_GROUP, d, ff), lambda t, tg_, tv_: (tg_[t], 0, 0)),
                pl.BlockSpec((EXPERTS_PER_GROUP, d, ff), lambda t, tg_, tv_: (tg_[t], 0, 0)),
                pl.BlockSpec((EXPERTS_PER_GROUP, ff, d), lambda t, tg_, tv_: (tg_[t], 0, 0)),
            ],
            out_specs=pl.BlockSpec((tile, d), lambda t, *_: (t, 0)),
        ),
        out_shape=jax.ShapeDtypeStruct((n_sorted, d), f32),
        compiler_params=_cparams(("arbitrary",)),
        name="moe_experts",
    )(tg, tv, xs, gs, wg, wu, wd)

    return pl.pallas_call(
        functools.partial(_combine_kernel, alpha),
        grid_spec=pltpu.PrefetchScalarGridSpec(
            num_scalar_prefetch=2,
            grid=(nblk,),
            in_specs=[
                pl.BlockSpec((tm, d), lambda i, *_: (i, 0)),
                pl.BlockSpec((1, tm), lambda i, *_: (0, i)),
                pl.BlockSpec(g.shape, lambda i, *_: (0, 0)),
                pl.BlockSpec(b.shape, lambda i, *_: (0, 0)),
                any_spec,
            ],
            out_specs=pl.BlockSpec((tm, d), lambda i, *_: (i, 0)),
            scratch_shapes=[pltpu.VMEM((2, n_loc, d), f32), pltpu.SemaphoreType.DMA((2,))],
        ),
        out_shape=jax.ShapeDtypeStruct((n, d), f32),
        compiler_params=_cparams(("arbitrary",)),
        name="moe_combine",
    )(cnt_flat, goff, x1, pos, g, b, ys)


def _pad_cols(w, n):
    return jnp.pad(w, ((0, 0), (0, n - w.shape[1])))


def _row(v, n=None):
    v = v.reshape(1, -1)
    return v if n is None else _pad_cols(v, n)


def kernel(x, w_in, gla_w_alpha, gla_b_alpha, gla_norm_g, ssd_conv_w, ssd_conv_b, ssd_dt_bias, ssd_A_log, ssd_D, ssd_norm_g, rwkv_mu, rwkv_w0, rwkv_w2, rwkv_a0, rwkv_a2, rwkv_g2, rwkv_k_k, rwkv_k_a, rwkv_r_k, rwkv_ln_g, rwkv_ln_b, w_out, ln1_g, ln1_b, router_w, router_b, exp_w_gate, exp_w_up, exp_w_down, ln2_g, ln2_b):
    nb, t, d = x.shape
    depth = w_in.shape[0]
    alpha = float((2 * depth) ** 0.25)
    gla_cols = 4 * GLA_WIDTH + GLA_GATE_RANK
    ssd_cols = SSD_WIDTH + SSD_XBC + SSD_HEADS
    rwt = router_w.T
    rb = router_b.reshape(-1, 1)
    x2 = x.reshape(nb * t, d)
    for l in range(depth):
        wl = w_in[l]
        wg = _bf(_pad_cols(wl[:, :gla_cols], GLA_COLS_PAD))
        ws = _bf(_pad_cols(wl[:, gla_cols:gla_cols + ssd_cols], SSD_COLS_PAD))
        wr = _bf(wl[:, gla_cols + ssd_cols:])
        pg, ps, pr = _in_proj(x2, wg, ws, wr)
        og = _gla(pg.reshape(nb, t, -1),
                  jnp.pad(gla_w_alpha[l], ((0, LANES - GLA_GATE_RANK), (0, 0))),
                  _row(gla_b_alpha[l]), _row(jnp.tile(gla_norm_g[l], GLA_WIDTH // HEAD_DIM)))
        os_ = _ssd(ps.reshape(nb, t, -1), ssd_conv_w[l], _row(ssd_conv_b[l]), _row(ssd_dt_bias[l], LANES),
                   _row(ssd_A_log[l], LANES), _row(jnp.repeat(ssd_D[l], HEAD_DIM)), _row(ssd_norm_g[l]))
        or_ = _rwkv(pr.reshape(nb, t, -1), _row(rwkv_mu[l]), _row(rwkv_w0[l]), rwkv_w2[l], _row(rwkv_a0[l]),
                    rwkv_a2[l], rwkv_g2[l], _row(rwkv_k_k[l]), _row(rwkv_k_a[l]), _row(rwkv_r_k[l]),
                    _row(rwkv_ln_g[l]), _row(rwkv_ln_b[l]))
        x1, gate, best, cnt = _out_proj(alpha, x2, og.reshape(nb * t, -1), os_.reshape(nb * t, -1),
                                        or_.reshape(nb * t, -1), _bf(w_out[l]), _row(ln1_g[l]), _row(ln1_b[l]),
                                        rwt, rb)
        x2 = _moe(alpha, x1, gate, best, cnt, _bf(exp_w_gate[l]), _bf(exp_w_up[l]), _bf(exp_w_down[l]),
                  _row(ln2_g[l]), _row(ln2_b[l]))
    return x2.reshape(nb, t, d)
```

```python
import functools
import math

import jax
import jax.numpy as jnp
from jax import lax
from jax.experimental import pallas as pl
from jax.experimental.pallas import tpu as pltpu

f32 = jnp.float32
bf16 = jnp.bfloat16

LANES = 128
HEAD_DIM = 64
GLA_WIDTH = 256
GLA_GATE_RANK = 16
GLA_GATE_TAU = 16.0
GLA_CHUNK = 16
GLA_BLOCK = 128
GLA_COLS_PAD = 4 * GLA_WIDTH + LANES
SSD_WIDTH = 512
SSD_HEADS = 8
SSD_GROUPS = 2
SSD_STATE = 64
SSD_CONV = 4
SSD_XBC = SSD_WIDTH + 2 * SSD_GROUPS * SSD_STATE
SSD_BLOCK = 128
SSD_COLS_PAD = SSD_WIDTH + SSD_XBC + LANES
RWKV_WIDTH = 256
RWKV_HEADS = 4
RWKV_CHUNK = 64
RWKV_GN_EPS = 64e-5
RWKV_COLS = 3 * RWKV_WIDTH + 64 + 64 + 128
N_EXPERTS = 16
N_EXPERT_GROUPS = 4
EXPERTS_PER_GROUP = 4
LN_EPS = 1e-5

IN_PROJ_ROWS = 512
OUT_PROJ_ROWS = 1024
MOE_SORT_ROWS = 512
MOE_TILE = 512
SEG_ALIGN = 8
VMEM_LIMIT = 56 * 1024 * 1024


def _bf(x):
    return x.astype(bf16)


def _dot(a, b):
    return jnp.dot(_bf(a), _bf(b), preferred_element_type=f32)


def _dot_nt(a, b):
    return lax.dot_general(_bf(a), _bf(b), (((1,), (1,)), ((), ())), preferred_element_type=f32)


def _dot_tn(a, b):
    return lax.dot_general(_bf(a), _bf(b), (((0,), (0,)), ((), ())), preferred_element_type=f32)


def _split(x, n):
    parts, r = [], x
    for _ in range(n):
        p = r.astype(bf16)
        parts.append(p)
        r = r - p.astype(f32)
    return parts


def _dot_sel_rhs(a, sel, n=3):
    out = None
    for p in _split(a, n):
        t = jnp.dot(p, sel, preferred_element_type=f32)
        out = t if out is None else out + t
    return out


def _dot_sel_lhs(sel, b, n=3):
    out = None
    for p in _split(b, n):
        t = jnp.dot(sel, p, preferred_element_type=f32)
        out = t if out is None else out + t
    return out


def _dot_hi(a, b):
    ah, al = _split(a, 2)
    bh, bl = _split(b, 2)
    d = functools.partial(jnp.dot, preferred_element_type=f32)
    return d(ah, bh) + d(al, bh) + d(ah, bl)


def _iota(shape, dim):
    return lax.broadcasted_iota(jnp.int32, shape, dim)


def _block_ones(n, blk):
    return (_iota((n, n), 0) // blk == _iota((n, n), 1) // blk)


def _silu(x):
    return x * jax.nn.sigmoid(x)


def _softplus(x):
    return jnp.maximum(x, 0.0) + jnp.log1p(jnp.exp(-jnp.abs(x)))


def _layer_norm(y, g, b, eps):
    mu = jnp.mean(y, axis=-1, keepdims=True)
    yc = y - mu
    var = jnp.mean(yc * yc, axis=-1, keepdims=True)
    return yc * lax.rsqrt(var + eps) * g + b


def _cparams(sem):
    return pltpu.CompilerParams(dimension_semantics=sem, vmem_limit_bytes=VMEM_LIMIT)


def _inproj_kernel(x_ref, wg_ref, ws_ref, wr_ref, og_ref, os_ref, or_ref):
    xb = _bf(x_ref[...])
    og_ref[...] = jnp.dot(xb, wg_ref[...], preferred_element_type=f32)
    os_ref[...] = jnp.dot(xb, ws_ref[...], preferred_element_type=f32)
    or_ref[...] = jnp.dot(xb, wr_ref[...], preferred_element_type=f32)


def _regroup_kernel(splits, w_ref, *o_refs):
    w = w_ref[0]
    start = 0
    for width, o_ref in zip(splits, o_refs):
        part = w[:, start:start + width]
        pad = o_ref.shape[1] - width
        if pad:
            part = jnp.concatenate([part, jnp.zeros((part.shape[0], pad), part.dtype)], axis=1)
        o_ref[...] = _bf(part)
        start += width


def _regroup_w_in(w_in, layer, splits, padded):
    _, d, n_in = w_in.shape
    rows = 256
    return pl.pallas_call(
        functools.partial(_regroup_kernel, splits),
        grid=(d // rows,),
        in_specs=[pl.BlockSpec((1, rows, n_in), lambda i: (layer, i, 0))],
        out_specs=[pl.BlockSpec((rows, p), lambda i: (i, 0)) for p in padded],
        out_shape=[jax.ShapeDtypeStruct((d, p), bf16) for p in padded],
        compiler_params=_cparams(("arbitrary",)),
        name="regroup_w_in",
    )(w_in)


def _in_proj(x2, wg, ws, wr):
    n, d = x2.shape
    tm = min(IN_PROJ_ROWS, n)
    full = lambda w: pl.BlockSpec(w.shape, lambda i: (0, 0))
    rows = lambda c: pl.BlockSpec((tm, c), lambda i: (i, 0))
    return pl.pallas_call(
        _inproj_kernel,
        grid=(n // tm,),
        in_specs=[rows(d), full(wg), full(ws), full(wr)],
        out_specs=[rows(wg.shape[1]), rows(ws.shape[1]), rows(wr.shape[1])],
        out_shape=[jax.ShapeDtypeStruct((n, w.shape[1]), f32) for w in (wg, ws, wr)],
        compiler_params=_cparams(("arbitrary",)),
        name="in_proj",
    )(x2, wg, ws, wr)


def _gla_kernel(p_ref, wa_ref, ba_ref, ng_ref, o_ref, st_ref, t_ref, d_ref, sx_ref):
    nb, tb, _ = p_ref.shape
    w, c = GLA_WIDTH, GLA_CHUNK
    nc = tb // c
    npt = w // LANES
    Q, K, V, CUM, OUT = range(5)

    @pl.when(pl.program_id(0) == 0)
    def _():
        st_ref[...] = jnp.zeros_like(st_ref)

    bones = _block_ones(w, HEAD_DIM).astype(bf16)
    pair_blk = _block_ones(LANES, HEAD_DIM)
    r_i, c_i = _iota((tb, tb), 0), _iota((tb, tb), 1)
    tri = ((r_i >= c_i) & (r_i // c == c_i // c)).astype(bf16)
    wa_h, wa_l = _split(wa_ref[...], 2)
    dd = functools.partial(jnp.dot, preferred_element_type=f32)

    for b in range(nb):
        a_h, a_l = _split(p_ref[b, :, 4 * w:4 * w + LANES], 2)
        z = dd(a_h, wa_h) + dd(a_l, wa_h) + dd(a_h, wa_l) + ba_ref[...]
        log_a = -_softplus(-z) * (1.0 / GLA_GATE_TAU)
        cum = _dot_sel_lhs(tri, log_a, 3)
        for lt in range(npt):
            lanes = slice(lt * LANES, (lt + 1) * LANES)
            t_ref[b, Q, lt] = p_ref[b, :, lanes] * (HEAD_DIM ** -0.5)
            t_ref[b, K, lt] = p_ref[b, :, w + lt * LANES:w + (lt + 1) * LANES]
            t_ref[b, V, lt] = p_ref[b, :, 2 * w + lt * LANES:2 * w + (lt + 1) * LANES]
            t_ref[b, CUM, lt] = cum[:, lanes]

    def ld(b, j, off):
        return jnp.concatenate([t_ref[b, j, lt, pl.ds(off, nc, stride=c), :] for lt in range(npt)], axis=1)

    for b in range(nb):
        qt = [ld(b, Q, o) for o in range(c)]
        kt = [ld(b, K, o) for o in range(c)]
        ct = [ld(b, CUM, o) for o in range(c)]
        pair = 0
        for t in range(c):
            for s in range(t + 1):
                qk = qt[t] * kt[s]
                d_ref[b, pair * nc:(pair + 1) * nc, :] = qk if s == t else qk * jnp.exp(ct[t] - ct[s])
                pair += 1
    for b in range(nb):
        sx_ref[b] = jnp.dot(_bf(d_ref[b]), bones, preferred_element_type=f32)
    for b in range(nb):
        vt = [ld(b, V, o) for o in range(c)]
        pair = 0
        for t in range(c):
            acc = None
            for s in range(t + 1):
                term = sx_ref[b, pair * nc:(pair + 1) * nc, :] * vt[s]
                acc = term if acc is None else acc + term
                pair += 1
            for lt in range(npt):
                t_ref[b, OUT, lt, pl.ds(t, nc, stride=c), :] = acc[:, lt * LANES:(lt + 1) * LANES]

    for ci in range(nc):
        sl = slice(ci * c, (ci + 1) * c)
        last = slice((ci + 1) * c - 1, (ci + 1) * c)
        for b in range(nb):
            for lt in range(npt):
                cum_c = t_ref[b, CUM, lt, sl, :]
                cum_last = t_ref[b, CUM, lt, last, :]
                st = st_ref[b, lt]
                t_ref[b, OUT, lt, sl, :] += _dot_nt(t_ref[b, Q, lt, sl, :] * jnp.exp(cum_c), st)
                upd = _dot_tn(t_ref[b, V, lt, sl, :], t_ref[b, K, lt, sl, :] * jnp.exp(cum_last - cum_c))
                st_ref[b, lt] = st * jnp.exp(cum_last) + jnp.where(pair_blk, upd, 0.0)

    o = jnp.concatenate(
        [jnp.concatenate([t_ref[b, OUT, lt] for lt in range(npt)], axis=1) for b in range(nb)], axis=0)
    ms = _dot_sel_rhs(o * o, bones, 2) * (1.0 / HEAD_DIM)
    o = o * lax.rsqrt(ms + LN_EPS) * ng_ref[...]
    for b in range(nb):
        o_ref[b] = o[b * tb:(b + 1) * tb] * _silu(p_ref[b, :, 3 * w:4 * w])


def _gla(pg, wa, ba, ng):
    nb, t, cols = pg.shape
    tb = GLA_BLOCK
    npt = GLA_WIDTH // LANES
    pair_rows = (GLA_CHUNK * (GLA_CHUNK + 1) // 2) * (tb // GLA_CHUNK)
    full2 = lambda a: pl.BlockSpec(a.shape, lambda i: (0, 0))
    return pl.pallas_call(
        _gla_kernel,
        grid=(t // tb,),
        in_specs=[pl.BlockSpec((nb, tb, cols), lambda i: (0, i, 0)), full2(wa), full2(ba), full2(ng)],
        out_specs=pl.BlockSpec((nb, tb, GLA_WIDTH), lambda i: (0, i, 0)),
        out_shape=jax.ShapeDtypeStruct((nb, t, GLA_WIDTH), f32),
        scratch_shapes=[
            pltpu.VMEM((nb, npt, LANES, LANES), f32),
            pltpu.VMEM((nb, 5, npt, tb, LANES), f32),
            pltpu.VMEM((nb, pair_rows, GLA_WIDTH), f32),
            pltpu.VMEM((nb, pair_rows, GLA_WIDTH), f32),
        ],
        compiler_params=_cparams(("arbitrary",)),
        name="gla_mixer",
    )(pg, wa, ba, ng)


def _ssd_kernel(p_ref, prev_ref, cw_ref, cb_ref, dtb_ref, alog_ref, dsk_ref, ng_ref, o_ref, st_ref):
    _, L, _ = p_ref.shape
    W, M, G = SSD_WIDTH, SSD_STATE, SSD_GROUPS
    gw = W // G
    i = pl.program_id(1)

    @pl.when(i == 0)
    def _():
        st_ref[...] = jnp.zeros_like(st_ref)

    z = p_ref[0, :, 0:W]
    xbc = p_ref[0, :, W:W + SSD_XBC]
    dt_raw = p_ref[0, :, W + SSD_XBC:W + SSD_XBC + LANES]
    prev = jnp.where(i > 0, prev_ref[0, :, W:W + SSD_XBC], 0.0)
    xx = jnp.concatenate([prev, xbc], axis=0)
    conv = cb_ref[...]
    for j in range(SSD_CONV):
        off = 8 - (SSD_CONV - 1) + j
        conv = conv + cw_ref[j:j + 1, :] * xx[off:off + L]
    act = _silu(conv)
    xs = act[:, 0:W]
    bm = act[:, W:W + G * M]
    cm = act[:, W + G * M:W + 2 * G * M]

    dt = _softplus(dt_raw + dtb_ref[...])
    a_neg = jnp.where(_iota((1, LANES), 1) < SSD_HEADS, -jnp.exp(alog_ref[...]), 0.0)
    da = dt * a_neg
    r_i, c_i = _iota((L, L), 0), _iota((L, L), 1)
    causal = r_i >= c_i
    cs = _dot_sel_lhs(causal.astype(bf16), da, 3)
    expand = (_iota((LANES, W), 0) == _iota((LANES, W), 1) // HEAD_DIM).astype(bf16)
    dtx = _dot_sel_rhs(dt, expand, 3)
    csx = _dot_sel_rhs(cs, expand, 3)
    eye_h = (_iota((8, LANES), 0) == _iota((8, LANES), 1)).astype(bf16)
    cs_t = None
    for p in _split(cs, 3):
        t = lax.dot_general(eye_h, p, (((1,), (1,)), ((), ())), preferred_element_type=f32)
        cs_t = t if cs_t is None else cs_t + t
    xdt = xs * dtx

    st = st_ref[...]
    y_parts = []
    for gi in range(G):
        bm_g = bm[:, gi * M:(gi + 1) * M]
        cm_g = cm[:, gi * M:(gi + 1) * M]
        cb = _dot_nt(cm_g, bm_g)
        for e in range(SSD_HEADS // G):
            h = gi * (SSD_HEADS // G) + e
            seg = jnp.where(causal, jnp.exp(jnp.broadcast_to(cs[:, h:h + 1], (L, L)) - cs_t[h:h + 1, :]), 0.0)
            y_parts.append(_dot(cb * seg, xdt[:, h * HEAD_DIM:(h + 1) * HEAD_DIM]))
    y = jnp.concatenate(y_parts, axis=1)
    y_inter = jnp.concatenate(
        [_dot(cm[:, gi * M:(gi + 1) * M], st[:, gi * gw:(gi + 1) * gw]) for gi in range(G)], axis=1)
    y = y + y_inter * jnp.exp(csx)

    cs_last = csx[L - 1:L, :]
    xdtw = xdt * jnp.exp(cs_last - csx)
    d_state = jnp.concatenate(
        [_dot_tn(bm[:, gi * M:(gi + 1) * M], xdtw[:, gi * gw:(gi + 1) * gw]) for gi in range(G)], axis=1)
    st_ref[...] = st * jnp.exp(cs_last) + d_state

    y = (y + dsk_ref[...] * xs) * _silu(z)
    outs = []
    for gi in range(G):
        yg = y[:, gi * gw:(gi + 1) * gw]
        ms = jnp.mean(yg * yg, axis=-1, keepdims=True)
        outs.append(yg * lax.rsqrt(ms + LN_EPS))
    o_ref[0] = jnp.concatenate(outs, axis=1) * ng_ref[...]


def _ssd(ps, cw, cb, dtb, alog, dsk, ng):
    nb, t, cols = ps.shape
    L = SSD_BLOCK
    full2 = lambda a: pl.BlockSpec(a.shape, lambda b, i: (0, 0))
    return pl.pallas_call(
        _ssd_kernel,
        grid=(nb, t // L),
        in_specs=[
            pl.BlockSpec((1, L, cols), lambda b, i: (b, i, 0)),
            pl.BlockSpec((1, 8, cols), lambda b, i: (b, jnp.maximum(i * (L // 8) - 1, 0), 0)),
            full2(cw), full2(cb), full2(dtb), full2(alog), full2(dsk), full2(ng),
        ],
        out_specs=pl.BlockSpec((1, L, SSD_WIDTH), lambda b, i: (b, i, 0)),
        out_shape=jax.ShapeDtypeStruct((nb, t, SSD_WIDTH), f32),
        scratch_shapes=[pltpu.VMEM((SSD_STATE, SSD_WIDTH), f32)],
        compiler_params=_cparams(("arbitrary", "arbitrary")),
        name="ssd_mixer",
    )(ps, ps, cw, cb, dtb, alog, dsk, ng)


def _rwkv_kernel(p_ref, prev_ref, mu_ref, w0_ref, w2_ref, a0_ref, a2_ref, g2_ref, kk_ref, ka_ref, rk_ref,
                 lng_ref, lnb_ref, o_ref, zt_ref):
    nb, C, _ = p_ref.shape
    W, H = RWKV_WIDTH, RWKV_HEADS
    i = pl.program_id(0)

    @pl.when(i == 0)
    def _():
        zt_ref[...] = jnp.zeros_like(zt_ref)

    head_blk = _block_ones(W, HEAD_DIM)
    bones = head_blk.astype(bf16)
    n = H * C
    rr, cc = _iota((n, n), 0), _iota((n, n), 1)
    same = rr // C == cc // C
    strict = same & (rr > cc)
    lower = same & (rr >= cc)
    eye = jnp.where(rr == cc, 1.0, 0.0)

    def stack(t):
        return jnp.where(head_blk, jnp.concatenate([t] * H, axis=0), 0.0)

    rows = range(nb)
    nr = nb * C
    cols = p_ref[...].reshape(nr, p_ref.shape[2])
    shifted = jnp.concatenate(
        [piece for b in rows for piece in (jnp.where(i > 0, prev_ref[b, 7:8, :], 0.0), p_ref[b, 0:C - 1, :])], axis=0)
    cols = cols + (shifted - cols) * mu_ref[...]
    r = cols[:, 0:W]
    k = cols[:, W:2 * W]
    v = cols[:, 2 * W:3 * W]
    w_lr = cols[:, 3 * W:3 * W + 64]
    a_lr = cols[:, 3 * W + 64:3 * W + 128]
    g_lr = cols[:, 3 * W + 128:3 * W + 256]

    wv = -_softplus(-(w0_ref[...] + _dot_hi(jnp.tanh(w_lr), w2_ref[...]))) - 0.5
    lw = -jnp.exp(wv)
    a = jax.nn.sigmoid(a0_ref[...] + _dot_hi(a_lr, a2_ref[...]))
    g = _dot_hi(jax.nn.sigmoid(g_lr), g2_ref[...])

    kk = k * kk_ref[...]
    k2 = k * (1.0 + (a - 1.0) * ka_ref[...])
    head_sums = _dot_sel_rhs(jnp.concatenate([kk * kk, r * k2 * rk_ref[...]], axis=0), bones, 2)
    kk = kk / jnp.maximum(jnp.sqrt(head_sums[:nr]), 1e-12)
    bonus_v = head_sums[nr:] * v
    bvec = kk * a

    rb_i, cb_i = _iota((nr, nr), 0), _iota((nr, nr), 1)
    tri = ((rb_i >= cb_i) & (rb_i // C == cb_i // C)).astype(bf16)
    cum = _dot_sel_lhs(tri, lw, 3)
    cum_last = [cum[(b + 1) * C - 1:(b + 1) * C, :] for b in rows]
    e_neg = jnp.exp(-cum)
    e_rem = jnp.exp(jnp.concatenate([jnp.broadcast_to(cl, (C, W)) for cl in cum_last], axis=0) - cum)
    per_token = dict(aw=-kk * jnp.exp(cum - lw), rw=r * jnp.exp(cum), bn=bvec * e_neg, kn=k2 * e_neg,
                     bt=bvec * e_rem, kt=k2 * e_rem, vs=v)
    q = [dict({name: stack(t[b * C:(b + 1) * C]) for name, t in per_token.items()}, wc=jnp.exp(cum_last[b]))
         for b in rows]

    a_ab = [jnp.where(strict, _dot_nt(q[b]["aw"], q[b]["bn"]), 0.0) for b in rows]
    a_ak = [jnp.where(strict, _dot_nt(q[b]["aw"], q[b]["kn"]), 0.0) for b in rows]
    a_rb = [jnp.where(lower, _dot_nt(q[b]["rw"], q[b]["bn"]), 0.0) for b in rows]
    a_rk = [jnp.where(lower, _dot_nt(q[b]["rw"], q[b]["kn"]), 0.0) for b in rows]

    pw = a_ab
    tinv = [eye + a_ab[b] for b in rows]
    for _ in range(int(math.log2(C)) - 1):
        pw = [_dot(pw[b], pw[b]) for b in rows]
        tinv = [tinv[b] + _dot(tinv[b], pw[b]) for b in rows]

    zt = [zt_ref[b] for b in rows]
    rhs = [_dot_nt(q[b]["aw"], zt[b]) + _dot(a_ak[b], q[b]["vs"]) for b in rows]
    y0 = [_dot_nt(q[b]["rw"], zt[b]) + _dot(a_rk[b], q[b]["vs"]) for b in rows]
    u = [_dot(tinv[b], rhs[b]) for b in rows]
    y_st = [y0[b] + _dot(a_rb[b], u[b]) for b in rows]
    for b in rows:
        zt_ref[b] = zt[b] * q[b]["wc"] + _dot_tn(u[b], q[b]["bt"]) + _dot_tn(q[b]["vs"], q[b]["kt"])

    inv_n = 1.0 / HEAD_DIM
    ys = []
    for b in rows:
        y = y_st[b][0:C]
        for h in range(1, H):
            y = y + y_st[b][h * C:(h + 1) * C]
        ys.append(y)
    y = jnp.concatenate(ys, axis=0)
    yc = y - _dot_sel_rhs(y, bones, 2) * inv_n
    var_h = _dot_sel_rhs(yc * yc, bones, 2) * inv_n
    y = yc * lax.rsqrt(var_h + RWKV_GN_EPS) * lng_ref[...] + lnb_ref[...]
    o_ref[...] = ((y + bonus_v) * g).reshape(nb, C, W)


def _rwkv(pr, mu, w0, w2, a0, a2, g2, k_k, k_a, r_k, ln_g, ln_b):
    nb, t, cols = pr.shape
    C = RWKV_CHUNK
    full2 = lambda a: pl.BlockSpec(a.shape, lambda i: (0, 0))
    params = (mu, w0, w2, a0, a2, g2, k_k, k_a, r_k, ln_g, ln_b)
    return pl.pallas_call(
        _rwkv_kernel,
        grid=(t // C,),
        in_specs=[
            pl.BlockSpec((nb, C, cols), lambda i: (0, i, 0)),
            pl.BlockSpec((nb, 8, cols), lambda i: (0, jnp.maximum(i * (C // 8) - 1, 0), 0)),
        ] + [full2(p) for p in params],
        out_specs=pl.BlockSpec((nb, C, RWKV_WIDTH), lambda i: (0, i, 0)),
        out_shape=jax.ShapeDtypeStruct((nb, t, RWKV_WIDTH), f32),
        scratch_shapes=[pltpu.VMEM((nb, RWKV_WIDTH, RWKV_WIDTH), f32)],
        compiler_params=_cparams(("arbitrary",)),
        name="rwkv7_mixer",
    )(pr, pr, *params)


def _outproj_kernel(alpha, x_ref, og_ref, os_ref, or_ref, wo_ref, g_ref, b_ref, rwt_ref, rb_ref,
                    x1_ref, gate_ref, best_ref, cnt_ref):
    tm = x_ref.shape[0]
    d = functools.partial(jnp.dot, preferred_element_type=f32)
    mix = (d(_bf(og_ref[...]), wo_ref[0:GLA_WIDTH, :])
           + d(_bf(os_ref[...]), wo_ref[GLA_WIDTH:GLA_WIDTH + SSD_WIDTH, :])
           + d(_bf(or_ref[...]), wo_ref[GLA_WIDTH + SSD_WIDTH:, :]))
    x1 = _layer_norm(alpha * x_ref[...] + mix, g_ref[...], b_ref[...], LN_EPS)
    x1_ref[...] = x1

    xh, xl = _split(x1, 2)
    wh, wl = _split(rwt_ref[...], 2)
    nt = lambda a, b: lax.dot_general(a, b, (((1,), (1,)), ((), ())), preferred_element_type=f32)
    logits = nt(wh, xh) + nt(wl, xh) + nt(wh, xl)
    scores = jax.nn.sigmoid(logits)
    sel = scores + rb_ref[...]
    srow = [sel[e:e + 1, :] for e in range(N_EXPERTS)]
    crow = [scores[e:e + 1, :] for e in range(N_EXPERTS)]

    def top2_sum(v0, v1, v2, v3):
        m01, n01 = jnp.maximum(v0, v1), jnp.minimum(v0, v1)
        m23, n23 = jnp.maximum(v2, v3), jnp.minimum(v2, v3)
        return jnp.maximum(m01, m23) + jnp.maximum(jnp.minimum(m01, m23), jnp.maximum(n01, n23))

    gscore = [top2_sum(*srow[4 * gi:4 * gi + 4]) for gi in range(N_EXPERT_GROUPS)]
    best = jnp.zeros((1, tm), jnp.int32)
    best_v = gscore[0]
    for gi in range(1, N_EXPERT_GROUPS):
        better = gscore[gi] > best_v
        best = jnp.where(better, gi, best)
        best_v = jnp.where(better, gscore[gi], best_v)
    sv, cv = [], []
    for j in range(EXPERTS_PER_GROUP):
        s_j, c_j = srow[j], crow[j]
        for gi in range(1, N_EXPERT_GROUPS):
            s_j = jnp.where(best == gi, srow[4 * gi + j], s_j)
            c_j = jnp.where(best == gi, crow[4 * gi + j], c_j)
        sv.append(s_j)
        cv.append(c_j)
    j1 = jnp.zeros((1, tm), jnp.int32)
    v1 = sv[0]
    for j in range(1, EXPERTS_PER_GROUP):
        better = sv[j] > v1
        j1 = jnp.where(better, j, j1)
        v1 = jnp.where(better, sv[j], v1)
    j2 = jnp.full((1, tm), -1, jnp.int32)
    v2 = jnp.full((1, tm), -jnp.inf, f32)
    for j in range(EXPERTS_PER_GROUP):
        better = (j1 != j) & ((j2 < 0) | (sv[j] > v2))
        j2 = jnp.where(better, j, j2)
        v2 = jnp.where(better, sv[j], v2)
    w1 = cv[0]
    w2 = cv[0]
    for j in range(1, EXPERTS_PER_GROUP):
        w1 = jnp.where(j1 == j, cv[j], w1)
        w2 = jnp.where(j2 == j, cv[j], w2)
    wsum = w1 + w2
    g1, g2 = w1 / wsum, w2 / wsum
    rows = []
    for e in range(N_EXPERTS):
        gi, j = divmod(e, EXPERTS_PER_GROUP)
        in_g = best == gi
        rows.append(jnp.where(in_g & (j1 == j), g1, jnp.where(in_g & (j2 == j), g2, 0.0)))
    gate_t = jnp.concatenate(rows, axis=0)
    eye = (_iota((N_EXPERTS, LANES), 0) == _iota((N_EXPERTS, LANES), 1)).astype(bf16)
    gate = None
    for p in _split(gate_t, 3):
        t = lax.dot_general(p, eye, (((0,), (0,)), ((), ())), preferred_element_type=f32)
        gate = t if gate is None else gate + t
    gate_ref[...] = gate
    best_ref[...] = best
    lane = _iota((1, LANES), 1)
    ts = tm // cnt_ref.shape[0]
    for sb in range(cnt_ref.shape[0]):
        cnt = jnp.zeros((1, LANES), jnp.int32)
        for gi in range(N_EXPERT_GROUPS):
            in_g = (best[:, sb * ts:(sb + 1) * ts] == gi).astype(jnp.int32)
            cnt = jnp.where(lane == gi, jnp.sum(in_g, axis=-1, keepdims=True), cnt)
        cnt_ref[sb] = cnt


def _out_proj(alpha, x2, og, os_, or_, wo, g, b, rwt, rb):
    n, d = x2.shape
    tm = min(OUT_PROJ_ROWS, n)
    ts = min(MOE_SORT_ROWS, tm)
    full2 = lambda a: pl.BlockSpec(a.shape, lambda i: (0, 0))
    rows = lambda c: pl.BlockSpec((tm, c), lambda i: (i, 0))
    return pl.pallas_call(
        functools.partial(_outproj_kernel, alpha),
        grid=(n // tm,),
        in_specs=[rows(d), rows(og.shape[1]), rows(os_.shape[1]), rows(or_.shape[1]),
                  full2(wo), full2(g), full2(b), full2(rwt), full2(rb)],
        out_specs=[rows(d), rows(LANES), pl.BlockSpec((1, tm), lambda i: (0, i)),
                   pl.BlockSpec((tm // ts, 1, LANES), lambda i: (i, 0, 0))],
        out_shape=[jax.ShapeDtypeStruct((n, d), f32), jax.ShapeDtypeStruct((n, LANES), f32),
                   jax.ShapeDtypeStruct((1, n), jnp.int32), jax.ShapeDtypeStruct((n // ts, 1, LANES), jnp.int32)],
        compiler_params=_cparams(("arbitrary",)),
        name="out_proj_router",
    )(x2, og, os_, or_, wo, g, b, rwt, rb)


def _seg_copy(src, dst, sem, src_off, dst_off, rows, max_rows, wait):
    k = SEG_ALIGN
    sizes = []
    while k <= max_rows:
        sizes.append(k)
        k *= 2
    for k in reversed(sizes):
        shift = int(math.log2(k)) + 1

        @pl.when((rows & k) != 0)
        def _():
            done = (rows >> shift) << shift
            cp = pltpu.make_async_copy(
                src.at[pl.ds(pl.multiple_of(src_off + done, SEG_ALIGN), k)],
                dst.at[pl.ds(pl.multiple_of(dst_off + done, SEG_ALIGN), k)], sem)
            if wait:
                cp.wait()
            else:
                cp.start()


def _padded(count):
    return (count + (SEG_ALIGN - 1)) & (-SEG_ALIGN)


def _one_hot_rows(pos, n_rows):
    return jnp.where(_iota((n_rows, pos.shape[1]), 0) == pos, 1.0, 0.0).astype(bf16)


def _dispatch_kernel(cnt_sm, goff_sm, fill_sm, x_ref, gate_ref, best_ref, xs_out, gs_out, pos_ref,
                     xloc, gloc, zx, zg, sems):
    i = pl.program_id(0)
    tm = x_ref.shape[0]
    n_loc = xloc.shape[1]
    slot = i % 2

    def copies(blk, slot_, wait):
        loff = 0
        for gi in range(N_EXPERT_GROUPS):
            rows = _padded(cnt_sm[blk * N_EXPERT_GROUPS + gi])
            dst = goff_sm[blk * N_EXPERT_GROUPS + gi]
            _seg_copy(xloc.at[slot_], xs_out, sems.at[0, slot_], loff, dst, rows, tm, wait)
            _seg_copy(gloc.at[slot_], gs_out, sems.at[1, slot_], loff, dst, rows, tm, wait)
            loff = loff + rows

    best = best_ref[...]
    onehot = (_iota((8, tm), 0) == best).astype(bf16)
    earlier = (_iota((tm, tm), 0) < _iota((tm, tm), 1)).astype(bf16)
    rank = jnp.dot(onehot, earlier, preferred_element_type=f32)
    pos = jnp.zeros((1, tm), f32)
    loff = 0
    for gi in range(N_EXPERT_GROUPS):
        pos = jnp.where(best == gi, rank[gi:gi + 1, :] + loff.astype(f32) if gi else rank[gi:gi + 1, :], pos)
        loff = loff + _padded(cnt_sm[i * N_EXPERT_GROUPS + gi])
    pos = pos.astype(jnp.int32)
    pos_ref[...] = pos
    perm = _one_hot_rows(pos, n_loc)
    xloc[slot] = jnp.dot(perm, _bf(x_ref[...]), preferred_element_type=f32)
    gloc[slot] = _dot_sel_lhs(perm, gate_ref[...], 3)
    copies(i, slot, False)

    @pl.when(i > 0)
    def _():
        copies(i - 1, 1 - slot, True)

    @pl.when(i == pl.num_programs(0) - 1)
    def _():
        copies(i, slot, True)
        zx[...] = jnp.zeros_like(zx)
        zg[...] = jnp.zeros_like(zg)
        tile = zx.shape[0]

        def fills(wait):
            for src, dst, sem in ((zx, xs_out, sems.at[2, 0]), (zg, gs_out, sems.at[2, 1])):
                for gi in range(N_EXPERT_GROUPS):
                    _seg_copy(src, dst, sem, 0, fill_sm[gi], fill_sm[N_EXPERT_GROUPS + gi], tile // 2, wait)

                def body(j, carry):
                    off = pl.multiple_of(fill_sm[2 * N_EXPERT_GROUPS] + j * tile, SEG_ALIGN)
                    cp = pltpu.make_async_copy(src, dst.at[pl.ds(off, tile)], sem)
                    if wait:
                        cp.wait()
                    else:
                        cp.start()
                    return carry

                lax.fori_loop(0, fill_sm[2 * N_EXPERT_GROUPS + 1], body, 0)

        fills(False)
        fills(True)


def _experts_kernel(tg_sm, tv_sm, x_ref, gate_ref, wg_ref, wu_ref, wd_ref, y_ref):
    t = pl.program_id(0)

    @pl.when(tv_sm[t] == 0)
    def _():
        y_ref[...] = jnp.zeros_like(y_ref)

    @pl.when(tv_sm[t] > 0)
    def _():
        first = tg_sm[t] * EXPERTS_PER_GROUP
        xb = _bf(x_ref[...])
        gate = gate_ref[...]
        lane = _iota(gate.shape, 1)
        hs = []
        for j in range(EXPERTS_PER_GROUP):
            hg = jnp.dot(xb, wg_ref[j], preferred_element_type=f32)
            hu = jnp.dot(xb, wu_ref[j], preferred_element_type=f32)
            gcol = jnp.sum(jnp.where(lane == first + j, gate, 0.0), axis=-1, keepdims=True)
            hs.append(_bf(_silu(hg) * hu * gcol))
        ff = wd_ref.shape[1]
        y_ref[...] = jnp.dot(jnp.concatenate(hs, axis=1), wd_ref[...].reshape(EXPERTS_PER_GROUP * ff, -1),
                             preferred_element_type=f32)


def _combine_kernel(alpha, cnt_sm, goff_sm, x_ref, pos_ref, g_ref, b_ref, ys_ref, o_ref, yloc, sems):
    i = pl.program_id(0)
    tm = x_ref.shape[0]
    n_loc = yloc.shape[1]
    slot = i % 2

    def copies(blk, slot_, wait):
        loff = 0
        for gi in range(N_EXPERT_GROUPS):
            rows = _padded(cnt_sm[blk * N_EXPERT_GROUPS + gi])
            src = goff_sm[blk * N_EXPERT_GROUPS + gi]
            _seg_copy(ys_ref, yloc.at[slot_], sems.at[slot_], src, loff, rows, tm, wait)
            loff = loff + rows

    @pl.when(i == 0)
    def _():
        yloc[...] = jnp.zeros_like(yloc)
        copies(0, 0, False)

    @pl.when(i + 1 < pl.num_programs(0))
    def _():
        copies(i + 1, 1 - slot, False)

    copies(i, slot, True)
    perm = _one_hot_rows(pos_ref[...], n_loc)
    y = lax.dot_general(perm, _bf(yloc[slot]), (((0,), (0,)), ((), ())), preferred_element_type=f32)
    o_ref[...] = _layer_norm(alpha * x_ref[...] + y, g_ref[...], b_ref[...], LN_EPS)


def _moe_tables(cnt, n_tiles, tile):
    pc = (cnt + (SEG_ALIGN - 1)) // SEG_ALIGN * SEG_ALIGN
    gtot = jnp.sum(pc, axis=0)
    gcap = (gtot + tile - 1) // tile * tile
    gend = jnp.cumsum(gcap)
    gstart = gend - gcap
    goff = gstart[None, :] + jnp.cumsum(pc, axis=0) - pc
    tstart = jnp.arange(n_tiles, dtype=jnp.int32) * tile
    tg = jnp.minimum(jnp.sum((tstart[:, None] >= gend[None, :]).astype(jnp.int32), axis=1), N_EXPERT_GROUPS - 1)
    tv = jnp.clip(gtot[tg] - (tstart - gstart[tg]), 0, tile)
    fill = jnp.concatenate([gstart + gtot, gcap - gtot, gend[-1:], n_tiles - gend[-1:] // tile])
    return goff.reshape(-1).astype(jnp.int32), tg.astype(jnp.int32), tv.astype(jnp.int32), fill.astype(jnp.int32)


def _moe(alpha, x1, gate, best, cnt, wg, wu, wd, g, b):
    n, d = x1.shape
    ne, _, ff = wg.shape
    nblk = cnt.shape[0]
    tm = n // nblk
    tile = MOE_TILE
    n_loc = tm + LANES
    n_sorted = -(-(n + nblk * N_EXPERT_GROUPS * SEG_ALIGN + N_EXPERT_GROUPS * tile) // tile) * tile
    n_tiles = n_sorted // tile
    cnt_flat = cnt[:, 0, :N_EXPERT_GROUPS].reshape(-1)
    goff, tg, tv, fill = _moe_tables(cnt[:, 0, :N_EXPERT_GROUPS], n_tiles, tile)
    any_spec = pl.BlockSpec(memory_space=pl.ANY)

    xs, gs, pos = pl.pallas_call(
        _dispatch_kernel,
        grid_spec=pltpu.PrefetchScalarGridSpec(
            num_scalar_prefetch=3,
            grid=(nblk,),
            in_specs=[
                pl.BlockSpec((tm, d), lambda i, *_: (i, 0)),
                pl.BlockSpec((tm, LANES), lambda i, *_: (i, 0)),
                pl.BlockSpec((1, tm), lambda i, *_: (0, i)),
            ],
            out_specs=[any_spec, any_spec, pl.BlockSpec((1, tm), lambda i, *_: (0, i))],
            scratch_shapes=[pltpu.VMEM((2, n_loc, d), f32), pltpu.VMEM((2, n_loc, LANES), f32),
                            pltpu.VMEM((tile, d), f32), pltpu.VMEM((tile, LANES), f32),
                            pltpu.SemaphoreType.DMA((3, 2))],
        ),
        out_shape=[jax.ShapeDtypeStruct((n_sorted, d), f32), jax.ShapeDtypeStruct((n_sorted, LANES), f32),
                   jax.ShapeDtypeStruct((1, n), jnp.int32)],
        compiler_params=_cparams(("arbitrary",)),
        name="moe_dispatch",
    )(cnt_flat, goff, fill, x1, gate, best)

    ys = pl.pallas_call(
        _experts_kernel,
        grid_spec=pltpu.PrefetchScalarGridSpec(
            num_scalar_prefetch=2,
            grid=(n_tiles,),
            in_specs=[
                pl.BlockSpec((tile, d), lambda t, *_: (t, 0)),
                pl.BlockSpec((tile, LANES), lambda t, *_: (t, 0)),
                pl.BlockSpec((EXPERTS_PER_GROUP, d, ff), lambda t, tg_, tv_: (tg_[t], 0, 0)),
                pl.BlockSpec((EXPERTS_PER_GROUP, d, ff), lambda t, tg_, tv_: (tg_[t], 0, 0)),
                pl.BlockSpec((EXPERTS_PER_GROUP, ff, d), lambda t, tg_, tv_: (tg_[t], 0, 0)),
            ],
            out_specs=pl.BlockSpec((tile, d), lambda t, *_: (t, 0)),
        ),
        out_shape=jax.ShapeDtypeStruct((n_sorted, d), f32),
        compiler_params=_cparams(("arbitrary",)),
        name="moe_experts",
    )(tg, tv, xs, gs, wg, wu, wd)

    return pl.pallas_call(
        functools.partial(_combine_kernel, alpha),
        grid_spec=pltpu.PrefetchScalarGridSpec(
            num_scalar_prefetch=2,
            grid=(nblk,),
            in_specs=[
                pl.BlockSpec((tm, d), lambda i, *_: (i, 0)),
                pl.BlockSpec((1, tm), lambda i, *_: (0, i)),
                pl.BlockSpec(g.shape, lambda i, *_: (0, 0)),
                pl.BlockSpec(b.shape, lambda i, *_: (0, 0)),
                any_spec,
            ],
            out_specs=pl.BlockSpec((tm, d), lambda i, *_: (i, 0)),
            scratch_shapes=[pltpu.VMEM((2, n_loc, d), f32), pltpu.SemaphoreType.DMA((2,))],
        ),
        out_shape=jax.ShapeDtypeStruct((n, d), f32),
        compiler_params=_cparams(("arbitrary",)),
        name="moe_combine",
    )(cnt_flat, goff, x1, pos, g, b, ys)


def _pad_cols(w, n):
    return jnp.pad(w, ((0, 0), (0, n - w.shape[1])))


def _row(v, n=None):
    v = v.reshape(1, -1)
    return v if n is None else _pad_cols(v, n)


def kernel(x, w_in, gla_w_alpha, gla_b_alpha, gla_norm_g, ssd_conv_w, ssd_conv_b, ssd_dt_bias, ssd_A_log, ssd_D, ssd_norm_g, rwkv_mu, rwkv_w0, rwkv_w2, rwkv_a0, rwkv_a2, rwkv_g2, rwkv_k_k, rwkv_k_a, rwkv_r_k, rwkv_ln_g, rwkv_ln_b, w_out, ln1_g, ln1_b, router_w, router_b, exp_w_gate, exp_w_up, exp_w_down, ln2_g, ln2_b):
    nb, t, d = x.shape
    depth = w_in.shape[0]
    alpha = float((2 * depth) ** 0.25)
    gla_cols = 4 * GLA_WIDTH + GLA_GATE_RANK
    ssd_cols = SSD_WIDTH + SSD_XBC + SSD_HEADS
    rwt = router_w.T
    rb = router_b.reshape(-1, 1)
    x2 = x.reshape(nb * t, d)
    for l in range(depth):
        wg, ws, wr = _regroup_w_in(w_in, l, (gla_cols, ssd_cols, RWKV_COLS), (GLA_COLS_PAD, SSD_COLS_PAD, RWKV_COLS))
        pg, ps, pr = _in_proj(x2, wg, ws, wr)
        og = _gla(pg.reshape(nb, t, -1),
                  jnp.pad(gla_w_alpha[l], ((0, LANES - GLA_GATE_RANK), (0, 0))),
                  _row(gla_b_alpha[l]), _row(jnp.tile(gla_norm_g[l], GLA_WIDTH // HEAD_DIM)))
        os_ = _ssd(ps.reshape(nb, t, -1), ssd_conv_w[l], _row(ssd_conv_b[l]), _row(ssd_dt_bias[l], LANES),
                   _row(ssd_A_log[l], LANES), _row(jnp.repeat(ssd_D[l], HEAD_DIM)), _row(ssd_norm_g[l]))
        or_ = _rwkv(pr.reshape(nb, t, -1), _row(rwkv_mu[l]), _row(rwkv_w0[l]), rwkv_w2[l], _row(rwkv_a0[l]),
                    rwkv_a2[l], rwkv_g2[l], _row(rwkv_k_k[l]), _row(rwkv_k_a[l]), _row(rwkv_r_k[l]),
                    _row(rwkv_ln_g[l]), _row(rwkv_ln_b[l]))
        x1, gate, best, cnt = _out_proj(alpha, x2, og.reshape(nb * t, -1), os_.reshape(nb * t, -1),
                                        or_.reshape(nb * t, -1), _bf(w_out[l]), _row(ln1_g[l]), _row(ln1_b[l]),
                                        rwt, rb)
        x2 = _moe(alpha, x1, gate, best, cnt, _bf(exp_w_gate[l]), _bf(exp_w_up[l]), _bf(exp_w_down[l]),
                  _row(ln2_g[l]), _row(ln2_b[l]))
    return x2.reshape(nb, t, d)
```

```python
import functools
import math

import jax
import jax.numpy as jnp
from jax import lax
from jax.experimental import pallas as pl
from jax.experimental.pallas import tpu as pltpu

f32 = jnp.float32
bf16 = jnp.bfloat16

LANES = 128
HEAD_DIM = 64
GLA_WIDTH = 256
GLA_GATE_RANK = 16
GLA_GATE_TAU = 16.0
GLA_CHUNK = 16
GLA_BLOCK = 128
GLA_COLS_PAD = 4 * GLA_WIDTH + LANES
SSD_WIDTH = 512
SSD_HEADS = 8
SSD_GROUPS = 2
SSD_STATE = 64
SSD_CONV = 4
SSD_XBC = SSD_WIDTH + 2 * SSD_GROUPS * SSD_STATE
SSD_BLOCK = 128
SSD_COLS_PAD = SSD_WIDTH + SSD_XBC + LANES
RWKV_WIDTH = 256
RWKV_HEADS = 4
RWKV_CHUNK = 64
RWKV_GN_EPS = 64e-5
RWKV_COLS = 3 * RWKV_WIDTH + 64 + 64 + 128
N_EXPERTS = 16
N_EXPERT_GROUPS = 4
EXPERTS_PER_GROUP = 4
LN_EPS = 1e-5

IN_PROJ_ROWS = 512
OUT_PROJ_ROWS = 1024
MOE_SORT_ROWS = 512
MOE_TILE = 512
SEG_ALIGN = 8
VMEM_LIMIT = 56 * 1024 * 1024


def _bf(x):
    return x.astype(bf16)


def _dot(a, b):
    return jnp.dot(_bf(a), _bf(b), preferred_element_type=f32)


def _dot_nt(a, b):
    return lax.dot_general(_bf(a), _bf(b), (((1,), (1,)), ((), ())), preferred_element_type=f32)


def _dot_tn(a, b):
    return lax.dot_general(_bf(a), _bf(b), (((0,), (0,)), ((), ())), preferred_element_type=f32)


def _split(x, n):
    parts, r = [], x
    for _ in range(n):
        p = r.astype(bf16)
        parts.append(p)
        r = r - p.astype(f32)
    return parts


def _dot_sel_rhs(a, sel, n=3):
    out = None
    for p in _split(a, n):
        t = jnp.dot(p, sel, preferred_element_type=f32)
        out = t if out is None else out + t
    return out


def _dot_sel_lhs(sel, b, n=3):
    out = None
    for p in _split(b, n):
        t = jnp.dot(sel, p, preferred_element_type=f32)
        out = t if out is None else out + t
    return out


def _dot_hi(a, b):
    ah, al = _split(a, 2)
    bh, bl = _split(b, 2)
    d = functools.partial(jnp.dot, preferred_element_type=f32)
    return d(ah, bh) + d(al, bh) + d(ah, bl)


def _iota(shape, dim):
    return lax.broadcasted_iota(jnp.int32, shape, dim)


def _block_ones(n, blk):
    return (_iota((n, n), 0) // blk == _iota((n, n), 1) // blk)


def _silu(x):
    return x * jax.nn.sigmoid(x)


def _softplus(x):
    return jnp.maximum(x, 0.0) + jnp.log1p(jnp.exp(-jnp.abs(x)))


def _layer_norm(y, g, b, eps):
    mu = jnp.mean(y, axis=-1, keepdims=True)
    yc = y - mu
    var = jnp.mean(yc * yc, axis=-1, keepdims=True)
    return yc * lax.rsqrt(var + eps) * g + b


def _cparams(sem):
    return pltpu.CompilerParams(dimension_semantics=sem, vmem_limit_bytes=VMEM_LIMIT)


def _inproj_kernel(x_ref, wg_ref, ws_ref, wr_ref, og_ref, os_ref, or_ref):
    xb = _bf(x_ref[...])
    og_ref[...] = jnp.dot(xb, wg_ref[...], preferred_element_type=f32)
    os_ref[...] = jnp.dot(xb, ws_ref[...], preferred_element_type=f32)
    or_ref[...] = jnp.dot(xb, wr_ref[...], preferred_element_type=f32)


def _regroup_kernel(splits, w_ref, *o_refs):
    w = w_ref[0]
    start = 0
    for width, o_ref in zip(splits, o_refs):
        part = w[:, start:start + width]
        pad = o_ref.shape[1] - width
        if pad:
            part = jnp.concatenate([part, jnp.zeros((part.shape[0], pad), part.dtype)], axis=1)
        o_ref[...] = _bf(part)
        start += width


def _regroup_w_in(w_in, layer, splits, padded):
    _, d, n_in = w_in.shape
    rows = 256
    return pl.pallas_call(
        functools.partial(_regroup_kernel, splits),
        grid=(d // rows,),
        in_specs=[pl.BlockSpec((1, rows, n_in), lambda i: (layer, i, 0))],
        out_specs=[pl.BlockSpec((rows, p), lambda i: (i, 0)) for p in padded],
        out_shape=[jax.ShapeDtypeStruct((d, p), bf16) for p in padded],
        compiler_params=_cparams(("arbitrary",)),
        name="regroup_w_in",
    )(w_in)


def _in_proj(x2, wg, ws, wr):
    n, d = x2.shape
    tm = min(IN_PROJ_ROWS, n)
    full = lambda w: pl.BlockSpec(w.shape, lambda i: (0, 0))
    rows = lambda c: pl.BlockSpec((tm, c), lambda i: (i, 0))
    return pl.pallas_call(
        _inproj_kernel,
        grid=(n // tm,),
        in_specs=[rows(d), full(wg), full(ws), full(wr)],
        out_specs=[rows(wg.shape[1]), rows(ws.shape[1]), rows(wr.shape[1])],
        out_shape=[jax.ShapeDtypeStruct((n, w.shape[1]), f32) for w in (wg, ws, wr)],
        compiler_params=_cparams(("arbitrary",)),
        name="in_proj",
    )(x2, wg, ws, wr)


def _gla_kernel(p_ref, wa_ref, ba_ref, ng_ref, o_ref, st_ref, t_ref, d_ref, sx_ref):
    nb, tb, _ = p_ref.shape
    w, c = GLA_WIDTH, GLA_CHUNK
    nc = tb // c
    npt = w // LANES
    Q, K, V, CUM, OUT = range(5)

    @pl.when(pl.program_id(0) == 0)
    def _():
        st_ref[...] = jnp.zeros_like(st_ref)

    bones = _block_ones(w, HEAD_DIM).astype(bf16)
    pair_blk = _block_ones(LANES, HEAD_DIM)
    r_i, c_i = _iota((tb, tb), 0), _iota((tb, tb), 1)
    tri = ((r_i >= c_i) & (r_i // c == c_i // c)).astype(bf16)
    wa_h, wa_l = _split(wa_ref[...], 2)
    dd = functools.partial(jnp.dot, preferred_element_type=f32)

    for b in range(nb):
        a_h, a_l = _split(p_ref[b, :, 4 * w:4 * w + LANES], 2)
        z = dd(a_h, wa_h) + dd(a_l, wa_h) + dd(a_h, wa_l) + ba_ref[...]
        log_a = -_softplus(-z) * (1.0 / GLA_GATE_TAU)
        cum = _dot_sel_lhs(tri, log_a, 3)
        for lt in range(npt):
            lanes = slice(lt * LANES, (lt + 1) * LANES)
            t_ref[b, Q, lt] = p_ref[b, :, lanes] * (HEAD_DIM ** -0.5)
            t_ref[b, K, lt] = p_ref[b, :, w + lt * LANES:w + (lt + 1) * LANES]
            t_ref[b, V, lt] = p_ref[b, :, 2 * w + lt * LANES:2 * w + (lt + 1) * LANES]
            t_ref[b, CUM, lt] = cum[:, lanes]

    def ld(b, j, off):
        return jnp.concatenate([t_ref[b, j, lt, pl.ds(off, nc, stride=c), :] for lt in range(npt)], axis=1)

    for b in range(nb):
        qt = [ld(b, Q, o) for o in range(c)]
        kt = [ld(b, K, o) for o in range(c)]
        ct = [ld(b, CUM, o) for o in range(c)]
        pair = 0
        for t in range(c):
            for s in range(t + 1):
                qk = qt[t] * kt[s]
                d_ref[b, pair * nc:(pair + 1) * nc, :] = qk if s == t else qk * jnp.exp(ct[t] - ct[s])
                pair += 1
    for b in range(nb):
        sx_ref[b] = jnp.dot(_bf(d_ref[b]), bones, preferred_element_type=f32)
    for b in range(nb):
        vt = [ld(b, V, o) for o in range(c)]
        pair = 0
        for t in range(c):
            acc = None
            for s in range(t + 1):
                term = sx_ref[b, pair * nc:(pair + 1) * nc, :] * vt[s]
                acc = term if acc is None else acc + term
                pair += 1
            for lt in range(npt):
                t_ref[b, OUT, lt, pl.ds(t, nc, stride=c), :] = acc[:, lt * LANES:(lt + 1) * LANES]

    for ci in range(nc):
        sl = slice(ci * c, (ci + 1) * c)
        last = slice((ci + 1) * c - 1, (ci + 1) * c)
        for b in range(nb):
            for lt in range(npt):
                cum_c = t_ref[b, CUM, lt, sl, :]
                cum_last = t_ref[b, CUM, lt, last, :]
                st = st_ref[b, lt]
                t_ref[b, OUT, lt, sl, :] += _dot_nt(t_ref[b, Q, lt, sl, :] * jnp.exp(cum_c), st)
                upd = _dot_tn(t_ref[b, V, lt, sl, :], t_ref[b, K, lt, sl, :] * jnp.exp(cum_last - cum_c))
                st_ref[b, lt] = st * jnp.exp(cum_last) + jnp.where(pair_blk, upd, 0.0)

    o = jnp.concatenate(
        [jnp.concatenate([t_ref[b, OUT, lt] for lt in range(npt)], axis=1) for b in range(nb)], axis=0)
    ms = _dot_sel_rhs(o * o, bones, 2) * (1.0 / HEAD_DIM)
    o = o * lax.rsqrt(ms + LN_EPS) * ng_ref[...]
    for b in range(nb):
        o_ref[b] = o[b * tb:(b + 1) * tb] * _silu(p_ref[b, :, 3 * w:4 * w])


def _gla(pg, wa, ba, ng):
    nb, t, cols = pg.shape
    tb = GLA_BLOCK
    npt = GLA_WIDTH // LANES
    pair_rows = (GLA_CHUNK * (GLA_CHUNK + 1) // 2) * (tb // GLA_CHUNK)
    full2 = lambda a: pl.BlockSpec(a.shape, lambda i: (0, 0))
    return pl.pallas_call(
        _gla_kernel,
        grid=(t // tb,),
        in_specs=[pl.BlockSpec((nb, tb, cols), lambda i: (0, i, 0)), full2(wa), full2(ba), full2(ng)],
        out_specs=pl.BlockSpec((nb, tb, GLA_WIDTH), lambda i: (0, i, 0)),
        out_shape=jax.ShapeDtypeStruct((nb, t, GLA_WIDTH), f32),
        scratch_shapes=[
            pltpu.VMEM((nb, npt, LANES, LANES), f32),
            pltpu.VMEM((nb, 5, npt, tb, LANES), f32),
            pltpu.VMEM((nb, pair_rows, GLA_WIDTH), f32),
            pltpu.VMEM((nb, pair_rows, GLA_WIDTH), f32),
        ],
        compiler_params=_cparams(("arbitrary",)),
        name="gla_mixer",
    )(pg, wa, ba, ng)


def _ssd_kernel(p_ref, prev_ref, cw_ref, cb_ref, dtb_ref, alog_ref, dsk_ref, ng_ref, o_ref, st_ref):
    _, L, _ = p_ref.shape
    W, M, G = SSD_WIDTH, SSD_STATE, SSD_GROUPS
    gw = W // G
    i = pl.program_id(1)

    @pl.when(i == 0)
    def _():
        st_ref[...] = jnp.zeros_like(st_ref)

    z = p_ref[0, :, 0:W]
    xbc = p_ref[0, :, W:W + SSD_XBC]
    dt_raw = p_ref[0, :, W + SSD_XBC:W + SSD_XBC + LANES]
    prev = jnp.where(i > 0, prev_ref[0, :, W:W + SSD_XBC], 0.0)
    xx = jnp.concatenate([prev, xbc], axis=0)
    conv = cb_ref[...]
    for j in range(SSD_CONV):
        off = 8 - (SSD_CONV - 1) + j
        conv = conv + cw_ref[j:j + 1, :] * xx[off:off + L]
    act = _silu(conv)
    xs = act[:, 0:W]
    bm = act[:, W:W + G * M]
    cm = act[:, W + G * M:W + 2 * G * M]

    dt = _softplus(dt_raw + dtb_ref[...])
    a_neg = jnp.where(_iota((1, LANES), 1) < SSD_HEADS, -jnp.exp(alog_ref[...]), 0.0)
    da = dt * a_neg
    r_i, c_i = _iota((L, L), 0), _iota((L, L), 1)
    causal = r_i >= c_i
    cs = _dot_sel_lhs(causal.astype(bf16), da, 3)
    expand = (_iota((LANES, W), 0) == _iota((LANES, W), 1) // HEAD_DIM).astype(bf16)
    dtx = _dot_sel_rhs(dt, expand, 3)
    csx = _dot_sel_rhs(cs, expand, 3)
    eye_h = (_iota((8, LANES), 0) == _iota((8, LANES), 1)).astype(bf16)
    cs_t = None
    for p in _split(cs, 3):
        t = lax.dot_general(eye_h, p, (((1,), (1,)), ((), ())), preferred_element_type=f32)
        cs_t = t if cs_t is None else cs_t + t
    xdt = xs * dtx

    st = st_ref[...]
    y_parts = []
    for gi in range(G):
        bm_g = bm[:, gi * M:(gi + 1) * M]
        cm_g = cm[:, gi * M:(gi + 1) * M]
        cb = _dot_nt(cm_g, bm_g)
        for e in range(SSD_HEADS // G):
            h = gi * (SSD_HEADS // G) + e
            seg = jnp.where(causal, jnp.exp(jnp.broadcast_to(cs[:, h:h + 1], (L, L)) - cs_t[h:h + 1, :]), 0.0)
            y_parts.append(_dot(cb * seg, xdt[:, h * HEAD_DIM:(h + 1) * HEAD_DIM]))
    y = jnp.concatenate(y_parts, axis=1)
    y_inter = jnp.concatenate(
        [_dot(cm[:, gi * M:(gi + 1) * M], st[:, gi * gw:(gi + 1) * gw]) for gi in range(G)], axis=1)
    y = y + y_inter * jnp.exp(csx)

    cs_last = csx[L - 1:L, :]
    xdtw = xdt * jnp.exp(cs_last - csx)
    d_state = jnp.concatenate(
        [_dot_tn(bm[:, gi * M:(gi + 1) * M], xdtw[:, gi * gw:(gi + 1) * gw]) for gi in range(G)], axis=1)
    st_ref[...] = st * jnp.exp(cs_last) + d_state

    y = (y + dsk_ref[...] * xs) * _silu(z)
    outs = []
    for gi in range(G):
        yg = y[:, gi * gw:(gi + 1) * gw]
        ms = jnp.mean(yg * yg, axis=-1, keepdims=True)
        outs.append(yg * lax.rsqrt(ms + LN_EPS))
    o_ref[0] = jnp.concatenate(outs, axis=1) * ng_ref[...]


def _ssd(ps, cw, cb, dtb, alog, dsk, ng):
    nb, t, cols = ps.shape
    L = SSD_BLOCK
    full2 = lambda a: pl.BlockSpec(a.shape, lambda b, i: (0, 0))
    return pl.pallas_call(
        _ssd_kernel,
        grid=(nb, t // L),
        in_specs=[
            pl.BlockSpec((1, L, cols), lambda b, i: (b, i, 0)),
            pl.BlockSpec((1, 8, cols), lambda b, i: (b, jnp.maximum(i * (L // 8) - 1, 0), 0)),
            full2(cw), full2(cb), full2(dtb), full2(alog), full2(dsk), full2(ng),
        ],
        out_specs=pl.BlockSpec((1, L, SSD_WIDTH), lambda b, i: (b, i, 0)),
        out_shape=jax.ShapeDtypeStruct((nb, t, SSD_WIDTH), f32),
        scratch_shapes=[pltpu.VMEM((SSD_STATE, SSD_WIDTH), f32)],
        compiler_params=_cparams(("arbitrary", "arbitrary")),
        name="ssd_mixer",
    )(ps, ps, cw, cb, dtb, alog, dsk, ng)


def _rwkv_kernel(p_ref, prev_ref, mu_ref, w0_ref, w2_ref, a0_ref, a2_ref, g2_ref, kk_ref, ka_ref, rk_ref,
                 lng_ref, lnb_ref, o_ref, zt_ref):
    nb, C, _ = p_ref.shape
    W = RWKV_WIDTH
    H = LANES // HEAD_DIM
    npair = W // LANES
    i = pl.program_id(0)

    @pl.when(i == 0)
    def _():
        zt_ref[...] = jnp.zeros_like(zt_ref)

    bones = _block_ones(W, HEAD_DIM).astype(bf16)
    pair_blk = _block_ones(LANES, HEAD_DIM)
    n = H * C
    rr, cc = _iota((n, n), 0), _iota((n, n), 1)
    same = rr // C == cc // C
    strict = same & (rr > cc)
    lower = same & (rr >= cc)
    eye = jnp.where(rr == cc, 1.0, 0.0)

    def stack(t):
        return jnp.where(pair_blk, jnp.concatenate([t] * H, axis=0), 0.0)

    rows = range(nb)
    nr = nb * C
    cols = p_ref[...].reshape(nr, p_ref.shape[2])
    shifted = jnp.concatenate(
        [piece for b in rows for piece in (jnp.where(i > 0, prev_ref[b, 7:8, :], 0.0), p_ref[b, 0:C - 1, :])], axis=0)
    cols = cols + (shifted - cols) * mu_ref[...]
    r = cols[:, 0:W]
    k = cols[:, W:2 * W]
    v = cols[:, 2 * W:3 * W]
    w_lr = cols[:, 3 * W:3 * W + 64]
    a_lr = cols[:, 3 * W + 64:3 * W + 128]
    g_lr = cols[:, 3 * W + 128:3 * W + 256]

    wv = -_softplus(-(w0_ref[...] + _dot_hi(jnp.tanh(w_lr), w2_ref[...]))) - 0.5
    lw = -jnp.exp(wv)
    a = jax.nn.sigmoid(a0_ref[...] + _dot_hi(a_lr, a2_ref[...]))
    g = _dot_hi(jax.nn.sigmoid(g_lr), g2_ref[...])

    kk = k * kk_ref[...]
    k2 = k * (1.0 + (a - 1.0) * ka_ref[...])
    head_sums = _dot_sel_rhs(jnp.concatenate([kk * kk, r * k2 * rk_ref[...]], axis=0), bones, 2)
    kk = kk / jnp.maximum(jnp.sqrt(head_sums[:nr]), 1e-12)
    bonus_v = head_sums[nr:] * v
    bvec = kk * a

    rb_i, cb_i = _iota((nr, nr), 0), _iota((nr, nr), 1)
    tri = ((rb_i >= cb_i) & (rb_i // C == cb_i // C)).astype(bf16)
    cum = _dot_sel_lhs(tri, lw, 3)
    cum_last = [cum[(b + 1) * C - 1:(b + 1) * C, :] for b in rows]
    e_neg = jnp.exp(-cum)
    e_rem = jnp.exp(jnp.concatenate([jnp.broadcast_to(cl, (C, W)) for cl in cum_last], axis=0) - cum)
    per_token = dict(aw=-kk * jnp.exp(cum - lw), rw=r * jnp.exp(cum), bn=bvec * e_neg, kn=k2 * e_neg,
                     bt=bvec * e_rem, kt=k2 * e_rem, vs=v)
    chains = [(b, p) for b in rows for p in range(npair)]
    q = [dict({name: stack(t[b * C:(b + 1) * C, p * LANES:(p + 1) * LANES]) for name, t in per_token.items()},
              wc=jnp.exp(cum_last[b][:, p * LANES:(p + 1) * LANES])) for b, p in chains]
    ids = range(len(chains))

    a_ab = [jnp.where(strict, _dot_nt(q[c]["aw"], q[c]["bn"]), 0.0) for c in ids]
    a_ak = [jnp.where(strict, _dot_nt(q[c]["aw"], q[c]["kn"]), 0.0) for c in ids]
    a_rb = [jnp.where(lower, _dot_nt(q[c]["rw"], q[c]["bn"]), 0.0) for c in ids]
    a_rk = [jnp.where(lower, _dot_nt(q[c]["rw"], q[c]["kn"]), 0.0) for c in ids]

    pw = a_ab
    tinv = [eye + a_ab[c] for c in ids]
    for _ in range(int(math.log2(C)) - 1):
        pw = [_dot(pw[c], pw[c]) for c in ids]
        tinv = [tinv[c] + _dot(tinv[c], pw[c]) for c in ids]

    zt = [zt_ref[b, p] for b, p in chains]
    rhs = [_dot_nt(q[c]["aw"], zt[c]) + _dot(a_ak[c], q[c]["vs"]) for c in ids]
    y0 = [_dot_nt(q[c]["rw"], zt[c]) + _dot(a_rk[c], q[c]["vs"]) for c in ids]
    u = [_dot(tinv[c], rhs[c]) for c in ids]
    y_st = [y0[c] + _dot(a_rb[c], u[c]) for c in ids]
    for c, (b, p) in enumerate(chains):
        zt_ref[b, p] = zt[c] * q[c]["wc"] + _dot_tn(u[c], q[c]["bt"]) + _dot_tn(q[c]["vs"], q[c]["kt"])

    inv_n = 1.0 / HEAD_DIM
    ys = []
    for b in rows:
        tiles = []
        for p in range(npair):
            y_c = y_st[b * npair + p]
            y = y_c[0:C]
            for h in range(1, H):
                y = y + y_c[h * C:(h + 1) * C]
            tiles.append(y)
        ys.append(jnp.concatenate(tiles, axis=1))
    y = jnp.concatenate(ys, axis=0)
    yc = y - _dot_sel_rhs(y, bones, 2) * inv_n
    var_h = _dot_sel_rhs(yc * yc, bones, 2) * inv_n
    y = yc * lax.rsqrt(var_h + RWKV_GN_EPS) * lng_ref[...] + lnb_ref[...]
    o_ref[...] = ((y + bonus_v) * g).reshape(nb, C, W)


def _rwkv(pr, mu, w0, w2, a0, a2, g2, k_k, k_a, r_k, ln_g, ln_b):
    nb, t, cols = pr.shape
    C = RWKV_CHUNK
    full2 = lambda a: pl.BlockSpec(a.shape, lambda i: (0, 0))
    params = (mu, w0, w2, a0, a2, g2, k_k, k_a, r_k, ln_g, ln_b)
    return pl.pallas_call(
        _rwkv_kernel,
        grid=(t // C,),
        in_specs=[
            pl.BlockSpec((nb, C, cols), lambda i: (0, i, 0)),
            pl.BlockSpec((nb, 8, cols), lambda i: (0, jnp.maximum(i * (C // 8) - 1, 0), 0)),
        ] + [full2(p) for p in params],
        out_specs=pl.BlockSpec((nb, C, RWKV_WIDTH), lambda i: (0, i, 0)),
        out_shape=jax.ShapeDtypeStruct((nb, t, RWKV_WIDTH), f32),
        scratch_shapes=[pltpu.VMEM((nb, RWKV_WIDTH // LANES, LANES, LANES), f32)],
        compiler_params=_cparams(("arbitrary",)),
        name="rwkv7_mixer",
    )(pr, pr, *params)


def _route(logits, rb):
    tm = logits.shape[1]
    scores = jax.nn.sigmoid(logits)
    sel = scores + rb
    srow = [sel[e:e + 1, :] for e in range(N_EXPERTS)]
    crow = [scores[e:e + 1, :] for e in range(N_EXPERTS)]

    def top2_sum(v0, v1, v2, v3):
        m01, n01 = jnp.maximum(v0, v1), jnp.minimum(v0, v1)
        m23, n23 = jnp.maximum(v2, v3), jnp.minimum(v2, v3)
        return jnp.maximum(m01, m23) + jnp.maximum(jnp.minimum(m01, m23), jnp.maximum(n01, n23))

    gscore = [top2_sum(*srow[4 * gi:4 * gi + 4]) for gi in range(N_EXPERT_GROUPS)]
    best = jnp.zeros((1, tm), jnp.int32)
    best_v = gscore[0]
    for gi in range(1, N_EXPERT_GROUPS):
        better = gscore[gi] > best_v
        best = jnp.where(better, gi, best)
        best_v = jnp.where(better, gscore[gi], best_v)
    sv, cv = [], []
    for j in range(EXPERTS_PER_GROUP):
        s_j, c_j = srow[j], crow[j]
        for gi in range(1, N_EXPERT_GROUPS):
            s_j = jnp.where(best == gi, srow[4 * gi + j], s_j)
            c_j = jnp.where(best == gi, crow[4 * gi + j], c_j)
        sv.append(s_j)
        cv.append(c_j)
    j1 = jnp.zeros((1, tm), jnp.int32)
    v1 = sv[0]
    for j in range(1, EXPERTS_PER_GROUP):
        better = sv[j] > v1
        j1 = jnp.where(better, j, j1)
        v1 = jnp.where(better, sv[j], v1)
    j2 = jnp.full((1, tm), -1, jnp.int32)
    v2 = jnp.full((1, tm), -jnp.inf, f32)
    for j in range(EXPERTS_PER_GROUP):
        better = (j1 != j) & ((j2 < 0) | (sv[j] > v2))
        j2 = jnp.where(better, j, j2)
        v2 = jnp.where(better, sv[j], v2)
    w1 = cv[0]
    w2 = cv[0]
    for j in range(1, EXPERTS_PER_GROUP):
        w1 = jnp.where(j1 == j, cv[j], w1)
        w2 = jnp.where(j2 == j, cv[j], w2)
    wsum = w1 + w2
    g1, g2 = w1 / wsum, w2 / wsum
    rows = []
    for e in range(N_EXPERTS):
        gi, j = divmod(e, EXPERTS_PER_GROUP)
        in_g = best == gi
        rows.append(jnp.where(in_g & (j1 == j), g1, jnp.where(in_g & (j2 == j), g2, 0.0)))
    gate_t = jnp.concatenate(rows, axis=0)
    return gate_t, best


def _outproj_kernel(alpha, x_ref, og_ref, os_ref, or_ref, wo_ref, g_ref, b_ref, rwt_ref, rb_ref,
                    x1_ref, gate_ref, best_ref, cnt_ref):
    tm = x_ref.shape[0]
    ts = tm // cnt_ref.shape[0]
    parts = [slice(sb * ts, (sb + 1) * ts) for sb in range(cnt_ref.shape[0])]
    d = functools.partial(jnp.dot, preferred_element_type=f32)
    mix = [d(_bf(og_ref[s, :]), wo_ref[0:GLA_WIDTH, :])
           + d(_bf(os_ref[s, :]), wo_ref[GLA_WIDTH:GLA_WIDTH + SSD_WIDTH, :])
           + d(_bf(or_ref[s, :]), wo_ref[GLA_WIDTH + SSD_WIDTH:, :]) for s in parts]
    x1 = [_layer_norm(alpha * x_ref[s, :] + m, g_ref[...], b_ref[...], LN_EPS) for s, m in zip(parts, mix)]
    for s, v in zip(parts, x1):
        x1_ref[s, :] = v

    wh, wl = _split(rwt_ref[...], 2)
    nt = lambda a, b: lax.dot_general(a, b, (((1,), (1,)), ((), ())), preferred_element_type=f32)
    logits = []
    for v in x1:
        xh, xl = _split(v, 2)
        logits.append(nt(wh, xh) + nt(wl, xh) + nt(wh, xl))
    routed = [_route(lg, rb_ref[...]) for lg in logits]
    eye = (_iota((N_EXPERTS, LANES), 0) == _iota((N_EXPERTS, LANES), 1)).astype(bf16)
    lane = _iota((1, LANES), 1)
    for sb, (s, (gate_t, best)) in enumerate(zip(parts, routed)):
        gate = None
        for p in _split(gate_t, 3):
            t = lax.dot_general(p, eye, (((0,), (0,)), ((), ())), preferred_element_type=f32)
            gate = t if gate is None else gate + t
        gate_ref[s, :] = gate
        best_ref[:, s] = best
        cnt = jnp.zeros((1, LANES), jnp.int32)
        for gi in range(N_EXPERT_GROUPS):
            n_g = jnp.sum((best == gi).astype(jnp.int32), axis=-1, keepdims=True)
            cnt = jnp.where(lane == gi, n_g, cnt)
        cnt_ref[sb] = cnt


def _out_proj(alpha, x2, og, os_, or_, wo, g, b, rwt, rb):
    n, d = x2.shape
    tm = min(OUT_PROJ_ROWS, n)
    ts = min(MOE_SORT_ROWS, tm)
    full2 = lambda a: pl.BlockSpec(a.shape, lambda i: (0, 0))
    rows = lambda c: pl.BlockSpec((tm, c), lambda i: (i, 0))
    return pl.pallas_call(
        functools.partial(_outproj_kernel, alpha),
        grid=(n // tm,),
        in_specs=[rows(d), rows(og.shape[1]), rows(os_.shape[1]), rows(or_.shape[1]),
                  full2(wo), full2(g), full2(b), full2(rwt), full2(rb)],
        out_specs=[rows(d), rows(LANES), pl.BlockSpec((1, tm), lambda i: (0, i)),
                   pl.BlockSpec((tm // ts, 1, LANES), lambda i: (i, 0, 0))],
        out_shape=[jax.ShapeDtypeStruct((n, d), f32), jax.ShapeDtypeStruct((n, LANES), f32),
                   jax.ShapeDtypeStruct((1, n), jnp.int32), jax.ShapeDtypeStruct((n // ts, 1, LANES), jnp.int32)],
        compiler_params=_cparams(("arbitrary",)),
        name="out_proj_router",
    )(x2, og, os_, or_, wo, g, b, rwt, rb)


def _seg_copy(src, dst, sem, src_off, dst_off, rows, max_rows, wait):
    k = SEG_ALIGN
    sizes = []
    while k <= max_rows:
        sizes.append(k)
        k *= 2
    for k in reversed(sizes):
        shift = int(math.log2(k)) + 1

        @pl.when((rows & k) != 0)
        def _():
            done = (rows >> shift) << shift
            cp = pltpu.make_async_copy(
                src.at[pl.ds(pl.multiple_of(src_off + done, SEG_ALIGN), k)],
                dst.at[pl.ds(pl.multiple_of(dst_off + done, SEG_ALIGN), k)], sem)
            if wait:
                cp.wait()
            else:
                cp.start()


def _padded(count):
    return (count + (SEG_ALIGN - 1)) & (-SEG_ALIGN)


def _one_hot_rows(pos, n_rows):
    return jnp.where(_iota((n_rows, pos.shape[1]), 0) == pos, 1.0, 0.0).astype(bf16)


def _dispatch_kernel(cnt_sm, goff_sm, fill_sm, x_ref, gate_ref, best_ref, xs_out, gs_out, pos_ref,
                     xloc, gloc, zx, zg, sems):
    i = pl.program_id(0)
    tm = x_ref.shape[0]
    n_loc = xloc.shape[1]
    slot = i % 2

    def copies(blk, slot_, wait):
        loff = 0
        for gi in range(N_EXPERT_GROUPS):
            rows = _padded(cnt_sm[blk * N_EXPERT_GROUPS + gi])
            dst = goff_sm[blk * N_EXPERT_GROUPS + gi]
            _seg_copy(xloc.at[slot_], xs_out, sems.at[0, slot_], loff, dst, rows, tm, wait)
            _seg_copy(gloc.at[slot_], gs_out, sems.at[1, slot_], loff, dst, rows, tm, wait)
            loff = loff + rows

    best = best_ref[...]
    onehot = (_iota((8, tm), 0) == best).astype(bf16)
    earlier = (_iota((tm, tm), 0) < _iota((tm, tm), 1)).astype(bf16)
    rank = jnp.dot(onehot, earlier, preferred_element_type=f32)
    pos = jnp.zeros((1, tm), f32)
    loff = 0
    for gi in range(N_EXPERT_GROUPS):
        pos = jnp.where(best == gi, rank[gi:gi + 1, :] + loff.astype(f32) if gi else rank[gi:gi + 1, :], pos)
        loff = loff + _padded(cnt_sm[i * N_EXPERT_GROUPS + gi])
    pos = pos.astype(jnp.int32)
    pos_ref[...] = pos
    perm = _one_hot_rows(pos, n_loc)
    xloc[slot] = jnp.dot(perm, _bf(x_ref[...]), preferred_element_type=f32)
    gloc[slot] = _dot_sel_lhs(perm, gate_ref[...], 3)
    copies(i, slot, False)

    @pl.when(i > 0)
    def _():
        copies(i - 1, 1 - slot, True)

    @pl.when(i == pl.num_programs(0) - 1)
    def _():
        copies(i, slot, True)
        zx[...] = jnp.zeros_like(zx)
        zg[...] = jnp.zeros_like(zg)
        tile = zx.shape[0]

        def fills(wait):
            for src, dst, sem in ((zx, xs_out, sems.at[2, 0]), (zg, gs_out, sems.at[2, 1])):
                for gi in range(N_EXPERT_GROUPS):
                    _seg_copy(src, dst, sem, 0, fill_sm[gi], fill_sm[N_EXPERT_GROUPS + gi], tile // 2, wait)

                def body(j, carry):
                    off = pl.multiple_of(fill_sm[2 * N_EXPERT_GROUPS] + j * tile, SEG_ALIGN)
                    cp = pltpu.make_async_copy(src, dst.at[pl.ds(off, tile)], sem)
                    if wait:
                        cp.wait()
                    else:
                        cp.start()
                    return carry

                lax.fori_loop(0, fill_sm[2 * N_EXPERT_GROUPS + 1], body, 0)

        fills(False)
        fills(True)


def _experts_kernel(tg_sm, tv_sm, x_ref, gate_ref, wg_ref, wu_ref, wd_ref, y_ref):
    t = pl.program_id(0)

    @pl.when(tv_sm[t] == 0)
    def _():
        y_ref[...] = jnp.zeros_like(y_ref)

    @pl.when(tv_sm[t] > 0)
    def _():
        first = tg_sm[t] * EXPERTS_PER_GROUP
        xb = _bf(x_ref[...])
        gate = gate_ref[...]
        lane = _iota(gate.shape, 1)
        hs = []
        for j in range(EXPERTS_PER_GROUP):
            hg = jnp.dot(xb, wg_ref[j], preferred_element_type=f32)
            hu = jnp.dot(xb, wu_ref[j], preferred_element_type=f32)
            gcol = jnp.sum(jnp.where(lane == first + j, gate, 0.0), axis=-1, keepdims=True)
            hs.append(_bf(_silu(hg) * hu * gcol))
        ff = wd_ref.shape[1]
        y_ref[...] = jnp.dot(jnp.concatenate(hs, axis=1), wd_ref[...].reshape(EXPERTS_PER_GROUP * ff, -1),
                             preferred_element_type=f32)


def _combine_kernel(alpha, cnt_sm, goff_sm, x_ref, pos_ref, g_ref, b_ref, ys_ref, o_ref, yloc, sems):
    i = pl.program_id(0)
    tm = x_ref.shape[0]
    n_loc = yloc.shape[1]
    slot = i % 2

    def copies(blk, slot_, wait):
        loff = 0
        for gi in range(N_EXPERT_GROUPS):
            rows = _padded(cnt_sm[blk * N_EXPERT_GROUPS + gi])
            src = goff_sm[blk * N_EXPERT_GROUPS + gi]
            _seg_copy(ys_ref, yloc.at[slot_], sems.at[slot_], src, loff, rows, tm, wait)
            loff = loff + rows

    @pl.when(i == 0)
    def _():
        yloc[...] = jnp.zeros_like(yloc)
        copies(0, 0, False)

    @pl.when(i + 1 < pl.num_programs(0))
    def _():
        copies(i + 1, 1 - slot, False)

    copies(i, slot, True)
    perm = _one_hot_rows(pos_ref[...], n_loc)
    y = lax.dot_general(perm, _bf(yloc[slot]), (((0,), (0,)), ((), ())), preferred_element_type=f32)
    o_ref[...] = _layer_norm(alpha * x_ref[...] + y, g_ref[...], b_ref[...], LN_EPS)


def _moe_tables(cnt, n_tiles, tile):
    pc = (cnt + (SEG_ALIGN - 1)) // SEG_ALIGN * SEG_ALIGN
    gtot = jnp.sum(pc, axis=0)
    gcap = (gtot + tile - 1) // tile * tile
    gend = jnp.cumsum(gcap)
    gstart = gend - gcap
    goff = gstart[None, :] + jnp.cumsum(pc, axis=0) - pc
    tstart = jnp.arange(n_tiles, dtype=jnp.int32) * tile
    tg = jnp.minimum(jnp.sum((tstart[:, None] >= gend[None, :]).astype(jnp.int32), axis=1), N_EXPERT_GROUPS - 1)
    tv = jnp.clip(gtot[tg] - (tstart - gstart[tg]), 0, tile)
    fill = jnp.concatenate([gstart + gtot, gcap - gtot, gend[-1:], n_tiles - gend[-1:] // tile])
    return goff.reshape(-1).astype(jnp.int32), tg.astype(jnp.int32), tv.astype(jnp.int32), fill.astype(jnp.int32)


def _moe(alpha, x1, gate, best, cnt, wg, wu, wd, g, b):
    n, d = x1.shape
    ne, _, ff = wg.shape
    nblk = cnt.shape[0]
    tm = n // nblk
    tile = MOE_TILE
    n_loc = tm + LANES
    n_sorted = -(-(n + nblk * N_EXPERT_GROUPS * SEG_ALIGN + N_EXPERT_GROUPS * tile) // tile) * tile
    n_tiles = n_sorted // tile
    cnt_flat = cnt[:, 0, :N_EXPERT_GROUPS].reshape(-1)
    goff, tg, tv, fill = _moe_tables(cnt[:, 0, :N_EXPERT_GROUPS], n_tiles, tile)
    any_spec = pl.BlockSpec(memory_space=pl.ANY)

    xs, gs, pos = pl.pallas_call(
        _dispatch_kernel,
        grid_spec=pltpu.PrefetchScalarGridSpec(
            num_scalar_prefetch=3,
            grid=(nblk,),
            in_specs=[
                pl.BlockSpec((tm, d), lambda i, *_: (i, 0)),
                pl.BlockSpec((tm, LANES), lambda i, *_: (i, 0)),
                pl.BlockSpec((1, tm), lambda i, *_: (0, i)),
            ],
            out_specs=[any_spec, any_spec, pl.BlockSpec((1, tm), lambda i, *_: (0, i))],
            scratch_shapes=[pltpu.VMEM((2, n_loc, d), f32), pltpu.VMEM((2, n_loc, LANES), f32),
                            pltpu.VMEM((tile, d), f32), pltpu.VMEM((tile, LANES), f32),
                            pltpu.SemaphoreType.DMA((3, 2))],
        ),
        out_shape=[jax.ShapeDtypeStruct((n_sorted, d), f32), jax.ShapeDtypeStruct((n_sorted, LANES), f32),
                   jax.ShapeDtypeStruct((1, n), jnp.int32)],
        compiler_params=_cparams(("arbitrary",)),
        name="moe_dispatch",
    )(cnt_flat, goff, fill, x1, gate, best)

    ys = pl.pallas_call(
        _experts_kernel,
        grid_spec=pltpu.PrefetchScalarGridSpec(
            num_scalar_prefetch=2,
            grid=(n_tiles,),
            in_specs=[
                pl.BlockSpec((tile, d), lambda t, *_: (t, 0)),
                pl.BlockSpec((tile, LANES), lambda t, *_: (t, 0)),
                pl.BlockSpec((EXPERTS_PER_GROUP, d, ff), lambda t, tg_, tv_: (tg_[t], 0, 0)),
                pl.BlockSpec((EXPERTS_PER_GROUP, d, ff), lambda t, tg_, tv_: (tg_[t], 0, 0)),
                pl.BlockSpec((EXPERTS_PER_GROUP, ff, d), lambda t, tg_, tv_: (tg_[t], 0, 0)),
            ],
            out_specs=pl.BlockSpec((tile, d), lambda t, *_: (t, 0)),
        ),
        out_shape=jax.ShapeDtypeStruct((n_sorted, d), f32),
        compiler_params=_cparams(("arbitrary",)),
        name="moe_experts",
    )(tg, tv, xs, gs, wg, wu, wd)

    return pl.pallas_call(
        functools.partial(_combine_kernel, alpha),
        grid_spec=pltpu.PrefetchScalarGridSpec(
            num_scalar_prefetch=2,
            grid=(nblk,),
            in_specs=[
                pl.BlockSpec((tm, d), lambda i, *_: (i, 0)),
                pl.BlockSpec((1, tm), lambda i, *_: (0, i)),
                pl.BlockSpec(g.shape, lambda i, *_: (0, 0)),
                pl.BlockSpec(b.shape, lambda i, *_: (0, 0)),
                any_spec,
            ],
            out_specs=pl.BlockSpec((tm, d), lambda i, *_: (i, 0)),
            scratch_shapes=[pltpu.VMEM((2, n_loc, d), f32), pltpu.SemaphoreType.DMA((2,))],
        ),
        out_shape=jax.ShapeDtypeStruct((n, d), f32),
        compiler_params=_cparams(("arbitrary",)),
        name="moe_combine",
    )(cnt_flat, goff, x1, pos, g, b, ys)


def _pad_cols(w, n):
    return jnp.pad(w, ((0, 0), (0, n - w.shape[1])))


def _row(v, n=None):
    v = v.reshape(1, -1)
    return v if n is None else _pad_cols(v, n)


def kernel(x, w_in, gla_w_alpha, gla_b_alpha, gla_norm_g, ssd_conv_w, ssd_conv_b, ssd_dt_bias, ssd_A_log, ssd_D, ssd_norm_g, rwkv_mu, rwkv_w0, rwkv_w2, rwkv_a0, rwkv_a2, rwkv_g2, rwkv_k_k, rwkv_k_a, rwkv_r_k, rwkv_ln_g, rwkv_ln_b, w_out, ln1_g, ln1_b, router_w, router_b, exp_w_gate, exp_w_up, exp_w_down, ln2_g, ln2_b):
    nb, t, d = x.shape
    depth = w_in.shape[0]
    alpha = float((2 * depth) ** 0.25)
    gla_cols = 4 * GLA_WIDTH + GLA_GATE_RANK
    ssd_cols = SSD_WIDTH + SSD_XBC + SSD_HEADS
    rwt = router_w.T
    rb = router_b.reshape(-1, 1)
    x2 = x.reshape(nb * t, d)
    for l in range(depth):
        wg, ws, wr = _regroup_w_in(w_in, l, (gla_cols, ssd_cols, RWKV_COLS), (GLA_COLS_PAD, SSD_COLS_PAD, RWKV_COLS))
        pg, ps, pr = _in_proj(x2, wg, ws, wr)
        og = _gla(pg.reshape(nb, t, -1),
                  jnp.pad(gla_w_alpha[l], ((0, LANES - GLA_GATE_RANK), (0, 0))),
                  _row(gla_b_alpha[l]), _row(jnp.tile(gla_norm_g[l], GLA_WIDTH // HEAD_DIM)))
        os_ = _ssd(ps.reshape(nb, t, -1), ssd_conv_w[l], _row(ssd_conv_b[l]), _row(ssd_dt_bias[l], LANES),
                   _row(ssd_A_log[l], LANES), _row(jnp.repeat(ssd_D[l], HEAD_DIM)), _row(ssd_norm_g[l]))
        or_ = _rwkv(pr.reshape(nb, t, -1), _row(rwkv_mu[l]), _row(rwkv_w0[l]), rwkv_w2[l], _row(rwkv_a0[l]),
                    rwkv_a2[l], rwkv_g2[l], _row(rwkv_k_k[l]), _row(rwkv_k_a[l]), _row(rwkv_r_k[l]),
                    _row(rwkv_ln_g[l]), _row(rwkv_ln_b[l]))
        x1, gate, best, cnt = _out_proj(alpha, x2, og.reshape(nb * t, -1), os_.reshape(nb * t, -1),
                                        or_.reshape(nb * t, -1), _bf(w_out[l]), _row(ln1_g[l]), _row(ln1_b[l]),
                                        rwt, rb)
        x2 = _moe(alpha, x1, gate, best, cnt, _bf(exp_w_gate[l]), _bf(exp_w_up[l]), _bf(exp_w_down[l]),
                  _row(ln2_g[l]), _row(ln2_b[l]))
    return x2.reshape(nb, t, d)
```

```python
import functools
import math

import jax
import jax.numpy as jnp
from jax import lax
from jax.experimental import pallas as pl
from jax.experimental.pallas import tpu as pltpu

f32 = jnp.float32
bf16 = jnp.bfloat16

LANES = 128
HEAD_DIM = 64
GLA_WIDTH = 256
GLA_GATE_RANK = 16
GLA_GATE_TAU = 16.0
GLA_CHUNK = 16
GLA_BLOCK = 128
GLA_COLS_PAD = 4 * GLA_WIDTH + LANES
SSD_WIDTH = 512
SSD_HEADS = 8
SSD_GROUPS = 2
SSD_STATE = 64
SSD_CONV = 4
SSD_XBC = SSD_WIDTH + 2 * SSD_GROUPS * SSD_STATE
SSD_BLOCK = 128
SSD_COLS_PAD = SSD_WIDTH + SSD_XBC + LANES
RWKV_WIDTH = 256
RWKV_HEADS = 4
RWKV_CHUNK = 64
RWKV_GN_EPS = 64e-5
RWKV_COLS = 3 * RWKV_WIDTH + 64 + 64 + 128
N_EXPERTS = 16
N_EXPERT_GROUPS = 4
EXPERTS_PER_GROUP = 4
LN_EPS = 1e-5

IN_PROJ_ROWS = 512
OUT_PROJ_ROWS = 1024
MOE_SORT_ROWS = 512
MOE_TILE = 512
SEG_ALIGN = 8
VMEM_LIMIT = 56 * 1024 * 1024


def _bf(x):
    return x.astype(bf16)


def _dot(a, b):
    return jnp.dot(_bf(a), _bf(b), preferred_element_type=f32)


def _dot_nt(a, b):
    return lax.dot_general(_bf(a), _bf(b), (((1,), (1,)), ((), ())), preferred_element_type=f32)


def _dot_tn(a, b):
    return lax.dot_general(_bf(a), _bf(b), (((0,), (0,)), ((), ())), preferred_element_type=f32)


def _split(x, n):
    parts, r = [], x
    for _ in range(n):
        p = r.astype(bf16)
        parts.append(p)
        r = r - p.astype(f32)
    return parts


def _dot_sel_rhs(a, sel, n=3):
    out = None
    for p in _split(a, n):
        t = jnp.dot(p, sel, preferred_element_type=f32)
        out = t if out is None else out + t
    return out


def _dot_sel_lhs(sel, b, n=3):
    out = None
    for p in _split(b, n):
        t = jnp.dot(sel, p, preferred_element_type=f32)
        out = t if out is None else out + t
    return out


def _dot_hi(a, b):
    ah, al = _split(a, 2)
    bh, bl = _split(b, 2)
    d = functools.partial(jnp.dot, preferred_element_type=f32)
    return d(ah, bh) + d(al, bh) + d(ah, bl)


def _iota(shape, dim):
    return lax.broadcasted_iota(jnp.int32, shape, dim)


def _block_ones(n, blk):
    return (_iota((n, n), 0) // blk == _iota((n, n), 1) // blk)


def _silu(x):
    return x * jax.nn.sigmoid(x)


def _softplus(x):
    return jnp.maximum(x, 0.0) + jnp.log1p(jnp.exp(-jnp.abs(x)))


def _layer_norm(y, g, b, eps):
    mu = jnp.mean(y, axis=-1, keepdims=True)
    yc = y - mu
    var = jnp.mean(yc * yc, axis=-1, keepdims=True)
    return yc * lax.rsqrt(var + eps) * g + b


def _cparams(sem):
    return pltpu.CompilerParams(dimension_semantics=sem, vmem_limit_bytes=VMEM_LIMIT)


def _inproj_kernel(x_ref, wg_ref, ws_ref, wr_ref, og_ref, os_ref, or_ref):
    xb = _bf(x_ref[...])
    og_ref[...] = jnp.dot(xb, wg_ref[...], preferred_element_type=f32)
    os_ref[...] = jnp.dot(xb, ws_ref[...], preferred_element_type=f32)
    or_ref[...] = jnp.dot(xb, wr_ref[...], preferred_element_type=f32)


def _regroup_kernel(splits, w_ref, *o_refs):
    w = w_ref[0]
    start = 0
    for width, o_ref in zip(splits, o_refs):
        part = w[:, start:start + width]
        pad = o_ref.shape[1] - width
        if pad:
            part = jnp.concatenate([part, jnp.zeros((part.shape[0], pad), part.dtype)], axis=1)
        o_ref[...] = _bf(part)
        start += width


def _regroup_w_in(w_in, layer, splits, padded):
    _, d, n_in = w_in.shape
    rows = 256
    return pl.pallas_call(
        functools.partial(_regroup_kernel, splits),
        grid=(d // rows,),
        in_specs=[pl.BlockSpec((1, rows, n_in), lambda i: (layer, i, 0))],
        out_specs=[pl.BlockSpec((rows, p), lambda i: (i, 0)) for p in padded],
        out_shape=[jax.ShapeDtypeStruct((d, p), bf16) for p in padded],
        compiler_params=_cparams(("arbitrary",)),
        name="regroup_w_in",
    )(w_in)


def _in_proj(x2, wg, ws, wr):
    n, d = x2.shape
    tm = min(IN_PROJ_ROWS, n)
    full = lambda w: pl.BlockSpec(w.shape, lambda i: (0, 0))
    rows = lambda c: pl.BlockSpec((tm, c), lambda i: (i, 0))
    return pl.pallas_call(
        _inproj_kernel,
        grid=(n // tm,),
        in_specs=[rows(d), full(wg), full(ws), full(wr)],
        out_specs=[rows(wg.shape[1]), rows(ws.shape[1]), rows(wr.shape[1])],
        out_shape=[jax.ShapeDtypeStruct((n, w.shape[1]), f32) for w in (wg, ws, wr)],
        compiler_params=_cparams(("arbitrary",)),
        name="in_proj",
    )(x2, wg, ws, wr)


def _gla_kernel(p_ref, wa_ref, ba_ref, ng_ref, o_ref, st_ref, t_ref, d_ref, sx_ref):
    nb, tb, _ = p_ref.shape
    w, c = GLA_WIDTH, GLA_CHUNK
    nc = tb // c
    npt = w // LANES
    Q, K, V, CUM, OUT = range(5)

    @pl.when(pl.program_id(0) == 0)
    def _():
        st_ref[...] = jnp.zeros_like(st_ref)

    bones = _block_ones(w, HEAD_DIM).astype(bf16)
    pair_blk = _block_ones(LANES, HEAD_DIM)
    r_i, c_i = _iota((tb, tb), 0), _iota((tb, tb), 1)
    tri = ((r_i >= c_i) & (r_i // c == c_i // c)).astype(bf16)
    wa_h, wa_l = _split(wa_ref[...], 2)
    dd = functools.partial(jnp.dot, preferred_element_type=f32)

    for b in range(nb):
        a_h, a_l = _split(p_ref[b, :, 4 * w:4 * w + LANES], 2)
        z = dd(a_h, wa_h) + dd(a_l, wa_h) + dd(a_h, wa_l) + ba_ref[...]
        log_a = -_softplus(-z) * (1.0 / GLA_GATE_TAU)
        cum = _dot_sel_lhs(tri, log_a, 3)
        for lt in range(npt):
            lanes = slice(lt * LANES, (lt + 1) * LANES)
            t_ref[b, Q, lt] = p_ref[b, :, lanes] * (HEAD_DIM ** -0.5)
            t_ref[b, K, lt] = p_ref[b, :, w + lt * LANES:w + (lt + 1) * LANES]
            t_ref[b, V, lt] = p_ref[b, :, 2 * w + lt * LANES:2 * w + (lt + 1) * LANES]
            t_ref[b, CUM, lt] = cum[:, lanes]

    def ld(b, j, off):
        return jnp.concatenate([t_ref[b, j, lt, pl.ds(off, nc, stride=c), :] for lt in range(npt)], axis=1)

    for b in range(nb):
        qt = [ld(b, Q, o) for o in range(c)]
        kt = [ld(b, K, o) for o in range(c)]
        ct = [ld(b, CUM, o) for o in range(c)]
        pair = 0
        for t in range(c):
            for s in range(t + 1):
                qk = qt[t] * kt[s]
                d_ref[b, pair * nc:(pair + 1) * nc, :] = qk if s == t else qk * jnp.exp(ct[t] - ct[s])
                pair += 1
    for b in range(nb):
        sx_ref[b] = jnp.dot(_bf(d_ref[b]), bones, preferred_element_type=f32)
    for b in range(nb):
        vt = [ld(b, V, o) for o in range(c)]
        pair = 0
        for t in range(c):
            acc = None
            for s in range(t + 1):
                term = sx_ref[b, pair * nc:(pair + 1) * nc, :] * vt[s]
                acc = term if acc is None else acc + term
                pair += 1
            for lt in range(npt):
                t_ref[b, OUT, lt, pl.ds(t, nc, stride=c), :] = acc[:, lt * LANES:(lt + 1) * LANES]

    for ci in range(nc):
        sl = slice(ci * c, (ci + 1) * c)
        last = slice((ci + 1) * c - 1, (ci + 1) * c)
        for b in range(nb):
            for lt in range(npt):
                cum_c = t_ref[b, CUM, lt, sl, :]
                cum_last = t_ref[b, CUM, lt, last, :]
                st = st_ref[b, lt]
                t_ref[b, OUT, lt, sl, :] += _dot_nt(t_ref[b, Q, lt, sl, :] * jnp.exp(cum_c), st)
                upd = _dot_tn(t_ref[b, V, lt, sl, :], t_ref[b, K, lt, sl, :] * jnp.exp(cum_last - cum_c))
                st_ref[b, lt] = st * jnp.exp(cum_last) + jnp.where(pair_blk, upd, 0.0)

    o = jnp.concatenate(
        [jnp.concatenate([t_ref[b, OUT, lt] for lt in range(npt)], axis=1) for b in range(nb)], axis=0)
    ms = _dot_sel_rhs(o * o, bones, 2) * (1.0 / HEAD_DIM)
    o = o * lax.rsqrt(ms + LN_EPS) * ng_ref[...]
    for b in range(nb):
        o_ref[b] = o[b * tb:(b + 1) * tb] * _silu(p_ref[b, :, 3 * w:4 * w])


def _gla(pg, wa, ba, ng):
    nb, t, cols = pg.shape
    tb = GLA_BLOCK
    npt = GLA_WIDTH // LANES
    pair_rows = (GLA_CHUNK * (GLA_CHUNK + 1) // 2) * (tb // GLA_CHUNK)
    full2 = lambda a: pl.BlockSpec(a.shape, lambda i: (0, 0))
    return pl.pallas_call(
        _gla_kernel,
        grid=(t // tb,),
        in_specs=[pl.BlockSpec((nb, tb, cols), lambda i: (0, i, 0)), full2(wa), full2(ba), full2(ng)],
        out_specs=pl.BlockSpec((nb, tb, GLA_WIDTH), lambda i: (0, i, 0)),
        out_shape=jax.ShapeDtypeStruct((nb, t, GLA_WIDTH), f32),
        scratch_shapes=[
            pltpu.VMEM((nb, npt, LANES, LANES), f32),
            pltpu.VMEM((nb, 5, npt, tb, LANES), f32),
            pltpu.VMEM((nb, pair_rows, GLA_WIDTH), f32),
            pltpu.VMEM((nb, pair_rows, GLA_WIDTH), f32),
        ],
        compiler_params=_cparams(("arbitrary",)),
        name="gla_mixer",
    )(pg, wa, ba, ng)


def _ssd_kernel(p_ref, prev_ref, cw_ref, cb_ref, dtb_ref, alog_ref, dsk_ref, ng_ref, o_ref, st_ref):
    _, L, _ = p_ref.shape
    W, M, G = SSD_WIDTH, SSD_STATE, SSD_GROUPS
    gw = W // G
    i = pl.program_id(1)

    @pl.when(i == 0)
    def _():
        st_ref[...] = jnp.zeros_like(st_ref)

    z = p_ref[0, :, 0:W]
    xbc = p_ref[0, :, W:W + SSD_XBC]
    dt_raw = p_ref[0, :, W + SSD_XBC:W + SSD_XBC + LANES]
    prev = jnp.where(i > 0, prev_ref[0, :, W:W + SSD_XBC], 0.0)
    xx = jnp.concatenate([prev, xbc], axis=0)
    conv = cb_ref[...]
    for j in range(SSD_CONV):
        off = 8 - (SSD_CONV - 1) + j
        conv = conv + cw_ref[j:j + 1, :] * xx[off:off + L]
    act = _silu(conv)
    xs = act[:, 0:W]
    bm = act[:, W:W + G * M]
    cm = act[:, W + G * M:W + 2 * G * M]

    dt = _softplus(dt_raw + dtb_ref[...])
    a_neg = jnp.where(_iota((1, LANES), 1) < SSD_HEADS, -jnp.exp(alog_ref[...]), 0.0)
    da = dt * a_neg
    r_i, c_i = _iota((L, L), 0), _iota((L, L), 1)
    causal = r_i >= c_i
    cs = _dot_sel_lhs(causal.astype(bf16), da, 3)
    expand = (_iota((LANES, W), 0) == _iota((LANES, W), 1) // HEAD_DIM).astype(bf16)
    dtx = _dot_sel_rhs(dt, expand, 3)
    csx = _dot_sel_rhs(cs, expand, 3)
    eye_h = (_iota((8, LANES), 0) == _iota((8, LANES), 1)).astype(bf16)
    cs_t = None
    for p in _split(cs, 3):
        t = lax.dot_general(eye_h, p, (((1,), (1,)), ((), ())), preferred_element_type=f32)
        cs_t = t if cs_t is None else cs_t + t
    xdt = xs * dtx

    st = st_ref[...]
    y_parts = []
    for gi in range(G):
        bm_g = bm[:, gi * M:(gi + 1) * M]
        cm_g = cm[:, gi * M:(gi + 1) * M]
        cb = _dot_nt(cm_g, bm_g)
        for e in range(SSD_HEADS // G):
            h = gi * (SSD_HEADS // G) + e
            seg = jnp.where(causal, jnp.exp(jnp.broadcast_to(cs[:, h:h + 1], (L, L)) - cs_t[h:h + 1, :]), 0.0)
            y_parts.append(_dot(cb * seg, xdt[:, h * HEAD_DIM:(h + 1) * HEAD_DIM]))
    y = jnp.concatenate(y_parts, axis=1)
    y_inter = jnp.concatenate(
        [_dot(cm[:, gi * M:(gi + 1) * M], st[:, gi * gw:(gi + 1) * gw]) for gi in range(G)], axis=1)
    y = y + y_inter * jnp.exp(csx)

    cs_last = csx[L - 1:L, :]
    xdtw = xdt * jnp.exp(cs_last - csx)
    d_state = jnp.concatenate(
        [_dot_tn(bm[:, gi * M:(gi + 1) * M], xdtw[:, gi * gw:(gi + 1) * gw]) for gi in range(G)], axis=1)
    st_ref[...] = st * jnp.exp(cs_last) + d_state

    y = (y + dsk_ref[...] * xs) * _silu(z)
    outs = []
    for gi in range(G):
        yg = y[:, gi * gw:(gi + 1) * gw]
        ms = jnp.mean(yg * yg, axis=-1, keepdims=True)
        outs.append(yg * lax.rsqrt(ms + LN_EPS))
    o_ref[0] = jnp.concatenate(outs, axis=1) * ng_ref[...]


def _ssd(ps, cw, cb, dtb, alog, dsk, ng):
    nb, t, cols = ps.shape
    L = SSD_BLOCK
    full2 = lambda a: pl.BlockSpec(a.shape, lambda b, i: (0, 0))
    return pl.pallas_call(
        _ssd_kernel,
        grid=(nb, t // L),
        in_specs=[
            pl.BlockSpec((1, L, cols), lambda b, i: (b, i, 0)),
            pl.BlockSpec((1, 8, cols), lambda b, i: (b, jnp.maximum(i * (L // 8) - 1, 0), 0)),
            full2(cw), full2(cb), full2(dtb), full2(alog), full2(dsk), full2(ng),
        ],
        out_specs=pl.BlockSpec((1, L, SSD_WIDTH), lambda b, i: (b, i, 0)),
        out_shape=jax.ShapeDtypeStruct((nb, t, SSD_WIDTH), f32),
        scratch_shapes=[pltpu.VMEM((SSD_STATE, SSD_WIDTH), f32)],
        compiler_params=_cparams(("arbitrary", "arbitrary")),
        name="ssd_mixer",
    )(ps, ps, cw, cb, dtb, alog, dsk, ng)


def _rwkv_kernel(p_ref, prev_ref, mu_ref, w0_ref, w2_ref, a0_ref, a2_ref, g2_ref, kk_ref, ka_ref, rk_ref,
                 lng_ref, lnb_ref, o_ref, zt_ref):
    nb, C, _ = p_ref.shape
    W = RWKV_WIDTH
    H = LANES // HEAD_DIM
    npair = W // LANES
    i = pl.program_id(0)

    @pl.when(i == 0)
    def _():
        zt_ref[...] = jnp.zeros_like(zt_ref)

    bones = _block_ones(W, HEAD_DIM).astype(bf16)
    pair_blk = _block_ones(LANES, HEAD_DIM)
    n = H * C
    rr, cc = _iota((n, n), 0), _iota((n, n), 1)
    same = rr // C == cc // C
    strict = same & (rr > cc)
    lower = same & (rr >= cc)
    eye = jnp.where(rr == cc, 1.0, 0.0)

    def stack(t):
        return jnp.where(pair_blk, jnp.concatenate([t] * H, axis=0), 0.0)

    rows = range(nb)
    nr = nb * C
    cols = p_ref[...].reshape(nr, p_ref.shape[2])
    shifted = jnp.concatenate(
        [piece for b in rows for piece in (jnp.where(i > 0, prev_ref[b, 7:8, :], 0.0), p_ref[b, 0:C - 1, :])], axis=0)
    cols = cols + (shifted - cols) * mu_ref[...]
    r = cols[:, 0:W]
    k = cols[:, W:2 * W]
    v = cols[:, 2 * W:3 * W]
    w_lr = cols[:, 3 * W:3 * W + 64]
    a_lr = cols[:, 3 * W + 64:3 * W + 128]
    g_lr = cols[:, 3 * W + 128:3 * W + 256]

    wv = -_softplus(-(w0_ref[...] + _dot_hi(jnp.tanh(w_lr), w2_ref[...]))) - 0.5
    lw = -jnp.exp(wv)
    a = jax.nn.sigmoid(a0_ref[...] + _dot_hi(a_lr, a2_ref[...]))
    g = _dot_hi(jax.nn.sigmoid(g_lr), g2_ref[...])

    kk = k * kk_ref[...]
    k2 = k * (1.0 + (a - 1.0) * ka_ref[...])
    head_sums = _dot_sel_rhs(jnp.concatenate([kk * kk, r * k2 * rk_ref[...]], axis=0), bones, 2)
    kk = kk / jnp.maximum(jnp.sqrt(head_sums[:nr]), 1e-12)
    bonus_v = head_sums[nr:] * v
    bvec = kk * a

    rb_i, cb_i = _iota((nr, nr), 0), _iota((nr, nr), 1)
    tri = ((rb_i >= cb_i) & (rb_i // C == cb_i // C)).astype(bf16)
    cum = _dot_sel_lhs(tri, lw, 3)
    cum_last = [cum[(b + 1) * C - 1:(b + 1) * C, :] for b in rows]
    e_neg = jnp.exp(-cum)
    e_rem = jnp.exp(jnp.concatenate([jnp.broadcast_to(cl, (C, W)) for cl in cum_last], axis=0) - cum)
    per_token = dict(aw=-kk * jnp.exp(cum - lw), rw=r * jnp.exp(cum), bn=bvec * e_neg, kn=k2 * e_neg,
                     bt=bvec * e_rem, kt=k2 * e_rem, vs=v)
    chains = [(b, p) for b in rows for p in range(npair)]
    q = [dict({name: stack(t[b * C:(b + 1) * C, p * LANES:(p + 1) * LANES]) for name, t in per_token.items()},
              wc=jnp.exp(cum_last[b][:, p * LANES:(p + 1) * LANES])) for b, p in chains]
    ids = range(len(chains))

    a_ab = [jnp.where(strict, _dot_nt(q[c]["aw"], q[c]["bn"]), 0.0) for c in ids]
    a_ak = [jnp.where(strict, _dot_nt(q[c]["aw"], q[c]["kn"]), 0.0) for c in ids]
    a_rb = [jnp.where(lower, _dot_nt(q[c]["rw"], q[c]["bn"]), 0.0) for c in ids]
    a_rk = [jnp.where(lower, _dot_nt(q[c]["rw"], q[c]["kn"]), 0.0) for c in ids]

    pw = a_ab
    tinv = [eye + a_ab[c] for c in ids]
    for _ in range(int(math.log2(C)) - 1):
        pw = [_dot(pw[c], pw[c]) for c in ids]
        tinv = [tinv[c] + _dot(tinv[c], pw[c]) for c in ids]

    zt = [zt_ref[b, p] for b, p in chains]
    rhs = [_dot_nt(q[c]["aw"], zt[c]) + _dot(a_ak[c], q[c]["vs"]) for c in ids]
    y0 = [_dot_nt(q[c]["rw"], zt[c]) + _dot(a_rk[c], q[c]["vs"]) for c in ids]
    u = [_dot(tinv[c], rhs[c]) for c in ids]
    y_st = [y0[c] + _dot(a_rb[c], u[c]) for c in ids]
    for c, (b, p) in enumerate(chains):
        zt_ref[b, p] = zt[c] * q[c]["wc"] + _dot_tn(u[c], q[c]["bt"]) + _dot_tn(q[c]["vs"], q[c]["kt"])

    inv_n = 1.0 / HEAD_DIM
    ys = []
    for b in rows:
        tiles = []
        for p in range(npair):
            y_c = y_st[b * npair + p]
            y = y_c[0:C]
            for h in range(1, H):
                y = y + y_c[h * C:(h + 1) * C]
            tiles.append(y)
        ys.append(jnp.concatenate(tiles, axis=1))
    y = jnp.concatenate(ys, axis=0)
    yc = y - _dot_sel_rhs(y, bones, 2) * inv_n
    var_h = _dot_sel_rhs(yc * yc, bones, 2) * inv_n
    y = yc * lax.rsqrt(var_h + RWKV_GN_EPS) * lng_ref[...] + lnb_ref[...]
    o_ref[...] = ((y + bonus_v) * g).reshape(nb, C, W)


def _rwkv(pr, mu, w0, w2, a0, a2, g2, k_k, k_a, r_k, ln_g, ln_b):
    nb, t, cols = pr.shape
    C = RWKV_CHUNK
    full2 = lambda a: pl.BlockSpec(a.shape, lambda i: (0, 0))
    params = (mu, w0, w2, a0, a2, g2, k_k, k_a, r_k, ln_g, ln_b)
    return pl.pallas_call(
        _rwkv_kernel,
        grid=(t // C,),
        in_specs=[
            pl.BlockSpec((nb, C, cols), lambda i: (0, i, 0)),
            pl.BlockSpec((nb, 8, cols), lambda i: (0, jnp.maximum(i * (C // 8) - 1, 0), 0)),
        ] + [full2(p) for p in params],
        out_specs=pl.BlockSpec((nb, C, RWKV_WIDTH), lambda i: (0, i, 0)),
        out_shape=jax.ShapeDtypeStruct((nb, t, RWKV_WIDTH), f32),
        scratch_shapes=[pltpu.VMEM((nb, RWKV_WIDTH // LANES, LANES, LANES), f32)],
        compiler_params=_cparams(("arbitrary",)),
        name="rwkv7_mixer",
    )(pr, pr, *params)


def _route(logits, rb):
    tm = logits.shape[1]
    scores = jax.nn.sigmoid(logits)
    sel = scores + rb
    srow = [sel[e:e + 1, :] for e in range(N_EXPERTS)]
    crow = [scores[e:e + 1, :] for e in range(N_EXPERTS)]

    def top2_sum(v0, v1, v2, v3):
        m01, n01 = jnp.maximum(v0, v1), jnp.minimum(v0, v1)
        m23, n23 = jnp.maximum(v2, v3), jnp.minimum(v2, v3)
        return jnp.maximum(m01, m23) + jnp.maximum(jnp.minimum(m01, m23), jnp.maximum(n01, n23))

    gscore = [top2_sum(*srow[4 * gi:4 * gi + 4]) for gi in range(N_EXPERT_GROUPS)]
    best = jnp.zeros((1, tm), jnp.int32)
    best_v = gscore[0]
    for gi in range(1, N_EXPERT_GROUPS):
        better = gscore[gi] > best_v
        best = jnp.where(better, gi, best)
        best_v = jnp.where(better, gscore[gi], best_v)
    sv, cv = [], []
    for j in range(EXPERTS_PER_GROUP):
        s_j, c_j = srow[j], crow[j]
        for gi in range(1, N_EXPERT_GROUPS):
            s_j = jnp.where(best == gi, srow[4 * gi + j], s_j)
            c_j = jnp.where(best == gi, crow[4 * gi + j], c_j)
        sv.append(s_j)
        cv.append(c_j)
    j1 = jnp.zeros((1, tm), jnp.int32)
    v1 = sv[0]
    for j in range(1, EXPERTS_PER_GROUP):
        better = sv[j] > v1
        j1 = jnp.where(better, j, j1)
        v1 = jnp.where(better, sv[j], v1)
    j2 = jnp.full((1, tm), -1, jnp.int32)
    v2 = jnp.full((1, tm), -jnp.inf, f32)
    for j in range(EXPERTS_PER_GROUP):
        better = (j1 != j) & ((j2 < 0) | (sv[j] > v2))
        j2 = jnp.where(better, j, j2)
        v2 = jnp.where(better, sv[j], v2)
    w1 = cv[0]
    w2 = cv[0]
    for j in range(1, EXPERTS_PER_GROUP):
        w1 = jnp.where(j1 == j, cv[j], w1)
        w2 = jnp.where(j2 == j, cv[j], w2)
    wsum = w1 + w2
    g1, g2 = w1 / wsum, w2 / wsum
    rows = []
    for e in range(N_EXPERTS):
        gi, j = divmod(e, EXPERTS_PER_GROUP)
        in_g = best == gi
        rows.append(jnp.where(in_g & (j1 == j), g1, jnp.where(in_g & (j2 == j), g2, 0.0)))
    gate_t = jnp.concatenate(rows, axis=0)
    return gate_t, best


def _outproj_kernel(alpha, x_ref, og_ref, os_ref, or_ref, wo_ref, g_ref, b_ref, rwt_ref, rb_ref,
                    x1_ref, gate_ref, best_ref, cnt_ref):
    tm = x_ref.shape[0]
    ts = tm // cnt_ref.shape[0]
    parts = [slice(sb * ts, (sb + 1) * ts) for sb in range(cnt_ref.shape[0])]
    d = functools.partial(jnp.dot, preferred_element_type=f32)
    mix = [d(_bf(og_ref[s, :]), wo_ref[0:GLA_WIDTH, :])
           + d(_bf(os_ref[s, :]), wo_ref[GLA_WIDTH:GLA_WIDTH + SSD_WIDTH, :])
           + d(_bf(or_ref[s, :]), wo_ref[GLA_WIDTH + SSD_WIDTH:, :]) for s in parts]
    x1 = [_layer_norm(alpha * x_ref[s, :] + m, g_ref[...], b_ref[...], LN_EPS) for s, m in zip(parts, mix)]
    for s, v in zip(parts, x1):
        x1_ref[s, :] = v

    wh, wl = _split(rwt_ref[...], 2)
    nt = lambda a, b: lax.dot_general(a, b, (((1,), (1,)), ((), ())), preferred_element_type=f32)
    logits = []
    for v in x1:
        xh, xl = _split(v, 2)
        logits.append(nt(wh, xh) + nt(wl, xh) + nt(wh, xl))
    routed = [_route(lg, rb_ref[...]) for lg in logits]
    eye = (_iota((N_EXPERTS, LANES), 0) == _iota((N_EXPERTS, LANES), 1)).astype(bf16)
    lane = _iota((1, LANES), 1)
    for sb, (s, (gate_t, best)) in enumerate(zip(parts, routed)):
        gate = None
        for p in _split(gate_t, 3):
            t = lax.dot_general(p, eye, (((0,), (0,)), ((), ())), preferred_element_type=f32)
            gate = t if gate is None else gate + t
        gate_ref[s, :] = gate
        best_ref[:, s] = best
        cnt = jnp.zeros((1, LANES), jnp.int32)
        for gi in range(N_EXPERT_GROUPS):
            n_g = jnp.sum((best == gi).astype(jnp.int32), axis=-1, keepdims=True)
            cnt = jnp.where(lane == gi, n_g, cnt)
        cnt_ref[sb] = cnt


def _out_proj(alpha, x2, og, os_, or_, wo, g, b, rwt, rb):
    n, d = x2.shape
    tm = min(OUT_PROJ_ROWS, n)
    ts = min(MOE_SORT_ROWS, tm)
    full2 = lambda a: pl.BlockSpec(a.shape, lambda i: (0, 0))
    rows = lambda c: pl.BlockSpec((tm, c), lambda i: (i, 0))
    return pl.pallas_call(
        functools.partial(_outproj_kernel, alpha),
        grid=(n // tm,),
        in_specs=[rows(d), rows(og.shape[1]), rows(os_.shape[1]), rows(or_.shape[1]),
                  full2(wo), full2(g), full2(b), full2(rwt), full2(rb)],
        out_specs=[rows(d), rows(LANES), pl.BlockSpec((1, tm), lambda i: (0, i)),
                   pl.BlockSpec((tm // ts, 1, LANES), lambda i: (i, 0, 0))],
        out_shape=[jax.ShapeDtypeStruct((n, d), f32), jax.ShapeDtypeStruct((n, LANES), f32),
                   jax.ShapeDtypeStruct((1, n), jnp.int32), jax.ShapeDtypeStruct((n // ts, 1, LANES), jnp.int32)],
        compiler_params=_cparams(("arbitrary",)),
        name="out_proj_router",
    )(x2, og, os_, or_, wo, g, b, rwt, rb)


def _seg_copy(src, dst, sem, src_off, dst_off, rows, max_rows, wait):
    k = SEG_ALIGN
    sizes = []
    while k <= max_rows:
        sizes.append(k)
        k *= 2
    for k in reversed(sizes):
        shift = int(math.log2(k)) + 1

        @pl.when((rows & k) != 0)
        def _():
            done = (rows >> shift) << shift
            cp = pltpu.make_async_copy(
                src.at[pl.ds(pl.multiple_of(src_off + done, SEG_ALIGN), k)],
                dst.at[pl.ds(pl.multiple_of(dst_off + done, SEG_ALIGN), k)], sem)
            if wait:
                cp.wait()
            else:
                cp.start()


def _padded(count):
    return (count + (SEG_ALIGN - 1)) & (-SEG_ALIGN)


def _one_hot_rows(pos, n_rows):
    return jnp.where(_iota((n_rows, pos.shape[1]), 0) == pos, 1.0, 0.0).astype(bf16)


def _dispatch_kernel(cnt_sm, goff_sm, fill_sm, x_ref, gate_ref, best_ref, xs_out, gs_out, pos_ref,
                     xloc, gloc, zx, zg, sems):
    i = pl.program_id(0)
    tm = x_ref.shape[0]
    n_loc = xloc.shape[1]
    slot = i % 2

    def copies(blk, slot_, wait):
        loff = 0
        for gi in range(N_EXPERT_GROUPS):
            rows = _padded(cnt_sm[blk * N_EXPERT_GROUPS + gi])
            dst = goff_sm[blk * N_EXPERT_GROUPS + gi]
            _seg_copy(xloc.at[slot_], xs_out, sems.at[0, slot_], loff, dst, rows, tm, wait)
            _seg_copy(gloc.at[slot_], gs_out, sems.at[1, slot_], loff, dst, rows, tm, wait)
            loff = loff + rows

    best = best_ref[...]
    onehot = (_iota((8, tm), 0) == best).astype(bf16)
    earlier = (_iota((tm, tm), 0) < _iota((tm, tm), 1)).astype(bf16)
    rank = jnp.dot(onehot, earlier, preferred_element_type=f32)
    pos = jnp.zeros((1, tm), f32)
    loff = 0
    for gi in range(N_EXPERT_GROUPS):
        pos = jnp.where(best == gi, rank[gi:gi + 1, :] + loff.astype(f32) if gi else rank[gi:gi + 1, :], pos)
        loff = loff + _padded(cnt_sm[i * N_EXPERT_GROUPS + gi])
    pos = pos.astype(jnp.int32)
    pos_ref[...] = pos
    perm = _one_hot_rows(pos, n_loc)
    xloc[slot] = jnp.dot(perm, _bf(x_ref[...]), preferred_element_type=f32)
    gloc[slot] = _dot_sel_lhs(perm, gate_ref[...], 3)
    copies(i, slot, False)

    @pl.when(i > 0)
    def _():
        copies(i - 1, 1 - slot, True)

    @pl.when(i == pl.num_programs(0) - 1)
    def _():
        copies(i, slot, True)
        zx[...] = jnp.zeros_like(zx)
        zg[...] = jnp.zeros_like(zg)
        tile = zx.shape[0]

        def fills(wait):
            for src, dst, sem in ((zx, xs_out, sems.at[2, 0]), (zg, gs_out, sems.at[2, 1])):
                for gi in range(N_EXPERT_GROUPS):
                    _seg_copy(src, dst, sem, 0, fill_sm[gi], fill_sm[N_EXPERT_GROUPS + gi], tile // 2, wait)

                def body(j, carry):
                    off = pl.multiple_of(fill_sm[2 * N_EXPERT_GROUPS] + j * tile, SEG_ALIGN)
                    cp = pltpu.make_async_copy(src, dst.at[pl.ds(off, tile)], sem)
                    if wait:
                        cp.wait()
                    else:
                        cp.start()
                    return carry

                lax.fori_loop(0, fill_sm[2 * N_EXPERT_GROUPS + 1], body, 0)

        fills(False)
        fills(True)


def _experts_kernel(tg_sm, tv_sm, x_ref, gate_ref, wg_ref, wu_ref, wd_ref, y_ref):
    t = pl.program_id(0)

    @pl.when(tv_sm[t] == 0)
    def _():
        y_ref[...] = jnp.zeros_like(y_ref)

    @pl.when(tv_sm[t] > 0)
    def _():
        first = tg_sm[t] * EXPERTS_PER_GROUP
        xb = _bf(x_ref[...])
        gate = gate_ref[...]
        lane = _iota(gate.shape, 1)
        hs = []
        for j in range(EXPERTS_PER_GROUP):
            hg = jnp.dot(xb, wg_ref[j], preferred_element_type=f32)
            hu = jnp.dot(xb, wu_ref[j], preferred_element_type=f32)
            gcol = jnp.sum(jnp.where(lane == first + j, gate, 0.0), axis=-1, keepdims=True)
            hs.append(_bf(_silu(hg) * hu * gcol))
        ff = wd_ref.shape[1]
        y_ref[...] = jnp.dot(jnp.concatenate(hs, axis=1), wd_ref[...].reshape(EXPERTS_PER_GROUP * ff, -1),
                             preferred_element_type=f32)


def _combine_kernel(alpha, cnt_sm, goff_sm, x_ref, pos_ref, g_ref, b_ref, ys_ref, o_ref, yloc, sems):
    i = pl.program_id(0)
    tm = x_ref.shape[0]
    n_loc = yloc.shape[1]
    slot = i % 2

    def copies(blk, slot_, wait):
        loff = 0
        for gi in range(N_EXPERT_GROUPS):
            rows = _padded(cnt_sm[blk * N_EXPERT_GROUPS + gi])
            src = goff_sm[blk * N_EXPERT_GROUPS + gi]
            _seg_copy(ys_ref, yloc.at[slot_], sems.at[slot_], src, loff, rows, tm, wait)
            loff = loff + rows

    @pl.when(i == 0)
    def _():
        yloc[...] = jnp.zeros_like(yloc)
        copies(0, 0, False)

    @pl.when(i + 1 < pl.num_programs(0))
    def _():
        copies(i + 1, 1 - slot, False)

    copies(i, slot, True)
    perm = _one_hot_rows(pos_ref[...], n_loc)
    y = lax.dot_general(perm, _bf(yloc[slot]), (((0,), (0,)), ((), ())), preferred_element_type=f32)
    o_ref[...] = _layer_norm(alpha * x_ref[...] + y, g_ref[...], b_ref[...], LN_EPS)


def _moe_tables(cnt, n_tiles, tile):
    pc = (cnt + (SEG_ALIGN - 1)) // SEG_ALIGN * SEG_ALIGN
    gtot = jnp.sum(pc, axis=0)
    gcap = (gtot + tile - 1) // tile * tile
    gend = jnp.cumsum(gcap)
    gstart = gend - gcap
    goff = gstart[None, :] + jnp.cumsum(pc, axis=0) - pc
    tstart = jnp.arange(n_tiles, dtype=jnp.int32) * tile
    tg = jnp.minimum(jnp.sum((tstart[:, None] >= gend[None, :]).astype(jnp.int32), axis=1), N_EXPERT_GROUPS - 1)
    tv = jnp.clip(gtot[tg] - (tstart - gstart[tg]), 0, tile)
    fill = jnp.concatenate([gstart + gtot, gcap - gtot, gend[-1:], n_tiles - gend[-1:] // tile])
    return goff.reshape(-1).astype(jnp.int32), tg.astype(jnp.int32), tv.astype(jnp.int32), fill.astype(jnp.int32)


def _moe(alpha, x1, gate, best, cnt, layer, wg, wu, wd, g, b):
    n, d = x1.shape
    ff = wg.shape[3]
    nblk = cnt.shape[0]
    tm = n // nblk
    tile = MOE_TILE
    n_loc = tm + LANES
    n_sorted = -(-(n + nblk * N_EXPERT_GROUPS * SEG_ALIGN + N_EXPERT_GROUPS * tile) // tile) * tile
    n_tiles = n_sorted // tile
    cnt_flat = cnt[:, 0, :N_EXPERT_GROUPS].reshape(-1)
    goff, tg, tv, fill = _moe_tables(cnt[:, 0, :N_EXPERT_GROUPS], n_tiles, tile)
    any_spec = pl.BlockSpec(memory_space=pl.ANY)

    xs, gs, pos = pl.pallas_call(
        _dispatch_kernel,
        grid_spec=pltpu.PrefetchScalarGridSpec(
            num_scalar_prefetch=3,
            grid=(nblk,),
            in_specs=[
                pl.BlockSpec((tm, d), lambda i, *_: (i, 0)),
                pl.BlockSpec((tm, LANES), lambda i, *_: (i, 0)),
                pl.BlockSpec((1, tm), lambda i, *_: (0, i)),
            ],
            out_specs=[any_spec, any_spec, pl.BlockSpec((1, tm), lambda i, *_: (0, i))],
            scratch_shapes=[pltpu.VMEM((2, n_loc, d), f32), pltpu.VMEM((2, n_loc, LANES), f32),
                            pltpu.VMEM((tile, d), f32), pltpu.VMEM((tile, LANES), f32),
                            pltpu.SemaphoreType.DMA((3, 2))],
        ),
        out_shape=[jax.ShapeDtypeStruct((n_sorted, d), f32), jax.ShapeDtypeStruct((n_sorted, LANES), f32),
                   jax.ShapeDtypeStruct((1, n), jnp.int32)],
        compiler_params=_cparams(("arbitrary",)),
        name="moe_dispatch",
    )(cnt_flat, goff, fill, x1, gate, best)

    ys = pl.pallas_call(
        _experts_kernel,
        grid_spec=pltpu.PrefetchScalarGridSpec(
            num_scalar_prefetch=2,
            grid=(n_tiles,),
            in_specs=[
                pl.BlockSpec((tile, d), lambda t, *_: (t, 0)),
                pl.BlockSpec((tile, LANES), lambda t, *_: (t, 0)),
                pl.BlockSpec((None, EXPERTS_PER_GROUP, d, ff), lambda t, tg_, tv_: (layer, tg_[t], 0, 0)),
                pl.BlockSpec((None, EXPERTS_PER_GROUP, d, ff), lambda t, tg_, tv_: (layer, tg_[t], 0, 0)),
                pl.BlockSpec((None, EXPERTS_PER_GROUP, ff, d), lambda t, tg_, tv_: (layer, tg_[t], 0, 0)),
            ],
            out_specs=pl.BlockSpec((tile, d), lambda t, *_: (t, 0)),
        ),
        out_shape=jax.ShapeDtypeStruct((n_sorted, d), f32),
        compiler_params=_cparams(("arbitrary",)),
        name="moe_experts",
    )(tg, tv, xs, gs, wg, wu, wd)

    return pl.pallas_call(
        functools.partial(_combine_kernel, alpha),
        grid_spec=pltpu.PrefetchScalarGridSpec(
            num_scalar_prefetch=2,
            grid=(nblk,),
            in_specs=[
                pl.BlockSpec((tm, d), lambda i, *_: (i, 0)),
                pl.BlockSpec((1, tm), lambda i, *_: (0, i)),
                pl.BlockSpec(g.shape, lambda i, *_: (0, 0)),
                pl.BlockSpec(b.shape, lambda i, *_: (0, 0)),
                any_spec,
            ],
            out_specs=pl.BlockSpec((tm, d), lambda i, *_: (i, 0)),
            scratch_shapes=[pltpu.VMEM((2, n_loc, d), f32), pltpu.SemaphoreType.DMA((2,))],
        ),
        out_shape=jax.ShapeDtypeStruct((n, d), f32),
        compiler_params=_cparams(("arbitrary",)),
        name="moe_combine",
    )(cnt_flat, goff, x1, pos, g, b, ys)


def _pad_cols(w, n):
    return jnp.pad(w, ((0, 0), (0, n - w.shape[1])))


def _row(v, n=None):
    v = v.reshape(1, -1)
    return v if n is None else _pad_cols(v, n)


def kernel(x, w_in, gla_w_alpha, gla_b_alpha, gla_norm_g, ssd_conv_w, ssd_conv_b, ssd_dt_bias, ssd_A_log, ssd_D, ssd_norm_g, rwkv_mu, rwkv_w0, rwkv_w2, rwkv_a0, rwkv_a2, rwkv_g2, rwkv_k_k, rwkv_k_a, rwkv_r_k, rwkv_ln_g, rwkv_ln_b, w_out, ln1_g, ln1_b, router_w, router_b, exp_w_gate, exp_w_up, exp_w_down, ln2_g, ln2_b):
    nb, t, d = x.shape
    depth = w_in.shape[0]
    alpha = float((2 * depth) ** 0.25)
    gla_cols = 4 * GLA_WIDTH + GLA_GATE_RANK
    ssd_cols = SSD_WIDTH + SSD_XBC + SSD_HEADS
    rwt = router_w.T
    rb = router_b.reshape(-1, 1)
    wg_all, wu_all, wd_all = _bf(exp_w_gate), _bf(exp_w_up), _bf(exp_w_down)
    x2 = x.reshape(nb * t, d)
    for l in range(depth):
        wg, ws, wr = _regroup_w_in(w_in, l, (gla_cols, ssd_cols, RWKV_COLS), (GLA_COLS_PAD, SSD_COLS_PAD, RWKV_COLS))
        pg, ps, pr = _in_proj(x2, wg, ws, wr)
        og = _gla(pg.reshape(nb, t, -1),
                  jnp.pad(gla_w_alpha[l], ((0, LANES - GLA_GATE_RANK), (0, 0))),
                  _row(gla_b_alpha[l]), _row(jnp.tile(gla_norm_g[l], GLA_WIDTH // HEAD_DIM)))
        os_ = _ssd(ps.reshape(nb, t, -1), ssd_conv_w[l], _row(ssd_conv_b[l]), _row(ssd_dt_bias[l], LANES),
                   _row(ssd_A_log[l], LANES), _row(jnp.repeat(ssd_D[l], HEAD_DIM)), _row(ssd_norm_g[l]))
        or_ = _rwkv(pr.reshape(nb, t, -1), _row(rwkv_mu[l]), _row(rwkv_w0[l]), rwkv_w2[l], _row(rwkv_a0[l]),
                    rwkv_a2[l], rwkv_g2[l], _row(rwkv_k_k[l]), _row(rwkv_k_a[l]), _row(rwkv_r_k[l]),
                    _row(rwkv_ln_g[l]), _row(rwkv_ln_b[l]))
        x1, gate, best, cnt = _out_proj(alpha, x2, og.reshape(nb * t, -1), os_.reshape(nb * t, -1),
                                        or_.reshape(nb * t, -1), _bf(w_out[l]), _row(ln1_g[l]), _row(ln1_b[l]),
                                        rwt, rb)
        x2 = _moe(alpha, x1, gate, best, cnt, l, wg_all, wu_all, wd_all, _row(ln2_g[l]), _row(ln2_b[l]))
    return x2.reshape(nb, t, d)
```

```python
import functools
import math

import jax
import jax.numpy as jnp
from jax import lax
from jax.experimental import pallas as pl
from jax.experimental.pallas import tpu as pltpu

f32 = jnp.float32
bf16 = jnp.bfloat16

LANES = 128
HEAD_DIM = 64
GLA_WIDTH = 256
GLA_GATE_RANK = 16
GLA_GATE_TAU = 16.0
GLA_CHUNK = 16
GLA_BLOCK = 128
GLA_COLS_PAD = 4 * GLA_WIDTH + LANES
SSD_WIDTH = 512
SSD_HEADS = 8
SSD_GROUPS = 2
SSD_STATE = 64
SSD_CONV = 4
SSD_XBC = SSD_WIDTH + 2 * SSD_GROUPS * SSD_STATE
SSD_BLOCK = 128
SSD_COLS_PAD = SSD_WIDTH + SSD_XBC + LANES
RWKV_WIDTH = 256
RWKV_HEADS = 4
RWKV_CHUNK = 64
RWKV_GN_EPS = 64e-5
RWKV_COLS = 3 * RWKV_WIDTH + 64 + 64 + 128
N_EXPERTS = 16
N_EXPERT_GROUPS = 4
EXPERTS_PER_GROUP = 4
LN_EPS = 1e-5

IN_PROJ_ROWS = 512
OUT_PROJ_ROWS = 1024
MOE_SORT_ROWS = 512
MOE_TILE = 512
SEG_ALIGN = 8
VMEM_LIMIT = 56 * 1024 * 1024


def _bf(x):
    return x.astype(bf16)


def _dot(a, b):
    return jnp.dot(_bf(a), _bf(b), preferred_element_type=f32)


def _dot_nt(a, b):
    return lax.dot_general(_bf(a), _bf(b), (((1,), (1,)), ((), ())), preferred_element_type=f32)


def _dot_tn(a, b):
    return lax.dot_general(_bf(a), _bf(b), (((0,), (0,)), ((), ())), preferred_element_type=f32)


def _split(x, n):
    parts, r = [], x
    for _ in range(n):
        p = r.astype(bf16)
        parts.append(p)
        r = r - p.astype(f32)
    return parts


def _dot_sel_rhs(a, sel, n=3):
    out = None
    for p in _split(a, n):
        t = jnp.dot(p, sel, preferred_element_type=f32)
        out = t if out is None else out + t
    return out


def _dot_sel_lhs(sel, b, n=3):
    out = None
    for p in _split(b, n):
        t = jnp.dot(sel, p, preferred_element_type=f32)
        out = t if out is None else out + t
    return out


def _dot_hi(a, b):
    ah, al = _split(a, 2)
    bh, bl = _split(b, 2)
    d = functools.partial(jnp.dot, preferred_element_type=f32)
    return d(ah, bh) + d(al, bh) + d(ah, bl)


def _iota(shape, dim):
    return lax.broadcasted_iota(jnp.int32, shape, dim)


def _block_ones(n, blk):
    return (_iota((n, n), 0) // blk == _iota((n, n), 1) // blk)


def _silu(x):
    return x * jax.nn.sigmoid(x)


def _softplus(x):
    return jnp.maximum(x, 0.0) + jnp.log1p(jnp.exp(-jnp.abs(x)))


def _layer_norm(y, g, b, eps):
    mu = jnp.mean(y, axis=-1, keepdims=True)
    yc = y - mu
    var = jnp.mean(yc * yc, axis=-1, keepdims=True)
    return yc * lax.rsqrt(var + eps) * g + b


def _cparams(sem):
    return pltpu.CompilerParams(dimension_semantics=sem, vmem_limit_bytes=VMEM_LIMIT)


def _inproj_kernel(x_ref, wg_ref, ws_ref, wr_ref, og_ref, os_ref, or_ref):
    xb = _bf(x_ref[...])
    og_ref[...] = jnp.dot(xb, wg_ref[...], preferred_element_type=f32)
    os_ref[...] = jnp.dot(xb, ws_ref[...], preferred_element_type=f32)
    or_ref[...] = jnp.dot(xb, wr_ref[...], preferred_element_type=f32)


def _regroup_kernel(splits, w_ref, *o_refs):
    w = w_ref[0]
    start = 0
    for width, o_ref in zip(splits, o_refs):
        part = w[:, start:start + width]
        pad = o_ref.shape[1] - width
        if pad:
            part = jnp.concatenate([part, jnp.zeros((part.shape[0], pad), part.dtype)], axis=1)
        o_ref[...] = _bf(part)
        start += width


def _regroup_w_in(w_in, layer, splits, padded):
    _, d, n_in = w_in.shape
    rows = 256
    return pl.pallas_call(
        functools.partial(_regroup_kernel, splits),
        grid=(d // rows,),
        in_specs=[pl.BlockSpec((1, rows, n_in), lambda i: (layer, i, 0))],
        out_specs=[pl.BlockSpec((rows, p), lambda i: (i, 0)) for p in padded],
        out_shape=[jax.ShapeDtypeStruct((d, p), bf16) for p in padded],
        compiler_params=_cparams(("arbitrary",)),
        name="regroup_w_in",
    )(w_in)


def _in_proj(x2, wg, ws, wr):
    n, d = x2.shape
    tm = min(IN_PROJ_ROWS, n)
    full = lambda w: pl.BlockSpec(w.shape, lambda i: (0, 0))
    rows = lambda c: pl.BlockSpec((tm, c), lambda i: (i, 0))
    return pl.pallas_call(
        _inproj_kernel,
        grid=(n // tm,),
        in_specs=[rows(d), full(wg), full(ws), full(wr)],
        out_specs=[rows(wg.shape[1]), rows(ws.shape[1]), rows(wr.shape[1])],
        out_shape=[jax.ShapeDtypeStruct((n, w.shape[1]), f32) for w in (wg, ws, wr)],
        compiler_params=_cparams(("arbitrary",)),
        name="in_proj",
    )(x2, wg, ws, wr)


def _gla_steps(p_ref, wa_ref, ba_ref, ng_ref, o_ref, st_ref, t_ref, d_ref, sx_ref):
    nb, tb, _ = p_ref.shape
    w, c = GLA_WIDTH, GLA_CHUNK
    nc = tb // c
    npt = w // LANES
    Q, K, V, CUM, OUT = range(5)

    @pl.when(pl.program_id(0) == 0)
    def _():
        st_ref[...] = jnp.zeros_like(st_ref)

    bones = _block_ones(w, HEAD_DIM).astype(bf16)
    pair_blk = _block_ones(LANES, HEAD_DIM)
    r_i, c_i = _iota((tb, tb), 0), _iota((tb, tb), 1)
    tri = ((r_i >= c_i) & (r_i // c == c_i // c)).astype(bf16)
    wa_h, wa_l = _split(wa_ref[...], 2)
    dd = functools.partial(jnp.dot, preferred_element_type=f32)

    for b in range(nb):
        a_h, a_l = _split(p_ref[b, :, 4 * w:4 * w + LANES], 2)
        z = dd(a_h, wa_h) + dd(a_l, wa_h) + dd(a_h, wa_l) + ba_ref[...]
        log_a = -_softplus(-z) * (1.0 / GLA_GATE_TAU)
        cum = _dot_sel_lhs(tri, log_a, 3)
        for lt in range(npt):
            lanes = slice(lt * LANES, (lt + 1) * LANES)
            t_ref[b, Q, lt] = p_ref[b, :, lanes] * (HEAD_DIM ** -0.5)
            t_ref[b, K, lt] = p_ref[b, :, w + lt * LANES:w + (lt + 1) * LANES]
            t_ref[b, V, lt] = p_ref[b, :, 2 * w + lt * LANES:2 * w + (lt + 1) * LANES]
            t_ref[b, CUM, lt] = cum[:, lanes]
        yield

    def ld(b, j, off):
        return jnp.concatenate([t_ref[b, j, lt, pl.ds(off, nc, stride=c), :] for lt in range(npt)], axis=1)

    for b in range(nb):
        qt = [ld(b, Q, o) for o in range(c)]
        kt = [ld(b, K, o) for o in range(c)]
        ct = [ld(b, CUM, o) for o in range(c)]
        pair = 0
        for t in range(c):
            for s in range(t + 1):
                qk = qt[t] * kt[s]
                d_ref[b, pair * nc:(pair + 1) * nc, :] = qk if s == t else qk * jnp.exp(ct[t] - ct[s])
                pair += 1
            if t % 4 == 3:
                yield
    for b in range(nb):
        sx_ref[b] = jnp.dot(_bf(d_ref[b]), bones, preferred_element_type=f32)
        yield
    for b in range(nb):
        vt = [ld(b, V, o) for o in range(c)]
        pair = 0
        for t in range(c):
            acc = None
            for s in range(t + 1):
                term = sx_ref[b, pair * nc:(pair + 1) * nc, :] * vt[s]
                acc = term if acc is None else acc + term
                pair += 1
            for lt in range(npt):
                t_ref[b, OUT, lt, pl.ds(t, nc, stride=c), :] = acc[:, lt * LANES:(lt + 1) * LANES]
            if t % 8 == 7:
                yield

    for ci in range(nc):
        sl = slice(ci * c, (ci + 1) * c)
        last = slice((ci + 1) * c - 1, (ci + 1) * c)
        for b in range(nb):
            for lt in range(npt):
                cum_c = t_ref[b, CUM, lt, sl, :]
                cum_last = t_ref[b, CUM, lt, last, :]
                st = st_ref[b, lt]
                t_ref[b, OUT, lt, sl, :] += _dot_nt(t_ref[b, Q, lt, sl, :] * jnp.exp(cum_c), st)
                upd = _dot_tn(t_ref[b, V, lt, sl, :], t_ref[b, K, lt, sl, :] * jnp.exp(cum_last - cum_c))
                st_ref[b, lt] = st * jnp.exp(cum_last) + jnp.where(pair_blk, upd, 0.0)
        yield

    o = jnp.concatenate(
        [jnp.concatenate([t_ref[b, OUT, lt] for lt in range(npt)], axis=1) for b in range(nb)], axis=0)
    ms = _dot_sel_rhs(o * o, bones, 2) * (1.0 / HEAD_DIM)
    o = o * lax.rsqrt(ms + LN_EPS) * ng_ref[...]
    for b in range(nb):
        o_ref[b] = o[b * tb:(b + 1) * tb] * _silu(p_ref[b, :, 3 * w:4 * w])


def _ssd_kernel(p_ref, prev_ref, cw_ref, cb_ref, dtb_ref, alog_ref, dsk_ref, ng_ref, o_ref, st_ref):
    _, L, _ = p_ref.shape
    W, M, G = SSD_WIDTH, SSD_STATE, SSD_GROUPS
    gw = W // G
    i = pl.program_id(1)

    @pl.when(i == 0)
    def _():
        st_ref[...] = jnp.zeros_like(st_ref)

    z = p_ref[0, :, 0:W]
    xbc = p_ref[0, :, W:W + SSD_XBC]
    dt_raw = p_ref[0, :, W + SSD_XBC:W + SSD_XBC + LANES]
    prev = jnp.where(i > 0, prev_ref[0, :, W:W + SSD_XBC], 0.0)
    xx = jnp.concatenate([prev, xbc], axis=0)
    conv = cb_ref[...]
    for j in range(SSD_CONV):
        off = 8 - (SSD_CONV - 1) + j
        conv = conv + cw_ref[j:j + 1, :] * xx[off:off + L]
    act = _silu(conv)
    xs = act[:, 0:W]
    bm = act[:, W:W + G * M]
    cm = act[:, W + G * M:W + 2 * G * M]

    dt = _softplus(dt_raw + dtb_ref[...])
    a_neg = jnp.where(_iota((1, LANES), 1) < SSD_HEADS, -jnp.exp(alog_ref[...]), 0.0)
    da = dt * a_neg
    r_i, c_i = _iota((L, L), 0), _iota((L, L), 1)
    causal = r_i >= c_i
    cs = _dot_sel_lhs(causal.astype(bf16), da, 3)
    expand = (_iota((LANES, W), 0) == _iota((LANES, W), 1) // HEAD_DIM).astype(bf16)
    dtx = _dot_sel_rhs(dt, expand, 3)
    csx = _dot_sel_rhs(cs, expand, 3)
    eye_h = (_iota((8, LANES), 0) == _iota((8, LANES), 1)).astype(bf16)
    cs_t = None
    for p in _split(cs, 3):
        t = lax.dot_general(eye_h, p, (((1,), (1,)), ((), ())), preferred_element_type=f32)
        cs_t = t if cs_t is None else cs_t + t
    xdt = xs * dtx

    st = st_ref[...]
    y_parts = []
    for gi in range(G):
        bm_g = bm[:, gi * M:(gi + 1) * M]
        cm_g = cm[:, gi * M:(gi + 1) * M]
        cb = _dot_nt(cm_g, bm_g)
        for e in range(SSD_HEADS // G):
            h = gi * (SSD_HEADS // G) + e
            seg = jnp.where(causal, jnp.exp(jnp.broadcast_to(cs[:, h:h + 1], (L, L)) - cs_t[h:h + 1, :]), 0.0)
            y_parts.append(_dot(cb * seg, xdt[:, h * HEAD_DIM:(h + 1) * HEAD_DIM]))
    y = jnp.concatenate(y_parts, axis=1)
    y_inter = jnp.concatenate(
        [_dot(cm[:, gi * M:(gi + 1) * M], st[:, gi * gw:(gi + 1) * gw]) for gi in range(G)], axis=1)
    y = y + y_inter * jnp.exp(csx)

    cs_last = csx[L - 1:L, :]
    xdtw = xdt * jnp.exp(cs_last - csx)
    d_state = jnp.concatenate(
        [_dot_tn(bm[:, gi * M:(gi + 1) * M], xdtw[:, gi * gw:(gi + 1) * gw]) for gi in range(G)], axis=1)
    st_ref[...] = st * jnp.exp(cs_last) + d_state

    y = (y + dsk_ref[...] * xs) * _silu(z)
    outs = []
    for gi in range(G):
        yg = y[:, gi * gw:(gi + 1) * gw]
        ms = jnp.mean(yg * yg, axis=-1, keepdims=True)
        outs.append(yg * lax.rsqrt(ms + LN_EPS))
    o_ref[0] = jnp.concatenate(outs, axis=1) * ng_ref[...]


def _ssd(ps, cw, cb, dtb, alog, dsk, ng):
    nb, t, cols = ps.shape
    L = SSD_BLOCK
    full2 = lambda a: pl.BlockSpec(a.shape, lambda b, i: (0, 0))
    return pl.pallas_call(
        _ssd_kernel,
        grid=(nb, t // L),
        in_specs=[
            pl.BlockSpec((1, L, cols), lambda b, i: (b, i, 0)),
            pl.BlockSpec((1, 8, cols), lambda b, i: (b, jnp.maximum(i * (L // 8) - 1, 0), 0)),
            full2(cw), full2(cb), full2(dtb), full2(alog), full2(dsk), full2(ng),
        ],
        out_specs=pl.BlockSpec((1, L, SSD_WIDTH), lambda b, i: (b, i, 0)),
        out_shape=jax.ShapeDtypeStruct((nb, t, SSD_WIDTH), f32),
        scratch_shapes=[pltpu.VMEM((SSD_STATE, SSD_WIDTH), f32)],
        compiler_params=_cparams(("arbitrary", "arbitrary")),
        name="ssd_mixer",
    )(ps, ps, cw, cb, dtb, alog, dsk, ng)


def _rwkv_steps(sub, p_ref, prev_ref, mu_ref, w0_ref, w2_ref, a0_ref, a2_ref, g2_ref, kk_ref, ka_ref, rk_ref,
                lng_ref, lnb_ref, o_ref, zt_ref):
    nb = p_ref.shape[0]
    C, W = RWKV_CHUNK, RWKV_WIDTH
    H = LANES // HEAD_DIM
    npair = W // LANES
    i = pl.program_id(0)
    r0 = sub * C

    if sub == 0:
        @pl.when(i == 0)
        def _():
            zt_ref[...] = jnp.zeros_like(zt_ref)

    bones = _block_ones(W, HEAD_DIM).astype(bf16)
    pair_blk = _block_ones(LANES, HEAD_DIM)
    n = H * C
    rr, cc = _iota((n, n), 0), _iota((n, n), 1)
    same = rr // C == cc // C
    strict = same & (rr > cc)
    lower = same & (rr >= cc)
    eye = jnp.where(rr == cc, 1.0, 0.0)

    def stack(t):
        return jnp.where(pair_blk, jnp.concatenate([t] * H, axis=0), 0.0)

    rows = range(nb)
    nr = nb * C
    cols = p_ref[:, r0:r0 + C, :].reshape(nr, p_ref.shape[2])

    def row_before(b):
        return p_ref[b, r0 - 1:r0, :] if sub else jnp.where(i > 0, prev_ref[b, 7:8, :], 0.0)

    shifted = jnp.concatenate(
        [piece for b in rows for piece in (row_before(b), p_ref[b, r0:r0 + C - 1, :])], axis=0)
    cols = cols + (shifted - cols) * mu_ref[...]
    r = cols[:, 0:W]
    k = cols[:, W:2 * W]
    v = cols[:, 2 * W:3 * W]
    w_lr = cols[:, 3 * W:3 * W + 64]
    a_lr = cols[:, 3 * W + 64:3 * W + 128]
    g_lr = cols[:, 3 * W + 128:3 * W + 256]

    wv = -_softplus(-(w0_ref[...] + _dot_hi(jnp.tanh(w_lr), w2_ref[...]))) - 0.5
    lw = -jnp.exp(wv)
    a = jax.nn.sigmoid(a0_ref[...] + _dot_hi(a_lr, a2_ref[...]))
    g = _dot_hi(jax.nn.sigmoid(g_lr), g2_ref[...])
    yield

    kk = k * kk_ref[...]
    k2 = k * (1.0 + (a - 1.0) * ka_ref[...])
    head_sums = _dot_sel_rhs(jnp.concatenate([kk * kk, r * k2 * rk_ref[...]], axis=0), bones, 2)
    kk = kk / jnp.maximum(jnp.sqrt(head_sums[:nr]), 1e-12)
    bonus_v = head_sums[nr:] * v
    bvec = kk * a
    yield

    rb_i, cb_i = _iota((nr, nr), 0), _iota((nr, nr), 1)
    tri = ((rb_i >= cb_i) & (rb_i // C == cb_i // C)).astype(bf16)
    cum = _dot_sel_lhs(tri, lw, 3)
    cum_last = [cum[(b + 1) * C - 1:(b + 1) * C, :] for b in rows]
    e_neg = jnp.exp(-cum)
    e_rem = jnp.exp(jnp.concatenate([jnp.broadcast_to(cl, (C, W)) for cl in cum_last], axis=0) - cum)
    per_token = dict(aw=-kk * jnp.exp(cum - lw), rw=r * jnp.exp(cum), bn=bvec * e_neg, kn=k2 * e_neg,
                     bt=bvec * e_rem, kt=k2 * e_rem, vs=v)
    chains = [(b, p) for b in rows for p in range(npair)]
    q = [dict({name: stack(t[b * C:(b + 1) * C, p * LANES:(p + 1) * LANES]) for name, t in per_token.items()},
              wc=jnp.exp(cum_last[b][:, p * LANES:(p + 1) * LANES])) for b, p in chains]
    ids = range(len(chains))
    yield

    a_ab = [jnp.where(strict, _dot_nt(q[c]["aw"], q[c]["bn"]), 0.0) for c in ids]
    a_ak = [jnp.where(strict, _dot_nt(q[c]["aw"], q[c]["kn"]), 0.0) for c in ids]
    yield
    a_rb = [jnp.where(lower, _dot_nt(q[c]["rw"], q[c]["bn"]), 0.0) for c in ids]
    a_rk = [jnp.where(lower, _dot_nt(q[c]["rw"], q[c]["kn"]), 0.0) for c in ids]
    yield

    pw = a_ab
    tinv = [eye + a_ab[c] for c in ids]
    for _ in range(int(math.log2(C)) - 1):
        pw = [_dot(pw[c], pw[c]) for c in ids]
        yield
        tinv = [tinv[c] + _dot(tinv[c], pw[c]) for c in ids]
        yield

    zt = [zt_ref[b, p] for b, p in chains]
    rhs = [_dot_nt(q[c]["aw"], zt[c]) + _dot(a_ak[c], q[c]["vs"]) for c in ids]
    yield
    y0 = [_dot_nt(q[c]["rw"], zt[c]) + _dot(a_rk[c], q[c]["vs"]) for c in ids]
    yield
    u = [_dot(tinv[c], rhs[c]) for c in ids]
    yield
    y_st = [y0[c] + _dot(a_rb[c], u[c]) for c in ids]
    yield
    for c, (b, p) in enumerate(chains):
        zt_ref[b, p] = zt[c] * q[c]["wc"] + _dot_tn(u[c], q[c]["bt"]) + _dot_tn(q[c]["vs"], q[c]["kt"])
    yield

    inv_n = 1.0 / HEAD_DIM
    ys = []
    for b in rows:
        tiles = []
        for p in range(npair):
            y_c = y_st[b * npair + p]
            y = y_c[0:C]
            for h in range(1, H):
                y = y + y_c[h * C:(h + 1) * C]
            tiles.append(y)
        ys.append(jnp.concatenate(tiles, axis=1))
    y = jnp.concatenate(ys, axis=0)
    yc = y - _dot_sel_rhs(y, bones, 2) * inv_n
    var_h = _dot_sel_rhs(yc * yc, bones, 2) * inv_n
    y = yc * lax.rsqrt(var_h + RWKV_GN_EPS) * lng_ref[...] + lnb_ref[...]
    o_ref[:, r0:r0 + C, :] = ((y + bonus_v) * g).reshape(nb, C, W)


N_GLA_IN, N_RWKV_IN = 4, 13


def _gla_rwkv_kernel(*refs):
    gla_in = refs[:N_GLA_IN]
    rwkv_in = refs[N_GLA_IN:N_GLA_IN + N_RWKV_IN]
    og_ref, or_ref, st_ref, t_ref, d_ref, sx_ref, zt_ref = refs[N_GLA_IN + N_RWKV_IN:]
    n_sub = rwkv_in[0].shape[1] // RWKV_CHUNK
    rwkv = (step for sub in range(n_sub) for step in _rwkv_steps(sub, *rwkv_in, or_ref, zt_ref))
    gla = _gla_steps(*gla_in, og_ref, st_ref, t_ref, d_ref, sx_ref)
    live = [rwkv, gla]
    while live:
        for gen in list(live):
            if next(gen, StopIteration) is StopIteration:
                live.remove(gen)


def _gla_rwkv(pg, gla_params, pr, rwkv_params):
    nb, t, gcols = pg.shape
    rcols = pr.shape[2]
    tb = GLA_BLOCK
    npt = GLA_WIDTH // LANES
    pair_rows = (GLA_CHUNK * (GLA_CHUNK + 1) // 2) * (tb // GLA_CHUNK)
    full2 = lambda a: pl.BlockSpec(a.shape, lambda i: (0, 0))
    blk = lambda c: pl.BlockSpec((nb, tb, c), lambda i: (0, i, 0))
    assert len(gla_params) + 1 == N_GLA_IN and len(rwkv_params) + 2 == N_RWKV_IN
    return pl.pallas_call(
        _gla_rwkv_kernel,
        grid=(t // tb,),
        in_specs=[blk(gcols)] + [full2(p) for p in gla_params]
        + [blk(rcols), pl.BlockSpec((nb, 8, rcols), lambda i: (0, jnp.maximum(i * (tb // 8) - 1, 0), 0))]
        + [full2(p) for p in rwkv_params],
        out_specs=[blk(GLA_WIDTH), blk(RWKV_WIDTH)],
        out_shape=[jax.ShapeDtypeStruct((nb, t, GLA_WIDTH), f32), jax.ShapeDtypeStruct((nb, t, RWKV_WIDTH), f32)],
        scratch_shapes=[
            pltpu.VMEM((nb, npt, LANES, LANES), f32),
            pltpu.VMEM((nb, 5, npt, tb, LANES), f32),
            pltpu.VMEM((nb, pair_rows, GLA_WIDTH), f32),
            pltpu.VMEM((nb, pair_rows, GLA_WIDTH), f32),
            pltpu.VMEM((nb, RWKV_WIDTH // LANES, LANES, LANES), f32),
        ],
        compiler_params=_cparams(("arbitrary",)),
        name="gla_rwkv_mixers",
    )(pg, *gla_params, pr, pr, *rwkv_params)


def _route(logits, rb):
    tm = logits.shape[1]
    scores = jax.nn.sigmoid(logits)
    sel = scores + rb
    srow = [sel[e:e + 1, :] for e in range(N_EXPERTS)]
    crow = [scores[e:e + 1, :] for e in range(N_EXPERTS)]

    def top2_sum(v0, v1, v2, v3):
        m01, n01 = jnp.maximum(v0, v1), jnp.minimum(v0, v1)
        m23, n23 = jnp.maximum(v2, v3), jnp.minimum(v2, v3)
        return jnp.maximum(m01, m23) + jnp.maximum(jnp.minimum(m01, m23), jnp.maximum(n01, n23))

    gscore = [top2_sum(*srow[4 * gi:4 * gi + 4]) for gi in range(N_EXPERT_GROUPS)]
    best = jnp.zeros((1, tm), jnp.int32)
    best_v = gscore[0]
    for gi in range(1, N_EXPERT_GROUPS):
        better = gscore[gi] > best_v
        best = jnp.where(better, gi, best)
        best_v = jnp.where(better, gscore[gi], best_v)
    sv, cv = [], []
    for j in range(EXPERTS_PER_GROUP):
        s_j, c_j = srow[j], crow[j]
        for gi in range(1, N_EXPERT_GROUPS):
            s_j = jnp.where(best == gi, srow[4 * gi + j], s_j)
            c_j = jnp.where(best == gi, crow[4 * gi + j], c_j)
        sv.append(s_j)
        cv.append(c_j)
    j1 = jnp.zeros((1, tm), jnp.int32)
    v1 = sv[0]
    for j in range(1, EXPERTS_PER_GROUP):
        better = sv[j] > v1
        j1 = jnp.where(better, j, j1)
        v1 = jnp.where(better, sv[j], v1)
    j2 = jnp.full((1, tm), -1, jnp.int32)
    v2 = jnp.full((1, tm), -jnp.inf, f32)
    for j in range(EXPERTS_PER_GROUP):
        better = (j1 != j) & ((j2 < 0) | (sv[j] > v2))
        j2 = jnp.where(better, j, j2)
        v2 = jnp.where(better, sv[j], v2)
    w1 = cv[0]
    w2 = cv[0]
    for j in range(1, EXPERTS_PER_GROUP):
        w1 = jnp.where(j1 == j, cv[j], w1)
        w2 = jnp.where(j2 == j, cv[j], w2)
    wsum = w1 + w2
    g1, g2 = w1 / wsum, w2 / wsum
    rows = []
    for e in range(N_EXPERTS):
        gi, j = divmod(e, EXPERTS_PER_GROUP)
        in_g = best == gi
        rows.append(jnp.where(in_g & (j1 == j), g1, jnp.where(in_g & (j2 == j), g2, 0.0)))
    gate_t = jnp.concatenate(rows, axis=0)
    return gate_t, best


def _outproj_kernel(alpha, x_ref, og_ref, os_ref, or_ref, wo_ref, g_ref, b_ref, rwt_ref, rb_ref,
                    x1_ref, gate_ref, best_ref, cnt_ref):
    tm = x_ref.shape[0]
    ts = tm // cnt_ref.shape[0]
    parts = [slice(sb * ts, (sb + 1) * ts) for sb in range(cnt_ref.shape[0])]
    d = functools.partial(jnp.dot, preferred_element_type=f32)
    mix = [d(_bf(og_ref[s, :]), wo_ref[0:GLA_WIDTH, :])
           + d(_bf(os_ref[s, :]), wo_ref[GLA_WIDTH:GLA_WIDTH + SSD_WIDTH, :])
           + d(_bf(or_ref[s, :]), wo_ref[GLA_WIDTH + SSD_WIDTH:, :]) for s in parts]
    x1 = [_layer_norm(alpha * x_ref[s, :] + m, g_ref[...], b_ref[...], LN_EPS) for s, m in zip(parts, mix)]
    for s, v in zip(parts, x1):
        x1_ref[s, :] = v

    wh, wl = _split(rwt_ref[...], 2)
    nt = lambda a, b: lax.dot_general(a, b, (((1,), (1,)), ((), ())), preferred_element_type=f32)
    logits = []
    for v in x1:
        xh, xl = _split(v, 2)
        logits.append(nt(wh, xh) + nt(wl, xh) + nt(wh, xl))
    routed = [_route(lg, rb_ref[...]) for lg in logits]
    eye = (_iota((N_EXPERTS, LANES), 0) == _iota((N_EXPERTS, LANES), 1)).astype(bf16)
    lane = _iota((1, LANES), 1)
    for sb, (s, (gate_t, best)) in enumerate(zip(parts, routed)):
        gate = None
        for p in _split(gate_t, 3):
            t = lax.dot_general(p, eye, (((0,), (0,)), ((), ())), preferred_element_type=f32)
            gate = t if gate is None else gate + t
        gate_ref[s, :] = gate
        best_ref[:, s] = best
        cnt = jnp.zeros((1, LANES), jnp.int32)
        for gi in range(N_EXPERT_GROUPS):
            n_g = jnp.sum((best == gi).astype(jnp.int32), axis=-1, keepdims=True)
            cnt = jnp.where(lane == gi, n_g, cnt)
        cnt_ref[sb] = cnt


def _out_proj(alpha, x2, og, os_, or_, wo, g, b, rwt, rb):
    n, d = x2.shape
    tm = min(OUT_PROJ_ROWS, n)
    ts = min(MOE_SORT_ROWS, tm)
    full2 = lambda a: pl.BlockSpec(a.shape, lambda i: (0, 0))
    rows = lambda c: pl.BlockSpec((tm, c), lambda i: (i, 0))
    return pl.pallas_call(
        functools.partial(_outproj_kernel, alpha),
        grid=(n // tm,),
        in_specs=[rows(d), rows(og.shape[1]), rows(os_.shape[1]), rows(or_.shape[1]),
                  full2(wo), full2(g), full2(b), full2(rwt), full2(rb)],
        out_specs=[rows(d), rows(LANES), pl.BlockSpec((1, tm), lambda i: (0, i)),
                   pl.BlockSpec((tm // ts, 1, LANES), lambda i: (i, 0, 0))],
        out_shape=[jax.ShapeDtypeStruct((n, d), f32), jax.ShapeDtypeStruct((n, LANES), f32),
                   jax.ShapeDtypeStruct((1, n), jnp.int32), jax.ShapeDtypeStruct((n // ts, 1, LANES), jnp.int32)],
        compiler_params=_cparams(("arbitrary",)),
        name="out_proj_router",
    )(x2, og, os_, or_, wo, g, b, rwt, rb)


def _seg_copy(src, dst, sem, src_off, dst_off, rows, max_rows, wait):
    k = SEG_ALIGN
    sizes = []
    while k <= max_rows:
        sizes.append(k)
        k *= 2
    for k in reversed(sizes):
        shift = int(math.log2(k)) + 1

        @pl.when((rows & k) != 0)
        def _():
            done = (rows >> shift) << shift
            cp = pltpu.make_async_copy(
                src.at[pl.ds(pl.multiple_of(src_off + done, SEG_ALIGN), k)],
                dst.at[pl.ds(pl.multiple_of(dst_off + done, SEG_ALIGN), k)], sem)
            if wait:
                cp.wait()
            else:
                cp.start()


def _padded(count):
    return (count + (SEG_ALIGN - 1)) & (-SEG_ALIGN)


def _one_hot_rows(pos, n_rows):
    return jnp.where(_iota((n_rows, pos.shape[1]), 0) == pos, 1.0, 0.0).astype(bf16)


def _dispatch_kernel(cnt_sm, goff_sm, fill_sm, x_ref, gate_ref, best_ref, xs_out, gs_out, pos_ref,
                     xloc, gloc, zx, zg, sems):
    i = pl.program_id(0)
    tm = x_ref.shape[0]
    n_loc = xloc.shape[1]
    slot = i % 2

    def copies(blk, slot_, wait):
        loff = 0
        for gi in range(N_EXPERT_GROUPS):
            rows = _padded(cnt_sm[blk * N_EXPERT_GROUPS + gi])
            dst = goff_sm[blk * N_EXPERT_GROUPS + gi]
            _seg_copy(xloc.at[slot_], xs_out, sems.at[0, slot_], loff, dst, rows, tm, wait)
            _seg_copy(gloc.at[slot_], gs_out, sems.at[1, slot_], loff, dst, rows, tm, wait)
            loff = loff + rows

    best = best_ref[...]
    onehot = (_iota((8, tm), 0) == best).astype(bf16)
    earlier = (_iota((tm, tm), 0) < _iota((tm, tm), 1)).astype(bf16)
    rank = jnp.dot(onehot, earlier, preferred_element_type=f32)
    pos = jnp.zeros((1, tm), f32)
    loff = 0
    for gi in range(N_EXPERT_GROUPS):
        pos = jnp.where(best == gi, rank[gi:gi + 1, :] + loff.astype(f32) if gi else rank[gi:gi + 1, :], pos)
        loff = loff + _padded(cnt_sm[i * N_EXPERT_GROUPS + gi])
    pos = pos.astype(jnp.int32)
    pos_ref[...] = pos
    perm = _one_hot_rows(pos, n_loc)
    xloc[slot] = jnp.dot(perm, _bf(x_ref[...]), preferred_element_type=f32)
    gloc[slot] = _dot_sel_lhs(perm, gate_ref[...], 3)
    copies(i, slot, False)

    @pl.when(i > 0)
    def _():
        copies(i - 1, 1 - slot, True)

    @pl.when(i == pl.num_programs(0) - 1)
    def _():
        copies(i, slot, True)
        zx[...] = jnp.zeros_like(zx)
        zg[...] = jnp.zeros_like(zg)
        tile = zx.shape[0]

        def fills(wait):
            for src, dst, sem in ((zx, xs_out, sems.at[2, 0]), (zg, gs_out, sems.at[2, 1])):
                for gi in range(N_EXPERT_GROUPS):
                    _seg_copy(src, dst, sem, 0, fill_sm[gi], fill_sm[N_EXPERT_GROUPS + gi], tile // 2, wait)

                def body(j, carry):
                    off = pl.multiple_of(fill_sm[2 * N_EXPERT_GROUPS] + j * tile, SEG_ALIGN)
                    cp = pltpu.make_async_copy(src, dst.at[pl.ds(off, tile)], sem)
                    if wait:
                        cp.wait()
                    else:
                        cp.start()
                    return carry

                lax.fori_loop(0, fill_sm[2 * N_EXPERT_GROUPS + 1], body, 0)

        fills(False)
        fills(True)


def _experts_kernel(tg_sm, tv_sm, x_ref, gate_ref, wg_ref, wu_ref, wd_ref, y_ref):
    t = pl.program_id(0)

    @pl.when(tv_sm[t] == 0)
    def _():
        y_ref[...] = jnp.zeros_like(y_ref)

    @pl.when(tv_sm[t] > 0)
    def _():
        first = tg_sm[t] * EXPERTS_PER_GROUP
        xb = _bf(x_ref[...])
        gate = gate_ref[...]
        lane = _iota(gate.shape, 1)
        hs = []
        for j in range(EXPERTS_PER_GROUP):
            hg = jnp.dot(xb, wg_ref[j], preferred_element_type=f32)
            hu = jnp.dot(xb, wu_ref[j], preferred_element_type=f32)
            gcol = jnp.sum(jnp.where(lane == first + j, gate, 0.0), axis=-1, keepdims=True)
            hs.append(_bf(_silu(hg) * hu * gcol))
        ff = wd_ref.shape[1]
        y_ref[...] = jnp.dot(jnp.concatenate(hs, axis=1), wd_ref[...].reshape(EXPERTS_PER_GROUP * ff, -1),
                             preferred_element_type=f32)


def _combine_kernel(alpha, cnt_sm, goff_sm, x_ref, pos_ref, g_ref, b_ref, ys_ref, o_ref, yloc, sems):
    i = pl.program_id(0)
    tm = x_ref.shape[0]
    n_loc = yloc.shape[1]
    slot = i % 2

    def copies(blk, slot_, wait):
        loff = 0
        for gi in range(N_EXPERT_GROUPS):
            rows = _padded(cnt_sm[blk * N_EXPERT_GROUPS + gi])
            src = goff_sm[blk * N_EXPERT_GROUPS + gi]
            _seg_copy(ys_ref, yloc.at[slot_], sems.at[slot_], src, loff, rows, tm, wait)
            loff = loff + rows

    @pl.when(i == 0)
    def _():
        yloc[...] = jnp.zeros_like(yloc)
        copies(0, 0, False)

    @pl.when(i + 1 < pl.num_programs(0))
    def _():
        copies(i + 1, 1 - slot, False)

    copies(i, slot, True)
    perm = _one_hot_rows(pos_ref[...], n_loc)
    y = lax.dot_general(perm, _bf(yloc[slot]), (((0,), (0,)), ((), ())), preferred_element_type=f32)
    o_ref[...] = _layer_norm(alpha * x_ref[...] + y, g_ref[...], b_ref[...], LN_EPS)


def _moe_tables(cnt, n_tiles, tile):
    pc = (cnt + (SEG_ALIGN - 1)) // SEG_ALIGN * SEG_ALIGN
    gtot = jnp.sum(pc, axis=0)
    gcap = (gtot + tile - 1) // tile * tile
    gend = jnp.cumsum(gcap)
    gstart = gend - gcap
    goff = gstart[None, :] + jnp.cumsum(pc, axis=0) - pc
    tstart = jnp.arange(n_tiles, dtype=jnp.int32) * tile
    tg = jnp.minimum(jnp.sum((tstart[:, None] >= gend[None, :]).astype(jnp.int32), axis=1), N_EXPERT_GROUPS - 1)
    tv = jnp.clip(gtot[tg] - (tstart - gstart[tg]), 0, tile)
    fill = jnp.concatenate([gstart + gtot, gcap - gtot, gend[-1:], n_tiles - gend[-1:] // tile])
    return goff.reshape(-1).astype(jnp.int32), tg.astype(jnp.int32), tv.astype(jnp.int32), fill.astype(jnp.int32)


def _moe(alpha, x1, gate, best, cnt, layer, wg, wu, wd, g, b):
    n, d = x1.shape
    ff = wg.shape[3]
    nblk = cnt.shape[0]
    tm = n // nblk
    tile = MOE_TILE
    n_loc = tm + LANES
    n_sorted = -(-(n + nblk * N_EXPERT_GROUPS * SEG_ALIGN + N_EXPERT_GROUPS * tile) // tile) * tile
    n_tiles = n_sorted // tile
    cnt_flat = cnt[:, 0, :N_EXPERT_GROUPS].reshape(-1)
    goff, tg, tv, fill = _moe_tables(cnt[:, 0, :N_EXPERT_GROUPS], n_tiles, tile)
    any_spec = pl.BlockSpec(memory_space=pl.ANY)

    xs, gs, pos = pl.pallas_call(
        _dispatch_kernel,
        grid_spec=pltpu.PrefetchScalarGridSpec(
            num_scalar_prefetch=3,
            grid=(nblk,),
            in_specs=[
                pl.BlockSpec((tm, d), lambda i, *_: (i, 0)),
                pl.BlockSpec((tm, LANES), lambda i, *_: (i, 0)),
                pl.BlockSpec((1, tm), lambda i, *_: (0, i)),
            ],
            out_specs=[any_spec, any_spec, pl.BlockSpec((1, tm), lambda i, *_: (0, i))],
            scratch_shapes=[pltpu.VMEM((2, n_loc, d), f32), pltpu.VMEM((2, n_loc, LANES), f32),
                            pltpu.VMEM((tile, d), f32), pltpu.VMEM((tile, LANES), f32),
                            pltpu.SemaphoreType.DMA((3, 2))],
        ),
        out_shape=[jax.ShapeDtypeStruct((n_sorted, d), f32), jax.ShapeDtypeStruct((n_sorted, LANES), f32),
                   jax.ShapeDtypeStruct((1, n), jnp.int32)],
        compiler_params=_cparams(("arbitrary",)),
        name="moe_dispatch",
    )(cnt_flat, goff, fill, x1, gate, best)

    ys = pl.pallas_call(
        _experts_kernel,
        grid_spec=pltpu.PrefetchScalarGridSpec(
            num_scalar_prefetch=2,
            grid=(n_tiles,),
            in_specs=[
                pl.BlockSpec((tile, d), lambda t, *_: (t, 0)),
                pl.BlockSpec((tile, LANES), lambda t, *_: (t, 0)),
                pl.BlockSpec((None, EXPERTS_PER_GROUP, d, ff), lambda t, tg_, tv_: (layer, tg_[t], 0, 0)),
                pl.BlockSpec((None, EXPERTS_PER_GROUP, d, ff), lambda t, tg_, tv_: (layer, tg_[t], 0, 0)),
                pl.BlockSpec((None, EXPERTS_PER_GROUP, ff, d), lambda t, tg_, tv_: (layer, tg_[t], 0, 0)),
            ],
            out_specs=pl.BlockSpec((tile, d), lambda t, *_: (t, 0)),
        ),
        out_shape=jax.ShapeDtypeStruct((n_sorted, d), f32),
        compiler_params=_cparams(("arbitrary",)),
        name="moe_experts",
    )(tg, tv, xs, gs, wg, wu, wd)

    return pl.pallas_call(
        functools.partial(_combine_kernel, alpha),
        grid_spec=pltpu.PrefetchScalarGridSpec(
            num_scalar_prefetch=2,
            grid=(nblk,),
            in_specs=[
                pl.BlockSpec((tm, d), lambda i, *_: (i, 0)),
                pl.BlockSpec((1, tm), lambda i, *_: (0, i)),
                pl.BlockSpec(g.shape, lambda i, *_: (0, 0)),
                pl.BlockSpec(b.shape, lambda i, *_: (0, 0)),
                any_spec,
            ],
            out_specs=pl.BlockSpec((tm, d), lambda i, *_: (i, 0)),
            scratch_shapes=[pltpu.VMEM((2, n_loc, d), f32), pltpu.SemaphoreType.DMA((2,))],
        ),
        out_shape=jax.ShapeDtypeStruct((n, d), f32),
        compiler_params=_cparams(("arbitrary",)),
        name="moe_combine",
    )(cnt_flat, goff, x1, pos, g, b, ys)


def _pad_cols(w, n):
    return jnp.pad(w, ((0, 0), (0, n - w.shape[1])))


def _row(v, n=None):
    v = v.reshape(1, -1)
    return v if n is None else _pad_cols(v, n)


def kernel(x, w_in, gla_w_alpha, gla_b_alpha, gla_norm_g, ssd_conv_w, ssd_conv_b, ssd_dt_bias, ssd_A_log, ssd_D, ssd_norm_g, rwkv_mu, rwkv_w0, rwkv_w2, rwkv_a0, rwkv_a2, rwkv_g2, rwkv_k_k, rwkv_k_a, rwkv_r_k, rwkv_ln_g, rwkv_ln_b, w_out, ln1_g, ln1_b, router_w, router_b, exp_w_gate, exp_w_up, exp_w_down, ln2_g, ln2_b):
    nb, t, d = x.shape
    depth = w_in.shape[0]
    alpha = float((2 * depth) ** 0.25)
    gla_cols = 4 * GLA_WIDTH + GLA_GATE_RANK
    ssd_cols = SSD_WIDTH + SSD_XBC + SSD_HEADS
    rwt = router_w.T
    rb = router_b.reshape(-1, 1)
    wg_all, wu_all, wd_all = _bf(exp_w_gate), _bf(exp_w_up), _bf(exp_w_down)
    x2 = x.reshape(nb * t, d)
    for l in range(depth):
        wg, ws, wr = _regroup_w_in(w_in, l, (gla_cols, ssd_cols, RWKV_COLS), (GLA_COLS_PAD, SSD_COLS_PAD, RWKV_COLS))
        pg, ps, pr = _in_proj(x2, wg, ws, wr)
        gla_params = (jnp.pad(gla_w_alpha[l], ((0, LANES - GLA_GATE_RANK), (0, 0))), _row(gla_b_alpha[l]),
                      _row(jnp.tile(gla_norm_g[l], GLA_WIDTH // HEAD_DIM)))
        rwkv_params = (_row(rwkv_mu[l]), _row(rwkv_w0[l]), rwkv_w2[l], _row(rwkv_a0[l]), rwkv_a2[l], rwkv_g2[l],
                       _row(rwkv_k_k[l]), _row(rwkv_k_a[l]), _row(rwkv_r_k[l]), _row(rwkv_ln_g[l]),
                       _row(rwkv_ln_b[l]))
        og, or_ = _gla_rwkv(pg.reshape(nb, t, -1), gla_params, pr.reshape(nb, t, -1), rwkv_params)
        os_ = _ssd(ps.reshape(nb, t, -1), ssd_conv_w[l], _row(ssd_conv_b[l]), _row(ssd_dt_bias[l], LANES),
                   _row(ssd_A_log[l], LANES), _row(jnp.repeat(ssd_D[l], HEAD_DIM)), _row(ssd_norm_g[l]))
        x1, gate, best, cnt = _out_proj(alpha, x2, og.reshape(nb * t, -1), os_.reshape(nb * t, -1),
                                        or_.reshape(nb * t, -1), _bf(w_out[l]), _row(ln1_g[l]), _row(ln1_b[l]),
                                        rwt, rb)
        x2 = _moe(alpha, x1, gate, best, cnt, l, wg_all, wu_all, wd_all, _row(ln2_g[l]), _row(ln2_b[l]))
    return x2.reshape(nb, t, d)
```

```python
import functools
import math

import jax
import jax.numpy as jnp
from jax import lax
from jax.experimental import pallas as pl
from jax.experimental.pallas import tpu as pltpu

f32 = jnp.float32
bf16 = jnp.bfloat16

LANES = 128
HEAD_DIM = 64
GLA_WIDTH = 256
GLA_GATE_RANK = 16
GLA_GATE_TAU = 16.0
GLA_CHUNK = 16
GLA_BLOCK = 128
GLA_COLS_PAD = 4 * GLA_WIDTH + LANES
SSD_WIDTH = 512
SSD_HEADS = 8
SSD_GROUPS = 2
SSD_STATE = 64
SSD_CONV = 4
SSD_XBC = SSD_WIDTH + 2 * SSD_GROUPS * SSD_STATE
SSD_BLOCK = 128
SSD_COLS_PAD = SSD_WIDTH + SSD_XBC + LANES
RWKV_WIDTH = 256
RWKV_HEADS = 4
RWKV_CHUNK = 64
RWKV_GN_EPS = 64e-5
RWKV_COLS = 3 * RWKV_WIDTH + 64 + 64 + 128
N_EXPERTS = 16
N_EXPERT_GROUPS = 4
EXPERTS_PER_GROUP = 4
LN_EPS = 1e-5

PROJ_CHUNK = 256
PROJ_PACE = 4
OUT_PROJ_ROWS = 1024
MOE_SORT_ROWS = 512
MOE_TILE = 512
SEG_ALIGN = 8
VMEM_LIMIT = 56 * 1024 * 1024


def _bf(x):
    return x.astype(bf16)


def _dot(a, b):
    return jnp.dot(_bf(a), _bf(b), preferred_element_type=f32)


def _dot_nt(a, b):
    return lax.dot_general(_bf(a), _bf(b), (((1,), (1,)), ((), ())), preferred_element_type=f32)


def _dot_tn(a, b):
    return lax.dot_general(_bf(a), _bf(b), (((0,), (0,)), ((), ())), preferred_element_type=f32)


def _split(x, n):
    parts, r = [], x
    for _ in range(n):
        p = r.astype(bf16)
        parts.append(p)
        r = r - p.astype(f32)
    return parts


def _dot_sel_rhs(a, sel, n=3):
    out = None
    for p in _split(a, n):
        t = jnp.dot(p, sel, preferred_element_type=f32)
        out = t if out is None else out + t
    return out


def _dot_sel_lhs(sel, b, n=3):
    out = None
    for p in _split(b, n):
        t = jnp.dot(sel, p, preferred_element_type=f32)
        out = t if out is None else out + t
    return out


def _dot_hi(a, b):
    ah, al = _split(a, 2)
    bh, bl = _split(b, 2)
    d = functools.partial(jnp.dot, preferred_element_type=f32)
    return d(ah, bh) + d(al, bh) + d(ah, bl)


def _iota(shape, dim):
    return lax.broadcasted_iota(jnp.int32, shape, dim)


def _block_ones(n, blk):
    return (_iota((n, n), 0) // blk == _iota((n, n), 1) // blk)


def _silu(x):
    return x * jax.nn.sigmoid(x)


def _softplus(x):
    return jnp.maximum(x, 0.0) + jnp.log1p(jnp.exp(-jnp.abs(x)))


def _layer_norm(y, g, b, eps):
    mu = jnp.mean(y, axis=-1, keepdims=True)
    yc = y - mu
    var = jnp.mean(yc * yc, axis=-1, keepdims=True)
    return yc * lax.rsqrt(var + eps) * g + b


def _cparams(sem):
    return pltpu.CompilerParams(dimension_semantics=sem, vmem_limit_bytes=VMEM_LIMIT)


def _regroup_kernel(splits, w_ref, *o_refs):
    w = w_ref[0]
    start = 0
    for width, o_ref in zip(splits, o_refs):
        part = w[:, start:start + width]
        pad = o_ref.shape[1] - width
        if pad:
            part = jnp.concatenate([part, jnp.zeros((part.shape[0], pad), part.dtype)], axis=1)
        o_ref[...] = _bf(part)
        start += width


def _regroup_w_in(w_in, layer, splits, padded):
    _, d, n_in = w_in.shape
    rows = 256
    return pl.pallas_call(
        functools.partial(_regroup_kernel, splits),
        grid=(d // rows,),
        in_specs=[pl.BlockSpec((1, rows, n_in), lambda i: (layer, i, 0))],
        out_specs=[pl.BlockSpec((rows, p), lambda i: (i, 0)) for p in padded],
        out_shape=[jax.ShapeDtypeStruct((d, p), bf16) for p in padded],
        compiler_params=_cparams(("arbitrary",)),
        name="regroup_w_in",
    )(w_in)


def _gla_steps(p_ref, wa_ref, ba_ref, ng_ref, o_ref, st_ref, t_ref, d_ref, sx_ref):
    nb, tb, _ = p_ref.shape
    w, c = GLA_WIDTH, GLA_CHUNK
    nc = tb // c
    npt = w // LANES
    Q, K, V, CUM, OUT = range(5)

    @pl.when(pl.program_id(0) == 0)
    def _():
        st_ref[...] = jnp.zeros_like(st_ref)

    bones = _block_ones(w, HEAD_DIM).astype(bf16)
    pair_blk = _block_ones(LANES, HEAD_DIM)
    r_i, c_i = _iota((tb, tb), 0), _iota((tb, tb), 1)
    tri = ((r_i >= c_i) & (r_i // c == c_i // c)).astype(bf16)
    wa_h, wa_l = _split(wa_ref[...], 2)
    dd = functools.partial(jnp.dot, preferred_element_type=f32)

    for b in range(nb):
        a_h, a_l = _split(p_ref[b, :, 4 * w:4 * w + LANES], 2)
        z = dd(a_h, wa_h) + dd(a_l, wa_h) + dd(a_h, wa_l) + ba_ref[...]
        log_a = -_softplus(-z) * (1.0 / GLA_GATE_TAU)
        cum = _dot_sel_lhs(tri, log_a, 3)
        for lt in range(npt):
            lanes = slice(lt * LANES, (lt + 1) * LANES)
            t_ref[b, Q, lt] = p_ref[b, :, lanes] * (HEAD_DIM ** -0.5)
            t_ref[b, K, lt] = p_ref[b, :, w + lt * LANES:w + (lt + 1) * LANES]
            t_ref[b, V, lt] = p_ref[b, :, 2 * w + lt * LANES:2 * w + (lt + 1) * LANES]
            t_ref[b, CUM, lt] = cum[:, lanes]
        yield

    def ld(b, j, off):
        return jnp.concatenate([t_ref[b, j, lt, pl.ds(off, nc, stride=c), :] for lt in range(npt)], axis=1)

    for b in range(nb):
        qt = [ld(b, Q, o) for o in range(c)]
        kt = [ld(b, K, o) for o in range(c)]
        ct = [ld(b, CUM, o) for o in range(c)]
        pair = 0
        for t in range(c):
            for s in range(t + 1):
                qk = qt[t] * kt[s]
                d_ref[b, pair * nc:(pair + 1) * nc, :] = qk if s == t else qk * jnp.exp(ct[t] - ct[s])
                pair += 1
            if t % 4 == 3:
                yield
    for b in range(nb):
        sx_ref[b] = jnp.dot(_bf(d_ref[b]), bones, preferred_element_type=f32)
        yield
    for b in range(nb):
        vt = [ld(b, V, o) for o in range(c)]
        pair = 0
        for t in range(c):
            acc = None
            for s in range(t + 1):
                term = sx_ref[b, pair * nc:(pair + 1) * nc, :] * vt[s]
                acc = term if acc is None else acc + term
                pair += 1
            for lt in range(npt):
                t_ref[b, OUT, lt, pl.ds(t, nc, stride=c), :] = acc[:, lt * LANES:(lt + 1) * LANES]
            if t % 8 == 7:
                yield

    for ci in range(nc):
        sl = slice(ci * c, (ci + 1) * c)
        last = slice((ci + 1) * c - 1, (ci + 1) * c)
        for b in range(nb):
            for lt in range(npt):
                cum_c = t_ref[b, CUM, lt, sl, :]
                cum_last = t_ref[b, CUM, lt, last, :]
                st = st_ref[b, lt]
                t_ref[b, OUT, lt, sl, :] += _dot_nt(t_ref[b, Q, lt, sl, :] * jnp.exp(cum_c), st)
                upd = _dot_tn(t_ref[b, V, lt, sl, :], t_ref[b, K, lt, sl, :] * jnp.exp(cum_last - cum_c))
                st_ref[b, lt] = st * jnp.exp(cum_last) + jnp.where(pair_blk, upd, 0.0)
        yield

    o = jnp.concatenate(
        [jnp.concatenate([t_ref[b, OUT, lt] for lt in range(npt)], axis=1) for b in range(nb)], axis=0)
    ms = _dot_sel_rhs(o * o, bones, 2) * (1.0 / HEAD_DIM)
    o = o * lax.rsqrt(ms + LN_EPS) * ng_ref[...]
    for b in range(nb):
        o_ref[b] = o[b * tb:(b + 1) * tb] * _silu(p_ref[b, :, 3 * w:4 * w])


def _round_robin(gens):
    live = list(gens)
    while live:
        for gen in list(live):
            if next(gen, StopIteration) is StopIteration:
                live.remove(gen)
            else:
                yield


def _ssd_steps(b, p_ref, halo_ref, cw_ref, cb_ref, dtb_ref, alog_ref, dsk_ref, ng_ref, o_ref, st_ref):
    L = p_ref.shape[1]
    W, M, G = SSD_WIDTH, SSD_STATE, SSD_GROUPS
    gw = W // G

    @pl.when(pl.program_id(0) == 0)
    def _():
        st_ref[b] = jnp.zeros(st_ref.shape[1:], f32)
        halo_ref[b] = jnp.zeros(halo_ref.shape[1:], f32)

    z = p_ref[b, :, 0:W]
    xbc = p_ref[b, :, W:W + SSD_XBC]
    dt_raw = p_ref[b, :, W + SSD_XBC:W + SSD_XBC + LANES]
    xx = jnp.concatenate([halo_ref[b], xbc], axis=0)
    halo_ref[b] = xbc[L - 8:L]
    conv = cb_ref[...]
    for j in range(SSD_CONV):
        off = 8 - (SSD_CONV - 1) + j
        conv = conv + cw_ref[j:j + 1, :] * xx[off:off + L]
    act = _silu(conv)
    xs = act[:, 0:W]
    bm = act[:, W:W + G * M]
    cm = act[:, W + G * M:W + 2 * G * M]
    yield

    dt = _softplus(dt_raw + dtb_ref[...])
    a_neg = jnp.where(_iota((1, LANES), 1) < SSD_HEADS, -jnp.exp(alog_ref[...]), 0.0)
    da = dt * a_neg
    r_i, c_i = _iota((L, L), 0), _iota((L, L), 1)
    causal = r_i >= c_i
    cs = _dot_sel_lhs(causal.astype(bf16), da, 3)
    expand = (_iota((LANES, W), 0) == _iota((LANES, W), 1) // HEAD_DIM).astype(bf16)
    dtx = _dot_sel_rhs(dt, expand, 2)
    csx = _dot_sel_rhs(cs, expand, 2)
    eye_h = (_iota((8, LANES), 0) == _iota((8, LANES), 1)).astype(bf16)
    cs_t = None
    for p in _split(cs, 2):
        t = lax.dot_general(eye_h, p, (((1,), (1,)), ((), ())), preferred_element_type=f32)
        cs_t = t if cs_t is None else cs_t + t
    xdt = xs * dtx
    yield

    st = st_ref[b]
    y_parts = []
    for gi in range(G):
        bm_g = bm[:, gi * M:(gi + 1) * M]
        cm_g = cm[:, gi * M:(gi + 1) * M]
        cb = _dot_nt(cm_g, bm_g)
        for e in range(SSD_HEADS // G):
            h = gi * (SSD_HEADS // G) + e
            seg = jnp.where(causal, jnp.exp(jnp.broadcast_to(cs[:, h:h + 1], (L, L)) - cs_t[h:h + 1, :]), 0.0)
            y_parts.append(_dot(cb * seg, xdt[:, h * HEAD_DIM:(h + 1) * HEAD_DIM]))
            if e % 2:
                yield
    y = jnp.concatenate(y_parts, axis=1)
    y_inter = jnp.concatenate(
        [_dot(cm[:, gi * M:(gi + 1) * M], st[:, gi * gw:(gi + 1) * gw]) for gi in range(G)], axis=1)
    y = y + y_inter * jnp.exp(csx)
    yield

    cs_last = csx[L - 1:L, :]
    xdtw = xdt * jnp.exp(cs_last - csx)
    d_state = jnp.concatenate(
        [_dot_tn(bm[:, gi * M:(gi + 1) * M], xdtw[:, gi * gw:(gi + 1) * gw]) for gi in range(G)], axis=1)
    st_ref[b] = st * jnp.exp(cs_last) + d_state
    yield

    y = (y + dsk_ref[...] * xs) * _silu(z)
    outs = []
    for gi in range(G):
        yg = y[:, gi * gw:(gi + 1) * gw]
        ms = jnp.mean(yg * yg, axis=-1, keepdims=True)
        outs.append(yg * lax.rsqrt(ms + LN_EPS))
    o_ref[b] = jnp.concatenate(outs, axis=1) * ng_ref[...]


def _inproj_ssd_kernel(x_ref, wg_ref, ws_ref, wr_ref, cw_ref, cb_ref, dtb_ref, alog_ref, dsk_ref, ng_ref,
                       pg_ref, pr_ref, os_ref, ps_ref, halo_ref, st_ref):
    nb, L, d = x_ref.shape
    xb = _bf(x_ref[...].reshape(nb * L, d))
    ps_ref[...] = jnp.dot(xb, ws_ref[...], preferred_element_type=f32).reshape(ps_ref.shape)

    def other_columns():
        for w_ref, o_ref in ((wg_ref, pg_ref), (wr_ref, pr_ref)):
            n = w_ref.shape[1]
            for c0 in range(0, n, PROJ_CHUNK):
                c1 = min(c0 + PROJ_CHUNK, n)
                o_ref[:, :, c0:c1] = jnp.dot(xb, w_ref[:, c0:c1], preferred_element_type=f32).reshape(nb, L, c1 - c0)
                for _ in range(PROJ_PACE):
                    yield

    ssd = _round_robin([_ssd_steps(b, ps_ref, halo_ref, cw_ref, cb_ref, dtb_ref, alog_ref, dsk_ref,
                                   ng_ref, os_ref, st_ref) for b in range(nb)])
    for _ in _round_robin([ssd, other_columns()]):
        pass


def _in_proj_ssd(x3, wg, ws, wr, cw, cb, dtb, alog, dsk, ng):
    nb, t, d = x3.shape
    L = SSD_BLOCK
    full2 = lambda a: pl.BlockSpec(a.shape, lambda i: (0, 0))
    blk = lambda c: pl.BlockSpec((nb, L, c), lambda i: (0, i, 0))
    return pl.pallas_call(
        _inproj_ssd_kernel,
        grid=(t // L,),
        in_specs=[blk(d)] + [full2(a) for a in (wg, ws, wr, cw, cb, dtb, alog, dsk, ng)],
        out_specs=[blk(wg.shape[1]), blk(wr.shape[1]), blk(SSD_WIDTH)],
        out_shape=[jax.ShapeDtypeStruct((nb, t, wg.shape[1]), f32), jax.ShapeDtypeStruct((nb, t, wr.shape[1]), f32),
                   jax.ShapeDtypeStruct((nb, t, SSD_WIDTH), f32)],
        scratch_shapes=[pltpu.VMEM((nb, L, ws.shape[1]), f32), pltpu.VMEM((nb, 8, SSD_XBC), f32),
                        pltpu.VMEM((nb, SSD_STATE, SSD_WIDTH), f32)],
        compiler_params=_cparams(("arbitrary",)),
        name="in_proj_ssd",
    )(x3, wg, ws, wr, cw, cb, dtb, alog, dsk, ng)


def _rwkv_steps(sub, p_ref, prev_ref, mu_ref, w0_ref, w2_ref, a0_ref, a2_ref, g2_ref, kk_ref, ka_ref, rk_ref,
                lng_ref, lnb_ref, o_ref, zt_ref):
    nb = p_ref.shape[0]
    C, W = RWKV_CHUNK, RWKV_WIDTH
    H = LANES // HEAD_DIM
    npair = W // LANES
    i = pl.program_id(0)
    r0 = sub * C

    if sub == 0:
        @pl.when(i == 0)
        def _():
            zt_ref[...] = jnp.zeros_like(zt_ref)

    bones = _block_ones(W, HEAD_DIM).astype(bf16)
    pair_blk = _block_ones(LANES, HEAD_DIM)
    n = H * C
    rr, cc = _iota((n, n), 0), _iota((n, n), 1)
    same = rr // C == cc // C
    strict = same & (rr > cc)
    lower = same & (rr >= cc)
    eye = jnp.where(rr == cc, 1.0, 0.0)

    def stack(t):
        return jnp.where(pair_blk, jnp.concatenate([t] * H, axis=0), 0.0)

    rows = range(nb)
    nr = nb * C
    cols = p_ref[:, r0:r0 + C, :].reshape(nr, p_ref.shape[2])

    def row_before(b):
        return p_ref[b, r0 - 1:r0, :] if sub else jnp.where(i > 0, prev_ref[b, 7:8, :], 0.0)

    shifted = jnp.concatenate(
        [piece for b in rows for piece in (row_before(b), p_ref[b, r0:r0 + C - 1, :])], axis=0)
    cols = cols + (shifted - cols) * mu_ref[...]
    r = cols[:, 0:W]
    k = cols[:, W:2 * W]
    v = cols[:, 2 * W:3 * W]
    w_lr = cols[:, 3 * W:3 * W + 64]
    a_lr = cols[:, 3 * W + 64:3 * W + 128]
    g_lr = cols[:, 3 * W + 128:3 * W + 256]

    wv = -_softplus(-(w0_ref[...] + _dot_hi(jnp.tanh(w_lr), w2_ref[...]))) - 0.5
    lw = -jnp.exp(wv)
    a = jax.nn.sigmoid(a0_ref[...] + _dot_hi(a_lr, a2_ref[...]))
    g = _dot_hi(jax.nn.sigmoid(g_lr), g2_ref[...])
    yield

    kk = k * kk_ref[...]
    k2 = k * (1.0 + (a - 1.0) * ka_ref[...])
    head_sums = _dot_sel_rhs(jnp.concatenate([kk * kk, r * k2 * rk_ref[...]], axis=0), bones, 2)
    kk = kk / jnp.maximum(jnp.sqrt(head_sums[:nr]), 1e-12)
    bonus_v = head_sums[nr:] * v
    bvec = kk * a
    yield

    rb_i, cb_i = _iota((nr, nr), 0), _iota((nr, nr), 1)
    tri = ((rb_i >= cb_i) & (rb_i // C == cb_i // C)).astype(bf16)
    cum = _dot_sel_lhs(tri, lw, 3)
    cum_last = [cum[(b + 1) * C - 1:(b + 1) * C, :] for b in rows]
    e_neg = jnp.exp(-cum)
    e_rem = jnp.exp(jnp.concatenate([jnp.broadcast_to(cl, (C, W)) for cl in cum_last], axis=0) - cum)
    per_token = dict(aw=-kk * jnp.exp(cum - lw), rw=r * jnp.exp(cum), bn=bvec * e_neg, kn=k2 * e_neg,
                     bt=bvec * e_rem, kt=k2 * e_rem, vs=v)
    chains = [(b, p) for b in rows for p in range(npair)]
    q = [dict({name: stack(t[b * C:(b + 1) * C, p * LANES:(p + 1) * LANES]) for name, t in per_token.items()},
              wc=jnp.exp(cum_last[b][:, p * LANES:(p + 1) * LANES])) for b, p in chains]
    ids = range(len(chains))
    yield

    a_ab = [jnp.where(strict, _dot_nt(q[c]["aw"], q[c]["bn"]), 0.0) for c in ids]
    a_ak = [jnp.where(strict, _dot_nt(q[c]["aw"], q[c]["kn"]), 0.0) for c in ids]
    yield
    a_rb = [jnp.where(lower, _dot_nt(q[c]["rw"], q[c]["bn"]), 0.0) for c in ids]
    a_rk = [jnp.where(lower, _dot_nt(q[c]["rw"], q[c]["kn"]), 0.0) for c in ids]
    yield

    pw = a_ab
    tinv = [eye + a_ab[c] for c in ids]
    for _ in range(int(math.log2(C)) - 1):
        pw = [_dot(pw[c], pw[c]) for c in ids]
        yield
        tinv = [tinv[c] + _dot(tinv[c], pw[c]) for c in ids]
        yield

    zt = [zt_ref[b, p] for b, p in chains]
    rhs = [_dot_nt(q[c]["aw"], zt[c]) + _dot(a_ak[c], q[c]["vs"]) for c in ids]
    yield
    y0 = [_dot_nt(q[c]["rw"], zt[c]) + _dot(a_rk[c], q[c]["vs"]) for c in ids]
    yield
    u = [_dot(tinv[c], rhs[c]) for c in ids]
    yield
    y_st = [y0[c] + _dot(a_rb[c], u[c]) for c in ids]
    yield
    for c, (b, p) in enumerate(chains):
        zt_ref[b, p] = zt[c] * q[c]["wc"] + _dot_tn(u[c], q[c]["bt"]) + _dot_tn(q[c]["vs"], q[c]["kt"])
    yield

    inv_n = 1.0 / HEAD_DIM
    ys = []
    for b in rows:
        tiles = []
        for p in range(npair):
            y_c = y_st[b * npair + p]
            y = y_c[0:C]
            for h in range(1, H):
                y = y + y_c[h * C:(h + 1) * C]
            tiles.append(y)
        ys.append(jnp.concatenate(tiles, axis=1))
    y = jnp.concatenate(ys, axis=0)
    yc = y - _dot_sel_rhs(y, bones, 2) * inv_n
    var_h = _dot_sel_rhs(yc * yc, bones, 2) * inv_n
    y = yc * lax.rsqrt(var_h + RWKV_GN_EPS) * lng_ref[...] + lnb_ref[...]
    o_ref[:, r0:r0 + C, :] = ((y + bonus_v) * g).reshape(nb, C, W)


N_GLA_IN, N_RWKV_IN = 4, 13


def _gla_rwkv_kernel(*refs):
    gla_in = refs[:N_GLA_IN]
    rwkv_in = refs[N_GLA_IN:N_GLA_IN + N_RWKV_IN]
    og_ref, or_ref, st_ref, t_ref, d_ref, sx_ref, zt_ref = refs[N_GLA_IN + N_RWKV_IN:]
    n_sub = rwkv_in[0].shape[1] // RWKV_CHUNK
    rwkv = (step for sub in range(n_sub) for step in _rwkv_steps(sub, *rwkv_in, or_ref, zt_ref))
    gla = _gla_steps(*gla_in, og_ref, st_ref, t_ref, d_ref, sx_ref)
    for _ in _round_robin([rwkv, gla]):
        pass


def _gla_rwkv(pg, gla_params, pr, rwkv_params):
    nb, t, gcols = pg.shape
    rcols = pr.shape[2]
    tb = GLA_BLOCK
    npt = GLA_WIDTH // LANES
    pair_rows = (GLA_CHUNK * (GLA_CHUNK + 1) // 2) * (tb // GLA_CHUNK)
    full2 = lambda a: pl.BlockSpec(a.shape, lambda i: (0, 0))
    blk = lambda c: pl.BlockSpec((nb, tb, c), lambda i: (0, i, 0))
    assert len(gla_params) + 1 == N_GLA_IN and len(rwkv_params) + 2 == N_RWKV_IN
    return pl.pallas_call(
        _gla_rwkv_kernel,
        grid=(t // tb,),
        in_specs=[blk(gcols)] + [full2(p) for p in gla_params]
        + [blk(rcols), pl.BlockSpec((nb, 8, rcols), lambda i: (0, jnp.maximum(i * (tb // 8) - 1, 0), 0))]
        + [full2(p) for p in rwkv_params],
        out_specs=[blk(GLA_WIDTH), blk(RWKV_WIDTH)],
        out_shape=[jax.ShapeDtypeStruct((nb, t, GLA_WIDTH), f32), jax.ShapeDtypeStruct((nb, t, RWKV_WIDTH), f32)],
        scratch_shapes=[
            pltpu.VMEM((nb, npt, LANES, LANES), f32),
            pltpu.VMEM((nb, 5, npt, tb, LANES), f32),
            pltpu.VMEM((nb, pair_rows, GLA_WIDTH), f32),
            pltpu.VMEM((nb, pair_rows, GLA_WIDTH), f32),
            pltpu.VMEM((nb, RWKV_WIDTH // LANES, LANES, LANES), f32),
        ],
        compiler_params=_cparams(("arbitrary",)),
        name="gla_rwkv_mixers",
    )(pg, *gla_params, pr, pr, *rwkv_params)


def _route(logits, rb):
    tm = logits.shape[1]
    scores = jax.nn.sigmoid(logits)
    sel = scores + rb
    srow = [sel[e:e + 1, :] for e in range(N_EXPERTS)]
    crow = [scores[e:e + 1, :] for e in range(N_EXPERTS)]

    def top2_sum(v0, v1, v2, v3):
        m01, n01 = jnp.maximum(v0, v1), jnp.minimum(v0, v1)
        m23, n23 = jnp.maximum(v2, v3), jnp.minimum(v2, v3)
        return jnp.maximum(m01, m23) + jnp.maximum(jnp.minimum(m01, m23), jnp.maximum(n01, n23))

    gscore = [top2_sum(*srow[4 * gi:4 * gi + 4]) for gi in range(N_EXPERT_GROUPS)]
    best = jnp.zeros((1, tm), jnp.int32)
    best_v = gscore[0]
    for gi in range(1, N_EXPERT_GROUPS):
        better = gscore[gi] > best_v
        best = jnp.where(better, gi, best)
        best_v = jnp.where(better, gscore[gi], best_v)
    sv, cv = [], []
    for j in range(EXPERTS_PER_GROUP):
        s_j, c_j = srow[j], crow[j]
        for gi in range(1, N_EXPERT_GROUPS):
            s_j = jnp.where(best == gi, srow[4 * gi + j], s_j)
            c_j = jnp.where(best == gi, crow[4 * gi + j], c_j)
        sv.append(s_j)
        cv.append(c_j)
    j1 = jnp.zeros((1, tm), jnp.int32)
    v1 = sv[0]
    for j in range(1, EXPERTS_PER_GROUP):
        better = sv[j] > v1
        j1 = jnp.where(better, j, j1)
        v1 = jnp.where(better, sv[j], v1)
    j2 = jnp.full((1, tm), -1, jnp.int32)
    v2 = jnp.full((1, tm), -jnp.inf, f32)
    for j in range(EXPERTS_PER_GROUP):
        better = (j1 != j) & ((j2 < 0) | (sv[j] > v2))
        j2 = jnp.where(better, j, j2)
        v2 = jnp.where(better, sv[j], v2)
    w1 = cv[0]
    w2 = cv[0]
    for j in range(1, EXPERTS_PER_GROUP):
        w1 = jnp.where(j1 == j, cv[j], w1)
        w2 = jnp.where(j2 == j, cv[j], w2)
    wsum = w1 + w2
    g1, g2 = w1 / wsum, w2 / wsum
    rows = []
    for e in range(N_EXPERTS):
        gi, j = divmod(e, EXPERTS_PER_GROUP)
        in_g = best == gi
        rows.append(jnp.where(in_g & (j1 == j), g1, jnp.where(in_g & (j2 == j), g2, 0.0)))
    gate_t = jnp.concatenate(rows, axis=0)
    return gate_t, best


def _outproj_kernel(alpha, x_ref, og_ref, os_ref, or_ref, wo_ref, g_ref, b_ref, rwt_ref, rb_ref,
                    x1_ref, gate_ref, best_ref, cnt_ref):
    tm = x_ref.shape[0]
    ts = tm // cnt_ref.shape[0]
    parts = [slice(sb * ts, (sb + 1) * ts) for sb in range(cnt_ref.shape[0])]
    d = functools.partial(jnp.dot, preferred_element_type=f32)
    mix = [d(_bf(og_ref[s, :]), wo_ref[0:GLA_WIDTH, :])
           + d(_bf(os_ref[s, :]), wo_ref[GLA_WIDTH:GLA_WIDTH + SSD_WIDTH, :])
           + d(_bf(or_ref[s, :]), wo_ref[GLA_WIDTH + SSD_WIDTH:, :]) for s in parts]
    x1 = [_layer_norm(alpha * x_ref[s, :] + m, g_ref[...], b_ref[...], LN_EPS) for s, m in zip(parts, mix)]
    for s, v in zip(parts, x1):
        x1_ref[s, :] = v

    wh, wl = _split(rwt_ref[...], 2)
    nt = lambda a, b: lax.dot_general(a, b, (((1,), (1,)), ((), ())), preferred_element_type=f32)
    logits = []
    for v in x1:
        xh, xl = _split(v, 2)
        logits.append(nt(wh, xh) + nt(wl, xh) + nt(wh, xl))
    routed = [_route(lg, rb_ref[...]) for lg in logits]
    eye = (_iota((N_EXPERTS, LANES), 0) == _iota((N_EXPERTS, LANES), 1)).astype(bf16)
    lane = _iota((1, LANES), 1)
    for sb, (s, (gate_t, best)) in enumerate(zip(parts, routed)):
        gate = None
        for p in _split(gate_t, 3):
            t = lax.dot_general(p, eye, (((0,), (0,)), ((), ())), preferred_element_type=f32)
            gate = t if gate is None else gate + t
        gate_ref[s, :] = gate
        best_ref[:, s] = best
        cnt = jnp.zeros((1, LANES), jnp.int32)
        for gi in range(N_EXPERT_GROUPS):
            n_g = jnp.sum((best == gi).astype(jnp.int32), axis=-1, keepdims=True)
            cnt = jnp.where(lane == gi, n_g, cnt)
        cnt_ref[sb] = cnt


def _out_proj(alpha, x2, og, os_, or_, wo, g, b, rwt, rb):
    n, d = x2.shape
    tm = min(OUT_PROJ_ROWS, n)
    ts = min(MOE_SORT_ROWS, tm)
    full2 = lambda a: pl.BlockSpec(a.shape, lambda i: (0, 0))
    rows = lambda c: pl.BlockSpec((tm, c), lambda i: (i, 0))
    return pl.pallas_call(
        functools.partial(_outproj_kernel, alpha),
        grid=(n // tm,),
        in_specs=[rows(d), rows(og.shape[1]), rows(os_.shape[1]), rows(or_.shape[1]),
                  full2(wo), full2(g), full2(b), full2(rwt), full2(rb)],
        out_specs=[rows(d), rows(LANES), pl.BlockSpec((1, tm), lambda i: (0, i)),
                   pl.BlockSpec((tm // ts, 1, LANES), lambda i: (i, 0, 0))],
        out_shape=[jax.ShapeDtypeStruct((n, d), f32), jax.ShapeDtypeStruct((n, LANES), f32),
                   jax.ShapeDtypeStruct((1, n), jnp.int32), jax.ShapeDtypeStruct((n // ts, 1, LANES), jnp.int32)],
        compiler_params=_cparams(("arbitrary",)),
        name="out_proj_router",
    )(x2, og, os_, or_, wo, g, b, rwt, rb)


def _seg_copy(src, dst, sem, src_off, dst_off, rows, max_rows, wait):
    k = SEG_ALIGN
    sizes = []
    while k <= max_rows:
        sizes.append(k)
        k *= 2
    for k in reversed(sizes):
        shift = int(math.log2(k)) + 1

        @pl.when((rows & k) != 0)
        def _():
            done = (rows >> shift) << shift
            cp = pltpu.make_async_copy(
                src.at[pl.ds(pl.multiple_of(src_off + done, SEG_ALIGN), k)],
                dst.at[pl.ds(pl.multiple_of(dst_off + done, SEG_ALIGN), k)], sem)
            if wait:
                cp.wait()
            else:
                cp.start()


def _padded(count):
    return (count + (SEG_ALIGN - 1)) & (-SEG_ALIGN)


def _one_hot_rows(pos, n_rows):
    return jnp.where(_iota((n_rows, pos.shape[1]), 0) == pos, 1.0, 0.0).astype(bf16)


def _dispatch_kernel(cnt_sm, goff_sm, fill_sm, x_ref, gate_ref, best_ref, xs_out, gs_out, pos_ref,
                     xloc, gloc, zx, zg, sems):
    i = pl.program_id(0)
    tm = x_ref.shape[0]
    n_loc = xloc.shape[1]
    slot = i % 2

    def copies(blk, slot_, wait):
        loff = 0
        for gi in range(N_EXPERT_GROUPS):
            rows = _padded(cnt_sm[blk * N_EXPERT_GROUPS + gi])
            dst = goff_sm[blk * N_EXPERT_GROUPS + gi]
            _seg_copy(xloc.at[slot_], xs_out, sems.at[0, slot_], loff, dst, rows, tm, wait)
            _seg_copy(gloc.at[slot_], gs_out, sems.at[1, slot_], loff, dst, rows, tm, wait)
            loff = loff + rows

    best = best_ref[...]
    onehot = (_iota((8, tm), 0) == best).astype(bf16)
    earlier = (_iota((tm, tm), 0) < _iota((tm, tm), 1)).astype(bf16)
    rank = jnp.dot(onehot, earlier, preferred_element_type=f32)
    pos = jnp.zeros((1, tm), f32)
    loff = 0
    for gi in range(N_EXPERT_GROUPS):
        pos = jnp.where(best == gi, rank[gi:gi + 1, :] + loff.astype(f32) if gi else rank[gi:gi + 1, :], pos)
        loff = loff + _padded(cnt_sm[i * N_EXPERT_GROUPS + gi])
    pos = pos.astype(jnp.int32)
    pos_ref[...] = pos
    perm = _one_hot_rows(pos, n_loc)
    xloc[slot] = jnp.dot(perm, _bf(x_ref[...]), preferred_element_type=f32)
    gloc[slot] = _dot_sel_lhs(perm, gate_ref[...], 3)
    copies(i, slot, False)

    @pl.when(i > 0)
    def _():
        copies(i - 1, 1 - slot, True)

    @pl.when(i == pl.num_programs(0) - 1)
    def _():
        copies(i, slot, True)
        zx[...] = jnp.zeros_like(zx)
        zg[...] = jnp.zeros_like(zg)
        tile = zx.shape[0]

        def fills(wait):
            for src, dst, sem in ((zx, xs_out, sems.at[2, 0]), (zg, gs_out, sems.at[2, 1])):
                for gi in range(N_EXPERT_GROUPS):
                    _seg_copy(src, dst, sem, 0, fill_sm[gi], fill_sm[N_EXPERT_GROUPS + gi], tile // 2, wait)

                def body(j, carry):
                    off = pl.multiple_of(fill_sm[2 * N_EXPERT_GROUPS] + j * tile, SEG_ALIGN)
                    cp = pltpu.make_async_copy(src, dst.at[pl.ds(off, tile)], sem)
                    if wait:
                        cp.wait()
                    else:
                        cp.start()
                    return carry

                lax.fori_loop(0, fill_sm[2 * N_EXPERT_GROUPS + 1], body, 0)

        fills(False)
        fills(True)


def _experts_kernel(tg_sm, tv_sm, x_ref, gate_ref, wg_ref, wu_ref, wd_ref, y_ref):
    t = pl.program_id(0)

    @pl.when(tv_sm[t] == 0)
    def _():
        y_ref[...] = jnp.zeros_like(y_ref)

    @pl.when(tv_sm[t] > 0)
    def _():
        first = tg_sm[t] * EXPERTS_PER_GROUP
        xb = _bf(x_ref[...])
        gate = gate_ref[...]
        lane = _iota(gate.shape, 1)
        hs = []
        for j in range(EXPERTS_PER_GROUP):
            hg = jnp.dot(xb, wg_ref[j], preferred_element_type=f32)
            hu = jnp.dot(xb, wu_ref[j], preferred_element_type=f32)
            gcol = jnp.sum(jnp.where(lane == first + j, gate, 0.0), axis=-1, keepdims=True)
            hs.append(_bf(_silu(hg) * hu * gcol))
        ff = wd_ref.shape[1]
        y_ref[...] = jnp.dot(jnp.concatenate(hs, axis=1), wd_ref[...].reshape(EXPERTS_PER_GROUP * ff, -1),
                             preferred_element_type=f32)


def _combine_kernel(alpha, cnt_sm, goff_sm, x_ref, pos_ref, g_ref, b_ref, ys_ref, o_ref, yloc, sems):
    i = pl.program_id(0)
    tm = x_ref.shape[0]
    n_loc = yloc.shape[1]
    slot = i % 2

    def copies(blk, slot_, wait):
        loff = 0
        for gi in range(N_EXPERT_GROUPS):
            rows = _padded(cnt_sm[blk * N_EXPERT_GROUPS + gi])
            src = goff_sm[blk * N_EXPERT_GROUPS + gi]
            _seg_copy(ys_ref, yloc.at[slot_], sems.at[slot_], src, loff, rows, tm, wait)
            loff = loff + rows

    @pl.when(i == 0)
    def _():
        yloc[...] = jnp.zeros_like(yloc)
        copies(0, 0, False)

    @pl.when(i + 1 < pl.num_programs(0))
    def _():
        copies(i + 1, 1 - slot, False)

    copies(i, slot, True)
    perm = _one_hot_rows(pos_ref[...], n_loc)
    y = lax.dot_general(perm, _bf(yloc[slot]), (((0,), (0,)), ((), ())), preferred_element_type=f32)
    o_ref[...] = _layer_norm(alpha * x_ref[...] + y, g_ref[...], b_ref[...], LN_EPS)


def _moe_tables(cnt, n_tiles, tile):
    pc = (cnt + (SEG_ALIGN - 1)) // SEG_ALIGN * SEG_ALIGN
    gtot = jnp.sum(pc, axis=0)
    gcap = (gtot + tile - 1) // tile * tile
    gend = jnp.cumsum(gcap)
    gstart = gend - gcap
    goff = gstart[None, :] + jnp.cumsum(pc, axis=0) - pc
    tstart = jnp.arange(n_tiles, dtype=jnp.int32) * tile
    tg = jnp.minimum(jnp.sum((tstart[:, None] >= gend[None, :]).astype(jnp.int32), axis=1), N_EXPERT_GROUPS - 1)
    tv = jnp.clip(gtot[tg] - (tstart - gstart[tg]), 0, tile)
    fill = jnp.concatenate([gstart + gtot, gcap - gtot, gend[-1:], n_tiles - gend[-1:] // tile])
    return goff.reshape(-1).astype(jnp.int32), tg.astype(jnp.int32), tv.astype(jnp.int32), fill.astype(jnp.int32)


def _moe(alpha, x1, gate, best, cnt, layer, wg, wu, wd, g, b):
    n, d = x1.shape
    ff = wg.shape[3]
    nblk = cnt.shape[0]
    tm = n // nblk
    tile = MOE_TILE
    n_loc = tm + LANES
    n_sorted = -(-(n + nblk * N_EXPERT_GROUPS * SEG_ALIGN + N_EXPERT_GROUPS * tile) // tile) * tile
    n_tiles = n_sorted // tile
    cnt_flat = cnt[:, 0, :N_EXPERT_GROUPS].reshape(-1)
    goff, tg, tv, fill = _moe_tables(cnt[:, 0, :N_EXPERT_GROUPS], n_tiles, tile)
    any_spec = pl.BlockSpec(memory_space=pl.ANY)

    xs, gs, pos = pl.pallas_call(
        _dispatch_kernel,
        grid_spec=pltpu.PrefetchScalarGridSpec(
            num_scalar_prefetch=3,
            grid=(nblk,),
            in_specs=[
                pl.BlockSpec((tm, d), lambda i, *_: (i, 0)),
                pl.BlockSpec((tm, LANES), lambda i, *_: (i, 0)),
                pl.BlockSpec((1, tm), lambda i, *_: (0, i)),
            ],
            out_specs=[any_spec, any_spec, pl.BlockSpec((1, tm), lambda i, *_: (0, i))],
            scratch_shapes=[pltpu.VMEM((2, n_loc, d), f32), pltpu.VMEM((2, n_loc, LANES), f32),
                            pltpu.VMEM((tile, d), f32), pltpu.VMEM((tile, LANES), f32),
                            pltpu.SemaphoreType.DMA((3, 2))],
        ),
        out_shape=[jax.ShapeDtypeStruct((n_sorted, d), f32), jax.ShapeDtypeStruct((n_sorted, LANES), f32),
                   jax.ShapeDtypeStruct((1, n), jnp.int32)],
        compiler_params=_cparams(("arbitrary",)),
        name="moe_dispatch",
    )(cnt_flat, goff, fill, x1, gate, best)

    ys = pl.pallas_call(
        _experts_kernel,
        grid_spec=pltpu.PrefetchScalarGridSpec(
            num_scalar_prefetch=2,
            grid=(n_tiles,),
            in_specs=[
                pl.BlockSpec((tile, d), lambda t, *_: (t, 0)),
                pl.BlockSpec((tile, LANES), lambda t, *_: (t, 0)),
                pl.BlockSpec((None, EXPERTS_PER_GROUP, d, ff), lambda t, tg_, tv_: (layer, tg_[t], 0, 0)),
                pl.BlockSpec((None, EXPERTS_PER_GROUP, d, ff), lambda t, tg_, tv_: (layer, tg_[t], 0, 0)),
                pl.BlockSpec((None, EXPERTS_PER_GROUP, ff, d), lambda t, tg_, tv_: (layer, tg_[t], 0, 0)),
            ],
            out_specs=pl.BlockSpec((tile, d), lambda t, *_: (t, 0)),
        ),
        out_shape=jax.ShapeDtypeStruct((n_sorted, d), f32),
        compiler_params=_cparams(("arbitrary",)),
        name="moe_experts",
    )(tg, tv, xs, gs, wg, wu, wd)

    return pl.pallas_call(
        functools.partial(_combine_kernel, alpha),
        grid_spec=pltpu.PrefetchScalarGridSpec(
            num_scalar_prefetch=2,
            grid=(nblk,),
            in_specs=[
                pl.BlockSpec((tm, d), lambda i, *_: (i, 0)),
                pl.BlockSpec((1, tm), lambda i, *_: (0, i)),
                pl.BlockSpec(g.shape, lambda i, *_: (0, 0)),
                pl.BlockSpec(b.shape, lambda i, *_: (0, 0)),
                any_spec,
            ],
            out_specs=pl.BlockSpec((tm, d), lambda i, *_: (i, 0)),
            scratch_shapes=[pltpu.VMEM((2, n_loc, d), f32), pltpu.SemaphoreType.DMA((2,))],
        ),
        out_shape=jax.ShapeDtypeStruct((n, d), f32),
        compiler_params=_cparams(("arbitrary",)),
        name="moe_combine",
    )(cnt_flat, goff, x1, pos, g, b, ys)


def _pad_cols(w, n):
    return jnp.pad(w, ((0, 0), (0, n - w.shape[1])))


def _row(v, n=None):
    v = v.reshape(1, -1)
    return v if n is None else _pad_cols(v, n)


def kernel(x, w_in, gla_w_alpha, gla_b_alpha, gla_norm_g, ssd_conv_w, ssd_conv_b, ssd_dt_bias, ssd_A_log, ssd_D, ssd_norm_g, rwkv_mu, rwkv_w0, rwkv_w2, rwkv_a0, rwkv_a2, rwkv_g2, rwkv_k_k, rwkv_k_a, rwkv_r_k, rwkv_ln_g, rwkv_ln_b, w_out, ln1_g, ln1_b, router_w, router_b, exp_w_gate, exp_w_up, exp_w_down, ln2_g, ln2_b):
    nb, t, d = x.shape
    depth = w_in.shape[0]
    alpha = float((2 * depth) ** 0.25)
    gla_cols = 4 * GLA_WIDTH + GLA_GATE_RANK
    ssd_cols = SSD_WIDTH + SSD_XBC + SSD_HEADS
    rwt = router_w.T
    rb = router_b.reshape(-1, 1)
    wg_all, wu_all, wd_all = _bf(exp_w_gate), _bf(exp_w_up), _bf(exp_w_down)
    x2 = x.reshape(nb * t, d)
    for l in range(depth):
        wg, ws, wr = _regroup_w_in(w_in, l, (gla_cols, ssd_cols, RWKV_COLS), (GLA_COLS_PAD, SSD_COLS_PAD, RWKV_COLS))
        pg, pr, os_ = _in_proj_ssd(x2.reshape(nb, t, d), wg, ws, wr, ssd_conv_w[l], _row(ssd_conv_b[l]),
                                   _row(ssd_dt_bias[l], LANES), _row(ssd_A_log[l], LANES),
                                   _row(jnp.repeat(ssd_D[l], HEAD_DIM)), _row(ssd_norm_g[l]))
        gla_params = (jnp.pad(gla_w_alpha[l], ((0, LANES - GLA_GATE_RANK), (0, 0))), _row(gla_b_alpha[l]),
                      _row(jnp.tile(gla_norm_g[l], GLA_WIDTH // HEAD_DIM)))
        rwkv_params = (_row(rwkv_mu[l]), _row(rwkv_w0[l]), rwkv_w2[l], _row(rwkv_a0[l]), rwkv_a2[l], rwkv_g2[l],
                       _row(rwkv_k_k[l]), _row(rwkv_k_a[l]), _row(rwkv_r_k[l]), _row(rwkv_ln_g[l]),
                       _row(rwkv_ln_b[l]))
        og, or_ = _gla_rwkv(pg, gla_params, pr, rwkv_params)
        x1, gate, best, cnt = _out_proj(alpha, x2, og.reshape(nb * t, -1), os_.reshape(nb * t, -1),
                                        or_.reshape(nb * t, -1), _bf(w_out[l]), _row(ln1_g[l]), _row(ln1_b[l]),
                                        rwt, rb)
        x2 = _moe(alpha, x1, gate, best, cnt, l, wg_all, wu_all, wd_all, _row(ln2_g[l]), _row(ln2_b[l]))
    return x2.reshape(nb, t, d)
```

```python
import functools
import math

import jax
import jax.numpy as jnp
from jax import lax
from jax.experimental import pallas as pl
from jax.experimental.pallas import tpu as pltpu

f32 = jnp.float32
bf16 = jnp.bfloat16

LANES = 128
HEAD_DIM = 64
GLA_WIDTH = 256
GLA_GATE_RANK = 16
GLA_GATE_TAU = 16.0
GLA_CHUNK = 16
GLA_BLOCK = 128
GLA_COLS_PAD = 4 * GLA_WIDTH + LANES
SSD_WIDTH = 512
SSD_HEADS = 8
SSD_GROUPS = 2
SSD_STATE = 64
SSD_CONV = 4
SSD_XBC = SSD_WIDTH + 2 * SSD_GROUPS * SSD_STATE
SSD_BLOCK = 128
SSD_COLS_PAD = SSD_WIDTH + SSD_XBC + LANES
RWKV_WIDTH = 256
RWKV_HEADS = 4
RWKV_CHUNK = 64
RWKV_GN_EPS = 64e-5
RWKV_COLS = 3 * RWKV_WIDTH + 64 + 64 + 128
N_EXPERTS = 16
N_EXPERT_GROUPS = 4
EXPERTS_PER_GROUP = 4
LN_EPS = 1e-5

PROJ_CHUNK = 256
PROJ_PACE = 4
OUT_PROJ_ROWS = 1024
MOE_SORT_ROWS = 512
MOE_TILE = 512
SEG_ALIGN = 8
VMEM_LIMIT = 56 * 1024 * 1024


def _bf(x):
    return x.astype(bf16)


def _dot(a, b):
    return jnp.dot(_bf(a), _bf(b), preferred_element_type=f32)


def _dot_nt(a, b):
    return lax.dot_general(_bf(a), _bf(b), (((1,), (1,)), ((), ())), preferred_element_type=f32)


def _dot_tn(a, b):
    return lax.dot_general(_bf(a), _bf(b), (((0,), (0,)), ((), ())), preferred_element_type=f32)


def _split(x, n):
    parts, r = [], x
    for _ in range(n):
        p = r.astype(bf16)
        parts.append(p)
        r = r - p.astype(f32)
    return parts


def _dot_sel_rhs(a, sel, n=3):
    out = None
    for p in _split(a, n):
        t = jnp.dot(p, sel, preferred_element_type=f32)
        out = t if out is None else out + t
    return out


def _dot_sel_lhs(sel, b, n=3):
    out = None
    for p in _split(b, n):
        t = jnp.dot(sel, p, preferred_element_type=f32)
        out = t if out is None else out + t
    return out


def _dot_hi(a, b):
    ah, al = _split(a, 2)
    bh, bl = _split(b, 2)
    d = functools.partial(jnp.dot, preferred_element_type=f32)
    return d(ah, bh) + d(al, bh) + d(ah, bl)


def _iota(shape, dim):
    return lax.broadcasted_iota(jnp.int32, shape, dim)


def _block_ones(n, blk):
    return (_iota((n, n), 0) // blk == _iota((n, n), 1) // blk)


def _silu(x):
    return x * jax.nn.sigmoid(x)


def _softplus(x):
    return jnp.maximum(x, 0.0) + jnp.log1p(jnp.exp(-jnp.abs(x)))


def _layer_norm(y, g, b, eps):
    mu = jnp.mean(y, axis=-1, keepdims=True)
    yc = y - mu
    var = jnp.mean(yc * yc, axis=-1, keepdims=True)
    return yc * lax.rsqrt(var + eps) * g + b


def _cparams(sem):
    return pltpu.CompilerParams(dimension_semantics=sem, vmem_limit_bytes=VMEM_LIMIT)


def _regroup_kernel(splits, w_ref, *o_refs):
    start = 0
    for width, o_ref in zip(splits, o_refs):
        part = w_ref[0, start:start + width, :]
        pad = o_ref.shape[1] - width
        if pad:
            part = jnp.concatenate([part, jnp.zeros((pad, part.shape[1]), part.dtype)], axis=0)
        o_ref[...] = _bf(part.T)
        start += width


def _regroup_w_in(w_in_t, layer, splits, padded):
    _, n_in, d = w_in_t.shape
    return pl.pallas_call(
        functools.partial(_regroup_kernel, splits),
        grid=(1,),
        in_specs=[pl.BlockSpec((1, n_in, d), lambda i: (layer, 0, 0))],
        out_specs=[pl.BlockSpec((d, p), lambda i: (0, 0)) for p in padded],
        out_shape=[jax.ShapeDtypeStruct((d, p), bf16) for p in padded],
        compiler_params=_cparams(("arbitrary",)),
        name="regroup_w_in",
    )(w_in_t)


def _gla_steps(p_ref, wa_ref, ba_ref, ng_ref, o_ref, st_ref, t_ref, d_ref, sx_ref):
    nb, tb, _ = p_ref.shape
    w, c = GLA_WIDTH, GLA_CHUNK
    nc = tb // c
    npt = w // LANES
    Q, K, V, CUM, OUT = range(5)

    @pl.when(pl.program_id(0) == 0)
    def _():
        st_ref[...] = jnp.zeros_like(st_ref)

    bones = _block_ones(w, HEAD_DIM).astype(bf16)
    pair_blk = _block_ones(LANES, HEAD_DIM)
    r_i, c_i = _iota((tb, tb), 0), _iota((tb, tb), 1)
    tri = ((r_i >= c_i) & (r_i // c == c_i // c)).astype(bf16)
    wa_h, wa_l = _split(wa_ref[...], 2)
    dd = functools.partial(jnp.dot, preferred_element_type=f32)

    for b in range(nb):
        a_h, a_l = _split(p_ref[b, :, 4 * w:4 * w + LANES], 2)
        z = dd(a_h, wa_h) + dd(a_l, wa_h) + dd(a_h, wa_l) + ba_ref[...]
        log_a = -_softplus(-z) * (1.0 / GLA_GATE_TAU)
        cum = _dot_sel_lhs(tri, log_a, 3)
        for lt in range(npt):
            lanes = slice(lt * LANES, (lt + 1) * LANES)
            t_ref[b, Q, lt] = p_ref[b, :, lanes] * (HEAD_DIM ** -0.5)
            t_ref[b, K, lt] = p_ref[b, :, w + lt * LANES:w + (lt + 1) * LANES]
            t_ref[b, V, lt] = p_ref[b, :, 2 * w + lt * LANES:2 * w + (lt + 1) * LANES]
            t_ref[b, CUM, lt] = cum[:, lanes]
        yield

    def ld(b, j, off):
        return jnp.concatenate([t_ref[b, j, lt, pl.ds(off, nc, stride=c), :] for lt in range(npt)], axis=1)

    for b in range(nb):
        qt = [ld(b, Q, o) for o in range(c)]
        kt = [ld(b, K, o) for o in range(c)]
        ct = [ld(b, CUM, o) for o in range(c)]
        pair = 0
        for t in range(c):
            for s in range(t + 1):
                qk = qt[t] * kt[s]
                d_ref[b, pair * nc:(pair + 1) * nc, :] = qk if s == t else qk * jnp.exp(ct[t] - ct[s])
                pair += 1
            if t % 4 == 3:
                yield
    for b in range(nb):
        sx_ref[b] = jnp.dot(_bf(d_ref[b]), bones, preferred_element_type=f32)
        yield
    for b in range(nb):
        vt = [ld(b, V, o) for o in range(c)]
        pair = 0
        for t in range(c):
            acc = None
            for s in range(t + 1):
                term = sx_ref[b, pair * nc:(pair + 1) * nc, :] * vt[s]
                acc = term if acc is None else acc + term
                pair += 1
            for lt in range(npt):
                t_ref[b, OUT, lt, pl.ds(t, nc, stride=c), :] = acc[:, lt * LANES:(lt + 1) * LANES]
            if t % 8 == 7:
                yield

    for ci in range(nc):
        sl = slice(ci * c, (ci + 1) * c)
        last = slice((ci + 1) * c - 1, (ci + 1) * c)
        for b in range(nb):
            for lt in range(npt):
                cum_c = t_ref[b, CUM, lt, sl, :]
                cum_last = t_ref[b, CUM, lt, last, :]
                st = st_ref[b, lt]
                t_ref[b, OUT, lt, sl, :] += _dot_nt(t_ref[b, Q, lt, sl, :] * jnp.exp(cum_c), st)
                upd = _dot_tn(t_ref[b, V, lt, sl, :], t_ref[b, K, lt, sl, :] * jnp.exp(cum_last - cum_c))
                st_ref[b, lt] = st * jnp.exp(cum_last) + jnp.where(pair_blk, upd, 0.0)
        yield

    o = jnp.concatenate(
        [jnp.concatenate([t_ref[b, OUT, lt] for lt in range(npt)], axis=1) for b in range(nb)], axis=0)
    ms = _dot_sel_rhs(o * o, bones, 2) * (1.0 / HEAD_DIM)
    o = o * lax.rsqrt(ms + LN_EPS) * ng_ref[...]
    for b in range(nb):
        o_ref[b] = o[b * tb:(b + 1) * tb] * _silu(p_ref[b, :, 3 * w:4 * w])


def _round_robin(gens):
    live = list(gens)
    while live:
        for gen in list(live):
            if next(gen, StopIteration) is StopIteration:
                live.remove(gen)
            else:
                yield


def _ssd_steps(b, p_ref, halo_ref, cw_ref, cb_ref, dtb_ref, alog_ref, dsk_ref, ng_ref, o_ref, st_ref):
    L = p_ref.shape[1]
    W, M, G = SSD_WIDTH, SSD_STATE, SSD_GROUPS
    gw = W // G

    @pl.when(pl.program_id(0) == 0)
    def _():
        st_ref[b] = jnp.zeros(st_ref.shape[1:], f32)
        halo_ref[b] = jnp.zeros(halo_ref.shape[1:], f32)

    z = p_ref[b, :, 0:W]
    xbc = p_ref[b, :, W:W + SSD_XBC]
    dt_raw = p_ref[b, :, W + SSD_XBC:W + SSD_XBC + LANES]
    xx = jnp.concatenate([halo_ref[b], xbc], axis=0)
    halo_ref[b] = xbc[L - 8:L]
    conv = cb_ref[...]
    for j in range(SSD_CONV):
        off = 8 - (SSD_CONV - 1) + j
        conv = conv + cw_ref[j:j + 1, :] * xx[off:off + L]
    act = _silu(conv)
    xs = act[:, 0:W]
    bm = act[:, W:W + G * M]
    cm = act[:, W + G * M:W + 2 * G * M]
    yield

    dt = _softplus(dt_raw + dtb_ref[...])
    a_neg = jnp.where(_iota((1, LANES), 1) < SSD_HEADS, -jnp.exp(alog_ref[...]), 0.0)
    da = dt * a_neg
    r_i, c_i = _iota((L, L), 0), _iota((L, L), 1)
    causal = r_i >= c_i
    cs = _dot_sel_lhs(causal.astype(bf16), da, 3)
    expand = (_iota((LANES, W), 0) == _iota((LANES, W), 1) // HEAD_DIM).astype(bf16)
    dtx = _dot_sel_rhs(dt, expand, 2)
    csx = _dot_sel_rhs(cs, expand, 2)
    eye_h = (_iota((8, LANES), 0) == _iota((8, LANES), 1)).astype(bf16)
    cs_t = None
    for p in _split(cs, 2):
        t = lax.dot_general(eye_h, p, (((1,), (1,)), ((), ())), preferred_element_type=f32)
        cs_t = t if cs_t is None else cs_t + t
    xdt = xs * dtx
    yield

    st = st_ref[b]
    y_parts = []
    for gi in range(G):
        bm_g = bm[:, gi * M:(gi + 1) * M]
        cm_g = cm[:, gi * M:(gi + 1) * M]
        cb = _dot_nt(cm_g, bm_g)
        for e in range(SSD_HEADS // G):
            h = gi * (SSD_HEADS // G) + e
            seg = jnp.where(causal, jnp.exp(jnp.broadcast_to(cs[:, h:h + 1], (L, L)) - cs_t[h:h + 1, :]), 0.0)
            y_parts.append(_dot(cb * seg, xdt[:, h * HEAD_DIM:(h + 1) * HEAD_DIM]))
            if e % 2:
                yield
    y = jnp.concatenate(y_parts, axis=1)
    y_inter = jnp.concatenate(
        [_dot(cm[:, gi * M:(gi + 1) * M], st[:, gi * gw:(gi + 1) * gw]) for gi in range(G)], axis=1)
    y = y + y_inter * jnp.exp(csx)
    yield

    cs_last = csx[L - 1:L, :]
    xdtw = xdt * jnp.exp(cs_last - csx)
    d_state = jnp.concatenate(
        [_dot_tn(bm[:, gi * M:(gi + 1) * M], xdtw[:, gi * gw:(gi + 1) * gw]) for gi in range(G)], axis=1)
    st_ref[b] = st * jnp.exp(cs_last) + d_state
    yield

    y = (y + dsk_ref[...] * xs) * _silu(z)
    outs = []
    for gi in range(G):
        yg = y[:, gi * gw:(gi + 1) * gw]
        ms = jnp.mean(yg * yg, axis=-1, keepdims=True)
        outs.append(yg * lax.rsqrt(ms + LN_EPS))
    o_ref[b] = jnp.concatenate(outs, axis=1) * ng_ref[...]


def _inproj_ssd_kernel(x_ref, wg_ref, ws_ref, wr_ref, cw_ref, cb_ref, dtb_ref, alog_ref, dsk_ref, ng_ref,
                       pg_ref, pr_ref, os_ref, ps_ref, halo_ref, st_ref):
    nb, L, d = x_ref.shape
    xb = _bf(x_ref[...].reshape(nb * L, d))
    ps_ref[...] = jnp.dot(xb, ws_ref[...], preferred_element_type=f32).reshape(ps_ref.shape)

    def other_columns():
        for w_ref, o_ref in ((wg_ref, pg_ref), (wr_ref, pr_ref)):
            n = w_ref.shape[1]
            for c0 in range(0, n, PROJ_CHUNK):
                c1 = min(c0 + PROJ_CHUNK, n)
                o_ref[:, :, c0:c1] = jnp.dot(xb, w_ref[:, c0:c1], preferred_element_type=f32).reshape(nb, L, c1 - c0)
                for _ in range(PROJ_PACE):
                    yield

    ssd = _round_robin([_ssd_steps(b, ps_ref, halo_ref, cw_ref, cb_ref, dtb_ref, alog_ref, dsk_ref,
                                   ng_ref, os_ref, st_ref) for b in range(nb)])
    for _ in _round_robin([ssd, other_columns()]):
        pass


def _in_proj_ssd(x3, wg, ws, wr, cw, cb, dtb, alog, dsk, ng):
    nb, t, d = x3.shape
    L = SSD_BLOCK
    full2 = lambda a: pl.BlockSpec(a.shape, lambda i: (0, 0))
    blk = lambda c: pl.BlockSpec((nb, L, c), lambda i: (0, i, 0))
    return pl.pallas_call(
        _inproj_ssd_kernel,
        grid=(t // L,),
        in_specs=[blk(d)] + [full2(a) for a in (wg, ws, wr, cw, cb, dtb, alog, dsk, ng)],
        out_specs=[blk(wg.shape[1]), blk(wr.shape[1]), blk(SSD_WIDTH)],
        out_shape=[jax.ShapeDtypeStruct((nb, t, wg.shape[1]), f32), jax.ShapeDtypeStruct((nb, t, wr.shape[1]), f32),
                   jax.ShapeDtypeStruct((nb, t, SSD_WIDTH), f32)],
        scratch_shapes=[pltpu.VMEM((nb, L, ws.shape[1]), f32), pltpu.VMEM((nb, 8, SSD_XBC), f32),
                        pltpu.VMEM((nb, SSD_STATE, SSD_WIDTH), f32)],
        compiler_params=_cparams(("arbitrary",)),
        name="in_proj_ssd",
    )(x3, wg, ws, wr, cw, cb, dtb, alog, dsk, ng)


def _rwkv_steps(sub, p_ref, prev_ref, mu_ref, w0_ref, w2_ref, a0_ref, a2_ref, g2_ref, kk_ref, ka_ref, rk_ref,
                lng_ref, lnb_ref, o_ref, zt_ref):
    nb = p_ref.shape[0]
    C, W = RWKV_CHUNK, RWKV_WIDTH
    H = LANES // HEAD_DIM
    npair = W // LANES
    i = pl.program_id(0)
    r0 = sub * C

    if sub == 0:
        @pl.when(i == 0)
        def _():
            zt_ref[...] = jnp.zeros_like(zt_ref)

    bones = _block_ones(W, HEAD_DIM).astype(bf16)
    pair_blk = _block_ones(LANES, HEAD_DIM)
    n = H * C
    rr, cc = _iota((n, n), 0), _iota((n, n), 1)
    same = rr // C == cc // C
    strict = same & (rr > cc)
    lower = same & (rr >= cc)
    eye = jnp.where(rr == cc, 1.0, 0.0)

    def stack(t):
        return jnp.where(pair_blk, jnp.concatenate([t] * H, axis=0), 0.0)

    rows = range(nb)
    nr = nb * C
    cols = p_ref[:, r0:r0 + C, :].reshape(nr, p_ref.shape[2])

    def row_before(b):
        return p_ref[b, r0 - 1:r0, :] if sub else jnp.where(i > 0, prev_ref[b, 7:8, :], 0.0)

    shifted = jnp.concatenate(
        [piece for b in rows for piece in (row_before(b), p_ref[b, r0:r0 + C - 1, :])], axis=0)
    cols = cols + (shifted - cols) * mu_ref[...]
    r = cols[:, 0:W]
    k = cols[:, W:2 * W]
    v = cols[:, 2 * W:3 * W]
    w_lr = cols[:, 3 * W:3 * W + 64]
    a_lr = cols[:, 3 * W + 64:3 * W + 128]
    g_lr = cols[:, 3 * W + 128:3 * W + 256]

    wv = -_softplus(-(w0_ref[...] + _dot_hi(jnp.tanh(w_lr), w2_ref[...]))) - 0.5
    lw = -jnp.exp(wv)
    a = jax.nn.sigmoid(a0_ref[...] + _dot_hi(a_lr, a2_ref[...]))
    g = _dot_hi(jax.nn.sigmoid(g_lr), g2_ref[...])
    yield

    kk = k * kk_ref[...]
    k2 = k * (1.0 + (a - 1.0) * ka_ref[...])
    head_sums = _dot_sel_rhs(jnp.concatenate([kk * kk, r * k2 * rk_ref[...]], axis=0), bones, 2)
    kk = kk / jnp.maximum(jnp.sqrt(head_sums[:nr]), 1e-12)
    bonus_v = head_sums[nr:] * v
    bvec = kk * a
    yield

    rb_i, cb_i = _iota((nr, nr), 0), _iota((nr, nr), 1)
    tri = ((rb_i >= cb_i) & (rb_i // C == cb_i // C)).astype(bf16)
    cum = _dot_sel_lhs(tri, lw, 3)
    cum_last = [cum[(b + 1) * C - 1:(b + 1) * C, :] for b in rows]
    e_neg = jnp.exp(-cum)
    e_rem = jnp.exp(jnp.concatenate([jnp.broadcast_to(cl, (C, W)) for cl in cum_last], axis=0) - cum)
    per_token = dict(aw=-kk * jnp.exp(cum - lw), rw=r * jnp.exp(cum), bn=bvec * e_neg, kn=k2 * e_neg,
                     bt=bvec * e_rem, kt=k2 * e_rem, vs=v)
    chains = [(b, p) for b in rows for p in range(npair)]
    q = [dict({name: stack(t[b * C:(b + 1) * C, p * LANES:(p + 1) * LANES]) for name, t in per_token.items()},
              wc=jnp.exp(cum_last[b][:, p * LANES:(p + 1) * LANES])) for b, p in chains]
    ids = range(len(chains))
    yield

    a_ab = [jnp.where(strict, _dot_nt(q[c]["aw"], q[c]["bn"]), 0.0) for c in ids]
    a_ak = [jnp.where(strict, _dot_nt(q[c]["aw"], q[c]["kn"]), 0.0) for c in ids]
    yield
    a_rb = [jnp.where(lower, _dot_nt(q[c]["rw"], q[c]["bn"]), 0.0) for c in ids]
    a_rk = [jnp.where(lower, _dot_nt(q[c]["rw"], q[c]["kn"]), 0.0) for c in ids]
    yield

    pw = a_ab
    tinv = [eye + a_ab[c] for c in ids]
    for _ in range(int(math.log2(C)) - 1):
        pw = [_dot(pw[c], pw[c]) for c in ids]
        yield
        tinv = [tinv[c] + _dot(tinv[c], pw[c]) for c in ids]
        yield

    zt = [zt_ref[b, p] for b, p in chains]
    rhs = [_dot_nt(q[c]["aw"], zt[c]) + _dot(a_ak[c], q[c]["vs"]) for c in ids]
    yield
    y0 = [_dot_nt(q[c]["rw"], zt[c]) + _dot(a_rk[c], q[c]["vs"]) for c in ids]
    yield
    u = [_dot(tinv[c], rhs[c]) for c in ids]
    yield
    y_st = [y0[c] + _dot(a_rb[c], u[c]) for c in ids]
    yield
    for c, (b, p) in enumerate(chains):
        zt_ref[b, p] = zt[c] * q[c]["wc"] + _dot_tn(u[c], q[c]["bt"]) + _dot_tn(q[c]["vs"], q[c]["kt"])
    yield

    inv_n = 1.0 / HEAD_DIM
    ys = []
    for b in rows:
        tiles = []
        for p in range(npair):
            y_c = y_st[b * npair + p]
            y = y_c[0:C]
            for h in range(1, H):
                y = y + y_c[h * C:(h + 1) * C]
            tiles.append(y)
        ys.append(jnp.concatenate(tiles, axis=1))
    y = jnp.concatenate(ys, axis=0)
    yc = y - _dot_sel_rhs(y, bones, 2) * inv_n
    var_h = _dot_sel_rhs(yc * yc, bones, 2) * inv_n
    y = yc * lax.rsqrt(var_h + RWKV_GN_EPS) * lng_ref[...] + lnb_ref[...]
    o_ref[:, r0:r0 + C, :] = ((y + bonus_v) * g).reshape(nb, C, W)


N_GLA_IN, N_RWKV_IN = 4, 13


def _gla_rwkv_kernel(*refs):
    gla_in = refs[:N_GLA_IN]
    rwkv_in = refs[N_GLA_IN:N_GLA_IN + N_RWKV_IN]
    og_ref, or_ref, st_ref, t_ref, d_ref, sx_ref, zt_ref = refs[N_GLA_IN + N_RWKV_IN:]
    n_sub = rwkv_in[0].shape[1] // RWKV_CHUNK
    rwkv = (step for sub in range(n_sub) for step in _rwkv_steps(sub, *rwkv_in, or_ref, zt_ref))
    gla = _gla_steps(*gla_in, og_ref, st_ref, t_ref, d_ref, sx_ref)
    for _ in _round_robin([rwkv, gla]):
        pass


def _gla_rwkv(pg, gla_params, pr, rwkv_params):
    nb, t, gcols = pg.shape
    rcols = pr.shape[2]
    tb = GLA_BLOCK
    npt = GLA_WIDTH // LANES
    pair_rows = (GLA_CHUNK * (GLA_CHUNK + 1) // 2) * (tb // GLA_CHUNK)
    full2 = lambda a: pl.BlockSpec(a.shape, lambda i: (0, 0))
    blk = lambda c: pl.BlockSpec((nb, tb, c), lambda i: (0, i, 0))
    assert len(gla_params) + 1 == N_GLA_IN and len(rwkv_params) + 2 == N_RWKV_IN
    return pl.pallas_call(
        _gla_rwkv_kernel,
        grid=(t // tb,),
        in_specs=[blk(gcols)] + [full2(p) for p in gla_params]
        + [blk(rcols), pl.BlockSpec((nb, 8, rcols), lambda i: (0, jnp.maximum(i * (tb // 8) - 1, 0), 0))]
        + [full2(p) for p in rwkv_params],
        out_specs=[blk(GLA_WIDTH), blk(RWKV_WIDTH)],
        out_shape=[jax.ShapeDtypeStruct((nb, t, GLA_WIDTH), f32), jax.ShapeDtypeStruct((nb, t, RWKV_WIDTH), f32)],
        scratch_shapes=[
            pltpu.VMEM((nb, npt, LANES, LANES), f32),
            pltpu.VMEM((nb, 5, npt, tb, LANES), f32),
            pltpu.VMEM((nb, pair_rows, GLA_WIDTH), f32),
            pltpu.VMEM((nb, pair_rows, GLA_WIDTH), f32),
            pltpu.VMEM((nb, RWKV_WIDTH // LANES, LANES, LANES), f32),
        ],
        compiler_params=_cparams(("arbitrary",)),
        name="gla_rwkv_mixers",
    )(pg, *gla_params, pr, pr, *rwkv_params)


def _route(logits, rb):
    tm = logits.shape[1]
    scores = jax.nn.sigmoid(logits)
    sel = scores + rb
    srow = [sel[e:e + 1, :] for e in range(N_EXPERTS)]
    crow = [scores[e:e + 1, :] for e in range(N_EXPERTS)]

    def top2_sum(v0, v1, v2, v3):
        m01, n01 = jnp.maximum(v0, v1), jnp.minimum(v0, v1)
        m23, n23 = jnp.maximum(v2, v3), jnp.minimum(v2, v3)
        return jnp.maximum(m01, m23) + jnp.maximum(jnp.minimum(m01, m23), jnp.maximum(n01, n23))

    gscore = [top2_sum(*srow[4 * gi:4 * gi + 4]) for gi in range(N_EXPERT_GROUPS)]
    best = jnp.zeros((1, tm), jnp.int32)
    best_v = gscore[0]
    for gi in range(1, N_EXPERT_GROUPS):
        better = gscore[gi] > best_v
        best = jnp.where(better, gi, best)
        best_v = jnp.where(better, gscore[gi], best_v)
    sv, cv = [], []
    for j in range(EXPERTS_PER_GROUP):
        s_j, c_j = srow[j], crow[j]
        for gi in range(1, N_EXPERT_GROUPS):
            s_j = jnp.where(best == gi, srow[4 * gi + j], s_j)
            c_j = jnp.where(best == gi, crow[4 * gi + j], c_j)
        sv.append(s_j)
        cv.append(c_j)
    j1 = jnp.zeros((1, tm), jnp.int32)
    v1 = sv[0]
    for j in range(1, EXPERTS_PER_GROUP):
        better = sv[j] > v1
        j1 = jnp.where(better, j, j1)
        v1 = jnp.where(better, sv[j], v1)
    j2 = jnp.full((1, tm), -1, jnp.int32)
    v2 = jnp.full((1, tm), -jnp.inf, f32)
    for j in range(EXPERTS_PER_GROUP):
        better = (j1 != j) & ((j2 < 0) | (sv[j] > v2))
        j2 = jnp.where(better, j, j2)
        v2 = jnp.where(better, sv[j], v2)
    w1 = cv[0]
    w2 = cv[0]
    for j in range(1, EXPERTS_PER_GROUP):
        w1 = jnp.where(j1 == j, cv[j], w1)
        w2 = jnp.where(j2 == j, cv[j], w2)
    wsum = w1 + w2
    g1, g2 = w1 / wsum, w2 / wsum
    rows = []
    for e in range(N_EXPERTS):
        gi, j = divmod(e, EXPERTS_PER_GROUP)
        in_g = best == gi
        rows.append(jnp.where(in_g & (j1 == j), g1, jnp.where(in_g & (j2 == j), g2, 0.0)))
    gate_t = jnp.concatenate(rows, axis=0)
    return gate_t, best


def _outproj_kernel(alpha, x_ref, og_ref, os_ref, or_ref, wo_ref, g_ref, b_ref, rwt_ref, rb_ref,
                    x1_ref, gate_ref, best_ref, cnt_ref):
    tm = x_ref.shape[0]
    ts = tm // cnt_ref.shape[0]
    parts = [slice(sb * ts, (sb + 1) * ts) for sb in range(cnt_ref.shape[0])]
    d = functools.partial(jnp.dot, preferred_element_type=f32)
    mix = [d(_bf(og_ref[s, :]), wo_ref[0:GLA_WIDTH, :])
           + d(_bf(os_ref[s, :]), wo_ref[GLA_WIDTH:GLA_WIDTH + SSD_WIDTH, :])
           + d(_bf(or_ref[s, :]), wo_ref[GLA_WIDTH + SSD_WIDTH:, :]) for s in parts]
    x1 = [_layer_norm(alpha * x_ref[s, :] + m, g_ref[...], b_ref[...], LN_EPS) for s, m in zip(parts, mix)]
    for s, v in zip(parts, x1):
        x1_ref[s, :] = v

    wh, wl = _split(rwt_ref[...], 2)
    nt = lambda a, b: lax.dot_general(a, b, (((1,), (1,)), ((), ())), preferred_element_type=f32)
    logits = []
    for v in x1:
        xh, xl = _split(v, 2)
        logits.append(nt(wh, xh) + nt(wl, xh) + nt(wh, xl))
    routed = [_route(lg, rb_ref[...]) for lg in logits]
    eye = (_iota((N_EXPERTS, LANES), 0) == _iota((N_EXPERTS, LANES), 1)).astype(bf16)
    lane = _iota((1, LANES), 1)
    for sb, (s, (gate_t, best)) in enumerate(zip(parts, routed)):
        gate = None
        for p in _split(gate_t, 3):
            t = lax.dot_general(p, eye, (((0,), (0,)), ((), ())), preferred_element_type=f32)
            gate = t if gate is None else gate + t
        gate_ref[s, :] = gate
        best_ref[:, s] = best
        cnt = jnp.zeros((1, LANES), jnp.int32)
        for gi in range(N_EXPERT_GROUPS):
            n_g = jnp.sum((best == gi).astype(jnp.int32), axis=-1, keepdims=True)
            cnt = jnp.where(lane == gi, n_g, cnt)
        cnt_ref[sb] = cnt


def _out_proj(alpha, x2, og, os_, or_, wo, g, b, rwt, rb):
    n, d = x2.shape
    tm = min(OUT_PROJ_ROWS, n)
    ts = min(MOE_SORT_ROWS, tm)
    full2 = lambda a: pl.BlockSpec(a.shape, lambda i: (0, 0))
    rows = lambda c: pl.BlockSpec((tm, c), lambda i: (i, 0))
    return pl.pallas_call(
        functools.partial(_outproj_kernel, alpha),
        grid=(n // tm,),
        in_specs=[rows(d), rows(og.shape[1]), rows(os_.shape[1]), rows(or_.shape[1]),
                  full2(wo), full2(g), full2(b), full2(rwt), full2(rb)],
        out_specs=[rows(d), rows(LANES), pl.BlockSpec((1, tm), lambda i: (0, i)),
                   pl.BlockSpec((tm // ts, 1, LANES), lambda i: (i, 0, 0))],
        out_shape=[jax.ShapeDtypeStruct((n, d), f32), jax.ShapeDtypeStruct((n, LANES), f32),
                   jax.ShapeDtypeStruct((1, n), jnp.int32), jax.ShapeDtypeStruct((n // ts, 1, LANES), jnp.int32)],
        compiler_params=_cparams(("arbitrary",)),
        name="out_proj_router",
    )(x2, og, os_, or_, wo, g, b, rwt, rb)


def _seg_copy(src, dst, sem, src_off, dst_off, rows, max_rows, wait):
    k = SEG_ALIGN
    sizes = []
    while k <= max_rows:
        sizes.append(k)
        k *= 2
    for k in reversed(sizes):
        shift = int(math.log2(k)) + 1

        @pl.when((rows & k) != 0)
        def _():
            done = (rows >> shift) << shift
            cp = pltpu.make_async_copy(
                src.at[pl.ds(pl.multiple_of(src_off + done, SEG_ALIGN), k)],
                dst.at[pl.ds(pl.multiple_of(dst_off + done, SEG_ALIGN), k)], sem)
            if wait:
                cp.wait()
            else:
                cp.start()


def _padded(count):
    return (count + (SEG_ALIGN - 1)) & (-SEG_ALIGN)


def _one_hot_rows(pos, n_rows):
    return jnp.where(_iota((n_rows, pos.shape[1]), 0) == pos, 1.0, 0.0).astype(bf16)


def _dispatch_kernel(cnt_sm, goff_sm, fill_sm, x_ref, gate_ref, best_ref, xs_out, gs_out, pos_ref,
                     xloc, gloc, zx, zg, sems):
    i = pl.program_id(0)
    tm = x_ref.shape[0]
    n_loc = xloc.shape[1]
    slot = i % 2

    def copies(blk, slot_, wait):
        loff = 0
        for gi in range(N_EXPERT_GROUPS):
            rows = _padded(cnt_sm[blk * N_EXPERT_GROUPS + gi])
            dst = goff_sm[blk * N_EXPERT_GROUPS + gi]
            _seg_copy(xloc.at[slot_], xs_out, sems.at[0, slot_], loff, dst, rows, tm, wait)
            _seg_copy(gloc.at[slot_], gs_out, sems.at[1, slot_], loff, dst, rows, tm, wait)
            loff = loff + rows

    best = best_ref[...]
    onehot = (_iota((8, tm), 0) == best).astype(bf16)
    earlier = (_iota((tm, tm), 0) < _iota((tm, tm), 1)).astype(bf16)
    rank = jnp.dot(onehot, earlier, preferred_element_type=f32)
    pos = jnp.zeros((1, tm), f32)
    loff = 0
    for gi in range(N_EXPERT_GROUPS):
        pos = jnp.where(best == gi, rank[gi:gi + 1, :] + loff.astype(f32) if gi else rank[gi:gi + 1, :], pos)
        loff = loff + _padded(cnt_sm[i * N_EXPERT_GROUPS + gi])
    pos = pos.astype(jnp.int32)
    pos_ref[...] = pos
    perm = _one_hot_rows(pos, n_loc)
    xloc[slot] = jnp.dot(perm, _bf(x_ref[...]), preferred_element_type=f32)
    gloc[slot] = _dot_sel_lhs(perm, gate_ref[...], 2)
    copies(i, slot, False)

    @pl.when(i > 0)
    def _():
        copies(i - 1, 1 - slot, True)

    @pl.when(i == pl.num_programs(0) - 1)
    def _():
        copies(i, slot, True)
        zx[...] = jnp.zeros_like(zx)
        zg[...] = jnp.zeros_like(zg)
        tile = zx.shape[0]

        def fills(wait):
            for src, dst, sem in ((zx, xs_out, sems.at[2, 0]), (zg, gs_out, sems.at[2, 1])):
                for gi in range(N_EXPERT_GROUPS):
                    _seg_copy(src, dst, sem, 0, fill_sm[gi], fill_sm[N_EXPERT_GROUPS + gi], tile // 2, wait)

                def body(j, carry):
                    off = pl.multiple_of(fill_sm[2 * N_EXPERT_GROUPS] + j * tile, SEG_ALIGN)
                    cp = pltpu.make_async_copy(src, dst.at[pl.ds(off, tile)], sem)
                    if wait:
                        cp.wait()
                    else:
                        cp.start()
                    return carry

                lax.fori_loop(0, fill_sm[2 * N_EXPERT_GROUPS + 1], body, 0)

        fills(False)
        fills(True)


def _experts_kernel(tg_sm, tv_sm, x_ref, gate_ref, wg_ref, wu_ref, wd_ref, y_ref):
    t = pl.program_id(0)

    @pl.when(tv_sm[t] == 0)
    def _():
        y_ref[...] = jnp.zeros_like(y_ref)

    @pl.when(tv_sm[t] > 0)
    def _():
        first = tg_sm[t] * EXPERTS_PER_GROUP
        xb = _bf(x_ref[...])
        gate = gate_ref[...]
        lane = _iota(gate.shape, 1)
        hs = []
        for j in range(EXPERTS_PER_GROUP):
            hg = jnp.dot(xb, wg_ref[j], preferred_element_type=f32)
            hu = jnp.dot(xb, wu_ref[j], preferred_element_type=f32)
            gcol = jnp.sum(jnp.where(lane == first + j, gate, 0.0), axis=-1, keepdims=True)
            hs.append(_bf(_silu(hg) * hu * gcol))
        ff = wd_ref.shape[1]
        y_ref[...] = jnp.dot(jnp.concatenate(hs, axis=1), wd_ref[...].reshape(EXPERTS_PER_GROUP * ff, -1),
                             preferred_element_type=f32)


def _combine_kernel(alpha, cnt_sm, goff_sm, x_ref, pos_ref, g_ref, b_ref, ys_ref, o_ref, yloc, sems):
    i = pl.program_id(0)
    tm = x_ref.shape[0]
    n_loc = yloc.shape[1]
    slot = i % 2

    def copies(blk, slot_, wait):
        loff = 0
        for gi in range(N_EXPERT_GROUPS):
            rows = _padded(cnt_sm[blk * N_EXPERT_GROUPS + gi])
            src = goff_sm[blk * N_EXPERT_GROUPS + gi]
            _seg_copy(ys_ref, yloc.at[slot_], sems.at[slot_], src, loff, rows, tm, wait)
            loff = loff + rows

    @pl.when(i == 0)
    def _():
        yloc[...] = jnp.zeros_like(yloc)
        copies(0, 0, False)

    @pl.when(i + 1 < pl.num_programs(0))
    def _():
        copies(i + 1, 1 - slot, False)

    copies(i, slot, True)
    perm = _one_hot_rows(pos_ref[...], n_loc)
    y = lax.dot_general(perm, _bf(yloc[slot]), (((0,), (0,)), ((), ())), preferred_element_type=f32)
    o_ref[...] = _layer_norm(alpha * x_ref[...] + y, g_ref[...], b_ref[...], LN_EPS)


def _moe_tables(cnt, n_tiles, tile):
    pc = (cnt + (SEG_ALIGN - 1)) // SEG_ALIGN * SEG_ALIGN
    gtot = jnp.sum(pc, axis=0)
    gcap = (gtot + tile - 1) // tile * tile
    gend = jnp.cumsum(gcap)
    gstart = gend - gcap
    goff = gstart[None, :] + jnp.cumsum(pc, axis=0) - pc
    tstart = jnp.arange(n_tiles, dtype=jnp.int32) * tile
    tg = jnp.minimum(jnp.sum((tstart[:, None] >= gend[None, :]).astype(jnp.int32), axis=1), N_EXPERT_GROUPS - 1)
    tv = jnp.clip(gtot[tg] - (tstart - gstart[tg]), 0, tile)
    fill = jnp.concatenate([gstart + gtot, gcap - gtot, gend[-1:], n_tiles - gend[-1:] // tile])
    return goff.reshape(-1).astype(jnp.int32), tg.astype(jnp.int32), tv.astype(jnp.int32), fill.astype(jnp.int32)


def _moe(alpha, x1, gate, best, cnt, layer, wg, wu, wd, g, b):
    n, d = x1.shape
    ff = wg.shape[3]
    nblk = cnt.shape[0]
    tm = n // nblk
    tile = MOE_TILE
    n_loc = tm + LANES
    n_sorted = -(-(n + nblk * N_EXPERT_GROUPS * SEG_ALIGN + N_EXPERT_GROUPS * tile) // tile) * tile
    n_tiles = n_sorted // tile
    cnt_flat = cnt[:, 0, :N_EXPERT_GROUPS].reshape(-1)
    goff, tg, tv, fill = _moe_tables(cnt[:, 0, :N_EXPERT_GROUPS], n_tiles, tile)
    any_spec = pl.BlockSpec(memory_space=pl.ANY)

    xs, gs, pos = pl.pallas_call(
        _dispatch_kernel,
        grid_spec=pltpu.PrefetchScalarGridSpec(
            num_scalar_prefetch=3,
            grid=(nblk,),
            in_specs=[
                pl.BlockSpec((tm, d), lambda i, *_: (i, 0)),
                pl.BlockSpec((tm, LANES), lambda i, *_: (i, 0)),
                pl.BlockSpec((1, tm), lambda i, *_: (0, i)),
            ],
            out_specs=[any_spec, any_spec, pl.BlockSpec((1, tm), lambda i, *_: (0, i))],
            scratch_shapes=[pltpu.VMEM((2, n_loc, d), f32), pltpu.VMEM((2, n_loc, LANES), f32),
                            pltpu.VMEM((tile, d), f32), pltpu.VMEM((tile, LANES), f32),
                            pltpu.SemaphoreType.DMA((3, 2))],
        ),
        out_shape=[jax.ShapeDtypeStruct((n_sorted, d), f32), jax.ShapeDtypeStruct((n_sorted, LANES), f32),
                   jax.ShapeDtypeStruct((1, n), jnp.int32)],
        compiler_params=_cparams(("arbitrary",)),
        name="moe_dispatch",
    )(cnt_flat, goff, fill, x1, gate, best)

    ys = pl.pallas_call(
        _experts_kernel,
        grid_spec=pltpu.PrefetchScalarGridSpec(
            num_scalar_prefetch=2,
            grid=(n_tiles,),
            in_specs=[
                pl.BlockSpec((tile, d), lambda t, *_: (t, 0)),
                pl.BlockSpec((tile, LANES), lambda t, *_: (t, 0)),
                pl.BlockSpec((None, EXPERTS_PER_GROUP, d, ff), lambda t, tg_, tv_: (layer, tg_[t], 0, 0)),
                pl.BlockSpec((None, EXPERTS_PER_GROUP, d, ff), lambda t, tg_, tv_: (layer, tg_[t], 0, 0)),
                pl.BlockSpec((None, EXPERTS_PER_GROUP, ff, d), lambda t, tg_, tv_: (layer, tg_[t], 0, 0)),
            ],
            out_specs=pl.BlockSpec((tile, d), lambda t, *_: (t, 0)),
        ),
        out_shape=jax.ShapeDtypeStruct((n_sorted, d), f32),
        compiler_params=_cparams(("arbitrary",)),
        name="moe_experts",
    )(tg, tv, xs, gs, wg, wu, wd)

    return pl.pallas_call(
        functools.partial(_combine_kernel, alpha),
        grid_spec=pltpu.PrefetchScalarGridSpec(
            num_scalar_prefetch=2,
            grid=(nblk,),
            in_specs=[
                pl.BlockSpec((tm, d), lambda i, *_: (i, 0)),
                pl.BlockSpec((1, tm), lambda i, *_: (0, i)),
                pl.BlockSpec(g.shape, lambda i, *_: (0, 0)),
                pl.BlockSpec(b.shape, lambda i, *_: (0, 0)),
                any_spec,
            ],
            out_specs=pl.BlockSpec((tm, d), lambda i, *_: (i, 0)),
            scratch_shapes=[pltpu.VMEM((2, n_loc, d), f32), pltpu.SemaphoreType.DMA((2,))],
        ),
        out_shape=jax.ShapeDtypeStruct((n, d), f32),
        compiler_params=_cparams(("arbitrary",)),
        name="moe_combine",
    )(cnt_flat, goff, x1, pos, g, b, ys)


def _pad_cols(w, n):
    return jnp.pad(w, ((0, 0), (0, n - w.shape[1])))


def _row(v, n=None):
    v = v.reshape(1, -1)
    return v if n is None else _pad_cols(v, n)


def kernel(x, w_in, gla_w_alpha, gla_b_alpha, gla_norm_g, ssd_conv_w, ssd_conv_b, ssd_dt_bias, ssd_A_log, ssd_D, ssd_norm_g, rwkv_mu, rwkv_w0, rwkv_w2, rwkv_a0, rwkv_a2, rwkv_g2, rwkv_k_k, rwkv_k_a, rwkv_r_k, rwkv_ln_g, rwkv_ln_b, w_out, ln1_g, ln1_b, router_w, router_b, exp_w_gate, exp_w_up, exp_w_down, ln2_g, ln2_b):
    nb, t, d = x.shape
    depth = w_in.shape[0]
    alpha = float((2 * depth) ** 0.25)
    gla_cols = 4 * GLA_WIDTH + GLA_GATE_RANK
    ssd_cols = SSD_WIDTH + SSD_XBC + SSD_HEADS
    rwt = router_w.T
    rb = router_b.reshape(-1, 1)
    wg_all, wu_all, wd_all = _bf(exp_w_gate), _bf(exp_w_up), _bf(exp_w_down)
    w_in_t = jnp.swapaxes(w_in, 1, 2)
    x2 = x.reshape(nb * t, d)
    for l in range(depth):
        wg, ws, wr = _regroup_w_in(w_in_t, l, (gla_cols, ssd_cols, RWKV_COLS), (GLA_COLS_PAD, SSD_COLS_PAD, RWKV_COLS))
        pg, pr, os_ = _in_proj_ssd(x2.reshape(nb, t, d), wg, ws, wr, ssd_conv_w[l], _row(ssd_conv_b[l]),
                                   _row(ssd_dt_bias[l], LANES), _row(ssd_A_log[l], LANES),
                                   _row(jnp.repeat(ssd_D[l], HEAD_DIM)), _row(ssd_norm_g[l]))
        gla_params = (jnp.pad(gla_w_alpha[l], ((0, LANES - GLA_GATE_RANK), (0, 0))), _row(gla_b_alpha[l]),
                      _row(jnp.tile(gla_norm_g[l], GLA_WIDTH // HEAD_DIM)))
        rwkv_params = (_row(rwkv_mu[l]), _row(rwkv_w0[l]), rwkv_w2[l], _row(rwkv_a0[l]), rwkv_a2[l], rwkv_g2[l],
                       _row(rwkv_k_k[l]), _row(rwkv_k_a[l]), _row(rwkv_r_k[l]), _row(rwkv_ln_g[l]),
                       _row(rwkv_ln_b[l]))
        og, or_ = _gla_rwkv(pg, gla_params, pr, rwkv_params)
        x1, gate, best, cnt = _out_proj(alpha, x2, og.reshape(nb * t, -1), os_.reshape(nb * t, -1),
                                        or_.reshape(nb * t, -1), _bf(w_out[l]), _row(ln1_g[l]), _row(ln1_b[l]),
                                        rwt, rb)
        x2 = _moe(alpha, x1, gate, best, cnt, l, wg_all, wu_all, wd_all, _row(ln2_g[l]), _row(ln2_b[l]))
    return x2.reshape(nb, t, d)
```

```python
import functools
import math

import jax
import jax.numpy as jnp
from jax import lax
from jax.experimental import pallas as pl
from jax.experimental.pallas import tpu as pltpu

f32 = jnp.float32
bf16 = jnp.bfloat16

LANES = 128
SUBLANES = 8
HEAD_DIM = 64
GLA_WIDTH = 256
GLA_GATE_RANK = 16
GLA_GATE_TAU = 16.0
GLA_CHUNK = 16
GLA_BLOCK = 128
GLA_COLS_PAD = 4 * GLA_WIDTH + LANES
SSD_WIDTH = 512
SSD_HEADS = 8
SSD_GROUPS = 2
SSD_STATE = 64
SSD_CONV = 4
SSD_XBC = SSD_WIDTH + 2 * SSD_GROUPS * SSD_STATE
SSD_BLOCK = 128
SSD_COLS_PAD = SSD_WIDTH + SSD_XBC + LANES
RWKV_WIDTH = 256
RWKV_CHUNK = 64
RWKV_GN_EPS = 64e-5
RWKV_COLS = 3 * RWKV_WIDTH + 64 + 64 + 128
N_EXPERTS = 16
N_EXPERT_GROUPS = 4
EXPERTS_PER_GROUP = 4
LN_EPS = 1e-5

PROJ_CHUNK = 256
PROJ_PACE = 4
OUT_PROJ_ROWS = 1024
MOE_SORT_ROWS = 512
MOE_TILE = 512
SEG_ALIGN = 8
VMEM_LIMIT = 56 * 1024 * 1024


def _bf(x):
    return x.astype(bf16)


def _dot(a, b):
    return jnp.dot(_bf(a), _bf(b), preferred_element_type=f32)


def _dot_nt(a, b):
    return lax.dot_general(_bf(a), _bf(b), (((1,), (1,)), ((), ())), preferred_element_type=f32)


def _dot_tn(a, b):
    return lax.dot_general(_bf(a), _bf(b), (((0,), (0,)), ((), ())), preferred_element_type=f32)


def _split(x, n):
    parts, r = [], x
    for _ in range(n):
        p = r.astype(bf16)
        parts.append(p)
        r = r - p.astype(f32)
    return parts


def _dot_sel_rhs(a, sel, n=3):
    out = None
    for p in _split(a, n):
        t = jnp.dot(p, sel, preferred_element_type=f32)
        out = t if out is None else out + t
    return out


def _dot_sel_lhs(sel, b, n=3):
    out = None
    for p in _split(b, n):
        t = jnp.dot(sel, p, preferred_element_type=f32)
        out = t if out is None else out + t
    return out


def _dot_hi(a, b):
    ah, al = _split(a, 2)
    bh, bl = _split(b, 2)
    d = functools.partial(jnp.dot, preferred_element_type=f32)
    return d(ah, bh) + d(al, bh) + d(ah, bl)


def _iota(shape, dim):
    return lax.broadcasted_iota(jnp.int32, shape, dim)


def _block_ones(n, blk):
    return (_iota((n, n), 0) // blk == _iota((n, n), 1) // blk)


def _silu(x):
    return x * jax.nn.sigmoid(x)


def _softplus(x):
    return jnp.maximum(x, 0.0) + jnp.log1p(jnp.exp(-jnp.abs(x)))


def _layer_norm(y, g, b, eps):
    mu = jnp.mean(y, axis=-1, keepdims=True)
    yc = y - mu
    var = jnp.mean(yc * yc, axis=-1, keepdims=True)
    return yc * lax.rsqrt(var + eps) * g + b


def _cparams(sem):
    return pltpu.CompilerParams(dimension_semantics=sem, vmem_limit_bytes=VMEM_LIMIT)


def _regroup_kernel(splits, w_ref, *o_refs):
    start = 0
    for width, o_ref in zip(splits, o_refs):
        part = w_ref[0, start:start + width, :]
        pad = o_ref.shape[1] - width
        if pad:
            part = jnp.concatenate([part, jnp.zeros((pad, part.shape[1]), part.dtype)], axis=0)
        o_ref[...] = _bf(part.T)
        start += width


def _regroup_w_in(w_in_t, layer, splits, padded):
    _, n_in, d = w_in_t.shape
    return pl.pallas_call(
        functools.partial(_regroup_kernel, splits),
        grid=(1,),
        in_specs=[pl.BlockSpec((1, n_in, d), lambda i: (layer, 0, 0))],
        out_specs=[pl.BlockSpec((d, p), lambda i: (0, 0)) for p in padded],
        out_shape=[jax.ShapeDtypeStruct((d, p), bf16) for p in padded],
        compiler_params=_cparams(("arbitrary",)),
        name="regroup_w_in",
    )(w_in_t)


def _gla_steps(p_ref, wa_ref, ba_ref, ng_ref, o_ref, st_ref, t_ref, d_ref, sx_ref):
    nb, tb, _ = p_ref.shape
    w, c = GLA_WIDTH, GLA_CHUNK
    nc = tb // c
    npt = w // LANES
    Q, K, V, CUM, OUT = range(5)

    @pl.when(pl.program_id(0) == 0)
    def _():
        st_ref[...] = jnp.zeros_like(st_ref)

    bones = _block_ones(w, HEAD_DIM).astype(bf16)
    pair_blk = _block_ones(LANES, HEAD_DIM)
    r_i, c_i = _iota((tb, tb), 0), _iota((tb, tb), 1)
    tri = ((r_i >= c_i) & (r_i // c == c_i // c)).astype(bf16)
    wa_h, wa_l = _split(wa_ref[...], 2)
    dd = functools.partial(jnp.dot, preferred_element_type=f32)

    for b in range(nb):
        a_h, a_l = _split(p_ref[b, :, 4 * w:4 * w + LANES], 2)
        z = dd(a_h, wa_h) + dd(a_l, wa_h) + dd(a_h, wa_l) + ba_ref[...]
        log_a = -_softplus(-z) * (1.0 / GLA_GATE_TAU)
        cum = _dot_sel_lhs(tri, log_a, 3)
        for lt in range(npt):
            lanes = slice(lt * LANES, (lt + 1) * LANES)
            t_ref[b, Q, lt] = p_ref[b, :, lanes] * (HEAD_DIM ** -0.5)
            t_ref[b, K, lt] = p_ref[b, :, w + lt * LANES:w + (lt + 1) * LANES]
            t_ref[b, V, lt] = p_ref[b, :, 2 * w + lt * LANES:2 * w + (lt + 1) * LANES]
            t_ref[b, CUM, lt] = cum[:, lanes]
        yield

    def ld(b, j, off):
        return jnp.concatenate([t_ref[b, j, lt, pl.ds(off, nc, stride=c), :] for lt in range(npt)], axis=1)

    for b in range(nb):
        qt = [ld(b, Q, o) for o in range(c)]
        kt = [ld(b, K, o) for o in range(c)]
        ct = [ld(b, CUM, o) for o in range(c)]
        pair = 0
        for t in range(c):
            for s in range(t + 1):
                qk = qt[t] * kt[s]
                d_ref[b, pair * nc:(pair + 1) * nc, :] = qk if s == t else qk * jnp.exp(ct[t] - ct[s])
                pair += 1
            if t % 4 == 3:
                yield
    for b in range(nb):
        sx_ref[b] = jnp.dot(_bf(d_ref[b]), bones, preferred_element_type=f32)
        yield
    for b in range(nb):
        vt = [ld(b, V, o) for o in range(c)]
        pair = 0
        for t in range(c):
            acc = None
            for s in range(t + 1):
                term = sx_ref[b, pair * nc:(pair + 1) * nc, :] * vt[s]
                acc = term if acc is None else acc + term
                pair += 1
            for lt in range(npt):
                t_ref[b, OUT, lt, pl.ds(t, nc, stride=c), :] = acc[:, lt * LANES:(lt + 1) * LANES]
            if t % 8 == 7:
                yield

    for ci in range(nc):
        sl = slice(ci * c, (ci + 1) * c)
        last = slice((ci + 1) * c - 1, (ci + 1) * c)
        for b in range(nb):
            for lt in range(npt):
                cum_c = t_ref[b, CUM, lt, sl, :]
                cum_last = t_ref[b, CUM, lt, last, :]
                st = st_ref[b, lt]
                t_ref[b, OUT, lt, sl, :] += _dot_nt(t_ref[b, Q, lt, sl, :] * jnp.exp(cum_c), st)
                upd = _dot_tn(t_ref[b, V, lt, sl, :], t_ref[b, K, lt, sl, :] * jnp.exp(cum_last - cum_c))
                st_ref[b, lt] = st * jnp.exp(cum_last) + jnp.where(pair_blk, upd, 0.0)
        yield

    o = jnp.concatenate(
        [jnp.concatenate([t_ref[b, OUT, lt] for lt in range(npt)], axis=1) for b in range(nb)], axis=0)
    ms = _dot_sel_rhs(o * o, bones, 2) * (1.0 / HEAD_DIM)
    o = o * lax.rsqrt(ms + LN_EPS) * ng_ref[...]
    for b in range(nb):
        o_ref[b] = o[b * tb:(b + 1) * tb] * _silu(p_ref[b, :, 3 * w:4 * w])


def _round_robin(gens):
    live = list(gens)
    while live:
        for gen in list(live):
            if next(gen, StopIteration) is StopIteration:
                live.remove(gen)
            else:
                yield


def _ssd_steps(b, p_ref, halo_ref, cw_ref, cb_ref, dtb_ref, alog_ref, dsk_ref, ng_ref, o_ref, st_ref):
    L = p_ref.shape[1]
    W, M, G = SSD_WIDTH, SSD_STATE, SSD_GROUPS
    gw = W // G

    @pl.when(pl.program_id(0) == 0)
    def _():
        st_ref[b] = jnp.zeros(st_ref.shape[1:], f32)
        halo_ref[b] = jnp.zeros(halo_ref.shape[1:], f32)

    z = p_ref[b, :, 0:W]
    xbc = p_ref[b, :, W:W + SSD_XBC]
    dt_raw = p_ref[b, :, W + SSD_XBC:W + SSD_XBC + LANES]
    xx = jnp.concatenate([halo_ref[b], xbc], axis=0)
    halo_ref[b] = xbc[L - SUBLANES:L]
    conv = cb_ref[...]
    for j in range(SSD_CONV):
        off = SUBLANES - (SSD_CONV - 1) + j
        conv = conv + cw_ref[j:j + 1, :] * xx[off:off + L]
    act = _silu(conv)
    xs = act[:, 0:W]
    bm = act[:, W:W + G * M]
    cm = act[:, W + G * M:W + 2 * G * M]
    yield

    dt = _softplus(dt_raw + dtb_ref[...])
    a_neg = jnp.where(_iota((1, LANES), 1) < SSD_HEADS, -jnp.exp(alog_ref[...]), 0.0)
    da = dt * a_neg
    r_i, c_i = _iota((L, L), 0), _iota((L, L), 1)
    causal = r_i >= c_i
    cs = _dot_sel_lhs(causal.astype(bf16), da, 3)
    expand = (_iota((LANES, W), 0) == _iota((LANES, W), 1) // HEAD_DIM).astype(bf16)
    dtx = _dot_sel_rhs(dt, expand, 2)
    csx = _dot_sel_rhs(cs, expand, 2)
    eye_h = (_iota((SSD_HEADS, LANES), 0) == _iota((SSD_HEADS, LANES), 1)).astype(bf16)
    cs_t = None
    for p in _split(cs, 2):
        t = lax.dot_general(eye_h, p, (((1,), (1,)), ((), ())), preferred_element_type=f32)
        cs_t = t if cs_t is None else cs_t + t
    xdt = xs * dtx
    yield

    st = st_ref[b]
    y_parts = []
    for gi in range(G):
        bm_g = bm[:, gi * M:(gi + 1) * M]
        cm_g = cm[:, gi * M:(gi + 1) * M]
        cb = _dot_nt(cm_g, bm_g)
        for e in range(SSD_HEADS // G):
            h = gi * (SSD_HEADS // G) + e
            seg = jnp.where(causal, jnp.exp(jnp.broadcast_to(cs[:, h:h + 1], (L, L)) - cs_t[h:h + 1, :]), 0.0)
            y_parts.append(_dot(cb * seg, xdt[:, h * HEAD_DIM:(h + 1) * HEAD_DIM]))
            if e % 2:
                yield
    y = jnp.concatenate(y_parts, axis=1)
    y_inter = jnp.concatenate(
        [_dot(cm[:, gi * M:(gi + 1) * M], st[:, gi * gw:(gi + 1) * gw]) for gi in range(G)], axis=1)
    y = y + y_inter * jnp.exp(csx)
    yield

    cs_last = csx[L - 1:L, :]
    xdtw = xdt * jnp.exp(cs_last - csx)
    d_state = jnp.concatenate(
        [_dot_tn(bm[:, gi * M:(gi + 1) * M], xdtw[:, gi * gw:(gi + 1) * gw]) for gi in range(G)], axis=1)
    st_ref[b] = st * jnp.exp(cs_last) + d_state
    yield

    y = (y + dsk_ref[...] * xs) * _silu(z)
    outs = []
    for gi in range(G):
        yg = y[:, gi * gw:(gi + 1) * gw]
        ms = jnp.mean(yg * yg, axis=-1, keepdims=True)
        outs.append(yg * lax.rsqrt(ms + LN_EPS))
    o_ref[b] = jnp.concatenate(outs, axis=1) * ng_ref[...]


def _inproj_ssd_kernel(x_ref, wg_ref, ws_ref, wr_ref, cw_ref, cb_ref, dtb_ref, alog_ref, dsk_ref, ng_ref,
                       pg_ref, pr_ref, os_ref, ps_ref, halo_ref, st_ref):
    nb, L, d = x_ref.shape
    xb = _bf(x_ref[...].reshape(nb * L, d))
    ps_ref[...] = jnp.dot(xb, ws_ref[...], preferred_element_type=f32).reshape(ps_ref.shape)

    def other_columns():
        for w_ref, o_ref in ((wg_ref, pg_ref), (wr_ref, pr_ref)):
            n = w_ref.shape[1]
            for c0 in range(0, n, PROJ_CHUNK):
                c1 = min(c0 + PROJ_CHUNK, n)
                o_ref[:, :, c0:c1] = jnp.dot(xb, w_ref[:, c0:c1], preferred_element_type=f32).reshape(nb, L, c1 - c0)
                for _ in range(PROJ_PACE):
                    yield

    ssd = _round_robin([_ssd_steps(b, ps_ref, halo_ref, cw_ref, cb_ref, dtb_ref, alog_ref, dsk_ref,
                                   ng_ref, os_ref, st_ref) for b in range(nb)])
    for _ in _round_robin([ssd, other_columns()]):
        pass


def _in_proj_ssd(x3, wg, ws, wr, cw, cb, dtb, alog, dsk, ng):
    nb, t, d = x3.shape
    L = SSD_BLOCK
    full2 = lambda a: pl.BlockSpec(a.shape, lambda i: (0, 0))
    blk = lambda c: pl.BlockSpec((nb, L, c), lambda i: (0, i, 0))
    return pl.pallas_call(
        _inproj_ssd_kernel,
        grid=(t // L,),
        in_specs=[blk(d)] + [full2(a) for a in (wg, ws, wr, cw, cb, dtb, alog, dsk, ng)],
        out_specs=[blk(wg.shape[1]), blk(wr.shape[1]), blk(SSD_WIDTH)],
        out_shape=[jax.ShapeDtypeStruct((nb, t, wg.shape[1]), f32), jax.ShapeDtypeStruct((nb, t, wr.shape[1]), f32),
                   jax.ShapeDtypeStruct((nb, t, SSD_WIDTH), f32)],
        scratch_shapes=[pltpu.VMEM((nb, L, ws.shape[1]), f32), pltpu.VMEM((nb, SUBLANES, SSD_XBC), f32),
                        pltpu.VMEM((nb, SSD_STATE, SSD_WIDTH), f32)],
        compiler_params=_cparams(("arbitrary",)),
        name="in_proj_ssd",
    )(x3, wg, ws, wr, cw, cb, dtb, alog, dsk, ng)


def _rwkv_steps(sub, p_ref, prev_ref, mu_ref, w0_ref, w2_ref, a0_ref, a2_ref, g2_ref, kk_ref, ka_ref, rk_ref,
                lng_ref, lnb_ref, o_ref, zt_ref):
    nb = p_ref.shape[0]
    C, W = RWKV_CHUNK, RWKV_WIDTH
    H = LANES // HEAD_DIM
    npair = W // LANES
    i = pl.program_id(0)
    r0 = sub * C

    if sub == 0:
        @pl.when(i == 0)
        def _():
            zt_ref[...] = jnp.zeros_like(zt_ref)

    bones = _block_ones(W, HEAD_DIM).astype(bf16)
    pair_blk = _block_ones(LANES, HEAD_DIM)
    n = H * C
    rr, cc = _iota((n, n), 0), _iota((n, n), 1)
    same = rr // C == cc // C
    strict = same & (rr > cc)
    lower = same & (rr >= cc)
    eye = jnp.where(rr == cc, 1.0, 0.0)

    def stack(t):
        return jnp.where(pair_blk, jnp.concatenate([t] * H, axis=0), 0.0)

    rows = range(nb)
    nr = nb * C
    cols = p_ref[:, r0:r0 + C, :].reshape(nr, p_ref.shape[2])

    def row_before(b):
        return p_ref[b, r0 - 1:r0, :] if sub else jnp.where(i > 0, prev_ref[b, SUBLANES - 1:SUBLANES, :], 0.0)

    shifted = jnp.concatenate(
        [piece for b in rows for piece in (row_before(b), p_ref[b, r0:r0 + C - 1, :])], axis=0)
    cols = cols + (shifted - cols) * mu_ref[...]
    r = cols[:, 0:W]
    k = cols[:, W:2 * W]
    v = cols[:, 2 * W:3 * W]
    w_lr = cols[:, 3 * W:3 * W + 64]
    a_lr = cols[:, 3 * W + 64:3 * W + 128]
    g_lr = cols[:, 3 * W + 128:3 * W + 256]

    wv = -_softplus(-(w0_ref[...] + _dot_hi(jnp.tanh(w_lr), w2_ref[...]))) - 0.5
    lw = -jnp.exp(wv)
    a = jax.nn.sigmoid(a0_ref[...] + _dot_hi(a_lr, a2_ref[...]))
    g = _dot_hi(jax.nn.sigmoid(g_lr), g2_ref[...])
    yield

    kk = k * kk_ref[...]
    k2 = k * (1.0 + (a - 1.0) * ka_ref[...])
    head_sums = _dot_sel_rhs(jnp.concatenate([kk * kk, r * k2 * rk_ref[...]], axis=0), bones, 2)
    kk = kk / jnp.maximum(jnp.sqrt(head_sums[:nr]), 1e-12)
    bonus_v = head_sums[nr:] * v
    bvec = kk * a
    yield

    rb_i, cb_i = _iota((nr, nr), 0), _iota((nr, nr), 1)
    tri = ((rb_i >= cb_i) & (rb_i // C == cb_i // C)).astype(bf16)
    cum = _dot_sel_lhs(tri, lw, 3)
    cum_last = [cum[(b + 1) * C - 1:(b + 1) * C, :] for b in rows]
    e_neg = jnp.exp(-cum)
    e_rem = jnp.exp(jnp.concatenate([jnp.broadcast_to(cl, (C, W)) for cl in cum_last], axis=0) - cum)
    per_token = dict(aw=-kk * jnp.exp(cum - lw), rw=r * jnp.exp(cum), bn=bvec * e_neg, kn=k2 * e_neg,
                     bt=bvec * e_rem, kt=k2 * e_rem, vs=v)
    chains = [(b, p) for b in rows for p in range(npair)]
    q = [dict({name: stack(t[b * C:(b + 1) * C, p * LANES:(p + 1) * LANES]) for name, t in per_token.items()},
              wc=jnp.exp(cum_last[b][:, p * LANES:(p + 1) * LANES])) for b, p in chains]
    ids = range(len(chains))
    yield

    a_ab = [jnp.where(strict, _dot_nt(q[c]["aw"], q[c]["bn"]), 0.0) for c in ids]
    a_ak = [jnp.where(strict, _dot_nt(q[c]["aw"], q[c]["kn"]), 0.0) for c in ids]
    yield
    a_rb = [jnp.where(lower, _dot_nt(q[c]["rw"], q[c]["bn"]), 0.0) for c in ids]
    a_rk = [jnp.where(lower, _dot_nt(q[c]["rw"], q[c]["kn"]), 0.0) for c in ids]
    yield

    pw = a_ab
    tinv = [eye + a_ab[c] for c in ids]
    for _ in range(int(math.log2(C)) - 1):
        pw = [_dot(pw[c], pw[c]) for c in ids]
        yield
        tinv = [tinv[c] + _dot(tinv[c], pw[c]) for c in ids]
        yield

    zt = [zt_ref[b, p] for b, p in chains]
    rhs = [_dot_nt(q[c]["aw"], zt[c]) + _dot(a_ak[c], q[c]["vs"]) for c in ids]
    yield
    y0 = [_dot_nt(q[c]["rw"], zt[c]) + _dot(a_rk[c], q[c]["vs"]) for c in ids]
    yield
    u = [_dot(tinv[c], rhs[c]) for c in ids]
    yield
    y_st = [y0[c] + _dot(a_rb[c], u[c]) for c in ids]
    yield
    for c, (b, p) in enumerate(chains):
        zt_ref[b, p] = zt[c] * q[c]["wc"] + _dot_tn(u[c], q[c]["bt"]) + _dot_tn(q[c]["vs"], q[c]["kt"])
    yield

    inv_n = 1.0 / HEAD_DIM
    ys = []
    for b in rows:
        tiles = []
        for p in range(npair):
            y_c = y_st[b * npair + p]
            y = y_c[0:C]
            for h in range(1, H):
                y = y + y_c[h * C:(h + 1) * C]
            tiles.append(y)
        ys.append(jnp.concatenate(tiles, axis=1))
    y = jnp.concatenate(ys, axis=0)
    yc = y - _dot_sel_rhs(y, bones, 2) * inv_n
    var_h = _dot_sel_rhs(yc * yc, bones, 2) * inv_n
    y = yc * lax.rsqrt(var_h + RWKV_GN_EPS) * lng_ref[...] + lnb_ref[...]
    o_ref[:, r0:r0 + C, :] = ((y + bonus_v) * g).reshape(nb, C, W)


N_GLA_IN, N_RWKV_IN = 4, 13


def _gla_rwkv_kernel(*refs):
    gla_in = refs[:N_GLA_IN]
    rwkv_in = refs[N_GLA_IN:N_GLA_IN + N_RWKV_IN]
    og_ref, or_ref, st_ref, t_ref, d_ref, sx_ref, zt_ref = refs[N_GLA_IN + N_RWKV_IN:]
    n_sub = rwkv_in[0].shape[1] // RWKV_CHUNK
    rwkv = (step for sub in range(n_sub) for step in _rwkv_steps(sub, *rwkv_in, or_ref, zt_ref))
    gla = _gla_steps(*gla_in, og_ref, st_ref, t_ref, d_ref, sx_ref)
    for _ in _round_robin([rwkv, gla]):
        pass


def _gla_rwkv(pg, gla_params, pr, rwkv_params):
    nb, t, gcols = pg.shape
    rcols = pr.shape[2]
    tb = GLA_BLOCK
    npt = GLA_WIDTH // LANES
    pair_rows = (GLA_CHUNK * (GLA_CHUNK + 1) // 2) * (tb // GLA_CHUNK)
    full2 = lambda a: pl.BlockSpec(a.shape, lambda i: (0, 0))
    blk = lambda c: pl.BlockSpec((nb, tb, c), lambda i: (0, i, 0))
    assert len(gla_params) + 1 == N_GLA_IN and len(rwkv_params) + 2 == N_RWKV_IN
    return pl.pallas_call(
        _gla_rwkv_kernel,
        grid=(t // tb,),
        in_specs=[blk(gcols)] + [full2(p) for p in gla_params]
        + [blk(rcols), pl.BlockSpec((nb, SUBLANES, rcols),
                                    lambda i: (0, jnp.maximum(i * (tb // SUBLANES) - 1, 0), 0))]
        + [full2(p) for p in rwkv_params],
        out_specs=[blk(GLA_WIDTH), blk(RWKV_WIDTH)],
        out_shape=[jax.ShapeDtypeStruct((nb, t, GLA_WIDTH), f32), jax.ShapeDtypeStruct((nb, t, RWKV_WIDTH), f32)],
        scratch_shapes=[
            pltpu.VMEM((nb, npt, LANES, LANES), f32),
            pltpu.VMEM((nb, 5, npt, tb, LANES), f32),
            pltpu.VMEM((nb, pair_rows, GLA_WIDTH), f32),
            pltpu.VMEM((nb, pair_rows, GLA_WIDTH), f32),
            pltpu.VMEM((nb, RWKV_WIDTH // LANES, LANES, LANES), f32),
        ],
        compiler_params=_cparams(("arbitrary",)),
        name="gla_rwkv_mixers",
    )(pg, *gla_params, pr, pr, *rwkv_params)


def _route(logits, rb):
    tm = logits.shape[1]
    scores = jax.nn.sigmoid(logits)
    sel = scores + rb
    srow = [sel[e:e + 1, :] for e in range(N_EXPERTS)]
    crow = [scores[e:e + 1, :] for e in range(N_EXPERTS)]

    def top2_sum(v0, v1, v2, v3):
        m01, n01 = jnp.maximum(v0, v1), jnp.minimum(v0, v1)
        m23, n23 = jnp.maximum(v2, v3), jnp.minimum(v2, v3)
        return jnp.maximum(m01, m23) + jnp.maximum(jnp.minimum(m01, m23), jnp.maximum(n01, n23))

    gscore = [top2_sum(*srow[4 * gi:4 * gi + 4]) for gi in range(N_EXPERT_GROUPS)]
    best = jnp.zeros((1, tm), jnp.int32)
    best_v = gscore[0]
    for gi in range(1, N_EXPERT_GROUPS):
        better = gscore[gi] > best_v
        best = jnp.where(better, gi, best)
        best_v = jnp.where(better, gscore[gi], best_v)
    sv, cv = [], []
    for j in range(EXPERTS_PER_GROUP):
        s_j, c_j = srow[j], crow[j]
        for gi in range(1, N_EXPERT_GROUPS):
            s_j = jnp.where(best == gi, srow[4 * gi + j], s_j)
            c_j = jnp.where(best == gi, crow[4 * gi + j], c_j)
        sv.append(s_j)
        cv.append(c_j)
    j1 = jnp.zeros((1, tm), jnp.int32)
    v1 = sv[0]
    for j in range(1, EXPERTS_PER_GROUP):
        better = sv[j] > v1
        j1 = jnp.where(better, j, j1)
        v1 = jnp.where(better, sv[j], v1)
    j2 = jnp.full((1, tm), -1, jnp.int32)
    v2 = jnp.full((1, tm), -jnp.inf, f32)
    for j in range(EXPERTS_PER_GROUP):
        better = (j1 != j) & ((j2 < 0) | (sv[j] > v2))
        j2 = jnp.where(better, j, j2)
        v2 = jnp.where(better, sv[j], v2)
    w1 = cv[0]
    w2 = cv[0]
    for j in range(1, EXPERTS_PER_GROUP):
        w1 = jnp.where(j1 == j, cv[j], w1)
        w2 = jnp.where(j2 == j, cv[j], w2)
    wsum = w1 + w2
    g1, g2 = w1 / wsum, w2 / wsum
    rows = []
    for e in range(N_EXPERTS):
        gi, j = divmod(e, EXPERTS_PER_GROUP)
        in_g = best == gi
        rows.append(jnp.where(in_g & (j1 == j), g1, jnp.where(in_g & (j2 == j), g2, 0.0)))
    gate_t = jnp.concatenate(rows, axis=0)
    return gate_t, best


def _outproj_kernel(alpha, x_ref, og_ref, os_ref, or_ref, wo_ref, g_ref, b_ref, rwt_ref, rb_ref,
                    x1_ref, gate_ref, best_ref, cnt_ref):
    tm = x_ref.shape[0]
    ts = tm // cnt_ref.shape[0]
    parts = [slice(sb * ts, (sb + 1) * ts) for sb in range(cnt_ref.shape[0])]
    d = functools.partial(jnp.dot, preferred_element_type=f32)
    mix = [d(_bf(og_ref[s, :]), wo_ref[0:GLA_WIDTH, :])
           + d(_bf(os_ref[s, :]), wo_ref[GLA_WIDTH:GLA_WIDTH + SSD_WIDTH, :])
           + d(_bf(or_ref[s, :]), wo_ref[GLA_WIDTH + SSD_WIDTH:, :]) for s in parts]
    x1 = [_layer_norm(alpha * x_ref[s, :] + m, g_ref[...], b_ref[...], LN_EPS) for s, m in zip(parts, mix)]
    for s, v in zip(parts, x1):
        x1_ref[s, :] = v

    wh, wl = _split(rwt_ref[...], 2)
    nt = lambda a, b: lax.dot_general(a, b, (((1,), (1,)), ((), ())), preferred_element_type=f32)
    logits = []
    for v in x1:
        xh, xl = _split(v, 2)
        logits.append(nt(wh, xh) + nt(wl, xh) + nt(wh, xl))
    routed = [_route(lg, rb_ref[...]) for lg in logits]
    eye = (_iota((N_EXPERTS, LANES), 0) == _iota((N_EXPERTS, LANES), 1)).astype(bf16)
    lane = _iota((1, LANES), 1)
    for sb, (s, (gate_t, best)) in enumerate(zip(parts, routed)):
        gate = None
        for p in _split(gate_t, 3):
            t = lax.dot_general(p, eye, (((0,), (0,)), ((), ())), preferred_element_type=f32)
            gate = t if gate is None else gate + t
        gate_ref[s, :] = gate
        best_ref[:, s] = best
        cnt = jnp.zeros((1, LANES), jnp.int32)
        for gi in range(N_EXPERT_GROUPS):
            n_g = jnp.sum((best == gi).astype(jnp.int32), axis=-1, keepdims=True)
            cnt = jnp.where(lane == gi, n_g, cnt)
        cnt_ref[sb] = cnt


def _out_proj(alpha, x2, og, os_, or_, wo, g, b, rwt, rb):
    n, d = x2.shape
    tm = min(OUT_PROJ_ROWS, n)
    ts = min(MOE_SORT_ROWS, tm)
    full2 = lambda a: pl.BlockSpec(a.shape, lambda i: (0, 0))
    rows = lambda c: pl.BlockSpec((tm, c), lambda i: (i, 0))
    return pl.pallas_call(
        functools.partial(_outproj_kernel, alpha),
        grid=(n // tm,),
        in_specs=[rows(d), rows(og.shape[1]), rows(os_.shape[1]), rows(or_.shape[1]),
                  full2(wo), full2(g), full2(b), full2(rwt), full2(rb)],
        out_specs=[rows(d), rows(LANES), pl.BlockSpec((1, tm), lambda i: (0, i)),
                   pl.BlockSpec((tm // ts, 1, LANES), lambda i: (i, 0, 0))],
        out_shape=[jax.ShapeDtypeStruct((n, d), f32), jax.ShapeDtypeStruct((n, LANES), f32),
                   jax.ShapeDtypeStruct((1, n), jnp.int32), jax.ShapeDtypeStruct((n // ts, 1, LANES), jnp.int32)],
        compiler_params=_cparams(("arbitrary",)),
        name="out_proj_router",
    )(x2, og, os_, or_, wo, g, b, rwt, rb)


def _seg_copy(src, dst, sem, src_off, dst_off, rows, max_rows, wait):
    k = SEG_ALIGN
    sizes = []
    while k <= max_rows:
        sizes.append(k)
        k *= 2
    for k in reversed(sizes):
        shift = int(math.log2(k)) + 1

        @pl.when((rows & k) != 0)
        def _():
            done = (rows >> shift) << shift
            cp = pltpu.make_async_copy(
                src.at[pl.ds(pl.multiple_of(src_off + done, SEG_ALIGN), k)],
                dst.at[pl.ds(pl.multiple_of(dst_off + done, SEG_ALIGN), k)], sem)
            if wait:
                cp.wait()
            else:
                cp.start()


def _padded(count):
    return (count + (SEG_ALIGN - 1)) & (-SEG_ALIGN)


def _one_hot_rows(pos, n_rows):
    return jnp.where(_iota((n_rows, pos.shape[1]), 0) == pos, 1.0, 0.0).astype(bf16)


def _dispatch_kernel(cnt_sm, goff_sm, fill_sm, x_ref, gate_ref, best_ref, xs_out, gs_out, pos_ref,
                     xloc, gloc, zx, zg, sems):
    i = pl.program_id(0)
    tm = x_ref.shape[0]
    n_loc = xloc.shape[1]
    slot = i % 2

    def copies(blk, slot_, wait):
        loff = 0
        for gi in range(N_EXPERT_GROUPS):
            rows = _padded(cnt_sm[blk * N_EXPERT_GROUPS + gi])
            dst = goff_sm[blk * N_EXPERT_GROUPS + gi]
            _seg_copy(xloc.at[slot_], xs_out, sems.at[0, slot_], loff, dst, rows, tm, wait)
            _seg_copy(gloc.at[slot_], gs_out, sems.at[1, slot_], loff, dst, rows, tm, wait)
            loff = loff + rows

    best = best_ref[...]
    onehot = (_iota((SUBLANES, tm), 0) == best).astype(bf16)
    earlier = (_iota((tm, tm), 0) < _iota((tm, tm), 1)).astype(bf16)
    rank = jnp.dot(onehot, earlier, preferred_element_type=f32)
    pos = jnp.zeros((1, tm), f32)
    loff = 0
    for gi in range(N_EXPERT_GROUPS):
        pos = jnp.where(best == gi, rank[gi:gi + 1, :] + loff.astype(f32) if gi else rank[gi:gi + 1, :], pos)
        loff = loff + _padded(cnt_sm[i * N_EXPERT_GROUPS + gi])
    pos = pos.astype(jnp.int32)
    pos_ref[...] = pos
    perm = _one_hot_rows(pos, n_loc)
    xloc[slot] = jnp.dot(perm, _bf(x_ref[...]), preferred_element_type=f32)
    gloc[slot] = _dot_sel_lhs(perm, gate_ref[...], 2)
    copies(i, slot, False)

    @pl.when(i > 0)
    def _():
        copies(i - 1, 1 - slot, True)

    @pl.when(i == pl.num_programs(0) - 1)
    def _():
        copies(i, slot, True)
        zx[...] = jnp.zeros_like(zx)
        zg[...] = jnp.zeros_like(zg)
        tile = zx.shape[0]

        def fills(wait):
            for src, dst, sem in ((zx, xs_out, sems.at[2, 0]), (zg, gs_out, sems.at[2, 1])):
                for gi in range(N_EXPERT_GROUPS):
                    _seg_copy(src, dst, sem, 0, fill_sm[gi], fill_sm[N_EXPERT_GROUPS + gi], tile // 2, wait)

                def body(j, carry):
                    off = pl.multiple_of(fill_sm[2 * N_EXPERT_GROUPS] + j * tile, SEG_ALIGN)
                    cp = pltpu.make_async_copy(src, dst.at[pl.ds(off, tile)], sem)
                    if wait:
                        cp.wait()
                    else:
                        cp.start()
                    return carry

                lax.fori_loop(0, fill_sm[2 * N_EXPERT_GROUPS + 1], body, 0)

        fills(False)
        fills(True)


def _experts_kernel(tg_sm, tv_sm, x_ref, gate_ref, wg_ref, wu_ref, wd_ref, y_ref):
    t = pl.program_id(0)

    @pl.when(tv_sm[t] == 0)
    def _():
        y_ref[...] = jnp.zeros_like(y_ref)

    @pl.when(tv_sm[t] > 0)
    def _():
        first = tg_sm[t] * EXPERTS_PER_GROUP
        xb = _bf(x_ref[...])
        gate = gate_ref[...]
        lane = _iota(gate.shape, 1)
        hs = []
        for j in range(EXPERTS_PER_GROUP):
            hg = jnp.dot(xb, wg_ref[j], preferred_element_type=f32)
            hu = jnp.dot(xb, wu_ref[j], preferred_element_type=f32)
            gcol = jnp.sum(jnp.where(lane == first + j, gate, 0.0), axis=-1, keepdims=True)
            hs.append(_bf(_silu(hg) * hu * gcol))
        ff = wd_ref.shape[1]
        y_ref[...] = jnp.dot(jnp.concatenate(hs, axis=1), wd_ref[...].reshape(EXPERTS_PER_GROUP * ff, -1),
                             preferred_element_type=f32)


def _combine_kernel(alpha, cnt_sm, goff_sm, x_ref, pos_ref, g_ref, b_ref, ys_ref, o_ref, yloc, sems):
    i = pl.program_id(0)
    tm = x_ref.shape[0]
    n_loc = yloc.shape[1]
    slot = i % 2

    def copies(blk, slot_, wait):
        loff = 0
        for gi in range(N_EXPERT_GROUPS):
            rows = _padded(cnt_sm[blk * N_EXPERT_GROUPS + gi])
            src = goff_sm[blk * N_EXPERT_GROUPS + gi]
            _seg_copy(ys_ref, yloc.at[slot_], sems.at[slot_], src, loff, rows, tm, wait)
            loff = loff + rows

    @pl.when(i == 0)
    def _():
        yloc[...] = jnp.zeros_like(yloc)
        copies(0, 0, False)

    @pl.when(i + 1 < pl.num_programs(0))
    def _():
        copies(i + 1, 1 - slot, False)

    copies(i, slot, True)
    perm = _one_hot_rows(pos_ref[...], n_loc)
    y = lax.dot_general(perm, _bf(yloc[slot]), (((0,), (0,)), ((), ())), preferred_element_type=f32)
    o_ref[...] = _layer_norm(alpha * x_ref[...] + y, g_ref[...], b_ref[...], LN_EPS)


def _moe_tables(cnt, n_tiles, tile):
    pc = (cnt + (SEG_ALIGN - 1)) // SEG_ALIGN * SEG_ALIGN
    gtot = jnp.sum(pc, axis=0)
    gcap = (gtot + tile - 1) // tile * tile
    gend = jnp.cumsum(gcap)
    gstart = gend - gcap
    goff = gstart[None, :] + jnp.cumsum(pc, axis=0) - pc
    tstart = jnp.arange(n_tiles, dtype=jnp.int32) * tile
    tg = jnp.minimum(jnp.sum((tstart[:, None] >= gend[None, :]).astype(jnp.int32), axis=1), N_EXPERT_GROUPS - 1)
    tv = jnp.clip(gtot[tg] - (tstart - gstart[tg]), 0, tile)
    fill = jnp.concatenate([gstart + gtot, gcap - gtot, gend[-1:], n_tiles - gend[-1:] // tile])
    return goff.reshape(-1).astype(jnp.int32), tg.astype(jnp.int32), tv.astype(jnp.int32), fill.astype(jnp.int32)


def _moe(alpha, x1, gate, best, cnt, layer, wg, wu, wd, g, b):
    n, d = x1.shape
    ff = wg.shape[3]
    nblk = cnt.shape[0]
    tm = n // nblk
    tile = MOE_TILE
    n_loc = tm + LANES
    n_sorted = -(-(n + nblk * N_EXPERT_GROUPS * SEG_ALIGN + N_EXPERT_GROUPS * tile) // tile) * tile
    n_tiles = n_sorted // tile
    cnt_flat = cnt[:, 0, :N_EXPERT_GROUPS].reshape(-1)
    goff, tg, tv, fill = _moe_tables(cnt[:, 0, :N_EXPERT_GROUPS], n_tiles, tile)
    any_spec = pl.BlockSpec(memory_space=pl.ANY)

    xs, gs, pos = pl.pallas_call(
        _dispatch_kernel,
        grid_spec=pltpu.PrefetchScalarGridSpec(
            num_scalar_prefetch=3,
            grid=(nblk,),
            in_specs=[
                pl.BlockSpec((tm, d), lambda i, *_: (i, 0)),
                pl.BlockSpec((tm, LANES), lambda i, *_: (i, 0)),
                pl.BlockSpec((1, tm), lambda i, *_: (0, i)),
            ],
            out_specs=[any_spec, any_spec, pl.BlockSpec((1, tm), lambda i, *_: (0, i))],
            scratch_shapes=[pltpu.VMEM((2, n_loc, d), f32), pltpu.VMEM((2, n_loc, LANES), f32),
                            pltpu.VMEM((tile, d), f32), pltpu.VMEM((tile, LANES), f32),
                            pltpu.SemaphoreType.DMA((3, 2))],
        ),
        out_shape=[jax.ShapeDtypeStruct((n_sorted, d), f32), jax.ShapeDtypeStruct((n_sorted, LANES), f32),
                   jax.ShapeDtypeStruct((1, n), jnp.int32)],
        compiler_params=_cparams(("arbitrary",)),
        name="moe_dispatch",
    )(cnt_flat, goff, fill, x1, gate, best)

    ys = pl.pallas_call(
        _experts_kernel,
        grid_spec=pltpu.PrefetchScalarGridSpec(
            num_scalar_prefetch=2,
            grid=(n_tiles,),
            in_specs=[
                pl.BlockSpec((tile, d), lambda t, *_: (t, 0)),
                pl.BlockSpec((tile, LANES), lambda t, *_: (t, 0)),
                pl.BlockSpec((None, EXPERTS_PER_GROUP, d, ff), lambda t, tg_, tv_: (layer, tg_[t], 0, 0)),
                pl.BlockSpec((None, EXPERTS_PER_GROUP, d, ff), lambda t, tg_, tv_: (layer, tg_[t], 0, 0)),
                pl.BlockSpec((None, EXPERTS_PER_GROUP, ff, d), lambda t, tg_, tv_: (layer, tg_[t], 0, 0)),
            ],
            out_specs=pl.BlockSpec((tile, d), lambda t, *_: (t, 0)),
        ),
        out_shape=jax.ShapeDtypeStruct((n_sorted, d), f32),
        compiler_params=_cparams(("arbitrary",)),
        name="moe_experts",
    )(tg, tv, xs, gs, wg, wu, wd)

    return pl.pallas_call(
        functools.partial(_combine_kernel, alpha),
        grid_spec=pltpu.PrefetchScalarGridSpec(
            num_scalar_prefetch=2,
            grid=(nblk,),
            in_specs=[
                pl.BlockSpec((tm, d), lambda i, *_: (i, 0)),
                pl.BlockSpec((1, tm), lambda i, *_: (0, i)),
                pl.BlockSpec(g.shape, lambda i, *_: (0, 0)),
                pl.BlockSpec(b.shape, lambda i, *_: (0, 0)),
                any_spec,
            ],
            out_specs=pl.BlockSpec((tm, d), lambda i, *_: (i, 0)),
            scratch_shapes=[pltpu.VMEM((2, n_loc, d), f32), pltpu.SemaphoreType.DMA((2,))],
        ),
        out_shape=jax.ShapeDtypeStruct((n, d), f32),
        compiler_params=_cparams(("arbitrary",)),
        name="moe_combine",
    )(cnt_flat, goff, x1, pos, g, b, ys)


def _pad_cols(w, n):
    return jnp.pad(w, ((0, 0), (0, n - w.shape[1])))


def _row(v, n=None):
    v = v.reshape(1, -1)
    return v if n is None else _pad_cols(v, n)


def kernel(x, w_in, gla_w_alpha, gla_b_alpha, gla_norm_g, ssd_conv_w, ssd_conv_b, ssd_dt_bias, ssd_A_log, ssd_D, ssd_norm_g, rwkv_mu, rwkv_w0, rwkv_w2, rwkv_a0, rwkv_a2, rwkv_g2, rwkv_k_k, rwkv_k_a, rwkv_r_k, rwkv_ln_g, rwkv_ln_b, w_out, ln1_g, ln1_b, router_w, router_b, exp_w_gate, exp_w_up, exp_w_down, ln2_g, ln2_b):
    nb, t, d = x.shape
    depth = w_in.shape[0]
    alpha = float((2 * depth) ** 0.25)
    gla_cols = 4 * GLA_WIDTH + GLA_GATE_RANK
    ssd_cols = SSD_WIDTH + SSD_XBC + SSD_HEADS
    rwt = router_w.T
    rb = router_b.reshape(-1, 1)
    wg_all, wu_all, wd_all = _bf(exp_w_gate), _bf(exp_w_up), _bf(exp_w_down)
    w_in_t = jnp.swapaxes(w_in, 1, 2)
    x2 = x.reshape(nb * t, d)
    for l in range(depth):
        wg, ws, wr = _regroup_w_in(w_in_t, l, (gla_cols, ssd_cols, RWKV_COLS), (GLA_COLS_PAD, SSD_COLS_PAD, RWKV_COLS))
        pg, pr, os_ = _in_proj_ssd(x2.reshape(nb, t, d), wg, ws, wr, ssd_conv_w[l], _row(ssd_conv_b[l]),
                                   _row(ssd_dt_bias[l], LANES), _row(ssd_A_log[l], LANES),
                                   _row(jnp.repeat(ssd_D[l], HEAD_DIM)), _row(ssd_norm_g[l]))
        gla_params = (jnp.pad(gla_w_alpha[l], ((0, LANES - GLA_GATE_RANK), (0, 0))), _row(gla_b_alpha[l]),
                      _row(jnp.tile(gla_norm_g[l], GLA_WIDTH // HEAD_DIM)))
        rwkv_params = (_row(rwkv_mu[l]), _row(rwkv_w0[l]), rwkv_w2[l], _row(rwkv_a0[l]), rwkv_a2[l], rwkv_g2[l],
                       _row(rwkv_k_k[l]), _row(rwkv_k_a[l]), _row(rwkv_r_k[l]), _row(rwkv_ln_g[l]),
                       _row(rwkv_ln_b[l]))
        og, or_ = _gla_rwkv(pg, gla_params, pr, rwkv_params)
        x1, gate, best, cnt = _out_proj(alpha, x2, og.reshape(nb * t, -1), os_.reshape(nb * t, -1),
                                        or_.reshape(nb * t, -1), _bf(w_out[l]), _row(ln1_g[l]), _row(ln1_b[l]),
                                        rwt, rb)
        x2 = _moe(alpha, x1, gate, best, cnt, l, wg_all, wu_all, wd_all, _row(ln2_g[l]), _row(ln2_b[l]))
    return x2.reshape(nb, t, d)
```

```python
import functools
import math

import jax
import jax.numpy as jnp
from jax import lax
from jax.experimental import pallas as pl
from jax.experimental.pallas import tpu as pltpu

f32 = jnp.float32
bf16 = jnp.bfloat16

LANES = 128
SUBLANES = 8
HEAD_DIM = 64
GLA_WIDTH = 256
GLA_GATE_RANK = 16
GLA_GATE_TAU = 16.0
GLA_CHUNK = 16
GLA_BLOCK = 128
GLA_COLS_PAD = 4 * GLA_WIDTH + LANES
SSD_WIDTH = 512
SSD_HEADS = 8
SSD_GROUPS = 2
SSD_STATE = 64
SSD_CONV = 4
SSD_XBC = SSD_WIDTH + 2 * SSD_GROUPS * SSD_STATE
SSD_BLOCK = 128
SSD_COLS_PAD = SSD_WIDTH + SSD_XBC + LANES
RWKV_WIDTH = 256
RWKV_CHUNK = 64
RWKV_GN_EPS = 64e-5
RWKV_COLS = 3 * RWKV_WIDTH + 64 + 64 + 128
N_EXPERTS = 16
N_EXPERT_GROUPS = 4
EXPERTS_PER_GROUP = 4
LN_EPS = 1e-5

PROJ_CHUNK = 256
PROJ_PACE = 4
OUT_PROJ_ROWS = 1024
MOE_SORT_ROWS = 512
MOE_TILE = 512
SEG_ALIGN = 8
VMEM_LIMIT = 56 * 1024 * 1024


def _bf(x):
    return x.astype(bf16)


def _dot(a, b):
    return jnp.dot(_bf(a), _bf(b), preferred_element_type=f32)


def _dot_nt(a, b):
    return lax.dot_general(_bf(a), _bf(b), (((1,), (1,)), ((), ())), preferred_element_type=f32)


def _dot_tn(a, b):
    return lax.dot_general(_bf(a), _bf(b), (((0,), (0,)), ((), ())), preferred_element_type=f32)


def _split(x, n):
    parts, r = [], x
    for _ in range(n):
        p = r.astype(bf16)
        parts.append(p)
        r = r - p.astype(f32)
    return parts


def _dot_sel_rhs(a, sel, n=3):
    out = None
    for p in _split(a, n):
        t = jnp.dot(p, sel, preferred_element_type=f32)
        out = t if out is None else out + t
    return out


def _dot_sel_lhs(sel, b, n=3):
    out = None
    for p in _split(b, n):
        t = jnp.dot(sel, p, preferred_element_type=f32)
        out = t if out is None else out + t
    return out


def _dot_hi(a, b):
    ah, al = _split(a, 2)
    bh, bl = _split(b, 2)
    d = functools.partial(jnp.dot, preferred_element_type=f32)
    return d(ah, bh) + d(al, bh) + d(ah, bl)


def _iota(shape, dim):
    return lax.broadcasted_iota(jnp.int32, shape, dim)


def _block_ones(n, blk):
    return (_iota((n, n), 0) // blk == _iota((n, n), 1) // blk)


def _silu(x):
    return x * jax.nn.sigmoid(x)


def _softplus(x):
    return jnp.maximum(x, 0.0) + jnp.log1p(jnp.exp(-jnp.abs(x)))


def _layer_norm(y, g, b, eps):
    mu = jnp.mean(y, axis=-1, keepdims=True)
    yc = y - mu
    var = jnp.mean(yc * yc, axis=-1, keepdims=True)
    return yc * lax.rsqrt(var + eps) * g + b


def _cparams(sem):
    return pltpu.CompilerParams(dimension_semantics=sem, vmem_limit_bytes=VMEM_LIMIT)


def _regroup_kernel(splits, w_ref, *o_refs):
    start = 0
    for width, o_ref in zip(splits, o_refs):
        part = w_ref[0, start:start + width, :]
        pad = o_ref.shape[1] - width
        if pad:
            part = jnp.concatenate([part, jnp.zeros((pad, part.shape[1]), part.dtype)], axis=0)
        o_ref[...] = _bf(part.T)
        start += width


def _regroup_w_in(w_in_t, layer, splits, padded):
    _, n_in, d = w_in_t.shape
    return pl.pallas_call(
        functools.partial(_regroup_kernel, splits),
        grid=(1,),
        in_specs=[pl.BlockSpec((1, n_in, d), lambda i: (layer, 0, 0))],
        out_specs=[pl.BlockSpec((d, p), lambda i: (0, 0)) for p in padded],
        out_shape=[jax.ShapeDtypeStruct((d, p), bf16) for p in padded],
        compiler_params=_cparams(("arbitrary",)),
        name="regroup_w_in",
    )(w_in_t)


def _gla_steps(p_ref, wa_ref, ba_ref, ng_ref, o_ref, st_ref, t_ref, d_ref, sx_ref):
    nb, tb, _ = p_ref.shape
    w, c = GLA_WIDTH, GLA_CHUNK
    nc = tb // c
    npt = w // LANES
    Q, K, V, CUM, OUT = range(5)

    @pl.when(pl.program_id(0) == 0)
    def _():
        st_ref[...] = jnp.zeros_like(st_ref)

    bones = _block_ones(w, HEAD_DIM).astype(bf16)
    pair_blk = _block_ones(LANES, HEAD_DIM)
    r_i, c_i = _iota((tb, tb), 0), _iota((tb, tb), 1)
    tri = ((r_i >= c_i) & (r_i // c == c_i // c)).astype(bf16)
    wa_h, wa_l = _split(wa_ref[...], 2)
    dd = functools.partial(jnp.dot, preferred_element_type=f32)

    for b in range(nb):
        a_h, a_l = _split(p_ref[b, :, 4 * w:4 * w + LANES], 2)
        z = dd(a_h, wa_h) + dd(a_l, wa_h) + dd(a_h, wa_l) + ba_ref[...]
        log_a = -_softplus(-z) * (1.0 / GLA_GATE_TAU)
        cum = _dot_sel_lhs(tri, log_a, 3)
        for lt in range(npt):
            lanes = slice(lt * LANES, (lt + 1) * LANES)
            t_ref[b, Q, lt] = p_ref[b, :, lanes] * (HEAD_DIM ** -0.5)
            t_ref[b, K, lt] = p_ref[b, :, w + lt * LANES:w + (lt + 1) * LANES]
            t_ref[b, V, lt] = p_ref[b, :, 2 * w + lt * LANES:2 * w + (lt + 1) * LANES]
            t_ref[b, CUM, lt] = cum[:, lanes]
        yield

    def ld(b, j, off):
        return jnp.concatenate([t_ref[b, j, lt, pl.ds(off, nc, stride=c), :] for lt in range(npt)], axis=1)

    for b in range(nb):
        qt = [ld(b, Q, o) for o in range(c)]
        kt = [ld(b, K, o) for o in range(c)]
        ct = [ld(b, CUM, o) for o in range(c)]
        pair = 0
        for t in range(c):
            for s in range(t + 1):
                qk = qt[t] * kt[s]
                d_ref[b, pair * nc:(pair + 1) * nc, :] = qk if s == t else qk * jnp.exp(ct[t] - ct[s])
                pair += 1
            if t % 4 == 3:
                yield
    for b in range(nb):
        sx_ref[b] = jnp.dot(_bf(d_ref[b]), bones, preferred_element_type=f32)
        yield
    for b in range(nb):
        vt = [ld(b, V, o) for o in range(c)]
        pair = 0
        for t in range(c):
            acc = None
            for s in range(t + 1):
                term = sx_ref[b, pair * nc:(pair + 1) * nc, :] * vt[s]
                acc = term if acc is None else acc + term
                pair += 1
            for lt in range(npt):
                t_ref[b, OUT, lt, pl.ds(t, nc, stride=c), :] = acc[:, lt * LANES:(lt + 1) * LANES]
            if t % 8 == 7:
                yield

    for ci in range(nc):
        sl = slice(ci * c, (ci + 1) * c)
        last = slice((ci + 1) * c - 1, (ci + 1) * c)
        for b in range(nb):
            for lt in range(npt):
                cum_c = t_ref[b, CUM, lt, sl, :]
                cum_last = t_ref[b, CUM, lt, last, :]
                st = st_ref[b, lt]
                t_ref[b, OUT, lt, sl, :] += _dot_nt(t_ref[b, Q, lt, sl, :] * jnp.exp(cum_c), st)
                upd = _dot_tn(t_ref[b, V, lt, sl, :], t_ref[b, K, lt, sl, :] * jnp.exp(cum_last - cum_c))
                st_ref[b, lt] = st * jnp.exp(cum_last) + jnp.where(pair_blk, upd, 0.0)
        yield

    o = jnp.concatenate(
        [jnp.concatenate([t_ref[b, OUT, lt] for lt in range(npt)], axis=1) for b in range(nb)], axis=0)
    ms = _dot_sel_rhs(o * o, bones, 2) * (1.0 / HEAD_DIM)
    o = o * lax.rsqrt(ms + LN_EPS) * ng_ref[...]
    for b in range(nb):
        o_ref[b] = o[b * tb:(b + 1) * tb] * _silu(p_ref[b, :, 3 * w:4 * w])


def _round_robin(gens):
    live = list(gens)
    while live:
        for gen in list(live):
            if next(gen, StopIteration) is StopIteration:
                live.remove(gen)
            else:
                yield


def _ssd_steps(b, p_ref, halo_ref, cw_ref, cb_ref, dtb_ref, alog_ref, dsk_ref, ng_ref, o_ref, st_ref):
    L = p_ref.shape[1]
    W, M, G = SSD_WIDTH, SSD_STATE, SSD_GROUPS
    gw = W // G

    @pl.when(pl.program_id(0) == 0)
    def _():
        st_ref[b] = jnp.zeros(st_ref.shape[1:], f32)
        halo_ref[b] = jnp.zeros(halo_ref.shape[1:], f32)

    z = p_ref[b, :, 0:W]
    xbc = p_ref[b, :, W:W + SSD_XBC]
    dt_raw = p_ref[b, :, W + SSD_XBC:W + SSD_XBC + LANES]
    xx = jnp.concatenate([halo_ref[b], xbc], axis=0)
    halo_ref[b] = xbc[L - SUBLANES:L]
    conv = cb_ref[...]
    for j in range(SSD_CONV):
        off = SUBLANES - (SSD_CONV - 1) + j
        conv = conv + cw_ref[j:j + 1, :] * xx[off:off + L]
    act = _silu(conv)
    xs = act[:, 0:W]
    bm = act[:, W:W + G * M]
    cm = act[:, W + G * M:W + 2 * G * M]
    yield

    dt = _softplus(dt_raw + dtb_ref[...])
    a_neg = jnp.where(_iota((1, LANES), 1) < SSD_HEADS, -jnp.exp(alog_ref[...]), 0.0)
    da = dt * a_neg
    r_i, c_i = _iota((L, L), 0), _iota((L, L), 1)
    causal = r_i >= c_i
    cs = _dot_sel_lhs(causal.astype(bf16), da, 3)
    expand = (_iota((LANES, W), 0) == _iota((LANES, W), 1) // HEAD_DIM).astype(bf16)
    dtx = _dot_sel_rhs(dt, expand, 2)
    csx = _dot_sel_rhs(cs, expand, 2)
    eye_h = (_iota((SSD_HEADS, LANES), 0) == _iota((SSD_HEADS, LANES), 1)).astype(bf16)
    cs_t = None
    for p in _split(cs, 2):
        t = lax.dot_general(eye_h, p, (((1,), (1,)), ((), ())), preferred_element_type=f32)
        cs_t = t if cs_t is None else cs_t + t
    xdt = xs * dtx
    yield

    st = st_ref[b]
    y_parts = []
    for gi in range(G):
        bm_g = bm[:, gi * M:(gi + 1) * M]
        cm_g = cm[:, gi * M:(gi + 1) * M]
        cb = _dot_nt(cm_g, bm_g)
        for e in range(SSD_HEADS // G):
            h = gi * (SSD_HEADS // G) + e
            seg = jnp.where(causal, jnp.exp(jnp.broadcast_to(cs[:, h:h + 1], (L, L)) - cs_t[h:h + 1, :]), 0.0)
            y_parts.append(_dot(cb * seg, xdt[:, h * HEAD_DIM:(h + 1) * HEAD_DIM]))
            if e % 2:
                yield
    y = jnp.concatenate(y_parts, axis=1)
    y_inter = jnp.concatenate(
        [_dot(cm[:, gi * M:(gi + 1) * M], st[:, gi * gw:(gi + 1) * gw]) for gi in range(G)], axis=1)
    y = y + y_inter * jnp.exp(csx)
    yield

    cs_last = csx[L - 1:L, :]
    xdtw = xdt * jnp.exp(cs_last - csx)
    d_state = jnp.concatenate(
        [_dot_tn(bm[:, gi * M:(gi + 1) * M], xdtw[:, gi * gw:(gi + 1) * gw]) for gi in range(G)], axis=1)
    st_ref[b] = st * jnp.exp(cs_last) + d_state
    yield

    y = (y + dsk_ref[...] * xs) * _silu(z)
    outs = []
    for gi in range(G):
        yg = y[:, gi * gw:(gi + 1) * gw]
        ms = jnp.mean(yg * yg, axis=-1, keepdims=True)
        outs.append(yg * lax.rsqrt(ms + LN_EPS))
    o_ref[b] = jnp.concatenate(outs, axis=1) * ng_ref[...]


def _inproj_ssd_kernel(x_ref, wg_ref, ws_ref, wr_ref, cw_ref, cb_ref, dtb_ref, alog_ref, dsk_ref, ng_ref,
                       pg_ref, pr_ref, os_ref, ps_ref, halo_ref, st_ref):
    nb, L, d = x_ref.shape
    xb = _bf(x_ref[...].reshape(nb * L, d))
    ps_ref[...] = jnp.dot(xb, ws_ref[...], preferred_element_type=f32).reshape(ps_ref.shape)

    def other_columns():
        for w_ref, o_ref in ((wg_ref, pg_ref), (wr_ref, pr_ref)):
            n = w_ref.shape[1]
            for c0 in range(0, n, PROJ_CHUNK):
                c1 = min(c0 + PROJ_CHUNK, n)
                o_ref[:, :, c0:c1] = jnp.dot(xb, w_ref[:, c0:c1], preferred_element_type=f32).reshape(nb, L, c1 - c0)
                for _ in range(PROJ_PACE):
                    yield

    ssd = _round_robin([_ssd_steps(b, ps_ref, halo_ref, cw_ref, cb_ref, dtb_ref, alog_ref, dsk_ref,
                                   ng_ref, os_ref, st_ref) for b in range(nb)])
    for _ in _round_robin([ssd, other_columns()]):
        pass


def _in_proj_ssd(x3, wg, ws, wr, cw, cb, dtb, alog, dsk, ng):
    nb, t, d = x3.shape
    L = SSD_BLOCK
    full2 = lambda a: pl.BlockSpec(a.shape, lambda i: (0, 0))
    blk = lambda c: pl.BlockSpec((nb, L, c), lambda i: (0, i, 0))
    return pl.pallas_call(
        _inproj_ssd_kernel,
        grid=(t // L,),
        in_specs=[blk(d)] + [full2(a) for a in (wg, ws, wr, cw, cb, dtb, alog, dsk, ng)],
        out_specs=[blk(wg.shape[1]), blk(wr.shape[1]), blk(SSD_WIDTH)],
        out_shape=[jax.ShapeDtypeStruct((nb, t, wg.shape[1]), f32), jax.ShapeDtypeStruct((nb, t, wr.shape[1]), f32),
                   jax.ShapeDtypeStruct((nb, t, SSD_WIDTH), f32)],
        scratch_shapes=[pltpu.VMEM((nb, L, ws.shape[1]), f32), pltpu.VMEM((nb, SUBLANES, SSD_XBC), f32),
                        pltpu.VMEM((nb, SSD_STATE, SSD_WIDTH), f32)],
        compiler_params=_cparams(("arbitrary",)),
        name="in_proj_ssd",
    )(x3, wg, ws, wr, cw, cb, dtb, alog, dsk, ng)


def _rwkv_steps(sub, p_ref, prev_ref, mu_ref, w0_ref, w2_ref, a0_ref, a2_ref, g2_ref, kk_ref, ka_ref, rk_ref,
                lng_ref, lnb_ref, o_ref, zt_ref):
    nb = p_ref.shape[0]
    C, W = RWKV_CHUNK, RWKV_WIDTH
    H = LANES // HEAD_DIM
    npair = W // LANES
    i = pl.program_id(0)
    r0 = sub * C

    if sub == 0:
        @pl.when(i == 0)
        def _():
            zt_ref[...] = jnp.zeros_like(zt_ref)

    bones = _block_ones(W, HEAD_DIM).astype(bf16)
    pair_blk = _block_ones(LANES, HEAD_DIM)
    n = H * C
    rr, cc = _iota((n, n), 0), _iota((n, n), 1)
    same = rr // C == cc // C
    strict = same & (rr > cc)
    lower = same & (rr >= cc)
    eye = jnp.where(rr == cc, 1.0, 0.0)

    def stack(t):
        return jnp.where(pair_blk, jnp.concatenate([t] * H, axis=0), 0.0)

    rows = range(nb)
    nr = nb * C
    cols = p_ref[:, r0:r0 + C, :].reshape(nr, p_ref.shape[2])

    def row_before(b):
        return p_ref[b, r0 - 1:r0, :] if sub else jnp.where(i > 0, prev_ref[b, SUBLANES - 1:SUBLANES, :], 0.0)

    shifted = jnp.concatenate(
        [piece for b in rows for piece in (row_before(b), p_ref[b, r0:r0 + C - 1, :])], axis=0)
    cols = cols + (shifted - cols) * mu_ref[...]
    r = cols[:, 0:W]
    k = cols[:, W:2 * W]
    v = cols[:, 2 * W:3 * W]
    w_lr = cols[:, 3 * W:3 * W + 64]
    a_lr = cols[:, 3 * W + 64:3 * W + 128]
    g_lr = cols[:, 3 * W + 128:3 * W + 256]

    wv = -_softplus(-(w0_ref[...] + _dot_hi(jnp.tanh(w_lr), w2_ref[...]))) - 0.5
    lw = -jnp.exp(wv)
    a = jax.nn.sigmoid(a0_ref[...] + _dot_hi(a_lr, a2_ref[...]))
    g = _dot_hi(jax.nn.sigmoid(g_lr), g2_ref[...])
    yield

    kk = k * kk_ref[...]
    k2 = k * (1.0 + (a - 1.0) * ka_ref[...])
    head_sums = _dot_sel_rhs(jnp.concatenate([kk * kk, r * k2 * rk_ref[...]], axis=0), bones, 2)
    kk = kk / jnp.maximum(jnp.sqrt(head_sums[:nr]), 1e-12)
    bonus_v = head_sums[nr:] * v
    bvec = kk * a
    yield

    rb_i, cb_i = _iota((nr, nr), 0), _iota((nr, nr), 1)
    tri = ((rb_i >= cb_i) & (rb_i // C == cb_i // C)).astype(bf16)
    cum = _dot_sel_lhs(tri, lw, 3)
    cum_last = [cum[(b + 1) * C - 1:(b + 1) * C, :] for b in rows]
    e_neg = jnp.exp(-cum)
    e_rem = jnp.exp(jnp.concatenate([jnp.broadcast_to(cl, (C, W)) for cl in cum_last], axis=0) - cum)
    per_token = dict(aw=-kk * jnp.exp(cum - lw), rw=r * jnp.exp(cum), bn=bvec * e_neg, kn=k2 * e_neg,
                     bt=bvec * e_rem, kt=k2 * e_rem, vs=v)
    chains = [(b, p) for b in rows for p in range(npair)]
    q = [dict({name: stack(t[b * C:(b + 1) * C, p * LANES:(p + 1) * LANES]) for name, t in per_token.items()},
              wc=jnp.exp(cum_last[b][:, p * LANES:(p + 1) * LANES])) for b, p in chains]
    ids = range(len(chains))
    yield

    a_ab = [jnp.where(strict, _dot_nt(q[c]["aw"], q[c]["bn"]), 0.0) for c in ids]
    a_ak = [jnp.where(strict, _dot_nt(q[c]["aw"], q[c]["kn"]), 0.0) for c in ids]
    yield
    a_rb = [jnp.where(lower, _dot_nt(q[c]["rw"], q[c]["bn"]), 0.0) for c in ids]
    a_rk = [jnp.where(lower, _dot_nt(q[c]["rw"], q[c]["kn"]), 0.0) for c in ids]
    yield

    pw = a_ab
    tinv = [eye + a_ab[c] for c in ids]
    for _ in range(int(math.log2(C)) - 1):
        pw = [_dot(pw[c], pw[c]) for c in ids]
        yield
        tinv = [tinv[c] + _dot(tinv[c], pw[c]) for c in ids]
        yield

    zt = [zt_ref[b, p] for b, p in chains]
    rhs = [_dot_nt(q[c]["aw"], zt[c]) + _dot(a_ak[c], q[c]["vs"]) for c in ids]
    yield
    y0 = [_dot_nt(q[c]["rw"], zt[c]) + _dot(a_rk[c], q[c]["vs"]) for c in ids]
    yield
    u = [_dot(tinv[c], rhs[c]) for c in ids]
    yield
    y_st = [y0[c] + _dot(a_rb[c], u[c]) for c in ids]
    yield
    for c, (b, p) in enumerate(chains):
        zt_ref[b, p] = zt[c] * q[c]["wc"] + _dot_tn(u[c], q[c]["bt"]) + _dot_tn(q[c]["vs"], q[c]["kt"])
    yield

    inv_n = 1.0 / HEAD_DIM
    ys = []
    for b in rows:
        tiles = []
        for p in range(npair):
            y_c = y_st[b * npair + p]
            y = y_c[0:C]
            for h in range(1, H):
                y = y + y_c[h * C:(h + 1) * C]
            tiles.append(y)
        ys.append(jnp.concatenate(tiles, axis=1))
    y = jnp.concatenate(ys, axis=0)
    yc = y - _dot_sel_rhs(y, bones, 2) * inv_n
    var_h = _dot_sel_rhs(yc * yc, bones, 2) * inv_n
    y = yc * lax.rsqrt(var_h + RWKV_GN_EPS) * lng_ref[...] + lnb_ref[...]
    o_ref[:, r0:r0 + C, :] = ((y + bonus_v) * g).reshape(nb, C, W)


N_GLA_IN, N_RWKV_IN = 4, 13


N_CAST = 3


def _gla_rwkv_kernel(*refs):
    gla_in = refs[:N_GLA_IN]
    rwkv_in = refs[N_GLA_IN:N_GLA_IN + N_RWKV_IN]
    k = N_GLA_IN + N_RWKV_IN
    cast_in = refs[k:k + N_CAST]
    og_ref, or_ref = refs[k + N_CAST:k + N_CAST + 2]
    cast_out = refs[k + N_CAST + 2:k + 2 * N_CAST + 2]
    st_ref, t_ref, d_ref, sx_ref, zt_ref = refs[k + 2 * N_CAST + 2:]

    def casts():
        for src, dst in zip(cast_in, cast_out):
            dst[...] = _bf(src[...])
            yield

    n_sub = rwkv_in[0].shape[1] // RWKV_CHUNK
    rwkv = (step for sub in range(n_sub) for step in _rwkv_steps(sub, *rwkv_in, or_ref, zt_ref))
    gla = _gla_steps(*gla_in, og_ref, st_ref, t_ref, d_ref, sx_ref)
    for _ in _round_robin([rwkv, gla, casts()]):
        pass


def _gla_rwkv(pg, gla_params, pr, rwkv_params, layer, expert_weights):
    nb, t, gcols = pg.shape
    rcols = pr.shape[2]
    tb = GLA_BLOCK
    n_steps = t // tb
    npt = GLA_WIDTH // LANES
    pair_rows = (GLA_CHUNK * (GLA_CHUNK + 1) // 2) * (tb // GLA_CHUNK)
    full2 = lambda a: pl.BlockSpec(a.shape, lambda i: (0, 0))
    blk = lambda c: pl.BlockSpec((nb, tb, c), lambda i: (0, i, 0))
    assert len(gla_params) + 1 == N_GLA_IN and len(rwkv_params) + 2 == N_RWKV_IN and len(expert_weights) == N_CAST
    cast_rows = [w.shape[1] // n_steps for w in expert_weights]
    return pl.pallas_call(
        _gla_rwkv_kernel,
        grid=(n_steps,),
        in_specs=[blk(gcols)] + [full2(p) for p in gla_params]
        + [blk(rcols), pl.BlockSpec((nb, SUBLANES, rcols),
                                    lambda i: (0, jnp.maximum(i * (tb // SUBLANES) - 1, 0), 0))]
        + [full2(p) for p in rwkv_params]
        + [pl.BlockSpec((None, r, w.shape[2]), lambda i: (layer, i, 0)) for r, w in zip(cast_rows, expert_weights)],
        out_specs=[blk(GLA_WIDTH), blk(RWKV_WIDTH)]
        + [pl.BlockSpec((r, w.shape[2]), lambda i: (i, 0)) for r, w in zip(cast_rows, expert_weights)],
        out_shape=[jax.ShapeDtypeStruct((nb, t, GLA_WIDTH), f32), jax.ShapeDtypeStruct((nb, t, RWKV_WIDTH), f32)]
        + [jax.ShapeDtypeStruct(w.shape[1:], bf16) for w in expert_weights],
        scratch_shapes=[
            pltpu.VMEM((nb, npt, LANES, LANES), f32),
            pltpu.VMEM((nb, 5, npt, tb, LANES), f32),
            pltpu.VMEM((nb, pair_rows, GLA_WIDTH), f32),
            pltpu.VMEM((nb, pair_rows, GLA_WIDTH), f32),
            pltpu.VMEM((nb, RWKV_WIDTH // LANES, LANES, LANES), f32),
        ],
        compiler_params=_cparams(("arbitrary",)),
        name="gla_rwkv_mixers",
    )(pg, *gla_params, pr, pr, *rwkv_params, *expert_weights)


def _route(logits, rb):
    tm = logits.shape[1]
    scores = jax.nn.sigmoid(logits)
    sel = scores + rb
    srow = [sel[e:e + 1, :] for e in range(N_EXPERTS)]
    crow = [scores[e:e + 1, :] for e in range(N_EXPERTS)]

    def top2_sum(v0, v1, v2, v3):
        m01, n01 = jnp.maximum(v0, v1), jnp.minimum(v0, v1)
        m23, n23 = jnp.maximum(v2, v3), jnp.minimum(v2, v3)
        return jnp.maximum(m01, m23) + jnp.maximum(jnp.minimum(m01, m23), jnp.maximum(n01, n23))

    gscore = [top2_sum(*srow[4 * gi:4 * gi + 4]) for gi in range(N_EXPERT_GROUPS)]
    best = jnp.zeros((1, tm), jnp.int32)
    best_v = gscore[0]
    for gi in range(1, N_EXPERT_GROUPS):
        better = gscore[gi] > best_v
        best = jnp.where(better, gi, best)
        best_v = jnp.where(better, gscore[gi], best_v)
    sv, cv = [], []
    for j in range(EXPERTS_PER_GROUP):
        s_j, c_j = srow[j], crow[j]
        for gi in range(1, N_EXPERT_GROUPS):
            s_j = jnp.where(best == gi, srow[4 * gi + j], s_j)
            c_j = jnp.where(best == gi, crow[4 * gi + j], c_j)
        sv.append(s_j)
        cv.append(c_j)
    j1 = jnp.zeros((1, tm), jnp.int32)
    v1 = sv[0]
    for j in range(1, EXPERTS_PER_GROUP):
        better = sv[j] > v1
        j1 = jnp.where(better, j, j1)
        v1 = jnp.where(better, sv[j], v1)
    j2 = jnp.full((1, tm), -1, jnp.int32)
    v2 = jnp.full((1, tm), -jnp.inf, f32)
    for j in range(EXPERTS_PER_GROUP):
        better = (j1 != j) & ((j2 < 0) | (sv[j] > v2))
        j2 = jnp.where(better, j, j2)
        v2 = jnp.where(better, sv[j], v2)
    w1 = cv[0]
    w2 = cv[0]
    for j in range(1, EXPERTS_PER_GROUP):
        w1 = jnp.where(j1 == j, cv[j], w1)
        w2 = jnp.where(j2 == j, cv[j], w2)
    wsum = w1 + w2
    g1, g2 = w1 / wsum, w2 / wsum
    rows = []
    for e in range(N_EXPERTS):
        gi, j = divmod(e, EXPERTS_PER_GROUP)
        in_g = best == gi
        rows.append(jnp.where(in_g & (j1 == j), g1, jnp.where(in_g & (j2 == j), g2, 0.0)))
    gate_t = jnp.concatenate(rows, axis=0)
    return gate_t, best


def _outproj_kernel(alpha, x_ref, og_ref, os_ref, or_ref, wo_ref, g_ref, b_ref, rwt_ref, rb_ref,
                    x1_ref, gate_ref, best_ref, cnt_ref):
    tm = x_ref.shape[0]
    ts = tm // cnt_ref.shape[0]
    parts = [slice(sb * ts, (sb + 1) * ts) for sb in range(cnt_ref.shape[0])]
    d = functools.partial(jnp.dot, preferred_element_type=f32)
    mix = [d(_bf(og_ref[s, :]), wo_ref[0:GLA_WIDTH, :])
           + d(_bf(os_ref[s, :]), wo_ref[GLA_WIDTH:GLA_WIDTH + SSD_WIDTH, :])
           + d(_bf(or_ref[s, :]), wo_ref[GLA_WIDTH + SSD_WIDTH:, :]) for s in parts]
    x1 = [_layer_norm(alpha * x_ref[s, :] + m, g_ref[...], b_ref[...], LN_EPS) for s, m in zip(parts, mix)]
    for s, v in zip(parts, x1):
        x1_ref[s, :] = v

    wh, wl = _split(rwt_ref[...], 2)
    nt = lambda a, b: lax.dot_general(a, b, (((1,), (1,)), ((), ())), preferred_element_type=f32)
    logits = []
    for v in x1:
        xh, xl = _split(v, 2)
        logits.append(nt(wh, xh) + nt(wl, xh) + nt(wh, xl))
    routed = [_route(lg, rb_ref[...]) for lg in logits]
    eye = (_iota((N_EXPERTS, LANES), 0) == _iota((N_EXPERTS, LANES), 1)).astype(bf16)
    lane = _iota((1, LANES), 1)
    for sb, (s, (gate_t, best)) in enumerate(zip(parts, routed)):
        gate = None
        for p in _split(gate_t, 3):
            t = lax.dot_general(p, eye, (((0,), (0,)), ((), ())), preferred_element_type=f32)
            gate = t if gate is None else gate + t
        gate_ref[s, :] = gate
        best_ref[:, s] = best
        cnt = jnp.zeros((1, LANES), jnp.int32)
        for gi in range(N_EXPERT_GROUPS):
            n_g = jnp.sum((best == gi).astype(jnp.int32), axis=-1, keepdims=True)
            cnt = jnp.where(lane == gi, n_g, cnt)
        cnt_ref[sb] = cnt


def _out_proj(alpha, x2, og, os_, or_, wo, g, b, rwt, rb):
    n, d = x2.shape
    tm = min(OUT_PROJ_ROWS, n)
    ts = min(MOE_SORT_ROWS, tm)
    full2 = lambda a: pl.BlockSpec(a.shape, lambda i: (0, 0))
    rows = lambda c: pl.BlockSpec((tm, c), lambda i: (i, 0))
    return pl.pallas_call(
        functools.partial(_outproj_kernel, alpha),
        grid=(n // tm,),
        in_specs=[rows(d), rows(og.shape[1]), rows(os_.shape[1]), rows(or_.shape[1]),
                  full2(wo), full2(g), full2(b), full2(rwt), full2(rb)],
        out_specs=[rows(d), rows(LANES), pl.BlockSpec((1, tm), lambda i: (0, i)),
                   pl.BlockSpec((tm // ts, 1, LANES), lambda i: (i, 0, 0))],
        out_shape=[jax.ShapeDtypeStruct((n, d), f32), jax.ShapeDtypeStruct((n, LANES), f32),
                   jax.ShapeDtypeStruct((1, n), jnp.int32), jax.ShapeDtypeStruct((n // ts, 1, LANES), jnp.int32)],
        compiler_params=_cparams(("arbitrary",)),
        name="out_proj_router",
    )(x2, og, os_, or_, wo, g, b, rwt, rb)


def _seg_copy(src, dst, sem, src_off, dst_off, rows, max_rows, wait):
    k = SEG_ALIGN
    sizes = []
    while k <= max_rows:
        sizes.append(k)
        k *= 2
    for k in reversed(sizes):
        shift = int(math.log2(k)) + 1

        @pl.when((rows & k) != 0)
        def _():
            done = (rows >> shift) << shift
            cp = pltpu.make_async_copy(
                src.at[pl.ds(pl.multiple_of(src_off + done, SEG_ALIGN), k)],
                dst.at[pl.ds(pl.multiple_of(dst_off + done, SEG_ALIGN), k)], sem)
            if wait:
                cp.wait()
            else:
                cp.start()


def _padded(count):
    return (count + (SEG_ALIGN - 1)) & (-SEG_ALIGN)


def _one_hot_rows(pos, n_rows):
    return jnp.where(_iota((n_rows, pos.shape[1]), 0) == pos, 1.0, 0.0).astype(bf16)


def _dispatch_kernel(cnt_sm, goff_sm, fill_sm, x_ref, gate_ref, best_ref, xs_out, gs_out, pos_ref,
                     xloc, gloc, zx, zg, sems):
    i = pl.program_id(0)
    tm = x_ref.shape[0]
    n_loc = xloc.shape[1]
    slot = i % 2

    def copies(blk, slot_, wait):
        loff = 0
        for gi in range(N_EXPERT_GROUPS):
            rows = _padded(cnt_sm[blk * N_EXPERT_GROUPS + gi])
            dst = goff_sm[blk * N_EXPERT_GROUPS + gi]
            _seg_copy(xloc.at[slot_], xs_out, sems.at[0, slot_], loff, dst, rows, tm, wait)
            _seg_copy(gloc.at[slot_], gs_out, sems.at[1, slot_], loff, dst, rows, tm, wait)
            loff = loff + rows

    best = best_ref[...]
    onehot = (_iota((SUBLANES, tm), 0) == best).astype(bf16)
    earlier = (_iota((tm, tm), 0) < _iota((tm, tm), 1)).astype(bf16)
    rank = jnp.dot(onehot, earlier, preferred_element_type=f32)
    pos = jnp.zeros((1, tm), f32)
    loff = 0
    for gi in range(N_EXPERT_GROUPS):
        pos = jnp.where(best == gi, rank[gi:gi + 1, :] + loff.astype(f32) if gi else rank[gi:gi + 1, :], pos)
        loff = loff + _padded(cnt_sm[i * N_EXPERT_GROUPS + gi])
    pos = pos.astype(jnp.int32)
    pos_ref[...] = pos
    perm = _one_hot_rows(pos, n_loc)
    xloc[slot] = jnp.dot(perm, _bf(x_ref[...]), preferred_element_type=f32)
    gloc[slot] = _dot_sel_lhs(perm, gate_ref[...], 2)
    copies(i, slot, False)

    @pl.when(i > 0)
    def _():
        copies(i - 1, 1 - slot, True)

    @pl.when(i == pl.num_programs(0) - 1)
    def _():
        copies(i, slot, True)
        zx[...] = jnp.zeros_like(zx)
        zg[...] = jnp.zeros_like(zg)
        tile = zx.shape[0]

        def fills(wait):
            for src, dst, sem in ((zx, xs_out, sems.at[2, 0]), (zg, gs_out, sems.at[2, 1])):
                for gi in range(N_EXPERT_GROUPS):
                    _seg_copy(src, dst, sem, 0, fill_sm[gi], fill_sm[N_EXPERT_GROUPS + gi], tile // 2, wait)

                def body(j, carry):
                    off = pl.multiple_of(fill_sm[2 * N_EXPERT_GROUPS] + j * tile, SEG_ALIGN)
                    cp = pltpu.make_async_copy(src, dst.at[pl.ds(off, tile)], sem)
                    if wait:
                        cp.wait()
                    else:
                        cp.start()
                    return carry

                lax.fori_loop(0, fill_sm[2 * N_EXPERT_GROUPS + 1], body, 0)

        fills(False)
        fills(True)


def _experts_kernel(tg_sm, tv_sm, x_ref, gate_ref, wg_ref, wu_ref, wd_ref, y_ref):
    t = pl.program_id(0)

    @pl.when(tv_sm[t] == 0)
    def _():
        y_ref[...] = jnp.zeros_like(y_ref)

    @pl.when(tv_sm[t] > 0)
    def _():
        first = tg_sm[t] * EXPERTS_PER_GROUP
        xb = _bf(x_ref[...])
        gate = gate_ref[...]
        lane = _iota(gate.shape, 1)
        hs = []
        for j in range(EXPERTS_PER_GROUP):
            hg = jnp.dot(xb, wg_ref[j], preferred_element_type=f32)
            hu = jnp.dot(xb, wu_ref[j], preferred_element_type=f32)
            gcol = jnp.sum(jnp.where(lane == first + j, gate, 0.0), axis=-1, keepdims=True)
            hs.append(_bf(_silu(hg) * hu * gcol))
        ff = wd_ref.shape[1]
        y_ref[...] = jnp.dot(jnp.concatenate(hs, axis=1), wd_ref[...].reshape(EXPERTS_PER_GROUP * ff, -1),
                             preferred_element_type=f32)


def _combine_kernel(alpha, cnt_sm, goff_sm, x_ref, pos_ref, g_ref, b_ref, ys_ref, o_ref, yloc, sems):
    i = pl.program_id(0)
    tm = x_ref.shape[0]
    n_loc = yloc.shape[1]
    slot = i % 2

    def copies(blk, slot_, wait):
        loff = 0
        for gi in range(N_EXPERT_GROUPS):
            rows = _padded(cnt_sm[blk * N_EXPERT_GROUPS + gi])
            src = goff_sm[blk * N_EXPERT_GROUPS + gi]
            _seg_copy(ys_ref, yloc.at[slot_], sems.at[slot_], src, loff, rows, tm, wait)
            loff = loff + rows

    @pl.when(i == 0)
    def _():
        yloc[...] = jnp.zeros_like(yloc)
        copies(0, 0, False)

    @pl.when(i + 1 < pl.num_programs(0))
    def _():
        copies(i + 1, 1 - slot, False)

    copies(i, slot, True)
    perm = _one_hot_rows(pos_ref[...], n_loc)
    y = lax.dot_general(perm, _bf(yloc[slot]), (((0,), (0,)), ((), ())), preferred_element_type=f32)
    o_ref[...] = _layer_norm(alpha * x_ref[...] + y, g_ref[...], b_ref[...], LN_EPS)


def _moe_tables(cnt, n_tiles, tile):
    pc = (cnt + (SEG_ALIGN - 1)) // SEG_ALIGN * SEG_ALIGN
    gtot = jnp.sum(pc, axis=0)
    gcap = (gtot + tile - 1) // tile * tile
    gend = jnp.cumsum(gcap)
    gstart = gend - gcap
    goff = gstart[None, :] + jnp.cumsum(pc, axis=0) - pc
    tstart = jnp.arange(n_tiles, dtype=jnp.int32) * tile
    tg = jnp.minimum(jnp.sum((tstart[:, None] >= gend[None, :]).astype(jnp.int32), axis=1), N_EXPERT_GROUPS - 1)
    tv = jnp.clip(gtot[tg] - (tstart - gstart[tg]), 0, tile)
    fill = jnp.concatenate([gstart + gtot, gcap - gtot, gend[-1:], n_tiles - gend[-1:] // tile])
    return goff.reshape(-1).astype(jnp.int32), tg.astype(jnp.int32), tv.astype(jnp.int32), fill.astype(jnp.int32)


def _moe(alpha, x1, gate, best, cnt, wg, wu, wd, g, b):
    n, d = x1.shape
    ff = wg.shape[2]
    nblk = cnt.shape[0]
    tm = n // nblk
    tile = MOE_TILE
    n_loc = tm + LANES
    n_sorted = -(-(n + nblk * N_EXPERT_GROUPS * SEG_ALIGN + N_EXPERT_GROUPS * tile) // tile) * tile
    n_tiles = n_sorted // tile
    cnt_flat = cnt[:, 0, :N_EXPERT_GROUPS].reshape(-1)
    goff, tg, tv, fill = _moe_tables(cnt[:, 0, :N_EXPERT_GROUPS], n_tiles, tile)
    any_spec = pl.BlockSpec(memory_space=pl.ANY)

    xs, gs, pos = pl.pallas_call(
        _dispatch_kernel,
        grid_spec=pltpu.PrefetchScalarGridSpec(
            num_scalar_prefetch=3,
            grid=(nblk,),
            in_specs=[
                pl.BlockSpec((tm, d), lambda i, *_: (i, 0)),
                pl.BlockSpec((tm, LANES), lambda i, *_: (i, 0)),
                pl.BlockSpec((1, tm), lambda i, *_: (0, i)),
            ],
            out_specs=[any_spec, any_spec, pl.BlockSpec((1, tm), lambda i, *_: (0, i))],
            scratch_shapes=[pltpu.VMEM((2, n_loc, d), f32), pltpu.VMEM((2, n_loc, LANES), f32),
                            pltpu.VMEM((tile, d), f32), pltpu.VMEM((tile, LANES), f32),
                            pltpu.SemaphoreType.DMA((3, 2))],
        ),
        out_shape=[jax.ShapeDtypeStruct((n_sorted, d), f32), jax.ShapeDtypeStruct((n_sorted, LANES), f32),
                   jax.ShapeDtypeStruct((1, n), jnp.int32)],
        compiler_params=_cparams(("arbitrary",)),
        name="moe_dispatch",
    )(cnt_flat, goff, fill, x1, gate, best)

    ys = pl.pallas_call(
        _experts_kernel,
        grid_spec=pltpu.PrefetchScalarGridSpec(
            num_scalar_prefetch=2,
            grid=(n_tiles,),
            in_specs=[
                pl.BlockSpec((tile, d), lambda t, *_: (t, 0)),
                pl.BlockSpec((tile, LANES), lambda t, *_: (t, 0)),
                pl.BlockSpec((EXPERTS_PER_GROUP, d, ff), lambda t, tg_, tv_: (tg_[t], 0, 0)),
                pl.BlockSpec((EXPERTS_PER_GROUP, d, ff), lambda t, tg_, tv_: (tg_[t], 0, 0)),
                pl.BlockSpec((EXPERTS_PER_GROUP, ff, d), lambda t, tg_, tv_: (tg_[t], 0, 0)),
            ],
            out_specs=pl.BlockSpec((tile, d), lambda t, *_: (t, 0)),
        ),
        out_shape=jax.ShapeDtypeStruct((n_sorted, d), f32),
        compiler_params=_cparams(("arbitrary",)),
        name="moe_experts",
    )(tg, tv, xs, gs, wg, wu, wd)

    return pl.pallas_call(
        functools.partial(_combine_kernel, alpha),
        grid_spec=pltpu.PrefetchScalarGridSpec(
            num_scalar_prefetch=2,
            grid=(nblk,),
            in_specs=[
                pl.BlockSpec((tm, d), lambda i, *_: (i, 0)),
                pl.BlockSpec((1, tm), lambda i, *_: (0, i)),
                pl.BlockSpec(g.shape, lambda i, *_: (0, 0)),
                pl.BlockSpec(b.shape, lambda i, *_: (0, 0)),
                any_spec,
            ],
            out_specs=pl.BlockSpec((tm, d), lambda i, *_: (i, 0)),
            scratch_shapes=[pltpu.VMEM((2, n_loc, d), f32), pltpu.SemaphoreType.DMA((2,))],
        ),
        out_shape=jax.ShapeDtypeStruct((n, d), f32),
        compiler_params=_cparams(("arbitrary",)),
        name="moe_combine",
    )(cnt_flat, goff, x1, pos, g, b, ys)


def _pad_cols(w, n):
    return jnp.pad(w, ((0, 0), (0, n - w.shape[1])))


def _row(v, n=None):
    v = v.reshape(1, -1)
    return v if n is None else _pad_cols(v, n)


def kernel(x, w_in, gla_w_alpha, gla_b_alpha, gla_norm_g, ssd_conv_w, ssd_conv_b, ssd_dt_bias, ssd_A_log, ssd_D, ssd_norm_g, rwkv_mu, rwkv_w0, rwkv_w2, rwkv_a0, rwkv_a2, rwkv_g2, rwkv_k_k, rwkv_k_a, rwkv_r_k, rwkv_ln_g, rwkv_ln_b, w_out, ln1_g, ln1_b, router_w, router_b, exp_w_gate, exp_w_up, exp_w_down, ln2_g, ln2_b):
    nb, t, d = x.shape
    depth = w_in.shape[0]
    alpha = float((2 * depth) ** 0.25)
    gla_cols = 4 * GLA_WIDTH + GLA_GATE_RANK
    ssd_cols = SSD_WIDTH + SSD_XBC + SSD_HEADS
    rwt = router_w.T
    rb = router_b.reshape(-1, 1)
    n_exp, _, ff = exp_w_gate.shape[1:]
    expert_views = (exp_w_gate.reshape(depth, n_exp * d, ff), exp_w_up.reshape(depth, n_exp * d, ff),
                    exp_w_down.reshape(depth, n_exp * ff, d))
    w_in_t =jnp.swapaxes(w_in, 1, 2)
    x2 = x.reshape(nb * t, d)
    for l in range(depth):
        wg, ws, wr = _regroup_w_in(w_in_t, l, (gla_cols, ssd_cols, RWKV_COLS), (GLA_COLS_PAD, SSD_COLS_PAD, RWKV_COLS))
        pg, pr, os_ = _in_proj_ssd(x2.reshape(nb, t, d), wg, ws, wr, ssd_conv_w[l], _row(ssd_conv_b[l]),
                                   _row(ssd_dt_bias[l], LANES), _row(ssd_A_log[l], LANES),
                                   _row(jnp.repeat(ssd_D[l], HEAD_DIM)), _row(ssd_norm_g[l]))
        gla_params = (jnp.pad(gla_w_alpha[l], ((0, LANES - GLA_GATE_RANK), (0, 0))), _row(gla_b_alpha[l]),
                      _row(jnp.tile(gla_norm_g[l], GLA_WIDTH // HEAD_DIM)))
        rwkv_params = (_row(rwkv_mu[l]), _row(rwkv_w0[l]), rwkv_w2[l], _row(rwkv_a0[l]), rwkv_a2[l], rwkv_g2[l],
                       _row(rwkv_k_k[l]), _row(rwkv_k_a[l]), _row(rwkv_r_k[l]), _row(rwkv_ln_g[l]),
                       _row(rwkv_ln_b[l]))
        og, or_, wg_b, wu_b, wd_b = _gla_rwkv(pg, gla_params, pr, rwkv_params, l, expert_views)
        x1, gate, best, cnt = _out_proj(alpha, x2, og.reshape(nb * t, -1), os_.reshape(nb * t, -1),
                                        or_.reshape(nb * t, -1), _bf(w_out[l]), _row(ln1_g[l]), _row(ln1_b[l]),
                                        rwt, rb)
        x2 = _moe(alpha, x1, gate, best, cnt, wg_b.reshape(n_exp, d, ff), wu_b.reshape(n_exp, d, ff),
                  wd_b.reshape(n_exp, ff, d), _row(ln2_g[l]), _row(ln2_b[l]))
    return x2.reshape(nb, t, d)
```

```python
import functools
import math

import jax
import jax.numpy as jnp
from jax import lax
from jax.experimental import pallas as pl
from jax.experimental.pallas import tpu as pltpu

f32 = jnp.float32
bf16 = jnp.bfloat16

LANES = 128
SUBLANES = 8
HEAD_DIM = 64
GLA_WIDTH = 256
GLA_GATE_RANK = 16
GLA_GATE_TAU = 16.0
GLA_CHUNK = 16
GLA_BLOCK = 128
GLA_COLS_PAD = 4 * GLA_WIDTH + LANES
SSD_WIDTH = 512
SSD_HEADS = 8
SSD_GROUPS = 2
SSD_STATE = 64
SSD_CONV = 4
SSD_XBC = SSD_WIDTH + 2 * SSD_GROUPS * SSD_STATE
SSD_BLOCK = 128
SSD_COLS_PAD = SSD_WIDTH + SSD_XBC + LANES
RWKV_WIDTH = 256
RWKV_CHUNK = 64
RWKV_GN_EPS = 64e-5
RWKV_COLS = 3 * RWKV_WIDTH + 64 + 64 + 128
N_EXPERTS = 16
N_EXPERT_GROUPS = 4
EXPERTS_PER_GROUP = 4
LN_EPS = 1e-5

PROJ_CHUNK = 256
PROJ_PACE = 4
OUT_PROJ_ROWS = 1024
MOE_SORT_ROWS = 512
MOE_TILE = 512
SEG_ALIGN = 8
VMEM_LIMIT = 56 * 1024 * 1024


def _bf(x):
    return x.astype(bf16)


def _dot(a, b):
    return jnp.dot(_bf(a), _bf(b), preferred_element_type=f32)


def _dot_nt(a, b):
    return lax.dot_general(_bf(a), _bf(b), (((1,), (1,)), ((), ())), preferred_element_type=f32)


def _dot_tn(a, b):
    return lax.dot_general(_bf(a), _bf(b), (((0,), (0,)), ((), ())), preferred_element_type=f32)


def _split(x, n):
    parts, r = [], x
    for _ in range(n):
        p = r.astype(bf16)
        parts.append(p)
        r = r - p.astype(f32)
    return parts


def _dot_sel_rhs(a, sel, n=3):
    out = None
    for p in _split(a, n):
        t = jnp.dot(p, sel, preferred_element_type=f32)
        out = t if out is None else out + t
    return out


def _dot_sel_lhs(sel, b, n=3):
    out = None
    for p in _split(b, n):
        t = jnp.dot(sel, p, preferred_element_type=f32)
        out = t if out is None else out + t
    return out


def _dot_hi(a, b):
    ah, al = _split(a, 2)
    bh, bl = _split(b, 2)
    d = functools.partial(jnp.dot, preferred_element_type=f32)
    return d(ah, bh) + d(al, bh) + d(ah, bl)


def _iota(shape, dim):
    return lax.broadcasted_iota(jnp.int32, shape, dim)


def _block_ones(n, blk):
    return (_iota((n, n), 0) // blk == _iota((n, n), 1) // blk)


def _silu(x):
    return x * jax.nn.sigmoid(x)


def _softplus(x):
    return jnp.maximum(x, 0.0) + jnp.log1p(jnp.exp(-jnp.abs(x)))


def _layer_norm(y, g, b, eps):
    mu = jnp.mean(y, axis=-1, keepdims=True)
    yc = y - mu
    var = jnp.mean(yc * yc, axis=-1, keepdims=True)
    return yc * lax.rsqrt(var + eps) * g + b


def _cparams(sem):
    return pltpu.CompilerParams(dimension_semantics=sem, vmem_limit_bytes=VMEM_LIMIT)


def _regroup_kernel(splits, w_ref, *o_refs):
    start = 0
    for width, o_ref in zip(splits, o_refs):
        part = w_ref[0, start:start + width, :]
        pad = o_ref.shape[1] - width
        if pad:
            part = jnp.concatenate([part, jnp.zeros((pad, part.shape[1]), part.dtype)], axis=0)
        o_ref[...] = _bf(part.T)
        start += width


def _regroup_w_in(w_in_t, layer, splits, padded):
    _, n_in, d = w_in_t.shape
    return pl.pallas_call(
        functools.partial(_regroup_kernel, splits),
        grid=(1,),
        in_specs=[pl.BlockSpec((1, n_in, d), lambda i: (layer, 0, 0))],
        out_specs=[pl.BlockSpec((d, p), lambda i: (0, 0)) for p in padded],
        out_shape=[jax.ShapeDtypeStruct((d, p), bf16) for p in padded],
        compiler_params=_cparams(("arbitrary",)),
        name="regroup_w_in",
    )(w_in_t)


def _gla_steps(p_ref, wa_ref, ba_ref, ng_ref, o_ref, st_ref, t_ref, d_ref, sx_ref):
    nb, tb, _ = p_ref.shape
    w, c = GLA_WIDTH, GLA_CHUNK
    nc = tb // c
    npt = w // LANES
    Q, K, V, CUM, OUT = range(5)

    @pl.when(pl.program_id(0) == 0)
    def _():
        st_ref[...] = jnp.zeros_like(st_ref)

    bones = _block_ones(w, HEAD_DIM).astype(bf16)
    pair_blk = _block_ones(LANES, HEAD_DIM)
    r_i, c_i = _iota((tb, tb), 0), _iota((tb, tb), 1)
    tri = ((r_i >= c_i) & (r_i // c == c_i // c)).astype(bf16)
    wa_h, wa_l = _split(wa_ref[...], 2)
    dd = functools.partial(jnp.dot, preferred_element_type=f32)

    for b in range(nb):
        a_h, a_l = _split(p_ref[b, :, 4 * w:4 * w + LANES], 2)
        z = dd(a_h, wa_h) + dd(a_l, wa_h) + dd(a_h, wa_l) + ba_ref[...]
        log_a = -_softplus(-z) * (1.0 / GLA_GATE_TAU)
        cum = _dot_sel_lhs(tri, log_a, 3)
        for lt in range(npt):
            lanes = slice(lt * LANES, (lt + 1) * LANES)
            t_ref[b, Q, lt] = p_ref[b, :, lanes] * (HEAD_DIM ** -0.5)
            t_ref[b, K, lt] = p_ref[b, :, w + lt * LANES:w + (lt + 1) * LANES]
            t_ref[b, V, lt] = p_ref[b, :, 2 * w + lt * LANES:2 * w + (lt + 1) * LANES]
            t_ref[b, CUM, lt] = cum[:, lanes]
        yield

    def ld(b, j, off):
        return jnp.concatenate([t_ref[b, j, lt, pl.ds(off, nc, stride=c), :] for lt in range(npt)], axis=1)

    for b in range(nb):
        qt = [ld(b, Q, o) for o in range(c)]
        kt = [ld(b, K, o) for o in range(c)]
        ct = [ld(b, CUM, o) for o in range(c)]
        pair = 0
        for t in range(c):
            for s in range(t + 1):
                qk = qt[t] * kt[s]
                d_ref[b, pair * nc:(pair + 1) * nc, :] = qk if s == t else qk * jnp.exp(ct[t] - ct[s])
                pair += 1
            if t % 4 == 3:
                yield
    for b in range(nb):
        sx_ref[b] = jnp.dot(_bf(d_ref[b]), bones, preferred_element_type=f32)
        yield
    for b in range(nb):
        vt = [ld(b, V, o) for o in range(c)]
        pair = 0
        for t in range(c):
            acc = None
            for s in range(t + 1):
                term = sx_ref[b, pair * nc:(pair + 1) * nc, :] * vt[s]
                acc = term if acc is None else acc + term
                pair += 1
            for lt in range(npt):
                t_ref[b, OUT, lt, pl.ds(t, nc, stride=c), :] = acc[:, lt * LANES:(lt + 1) * LANES]
            if t % 8 == 7:
                yield

    for ci in range(nc):
        sl = slice(ci * c, (ci + 1) * c)
        last = slice((ci + 1) * c - 1, (ci + 1) * c)
        for b in range(nb):
            for lt in range(npt):
                cum_c = t_ref[b, CUM, lt, sl, :]
                cum_last = t_ref[b, CUM, lt, last, :]
                st = st_ref[b, lt]
                t_ref[b, OUT, lt, sl, :] += _dot_nt(t_ref[b, Q, lt, sl, :] * jnp.exp(cum_c), st)
                upd = _dot_tn(t_ref[b, V, lt, sl, :], t_ref[b, K, lt, sl, :] * jnp.exp(cum_last - cum_c))
                st_ref[b, lt] = st * jnp.exp(cum_last) + jnp.where(pair_blk, upd, 0.0)
        yield

    o = jnp.concatenate(
        [jnp.concatenate([t_ref[b, OUT, lt] for lt in range(npt)], axis=1) for b in range(nb)], axis=0)
    ms = _dot_sel_rhs(o * o, bones, 2) * (1.0 / HEAD_DIM)
    o = o * lax.rsqrt(ms + LN_EPS) * ng_ref[...]
    for b in range(nb):
        o_ref[b] = o[b * tb:(b + 1) * tb] * _silu(p_ref[b, :, 3 * w:4 * w])


def _round_robin(gens):
    live = list(gens)
    while live:
        for gen in list(live):
            if next(gen, StopIteration) is StopIteration:
                live.remove(gen)
            else:
                yield


def _ssd_steps(b, p_ref, halo_ref, cw_ref, cb_ref, dtb_ref, alog_ref, dsk_ref, ng_ref, o_ref, st_ref):
    L = p_ref.shape[1]
    W, M, G = SSD_WIDTH, SSD_STATE, SSD_GROUPS
    gw = W // G

    @pl.when(pl.program_id(0) == 0)
    def _():
        st_ref[b] = jnp.zeros(st_ref.shape[1:], f32)
        halo_ref[b] = jnp.zeros(halo_ref.shape[1:], f32)

    z = p_ref[b, :, 0:W]
    xbc = p_ref[b, :, W:W + SSD_XBC]
    dt_raw = p_ref[b, :, W + SSD_XBC:W + SSD_XBC + LANES]
    xx = jnp.concatenate([halo_ref[b], xbc], axis=0)
    halo_ref[b] = xbc[L - SUBLANES:L]
    conv = cb_ref[...]
    for j in range(SSD_CONV):
        off = SUBLANES - (SSD_CONV - 1) + j
        conv = conv + cw_ref[j:j + 1, :] * xx[off:off + L]
    act = _silu(conv)
    xs = act[:, 0:W]
    bm = act[:, W:W + G * M]
    cm = act[:, W + G * M:W + 2 * G * M]
    yield

    dt = _softplus(dt_raw + dtb_ref[...])
    a_neg = jnp.where(_iota((1, LANES), 1) < SSD_HEADS, -jnp.exp(alog_ref[...]), 0.0)
    da = dt * a_neg
    r_i, c_i = _iota((L, L), 0), _iota((L, L), 1)
    causal = r_i >= c_i
    cs = _dot_sel_lhs(causal.astype(bf16), da, 3)
    expand = (_iota((LANES, W), 0) == _iota((LANES, W), 1) // HEAD_DIM).astype(bf16)
    dtx = _dot_sel_rhs(dt, expand, 2)
    csx = _dot_sel_rhs(cs, expand, 2)
    eye_h = (_iota((SSD_HEADS, LANES), 0) == _iota((SSD_HEADS, LANES), 1)).astype(bf16)
    cs_t = None
    for p in _split(cs, 2):
        t = lax.dot_general(eye_h, p, (((1,), (1,)), ((), ())), preferred_element_type=f32)
        cs_t = t if cs_t is None else cs_t + t
    xdt = xs * dtx
    yield

    st = st_ref[b]
    y_parts = []
    for gi in range(G):
        bm_g = bm[:, gi * M:(gi + 1) * M]
        cm_g = cm[:, gi * M:(gi + 1) * M]
        cb = _dot_nt(cm_g, bm_g)
        for e in range(SSD_HEADS // G):
            h = gi * (SSD_HEADS // G) + e
            seg = jnp.where(causal, jnp.exp(jnp.broadcast_to(cs[:, h:h + 1], (L, L)) - cs_t[h:h + 1, :]), 0.0)
            y_parts.append(_dot(cb * seg, xdt[:, h * HEAD_DIM:(h + 1) * HEAD_DIM]))
            if e % 2:
                yield
    y = jnp.concatenate(y_parts, axis=1)
    y_inter = jnp.concatenate(
        [_dot(cm[:, gi * M:(gi + 1) * M], st[:, gi * gw:(gi + 1) * gw]) for gi in range(G)], axis=1)
    y = y + y_inter * jnp.exp(csx)
    yield

    cs_last = csx[L - 1:L, :]
    xdtw = xdt * jnp.exp(cs_last - csx)
    d_state = jnp.concatenate(
        [_dot_tn(bm[:, gi * M:(gi + 1) * M], xdtw[:, gi * gw:(gi + 1) * gw]) for gi in range(G)], axis=1)
    st_ref[b] = st * jnp.exp(cs_last) + d_state
    yield

    y = (y + dsk_ref[...] * xs) * _silu(z)
    outs = []
    for gi in range(G):
        yg = y[:, gi * gw:(gi + 1) * gw]
        ms = jnp.mean(yg * yg, axis=-1, keepdims=True)
        outs.append(yg * lax.rsqrt(ms + LN_EPS))
    o_ref[b] = jnp.concatenate(outs, axis=1) * ng_ref[...]


def _inproj_ssd_kernel(x_ref, wg_ref, ws_ref, wr_ref, cw_ref, cb_ref, dtb_ref, alog_ref, dsk_ref, ng_ref,
                       pg_ref, pr_ref, os_ref, ps_ref, halo_ref, st_ref):
    nb, L, d = x_ref.shape
    xb = _bf(x_ref[...].reshape(nb * L, d))
    ps_ref[...] = jnp.dot(xb, ws_ref[...], preferred_element_type=f32).reshape(ps_ref.shape)

    def other_columns():
        for w_ref, o_ref in ((wg_ref, pg_ref), (wr_ref, pr_ref)):
            n = w_ref.shape[1]
            for c0 in range(0, n, PROJ_CHUNK):
                c1 = min(c0 + PROJ_CHUNK, n)
                o_ref[:, :, c0:c1] = jnp.dot(xb, w_ref[:, c0:c1], preferred_element_type=f32).reshape(nb, L, c1 - c0)
                for _ in range(PROJ_PACE):
                    yield

    ssd = _round_robin([_ssd_steps(b, ps_ref, halo_ref, cw_ref, cb_ref, dtb_ref, alog_ref, dsk_ref,
                                   ng_ref, os_ref, st_ref) for b in range(nb)])
    for _ in _round_robin([ssd, other_columns()]):
        pass


def _in_proj_ssd(x3, wg, ws, wr, cw, cb, dtb, alog, dsk, ng):
    nb, t, d = x3.shape
    L = SSD_BLOCK
    full2 = lambda a: pl.BlockSpec(a.shape, lambda i: (0, 0))
    blk = lambda c: pl.BlockSpec((nb, L, c), lambda i: (0, i, 0))
    return pl.pallas_call(
        _inproj_ssd_kernel,
        grid=(t // L,),
        in_specs=[blk(d)] + [full2(a) for a in (wg, ws, wr, cw, cb, dtb, alog, dsk, ng)],
        out_specs=[blk(wg.shape[1]), blk(wr.shape[1]), blk(SSD_WIDTH)],
        out_shape=[jax.ShapeDtypeStruct((nb, t, wg.shape[1]), f32), jax.ShapeDtypeStruct((nb, t, wr.shape[1]), f32),
                   jax.ShapeDtypeStruct((nb, t, SSD_WIDTH), f32)],
        scratch_shapes=[pltpu.VMEM((nb, L, ws.shape[1]), f32), pltpu.VMEM((nb, SUBLANES, SSD_XBC), f32),
                        pltpu.VMEM((nb, SSD_STATE, SSD_WIDTH), f32)],
        compiler_params=_cparams(("arbitrary",)),
        name="in_proj_ssd",
    )(x3, wg, ws, wr, cw, cb, dtb, alog, dsk, ng)


def _rwkv_steps(sub, p_ref, prev_ref, mu_ref, w0_ref, w2_ref, a0_ref, a2_ref, g2_ref, kk_ref, ka_ref, rk_ref,
                lng_ref, lnb_ref, o_ref, zt_ref):
    nb = p_ref.shape[0]
    C, W = RWKV_CHUNK, RWKV_WIDTH
    H = LANES // HEAD_DIM
    npair = W // LANES
    i = pl.program_id(0)
    r0 = sub * C

    if sub == 0:
        @pl.when(i == 0)
        def _():
            zt_ref[...] = jnp.zeros_like(zt_ref)

    bones = _block_ones(W, HEAD_DIM).astype(bf16)
    pair_blk = _block_ones(LANES, HEAD_DIM)
    n = H * C
    rr, cc = _iota((n, n), 0), _iota((n, n), 1)
    same = rr // C == cc // C
    strict = same & (rr > cc)
    lower = same & (rr >= cc)
    eye = jnp.where(rr == cc, 1.0, 0.0)

    def stack(t):
        return jnp.where(pair_blk, jnp.concatenate([t] * H, axis=0), 0.0)

    rows = range(nb)
    nr = nb * C
    cols = p_ref[:, r0:r0 + C, :].reshape(nr, p_ref.shape[2])

    def row_before(b):
        return p_ref[b, r0 - 1:r0, :] if sub else jnp.where(i > 0, prev_ref[b, SUBLANES - 1:SUBLANES, :], 0.0)

    shifted = jnp.concatenate(
        [piece for b in rows for piece in (row_before(b), p_ref[b, r0:r0 + C - 1, :])], axis=0)
    cols = cols + (shifted - cols) * mu_ref[...]
    r = cols[:, 0:W]
    k = cols[:, W:2 * W]
    v = cols[:, 2 * W:3 * W]
    w_lr = cols[:, 3 * W:3 * W + 64]
    a_lr = cols[:, 3 * W + 64:3 * W + 128]
    g_lr = cols[:, 3 * W + 128:3 * W + 256]

    wv = -_softplus(-(w0_ref[...] + _dot_hi(jnp.tanh(w_lr), w2_ref[...]))) - 0.5
    lw = -jnp.exp(wv)
    a = jax.nn.sigmoid(a0_ref[...] + _dot_hi(a_lr, a2_ref[...]))
    g = _dot_hi(jax.nn.sigmoid(g_lr), g2_ref[...])
    yield

    kk = k * kk_ref[...]
    k2 = k * (1.0 + (a - 1.0) * ka_ref[...])
    head_sums = _dot_sel_rhs(jnp.concatenate([kk * kk, r * k2 * rk_ref[...]], axis=0), bones, 2)
    kk = kk / jnp.maximum(jnp.sqrt(head_sums[:nr]), 1e-12)
    bonus_v = head_sums[nr:] * v
    bvec = kk * a
    yield

    rb_i, cb_i = _iota((nr, nr), 0), _iota((nr, nr), 1)
    tri = ((rb_i >= cb_i) & (rb_i // C == cb_i // C)).astype(bf16)
    cum = _dot_sel_lhs(tri, lw, 3)
    cum_last = [cum[(b + 1) * C - 1:(b + 1) * C, :] for b in rows]
    e_neg = jnp.exp(-cum)
    e_rem = jnp.exp(jnp.concatenate([jnp.broadcast_to(cl, (C, W)) for cl in cum_last], axis=0) - cum)
    per_token = dict(aw=-kk * jnp.exp(cum - lw), rw=r * jnp.exp(cum), bn=bvec * e_neg, kn=k2 * e_neg,
                     bt=bvec * e_rem, kt=k2 * e_rem, vs=v)
    chains = [(b, p) for b in rows for p in range(npair)]
    q = [dict({name: stack(t[b * C:(b + 1) * C, p * LANES:(p + 1) * LANES]) for name, t in per_token.items()},
              wc=jnp.exp(cum_last[b][:, p * LANES:(p + 1) * LANES])) for b, p in chains]
    ids = range(len(chains))
    yield

    a_ab = [jnp.where(strict, _dot_nt(q[c]["aw"], q[c]["bn"]), 0.0) for c in ids]
    a_ak = [jnp.where(strict, _dot_nt(q[c]["aw"], q[c]["kn"]), 0.0) for c in ids]
    yield
    a_rb = [jnp.where(lower, _dot_nt(q[c]["rw"], q[c]["bn"]), 0.0) for c in ids]
    a_rk = [jnp.where(lower, _dot_nt(q[c]["rw"], q[c]["kn"]), 0.0) for c in ids]
    yield

    pw = a_ab
    tinv = [eye + a_ab[c] for c in ids]
    for _ in range(int(math.log2(C)) - 1):
        pw = [_dot(pw[c], pw[c]) for c in ids]
        yield
        tinv = [tinv[c] + _dot(tinv[c], pw[c]) for c in ids]
        yield

    yield "state"
    zt = [zt_ref[b, p] for b, p in chains]
    rhs = [_dot_nt(q[c]["aw"], zt[c]) + _dot(a_ak[c], q[c]["vs"]) for c in ids]
    yield
    y0 = [_dot_nt(q[c]["rw"], zt[c]) + _dot(a_rk[c], q[c]["vs"]) for c in ids]
    yield
    u = [_dot(tinv[c], rhs[c]) for c in ids]
    yield
    y_st = [y0[c] + _dot(a_rb[c], u[c]) for c in ids]
    yield
    for c, (b, p) in enumerate(chains):
        zt_ref[b, p] = zt[c] * q[c]["wc"] + _dot_tn(u[c], q[c]["bt"]) + _dot_tn(q[c]["vs"], q[c]["kt"])
    yield

    inv_n = 1.0 / HEAD_DIM
    ys = []
    for b in rows:
        tiles = []
        for p in range(npair):
            y_c = y_st[b * npair + p]
            y = y_c[0:C]
            for h in range(1, H):
                y = y + y_c[h * C:(h + 1) * C]
            tiles.append(y)
        ys.append(jnp.concatenate(tiles, axis=1))
    y = jnp.concatenate(ys, axis=0)
    yc = y - _dot_sel_rhs(y, bones, 2) * inv_n
    var_h = _dot_sel_rhs(yc * yc, bones, 2) * inv_n
    y = yc * lax.rsqrt(var_h + RWKV_GN_EPS) * lng_ref[...] + lnb_ref[...]
    o_ref[:, r0:r0 + C, :] = ((y + bonus_v) * g).reshape(nb, C, W)


N_GLA_IN, N_RWKV_IN = 4, 13


N_CAST = 3


def _gla_rwkv_kernel(*refs):
    gla_in = refs[:N_GLA_IN]
    rwkv_in = refs[N_GLA_IN:N_GLA_IN + N_RWKV_IN]
    k = N_GLA_IN + N_RWKV_IN
    cast_in = refs[k:k + N_CAST]
    og_ref, or_ref = refs[k + N_CAST:k + N_CAST + 2]
    cast_out = refs[k + N_CAST + 2:k + 2 * N_CAST + 2]
    st_ref, t_ref, d_ref, sx_ref, zt_ref = refs[k + 2 * N_CAST + 2:]

    def casts():
        for src, dst in zip(cast_in, cast_out):
            dst[...] = _bf(src[...])
            yield

    n_sub = rwkv_in[0].shape[1] // RWKV_CHUNK

    def rwkv_chunks():
        gens = [_rwkv_steps(sub, *rwkv_in, or_ref, zt_ref) for sub in range(n_sub)]
        parked, done = set(), set()
        while len(done) < n_sub:
            for sub, gen in enumerate(gens):
                if sub in done or (sub in parked and any(s not in done for s in range(sub))):
                    continue
                step = next(gen, StopIteration)
                if step is StopIteration:
                    done.add(sub)
                elif step == "state":
                    parked.add(sub)
                yield

    rwkv = rwkv_chunks()
    gla = _gla_steps(*gla_in, og_ref, st_ref, t_ref, d_ref, sx_ref)
    for _ in _round_robin([rwkv, gla, casts()]):
        pass


def _gla_rwkv(pg, gla_params, pr, rwkv_params, layer, expert_weights):
    nb, t, gcols = pg.shape
    rcols = pr.shape[2]
    tb = GLA_BLOCK
    n_steps = t // tb
    npt = GLA_WIDTH // LANES
    pair_rows = (GLA_CHUNK * (GLA_CHUNK + 1) // 2) * (tb // GLA_CHUNK)
    full2 = lambda a: pl.BlockSpec(a.shape, lambda i: (0, 0))
    blk = lambda c: pl.BlockSpec((nb, tb, c), lambda i: (0, i, 0))
    assert len(gla_params) + 1 == N_GLA_IN and len(rwkv_params) + 2 == N_RWKV_IN and len(expert_weights) == N_CAST
    cast_rows = [w.shape[1] // n_steps for w in expert_weights]
    return pl.pallas_call(
        _gla_rwkv_kernel,
        grid=(n_steps,),
        in_specs=[blk(gcols)] + [full2(p) for p in gla_params]
        + [blk(rcols), pl.BlockSpec((nb, SUBLANES, rcols),
                                    lambda i: (0, jnp.maximum(i * (tb // SUBLANES) - 1, 0), 0))]
        + [full2(p) for p in rwkv_params]
        + [pl.BlockSpec((None, r, w.shape[2]), lambda i: (layer, i, 0)) for r, w in zip(cast_rows, expert_weights)],
        out_specs=[blk(GLA_WIDTH), blk(RWKV_WIDTH)]
        + [pl.BlockSpec((r, w.shape[2]), lambda i: (i, 0)) for r, w in zip(cast_rows, expert_weights)],
        out_shape=[jax.ShapeDtypeStruct((nb, t, GLA_WIDTH), f32), jax.ShapeDtypeStruct((nb, t, RWKV_WIDTH), f32)]
        + [jax.ShapeDtypeStruct(w.shape[1:], bf16) for w in expert_weights],
        scratch_shapes=[
            pltpu.VMEM((nb, npt, LANES, LANES), f32),
            pltpu.VMEM((nb, 5, npt, tb, LANES), f32),
            pltpu.VMEM((nb, pair_rows, GLA_WIDTH), f32),
            pltpu.VMEM((nb, pair_rows, GLA_WIDTH), f32),
            pltpu.VMEM((nb, RWKV_WIDTH // LANES, LANES, LANES), f32),
        ],
        compiler_params=_cparams(("arbitrary",)),
        name="gla_rwkv_mixers",
    )(pg, *gla_params, pr, pr, *rwkv_params, *expert_weights)


def _route(logits, rb):
    tm = logits.shape[1]
    scores = jax.nn.sigmoid(logits)
    sel = scores + rb
    srow = [sel[e:e + 1, :] for e in range(N_EXPERTS)]
    crow = [scores[e:e + 1, :] for e in range(N_EXPERTS)]

    def top2_sum(v0, v1, v2, v3):
        m01, n01 = jnp.maximum(v0, v1), jnp.minimum(v0, v1)
        m23, n23 = jnp.maximum(v2, v3), jnp.minimum(v2, v3)
        return jnp.maximum(m01, m23) + jnp.maximum(jnp.minimum(m01, m23), jnp.maximum(n01, n23))

    gscore = [top2_sum(*srow[4 * gi:4 * gi + 4]) for gi in range(N_EXPERT_GROUPS)]
    best = jnp.zeros((1, tm), jnp.int32)
    best_v = gscore[0]
    for gi in range(1, N_EXPERT_GROUPS):
        better = gscore[gi] > best_v
        best = jnp.where(better, gi, best)
        best_v = jnp.where(better, gscore[gi], best_v)
    sv, cv = [], []
    for j in range(EXPERTS_PER_GROUP):
        s_j, c_j = srow[j], crow[j]
        for gi in range(1, N_EXPERT_GROUPS):
            s_j = jnp.where(best == gi, srow[4 * gi + j], s_j)
            c_j = jnp.where(best == gi, crow[4 * gi + j], c_j)
        sv.append(s_j)
        cv.append(c_j)
    j1 = jnp.zeros((1, tm), jnp.int32)
    v1 = sv[0]
    for j in range(1, EXPERTS_PER_GROUP):
        better = sv[j] > v1
        j1 = jnp.where(better, j, j1)
        v1 = jnp.where(better, sv[j], v1)
    j2 = jnp.full((1, tm), -1, jnp.int32)
    v2 = jnp.full((1, tm), -jnp.inf, f32)
    for j in range(EXPERTS_PER_GROUP):
        better = (j1 != j) & ((j2 < 0) | (sv[j] > v2))
        j2 = jnp.where(better, j, j2)
        v2 = jnp.where(better, sv[j], v2)
    w1 = cv[0]
    w2 = cv[0]
    for j in range(1, EXPERTS_PER_GROUP):
        w1 = jnp.where(j1 == j, cv[j], w1)
        w2 = jnp.where(j2 == j, cv[j], w2)
    wsum = w1 + w2
    g1, g2 = w1 / wsum, w2 / wsum
    rows = []
    for e in range(N_EXPERTS):
        gi, j = divmod(e, EXPERTS_PER_GROUP)
        in_g = best == gi
        rows.append(jnp.where(in_g & (j1 == j), g1, jnp.where(in_g & (j2 == j), g2, 0.0)))
    gate_t = jnp.concatenate(rows, axis=0)
    return gate_t, best


def _outproj_kernel(alpha, x_ref, og_ref, os_ref, or_ref, wo_ref, g_ref, b_ref, rwt_ref, rb_ref,
                    x1_ref, gate_ref, best_ref, cnt_ref):
    tm = x_ref.shape[0]
    ts = tm // cnt_ref.shape[0]
    parts = [slice(sb * ts, (sb + 1) * ts) for sb in range(cnt_ref.shape[0])]
    d = functools.partial(jnp.dot, preferred_element_type=f32)
    mix = [d(_bf(og_ref[s, :]), wo_ref[0:GLA_WIDTH, :])
           + d(_bf(os_ref[s, :]), wo_ref[GLA_WIDTH:GLA_WIDTH + SSD_WIDTH, :])
           + d(_bf(or_ref[s, :]), wo_ref[GLA_WIDTH + SSD_WIDTH:, :]) for s in parts]
    x1 = [_layer_norm(alpha * x_ref[s, :] + m, g_ref[...], b_ref[...], LN_EPS) for s, m in zip(parts, mix)]
    for s, v in zip(parts, x1):
        x1_ref[s, :] = v

    wh, wl = _split(rwt_ref[...], 2)
    nt = lambda a, b: lax.dot_general(a, b, (((1,), (1,)), ((), ())), preferred_element_type=f32)
    logits = []
    for v in x1:
        xh, xl = _split(v, 2)
        logits.append(nt(wh, xh) + nt(wl, xh) + nt(wh, xl))
    routed = [_route(lg, rb_ref[...]) for lg in logits]
    eye = (_iota((N_EXPERTS, LANES), 0) == _iota((N_EXPERTS, LANES), 1)).astype(bf16)
    lane = _iota((1, LANES), 1)
    for sb, (s, (gate_t, best)) in enumerate(zip(parts, routed)):
        gate = None
        for p in _split(gate_t, 3):
            t = lax.dot_general(p, eye, (((0,), (0,)), ((), ())), preferred_element_type=f32)
            gate = t if gate is None else gate + t
        gate_ref[s, :] = gate
        best_ref[:, s] = best
        cnt = jnp.zeros((1, LANES), jnp.int32)
        for gi in range(N_EXPERT_GROUPS):
            n_g = jnp.sum((best == gi).astype(jnp.int32), axis=-1, keepdims=True)
            cnt = jnp.where(lane == gi, n_g, cnt)
        cnt_ref[sb] = cnt


def _out_proj(alpha, x2, og, os_, or_, wo, g, b, rwt, rb):
    n, d = x2.shape
    tm = min(OUT_PROJ_ROWS, n)
    ts = min(MOE_SORT_ROWS, tm)
    full2 = lambda a: pl.BlockSpec(a.shape, lambda i: (0, 0))
    rows = lambda c: pl.BlockSpec((tm, c), lambda i: (i, 0))
    return pl.pallas_call(
        functools.partial(_outproj_kernel, alpha),
        grid=(n // tm,),
        in_specs=[rows(d), rows(og.shape[1]), rows(os_.shape[1]), rows(or_.shape[1]),
                  full2(wo), full2(g), full2(b), full2(rwt), full2(rb)],
        out_specs=[rows(d), rows(LANES), pl.BlockSpec((1, tm), lambda i: (0, i)),
                   pl.BlockSpec((tm // ts, 1, LANES), lambda i: (i, 0, 0))],
        out_shape=[jax.ShapeDtypeStruct((n, d), f32), jax.ShapeDtypeStruct((n, LANES), f32),
                   jax.ShapeDtypeStruct((1, n), jnp.int32), jax.ShapeDtypeStruct((n // ts, 1, LANES), jnp.int32)],
        compiler_params=_cparams(("arbitrary",)),
        name="out_proj_router",
    )(x2, og, os_, or_, wo, g, b, rwt, rb)


def _seg_copy(src, dst, sem, src_off, dst_off, rows, max_rows, wait):
    k = SEG_ALIGN
    sizes = []
    while k <= max_rows:
        sizes.append(k)
        k *= 2
    for k in reversed(sizes):
        shift = int(math.log2(k)) + 1

        @pl.when((rows & k) != 0)
        def _():
            done = (rows >> shift) << shift
            cp = pltpu.make_async_copy(
                src.at[pl.ds(pl.multiple_of(src_off + done, SEG_ALIGN), k)],
                dst.at[pl.ds(pl.multiple_of(dst_off + done, SEG_ALIGN), k)], sem)
            if wait:
                cp.wait()
            else:
                cp.start()


def _padded(count):
    return (count + (SEG_ALIGN - 1)) & (-SEG_ALIGN)


def _one_hot_rows(pos, n_rows):
    return jnp.where(_iota((n_rows, pos.shape[1]), 0) == pos, 1.0, 0.0).astype(bf16)


def _dispatch_kernel(cnt_sm, goff_sm, fill_sm, x_ref, gate_ref, best_ref, xs_out, gs_out, pos_ref,
                     xloc, gloc, zx, zg, sems):
    i = pl.program_id(0)
    tm = x_ref.shape[0]
    n_loc = xloc.shape[1]
    slot = i % 2

    def copies(blk, slot_, wait):
        loff = 0
        for gi in range(N_EXPERT_GROUPS):
            rows = _padded(cnt_sm[blk * N_EXPERT_GROUPS + gi])
            dst = goff_sm[blk * N_EXPERT_GROUPS + gi]
            _seg_copy(xloc.at[slot_], xs_out, sems.at[0, slot_], loff, dst, rows, tm, wait)
            _seg_copy(gloc.at[slot_], gs_out, sems.at[1, slot_], loff, dst, rows, tm, wait)
            loff = loff + rows

    best = best_ref[...]
    onehot = (_iota((SUBLANES, tm), 0) == best).astype(bf16)
    earlier = (_iota((tm, tm), 0) < _iota((tm, tm), 1)).astype(bf16)
    rank = jnp.dot(onehot, earlier, preferred_element_type=f32)
    pos = jnp.zeros((1, tm), f32)
    loff = 0
    for gi in range(N_EXPERT_GROUPS):
        pos = jnp.where(best == gi, rank[gi:gi + 1, :] + loff.astype(f32) if gi else rank[gi:gi + 1, :], pos)
        loff = loff + _padded(cnt_sm[i * N_EXPERT_GROUPS + gi])
    pos = pos.astype(jnp.int32)
    pos_ref[...] = pos
    perm = _one_hot_rows(pos, n_loc)
    xloc[slot] = jnp.dot(perm, _bf(x_ref[...]), preferred_element_type=f32)
    gloc[slot] = _dot_sel_lhs(perm, gate_ref[...], 2)
    copies(i, slot, False)

    @pl.when(i > 0)
    def _():
        copies(i - 1, 1 - slot, True)

    @pl.when(i == pl.num_programs(0) - 1)
    def _():
        copies(i, slot, True)
        zx[...] = jnp.zeros_like(zx)
        zg[...] = jnp.zeros_like(zg)
        tile = zx.shape[0]

        def fills(wait):
            for src, dst, sem in ((zx, xs_out, sems.at[2, 0]), (zg, gs_out, sems.at[2, 1])):
                for gi in range(N_EXPERT_GROUPS):
                    _seg_copy(src, dst, sem, 0, fill_sm[gi], fill_sm[N_EXPERT_GROUPS + gi], tile // 2, wait)

                def body(j, carry):
                    off = pl.multiple_of(fill_sm[2 * N_EXPERT_GROUPS] + j * tile, SEG_ALIGN)
                    cp = pltpu.make_async_copy(src, dst.at[pl.ds(off, tile)], sem)
                    if wait:
                        cp.wait()
                    else:
                        cp.start()
                    return carry

                lax.fori_loop(0, fill_sm[2 * N_EXPERT_GROUPS + 1], body, 0)

        fills(False)
        fills(True)


def _experts_kernel(tg_sm, tv_sm, x_ref, gate_ref, wg_ref, wu_ref, wd_ref, y_ref):
    t = pl.program_id(0)

    @pl.when(tv_sm[t] == 0)
    def _():
        y_ref[...] = jnp.zeros_like(y_ref)

    @pl.when(tv_sm[t] > 0)
    def _():
        first = tg_sm[t] * EXPERTS_PER_GROUP
        xb = _bf(x_ref[...])
        gate = gate_ref[...]
        lane = _iota(gate.shape, 1)
        hs = []
        for j in range(EXPERTS_PER_GROUP):
            hg = jnp.dot(xb, wg_ref[j], preferred_element_type=f32)
            hu = jnp.dot(xb, wu_ref[j], preferred_element_type=f32)
            gcol = jnp.sum(jnp.where(lane == first + j, gate, 0.0), axis=-1, keepdims=True)
            hs.append(_bf(_silu(hg) * hu * gcol))
        ff = wd_ref.shape[1]
        y_ref[...] = jnp.dot(jnp.concatenate(hs, axis=1), wd_ref[...].reshape(EXPERTS_PER_GROUP * ff, -1),
                             preferred_element_type=f32)


def _combine_kernel(alpha, cnt_sm, goff_sm, x_ref, pos_ref, g_ref, b_ref, ys_ref, o_ref, yloc, sems):
    i = pl.program_id(0)
    tm = x_ref.shape[0]
    n_loc = yloc.shape[1]
    slot = i % 2

    def copies(blk, slot_, wait):
        loff = 0
        for gi in range(N_EXPERT_GROUPS):
            rows = _padded(cnt_sm[blk * N_EXPERT_GROUPS + gi])
            src = goff_sm[blk * N_EXPERT_GROUPS + gi]
            _seg_copy(ys_ref, yloc.at[slot_], sems.at[slot_], src, loff, rows, tm, wait)
            loff = loff + rows

    @pl.when(i == 0)
    def _():
        yloc[...] = jnp.zeros_like(yloc)
        copies(0, 0, False)

    @pl.when(i + 1 < pl.num_programs(0))
    def _():
        copies(i + 1, 1 - slot, False)

    copies(i, slot, True)
    perm = _one_hot_rows(pos_ref[...], n_loc)
    y = lax.dot_general(perm, _bf(yloc[slot]), (((0,), (0,)), ((), ())), preferred_element_type=f32)
    o_ref[...] = _layer_norm(alpha * x_ref[...] + y, g_ref[...], b_ref[...], LN_EPS)


def _moe_tables(cnt, n_tiles, tile):
    pc = (cnt + (SEG_ALIGN - 1)) // SEG_ALIGN * SEG_ALIGN
    gtot = jnp.sum(pc, axis=0)
    gcap = (gtot + tile - 1) // tile * tile
    gend = jnp.cumsum(gcap)
    gstart = gend - gcap
    goff = gstart[None, :] + jnp.cumsum(pc, axis=0) - pc
    tstart = jnp.arange(n_tiles, dtype=jnp.int32) * tile
    tg = jnp.minimum(jnp.sum((tstart[:, None] >= gend[None, :]).astype(jnp.int32), axis=1), N_EXPERT_GROUPS - 1)
    tv = jnp.clip(gtot[tg] - (tstart - gstart[tg]), 0, tile)
    fill = jnp.concatenate([gstart + gtot, gcap - gtot, gend[-1:], n_tiles - gend[-1:] // tile])
    return goff.reshape(-1).astype(jnp.int32), tg.astype(jnp.int32), tv.astype(jnp.int32), fill.astype(jnp.int32)


def _moe(alpha, x1, gate, best, cnt, wg, wu, wd, g, b):
    n, d = x1.shape
    ff = wg.shape[2]
    nblk = cnt.shape[0]
    tm = n // nblk
    tile = MOE_TILE
    n_loc = tm + LANES
    n_sorted = -(-(n + nblk * N_EXPERT_GROUPS * SEG_ALIGN + N_EXPERT_GROUPS * tile) // tile) * tile
    n_tiles = n_sorted // tile
    cnt_flat = cnt[:, 0, :N_EXPERT_GROUPS].reshape(-1)
    goff, tg, tv, fill = _moe_tables(cnt[:, 0, :N_EXPERT_GROUPS], n_tiles, tile)
    any_spec = pl.BlockSpec(memory_space=pl.ANY)

    xs, gs, pos = pl.pallas_call(
        _dispatch_kernel,
        grid_spec=pltpu.PrefetchScalarGridSpec(
            num_scalar_prefetch=3,
            grid=(nblk,),
            in_specs=[
                pl.BlockSpec((tm, d), lambda i, *_: (i, 0)),
                pl.BlockSpec((tm, LANES), lambda i, *_: (i, 0)),
                pl.BlockSpec((1, tm), lambda i, *_: (0, i)),
            ],
            out_specs=[any_spec, any_spec, pl.BlockSpec((1, tm), lambda i, *_: (0, i))],
            scratch_shapes=[pltpu.VMEM((2, n_loc, d), f32), pltpu.VMEM((2, n_loc, LANES), f32),
                            pltpu.VMEM((tile, d), f32), pltpu.VMEM((tile, LANES), f32),
                            pltpu.SemaphoreType.DMA((3, 2))],
        ),
        out_shape=[jax.ShapeDtypeStruct((n_sorted, d), f32), jax.ShapeDtypeStruct((n_sorted, LANES), f32),
                   jax.ShapeDtypeStruct((1, n), jnp.int32)],
        compiler_params=_cparams(("arbitrary",)),
        name="moe_dispatch",
    )(cnt_flat, goff, fill, x1, gate, best)

    ys = pl.pallas_call(
        _experts_kernel,
        grid_spec=pltpu.PrefetchScalarGridSpec(
            num_scalar_prefetch=2,
            grid=(n_tiles,),
            in_specs=[
                pl.BlockSpec((tile, d), lambda t, *_: (t, 0)),
                pl.BlockSpec((tile, LANES), lambda t, *_: (t, 0)),
                pl.BlockSpec((EXPERTS_PER_GROUP, d, ff), lambda t, tg_, tv_: (tg_[t], 0, 0)),
                pl.BlockSpec((EXPERTS_PER_GROUP, d, ff), lambda t, tg_, tv_: (tg_[t], 0, 0)),
                pl.BlockSpec((EXPERTS_PER_GROUP, ff, d), lambda t, tg_, tv_: (tg_[t], 0, 0)),
            ],
            out_specs=pl.BlockSpec((tile, d), lambda t, *_: (t, 0)),
        ),
        out_shape=jax.ShapeDtypeStruct((n_sorted, d), f32),
        compiler_params=_cparams(("arbitrary",)),
        name="moe_experts",
    )(tg, tv, xs, gs, wg, wu, wd)

    return pl.pallas_call(
        functools.partial(_combine_kernel, alpha),
        grid_spec=pltpu.PrefetchScalarGridSpec(
            num_scalar_prefetch=2,
            grid=(nblk,),
            in_specs=[
                pl.BlockSpec((tm, d), lambda i, *_: (i, 0)),
                pl.BlockSpec((1, tm), lambda i, *_: (0, i)),
                pl.BlockSpec(g.shape, lambda i, *_: (0, 0)),
                pl.BlockSpec(b.shape, lambda i, *_: (0, 0)),
                any_spec,
            ],
            out_specs=pl.BlockSpec((tm, d), lambda i, *_: (i, 0)),
            scratch_shapes=[pltpu.VMEM((2, n_loc, d), f32), pltpu.SemaphoreType.DMA((2,))],
        ),
        out_shape=jax.ShapeDtypeStruct((n, d), f32),
        compiler_params=_cparams(("arbitrary",)),
        name="moe_combine",
    )(cnt_flat, goff, x1, pos, g, b, ys)


def _pad_cols(w, n):
    return jnp.pad(w, ((0, 0), (0, n - w.shape[1])))


def _row(v, n=None):
    v = v.reshape(1, -1)
    return v if n is None else _pad_cols(v, n)


def kernel(x, w_in, gla_w_alpha, gla_b_alpha, gla_norm_g, ssd_conv_w, ssd_conv_b, ssd_dt_bias, ssd_A_log, ssd_D, ssd_norm_g, rwkv_mu, rwkv_w0, rwkv_w2, rwkv_a0, rwkv_a2, rwkv_g2, rwkv_k_k, rwkv_k_a, rwkv_r_k, rwkv_ln_g, rwkv_ln_b, w_out, ln1_g, ln1_b, router_w, router_b, exp_w_gate, exp_w_up, exp_w_down, ln2_g, ln2_b):
    nb, t, d = x.shape
    depth = w_in.shape[0]
    alpha = float((2 * depth) ** 0.25)
    gla_cols = 4 * GLA_WIDTH + GLA_GATE_RANK
    ssd_cols = SSD_WIDTH + SSD_XBC + SSD_HEADS
    rwt = router_w.T
    rb = router_b.reshape(-1, 1)
    n_exp, _, ff = exp_w_gate.shape[1:]
    expert_views = (exp_w_gate.reshape(depth, n_exp * d, ff), exp_w_up.reshape(depth, n_exp * d, ff),
                    exp_w_down.reshape(depth, n_exp * ff, d))
    w_in_t =jnp.swapaxes(w_in, 1, 2)
    x2 = x.reshape(nb * t, d)
    for l in range(depth):
        wg, ws, wr = _regroup_w_in(w_in_t, l, (gla_cols, ssd_cols, RWKV_COLS), (GLA_COLS_PAD, SSD_COLS_PAD, RWKV_COLS))
        pg, pr, os_ = _in_proj_ssd(x2.reshape(nb, t, d), wg, ws, wr, ssd_conv_w[l], _row(ssd_conv_b[l]),
                                   _row(ssd_dt_bias[l], LANES), _row(ssd_A_log[l], LANES),
                                   _row(jnp.repeat(ssd_D[l], HEAD_DIM)), _row(ssd_norm_g[l]))
        gla_params = (jnp.pad(gla_w_alpha[l], ((0, LANES - GLA_GATE_RANK), (0, 0))), _row(gla_b_alpha[l]),
                      _row(jnp.tile(gla_norm_g[l], GLA_WIDTH // HEAD_DIM)))
        rwkv_params = (_row(rwkv_mu[l]), _row(rwkv_w0[l]), rwkv_w2[l], _row(rwkv_a0[l]), rwkv_a2[l], rwkv_g2[l],
                       _row(rwkv_k_k[l]), _row(rwkv_k_a[l]), _row(rwkv_r_k[l]), _row(rwkv_ln_g[l]),
                       _row(rwkv_ln_b[l]))
        og, or_, wg_b, wu_b, wd_b = _gla_rwkv(pg, gla_params, pr, rwkv_params, l, expert_views)
        x1, gate, best, cnt = _out_proj(alpha, x2, og.reshape(nb * t, -1), os_.reshape(nb * t, -1),
                                        or_.reshape(nb * t, -1), _bf(w_out[l]), _row(ln1_g[l]), _row(ln1_b[l]),
                                        rwt, rb)
        x2 = _moe(alpha, x1, gate, best, cnt, wg_b.reshape(n_exp, d, ff), wu_b.reshape(n_exp, d, ff),
                  wd_b.reshape(n_exp, ff, d), _row(ln2_g[l]), _row(ln2_b[l]))
    return x2.reshape(nb, t, d)
```

```python
import functools
import math

import jax
import jax.numpy as jnp
from jax import lax
from jax.experimental import pallas as pl
from jax.experimental.pallas import tpu as pltpu

f32 = jnp.float32
bf16 = jnp.bfloat16

LANES = 128
SUBLANES = 8
HEAD_DIM = 64
GLA_WIDTH = 256
GLA_GATE_RANK = 16
GLA_GATE_TAU = 16.0
GLA_CHUNK = 16
GLA_BLOCK = 128
GLA_COLS_PAD = 4 * GLA_WIDTH + LANES
SSD_WIDTH = 512
SSD_HEADS = 8
SSD_GROUPS = 2
SSD_STATE = 64
SSD_CONV = 4
SSD_XBC = SSD_WIDTH + 2 * SSD_GROUPS * SSD_STATE
SSD_BLOCK = 128
SSD_COLS_PAD = SSD_WIDTH + SSD_XBC + LANES
RWKV_WIDTH = 256
RWKV_CHUNK = 64
RWKV_GN_EPS = 64e-5
RWKV_COLS = 3 * RWKV_WIDTH + 64 + 64 + 128
N_EXPERTS = 16
N_EXPERT_GROUPS = 4
EXPERTS_PER_GROUP = 4
LN_EPS = 1e-5

PROJ_CHUNK = 512
PROJ_PACE = 8
OUT_PROJ_ROWS = 1024
MOE_SORT_ROWS = 512
MOE_TILE = 512
SEG_ALIGN = 8
VMEM_LIMIT = 56 * 1024 * 1024


def _bf(x):
    return x.astype(bf16)


def _dot(a, b):
    return jnp.dot(_bf(a), _bf(b), preferred_element_type=f32)


def _dot_nt(a, b):
    return lax.dot_general(_bf(a), _bf(b), (((1,), (1,)), ((), ())), preferred_element_type=f32)


def _dot_tn(a, b):
    return lax.dot_general(_bf(a), _bf(b), (((0,), (0,)), ((), ())), preferred_element_type=f32)


def _split(x, n):
    parts, r = [], x
    for _ in range(n):
        p = r.astype(bf16)
        parts.append(p)
        r = r - p.astype(f32)
    return parts


def _dot_sel_rhs(a, sel, n=3):
    out = None
    for p in _split(a, n):
        t = jnp.dot(p, sel, preferred_element_type=f32)
        out = t if out is None else out + t
    return out


def _dot_sel_lhs(sel, b, n=3):
    out = None
    for p in _split(b, n):
        t = jnp.dot(sel, p, preferred_element_type=f32)
        out = t if out is None else out + t
    return out


def _dot_hi(a, b):
    ah, al = _split(a, 2)
    bh, bl = _split(b, 2)
    d = functools.partial(jnp.dot, preferred_element_type=f32)
    return d(ah, bh) + d(al, bh) + d(ah, bl)


def _iota(shape, dim):
    return lax.broadcasted_iota(jnp.int32, shape, dim)


def _block_ones(n, blk):
    return (_iota((n, n), 0) // blk == _iota((n, n), 1) // blk)


def _silu(x):
    return x * jax.nn.sigmoid(x)


def _softplus(x):
    return jnp.maximum(x, 0.0) + jnp.log1p(jnp.exp(-jnp.abs(x)))


def _layer_norm(y, g, b, eps):
    mu = jnp.mean(y, axis=-1, keepdims=True)
    yc = y - mu
    var = jnp.mean(yc * yc, axis=-1, keepdims=True)
    return yc * lax.rsqrt(var + eps) * g + b


def _cparams(sem):
    return pltpu.CompilerParams(dimension_semantics=sem, vmem_limit_bytes=VMEM_LIMIT)


def _regroup_kernel(splits, w_ref, *o_refs):
    start = 0
    for width, o_ref in zip(splits, o_refs):
        part = w_ref[0, start:start + width, :]
        pad = o_ref.shape[1] - width
        if pad:
            part = jnp.concatenate([part, jnp.zeros((pad, part.shape[1]), part.dtype)], axis=0)
        o_ref[...] = _bf(part.T)
        start += width


def _regroup_w_in(w_in_t, layer, splits, padded):
    _, n_in, d = w_in_t.shape
    return pl.pallas_call(
        functools.partial(_regroup_kernel, splits),
        grid=(1,),
        in_specs=[pl.BlockSpec((1, n_in, d), lambda i: (layer, 0, 0))],
        out_specs=[pl.BlockSpec((d, p), lambda i: (0, 0)) for p in padded],
        out_shape=[jax.ShapeDtypeStruct((d, p), bf16) for p in padded],
        compiler_params=_cparams(("arbitrary",)),
        name="regroup_w_in",
    )(w_in_t)


def _gla_steps(p_ref, wa_ref, ba_ref, ng_ref, o_ref, st_ref, t_ref, d_ref, sx_ref):
    nb, tb, _ = p_ref.shape
    w, c = GLA_WIDTH, GLA_CHUNK
    nc = tb // c
    npt = w // LANES
    Q, K, V, CUM, OUT = range(5)

    @pl.when(pl.program_id(0) == 0)
    def _():
        st_ref[...] = jnp.zeros_like(st_ref)

    bones = _block_ones(w, HEAD_DIM).astype(bf16)
    pair_blk = _block_ones(LANES, HEAD_DIM)
    r_i, c_i = _iota((tb, tb), 0), _iota((tb, tb), 1)
    tri = ((r_i >= c_i) & (r_i // c == c_i // c)).astype(bf16)
    wa_h, wa_l = _split(wa_ref[...], 2)
    dd = functools.partial(jnp.dot, preferred_element_type=f32)

    for b in range(nb):
        a_h, a_l = _split(p_ref[b, :, 4 * w:4 * w + LANES], 2)
        z = dd(a_h, wa_h) + dd(a_l, wa_h) + dd(a_h, wa_l) + ba_ref[...]
        log_a = -_softplus(-z) * (1.0 / GLA_GATE_TAU)
        cum = _dot_sel_lhs(tri, log_a, 3)
        for lt in range(npt):
            lanes = slice(lt * LANES, (lt + 1) * LANES)
            t_ref[b, Q, lt] = p_ref[b, :, lanes] * (HEAD_DIM ** -0.5)
            t_ref[b, K, lt] = p_ref[b, :, w + lt * LANES:w + (lt + 1) * LANES]
            t_ref[b, V, lt] = p_ref[b, :, 2 * w + lt * LANES:2 * w + (lt + 1) * LANES]
            t_ref[b, CUM, lt] = cum[:, lanes]
        yield

    def ld(b, j, off):
        return jnp.concatenate([t_ref[b, j, lt, pl.ds(off, nc, stride=c), :] for lt in range(npt)], axis=1)

    for b in range(nb):
        qt = [ld(b, Q, o) for o in range(c)]
        kt = [ld(b, K, o) for o in range(c)]
        ct = [ld(b, CUM, o) for o in range(c)]
        pair = 0
        for t in range(c):
            for s in range(t + 1):
                qk = qt[t] * kt[s]
                d_ref[b, pair * nc:(pair + 1) * nc, :] = qk if s == t else qk * jnp.exp(ct[t] - ct[s])
                pair += 1
            if t % 4 == 3:
                yield
    for b in range(nb):
        sx_ref[b] = jnp.dot(_bf(d_ref[b]), bones, preferred_element_type=f32)
        yield
    for b in range(nb):
        vt = [ld(b, V, o) for o in range(c)]
        pair = 0
        for t in range(c):
            acc = None
            for s in range(t + 1):
                term = sx_ref[b, pair * nc:(pair + 1) * nc, :] * vt[s]
                acc = term if acc is None else acc + term
                pair += 1
            for lt in range(npt):
                t_ref[b, OUT, lt, pl.ds(t, nc, stride=c), :] = acc[:, lt * LANES:(lt + 1) * LANES]
            if t % 8 == 7:
                yield

    for ci in range(nc):
        sl = slice(ci * c, (ci + 1) * c)
        last = slice((ci + 1) * c - 1, (ci + 1) * c)
        for b in range(nb):
            for lt in range(npt):
                cum_c = t_ref[b, CUM, lt, sl, :]
                cum_last = t_ref[b, CUM, lt, last, :]
                st = st_ref[b, lt]
                t_ref[b, OUT, lt, sl, :] += _dot_nt(t_ref[b, Q, lt, sl, :] * jnp.exp(cum_c), st)
                upd = _dot_tn(t_ref[b, V, lt, sl, :], t_ref[b, K, lt, sl, :] * jnp.exp(cum_last - cum_c))
                st_ref[b, lt] = st * jnp.exp(cum_last) + jnp.where(pair_blk, upd, 0.0)
        yield

    o = jnp.concatenate(
        [jnp.concatenate([t_ref[b, OUT, lt] for lt in range(npt)], axis=1) for b in range(nb)], axis=0)
    ms = _dot_sel_rhs(o * o, bones, 2) * (1.0 / HEAD_DIM)
    o = o * lax.rsqrt(ms + LN_EPS) * ng_ref[...]
    for b in range(nb):
        o_ref[b] = o[b * tb:(b + 1) * tb] * _silu(p_ref[b, :, 3 * w:4 * w])


def _round_robin(gens):
    live = list(gens)
    while live:
        for gen in list(live):
            if next(gen, StopIteration) is StopIteration:
                live.remove(gen)
            else:
                yield


def _ssd_steps(b, p_ref, halo_ref, cw_ref, cb_ref, dtb_ref, alog_ref, dsk_ref, ng_ref, o_ref, st_ref):
    L = p_ref.shape[1]
    W, M, G = SSD_WIDTH, SSD_STATE, SSD_GROUPS
    gw = W // G

    @pl.when(pl.program_id(0) == 0)
    def _():
        st_ref[b] = jnp.zeros(st_ref.shape[1:], f32)
        halo_ref[b] = jnp.zeros(halo_ref.shape[1:], f32)

    z = p_ref[b, :, 0:W]
    xbc = p_ref[b, :, W:W + SSD_XBC]
    dt_raw = p_ref[b, :, W + SSD_XBC:W + SSD_XBC + LANES]
    xx = jnp.concatenate([halo_ref[b], xbc], axis=0)
    halo_ref[b] = xbc[L - SUBLANES:L]
    conv = cb_ref[...]
    for j in range(SSD_CONV):
        off = SUBLANES - (SSD_CONV - 1) + j
        conv = conv + cw_ref[j:j + 1, :] * xx[off:off + L]
    act = _silu(conv)
    xs = act[:, 0:W]
    bm = act[:, W:W + G * M]
    cm = act[:, W + G * M:W + 2 * G * M]
    yield

    dt = _softplus(dt_raw + dtb_ref[...])
    a_neg = jnp.where(_iota((1, LANES), 1) < SSD_HEADS, -jnp.exp(alog_ref[...]), 0.0)
    da = dt * a_neg
    r_i, c_i = _iota((L, L), 0), _iota((L, L), 1)
    causal = r_i >= c_i
    cs = _dot_sel_lhs(causal.astype(bf16), da, 3)
    expand = (_iota((LANES, W), 0) == _iota((LANES, W), 1) // HEAD_DIM).astype(bf16)
    dtx = _dot_sel_rhs(dt, expand, 2)
    csx = _dot_sel_rhs(cs, expand, 2)
    eye_h = (_iota((SSD_HEADS, LANES), 0) == _iota((SSD_HEADS, LANES), 1)).astype(bf16)
    cs_t = None
    for p in _split(cs, 2):
        t = lax.dot_general(eye_h, p, (((1,), (1,)), ((), ())), preferred_element_type=f32)
        cs_t = t if cs_t is None else cs_t + t
    xdt = xs * dtx
    yield

    st = st_ref[b]
    y_parts = []
    for gi in range(G):
        bm_g = bm[:, gi * M:(gi + 1) * M]
        cm_g = cm[:, gi * M:(gi + 1) * M]
        cb = _dot_nt(cm_g, bm_g)
        for e in range(SSD_HEADS // G):
            h = gi * (SSD_HEADS // G) + e
            seg = jnp.where(causal, jnp.exp(jnp.broadcast_to(cs[:, h:h + 1], (L, L)) - cs_t[h:h + 1, :]), 0.0)
            y_parts.append(_dot(cb * seg, xdt[:, h * HEAD_DIM:(h + 1) * HEAD_DIM]))
            if e % 2:
                yield
    y = jnp.concatenate(y_parts, axis=1)
    y_inter = jnp.concatenate(
        [_dot(cm[:, gi * M:(gi + 1) * M], st[:, gi * gw:(gi + 1) * gw]) for gi in range(G)], axis=1)
    y = y + y_inter * jnp.exp(csx)
    yield

    cs_last = csx[L - 1:L, :]
    xdtw = xdt * jnp.exp(cs_last - csx)
    d_state = jnp.concatenate(
        [_dot_tn(bm[:, gi * M:(gi + 1) * M], xdtw[:, gi * gw:(gi + 1) * gw]) for gi in range(G)], axis=1)
    st_ref[b] = st * jnp.exp(cs_last) + d_state
    yield

    y = (y + dsk_ref[...] * xs) * _silu(z)
    outs = []
    for gi in range(G):
        yg = y[:, gi * gw:(gi + 1) * gw]
        ms = jnp.mean(yg * yg, axis=-1, keepdims=True)
        outs.append(yg * lax.rsqrt(ms + LN_EPS))
    o_ref[b] = jnp.concatenate(outs, axis=1) * ng_ref[...]


def _inproj_ssd_kernel(x_ref, wg_ref, ws_ref, wr_ref, cw_ref, cb_ref, dtb_ref, alog_ref, dsk_ref, ng_ref,
                       pg_ref, pr_ref, os_ref, ps_ref, halo_ref, st_ref):
    nb, L, d = x_ref.shape
    xb = _bf(x_ref[...].reshape(nb * L, d))
    ps_ref[...] = jnp.dot(xb, ws_ref[...], preferred_element_type=f32).reshape(ps_ref.shape)

    def other_columns():
        for w_ref, o_ref in ((wg_ref, pg_ref), (wr_ref, pr_ref)):
            n = w_ref.shape[1]
            for c0 in range(0, n, PROJ_CHUNK):
                c1 = min(c0 + PROJ_CHUNK, n)
                o_ref[:, :, c0:c1] = jnp.dot(xb, w_ref[:, c0:c1], preferred_element_type=f32).reshape(nb, L, c1 - c0)
                for _ in range(PROJ_PACE):
                    yield

    ssd = _round_robin([_ssd_steps(b, ps_ref, halo_ref, cw_ref, cb_ref, dtb_ref, alog_ref, dsk_ref,
                                   ng_ref, os_ref, st_ref) for b in range(nb)])
    for _ in _round_robin([ssd, other_columns()]):
        pass


def _in_proj_ssd(x3, wg, ws, wr, cw, cb, dtb, alog, dsk, ng):
    nb, t, d = x3.shape
    L = SSD_BLOCK
    full2 = lambda a: pl.BlockSpec(a.shape, lambda i: (0, 0))
    blk = lambda c: pl.BlockSpec((nb, L, c), lambda i: (0, i, 0))
    return pl.pallas_call(
        _inproj_ssd_kernel,
        grid=(t // L,),
        in_specs=[blk(d)] + [full2(a) for a in (wg, ws, wr, cw, cb, dtb, alog, dsk, ng)],
        out_specs=[blk(wg.shape[1]), blk(wr.shape[1]), blk(SSD_WIDTH)],
        out_shape=[jax.ShapeDtypeStruct((nb, t, wg.shape[1]), f32), jax.ShapeDtypeStruct((nb, t, wr.shape[1]), f32),
                   jax.ShapeDtypeStruct((nb, t, SSD_WIDTH), f32)],
        scratch_shapes=[pltpu.VMEM((nb, L, ws.shape[1]), f32), pltpu.VMEM((nb, SUBLANES, SSD_XBC), f32),
                        pltpu.VMEM((nb, SSD_STATE, SSD_WIDTH), f32)],
        compiler_params=_cparams(("arbitrary",)),
        name="in_proj_ssd",
    )(x3, wg, ws, wr, cw, cb, dtb, alog, dsk, ng)


def _rwkv_steps(sub, p_ref, prev_ref, mu_ref, w0_ref, w2_ref, a0_ref, a2_ref, g2_ref, kk_ref, ka_ref, rk_ref,
                lng_ref, lnb_ref, o_ref, zt_ref):
    nb = p_ref.shape[0]
    C, W = RWKV_CHUNK, RWKV_WIDTH
    H = LANES // HEAD_DIM
    npair = W // LANES
    i = pl.program_id(0)
    r0 = sub * C

    if sub == 0:
        @pl.when(i == 0)
        def _():
            zt_ref[...] = jnp.zeros_like(zt_ref)

    bones = _block_ones(W, HEAD_DIM).astype(bf16)
    pair_blk = _block_ones(LANES, HEAD_DIM)
    n = H * C
    rr, cc = _iota((n, n), 0), _iota((n, n), 1)
    same = rr // C == cc // C
    strict = same & (rr > cc)
    lower = same & (rr >= cc)
    eye = jnp.where(rr == cc, 1.0, 0.0)

    def stack(t):
        return jnp.where(pair_blk, jnp.concatenate([t] * H, axis=0), 0.0)

    rows = range(nb)
    nr = nb * C
    cols = p_ref[:, r0:r0 + C, :].reshape(nr, p_ref.shape[2])

    def row_before(b):
        return p_ref[b, r0 - 1:r0, :] if sub else jnp.where(i > 0, prev_ref[b, SUBLANES - 1:SUBLANES, :], 0.0)

    shifted = jnp.concatenate(
        [piece for b in rows for piece in (row_before(b), p_ref[b, r0:r0 + C - 1, :])], axis=0)
    cols = cols + (shifted - cols) * mu_ref[...]
    r = cols[:, 0:W]
    k = cols[:, W:2 * W]
    v = cols[:, 2 * W:3 * W]
    w_lr = cols[:, 3 * W:3 * W + 64]
    a_lr = cols[:, 3 * W + 64:3 * W + 128]
    g_lr = cols[:, 3 * W + 128:3 * W + 256]

    wv = -_softplus(-(w0_ref[...] + _dot_hi(jnp.tanh(w_lr), w2_ref[...]))) - 0.5
    lw = -jnp.exp(wv)
    a = jax.nn.sigmoid(a0_ref[...] + _dot_hi(a_lr, a2_ref[...]))
    g = _dot_hi(jax.nn.sigmoid(g_lr), g2_ref[...])
    yield

    kk = k * kk_ref[...]
    k2 = k * (1.0 + (a - 1.0) * ka_ref[...])
    head_sums = _dot_sel_rhs(jnp.concatenate([kk * kk, r * k2 * rk_ref[...]], axis=0), bones, 2)
    kk = kk / jnp.maximum(jnp.sqrt(head_sums[:nr]), 1e-12)
    bonus_v = head_sums[nr:] * v
    bvec = kk * a
    yield

    rb_i, cb_i = _iota((nr, nr), 0), _iota((nr, nr), 1)
    tri = ((rb_i >= cb_i) & (rb_i // C == cb_i // C)).astype(bf16)
    cum = _dot_sel_lhs(tri, lw, 3)
    cum_last = [cum[(b + 1) * C - 1:(b + 1) * C, :] for b in rows]
    e_neg = jnp.exp(-cum)
    e_rem = jnp.exp(jnp.concatenate([jnp.broadcast_to(cl, (C, W)) for cl in cum_last], axis=0) - cum)
    per_token = dict(aw=-kk * jnp.exp(cum - lw), rw=r * jnp.exp(cum), bn=bvec * e_neg, kn=k2 * e_neg,
                     bt=bvec * e_rem, kt=k2 * e_rem, vs=v)
    chains = [(b, p) for b in rows for p in range(npair)]
    q = [dict({name: stack(t[b * C:(b + 1) * C, p * LANES:(p + 1) * LANES]) for name, t in per_token.items()},
              wc=jnp.exp(cum_last[b][:, p * LANES:(p + 1) * LANES])) for b, p in chains]
    ids = range(len(chains))
    yield

    a_ab = [jnp.where(strict, _dot_nt(q[c]["aw"], q[c]["bn"]), 0.0) for c in ids]
    a_ak = [jnp.where(strict, _dot_nt(q[c]["aw"], q[c]["kn"]), 0.0) for c in ids]
    yield
    a_rb = [jnp.where(lower, _dot_nt(q[c]["rw"], q[c]["bn"]), 0.0) for c in ids]
    a_rk = [jnp.where(lower, _dot_nt(q[c]["rw"], q[c]["kn"]), 0.0) for c in ids]
    yield

    pw = a_ab
    tinv = [eye + a_ab[c] for c in ids]
    for _ in range(int(math.log2(C)) - 1):
        pw = [_dot(pw[c], pw[c]) for c in ids]
        yield
        tinv = [tinv[c] + _dot(tinv[c], pw[c]) for c in ids]
        yield

    yield "state"
    zt = [zt_ref[b, p] for b, p in chains]
    rhs = [_dot_nt(q[c]["aw"], zt[c]) + _dot(a_ak[c], q[c]["vs"]) for c in ids]
    yield
    y0 = [_dot_nt(q[c]["rw"], zt[c]) + _dot(a_rk[c], q[c]["vs"]) for c in ids]
    yield
    u = [_dot(tinv[c], rhs[c]) for c in ids]
    yield
    y_st = [y0[c] + _dot(a_rb[c], u[c]) for c in ids]
    yield
    for c, (b, p) in enumerate(chains):
        zt_ref[b, p] = zt[c] * q[c]["wc"] + _dot_tn(u[c], q[c]["bt"]) + _dot_tn(q[c]["vs"], q[c]["kt"])
    yield

    inv_n = 1.0 / HEAD_DIM
    ys = []
    for b in rows:
        tiles = []
        for p in range(npair):
            y_c = y_st[b * npair + p]
            y = y_c[0:C]
            for h in range(1, H):
                y = y + y_c[h * C:(h + 1) * C]
            tiles.append(y)
        ys.append(jnp.concatenate(tiles, axis=1))
    y = jnp.concatenate(ys, axis=0)
    yc = y - _dot_sel_rhs(y, bones, 2) * inv_n
    var_h = _dot_sel_rhs(yc * yc, bones, 2) * inv_n
    y = yc * lax.rsqrt(var_h + RWKV_GN_EPS) * lng_ref[...] + lnb_ref[...]
    o_ref[:, r0:r0 + C, :] = ((y + bonus_v) * g).reshape(nb, C, W)


N_GLA_IN, N_RWKV_IN = 4, 13


N_CAST = 3


def _gla_rwkv_kernel(*refs):
    gla_in = refs[:N_GLA_IN]
    rwkv_in = refs[N_GLA_IN:N_GLA_IN + N_RWKV_IN]
    k = N_GLA_IN + N_RWKV_IN
    cast_in = refs[k:k + N_CAST]
    og_ref, or_ref = refs[k + N_CAST:k + N_CAST + 2]
    cast_out = refs[k + N_CAST + 2:k + 2 * N_CAST + 2]
    st_ref, t_ref, d_ref, sx_ref, zt_ref = refs[k + 2 * N_CAST + 2:]

    def casts():
        for src, dst in zip(cast_in, cast_out):
            dst[...] = _bf(src[...])
            yield

    n_sub = rwkv_in[0].shape[1] // RWKV_CHUNK

    def rwkv_chunks():
        gens = [_rwkv_steps(sub, *rwkv_in, or_ref, zt_ref) for sub in range(n_sub)]
        parked, done = set(), set()
        while len(done) < n_sub:
            for sub, gen in enumerate(gens):
                if sub in done or (sub in parked and any(s not in done for s in range(sub))):
                    continue
                step = next(gen, StopIteration)
                if step is StopIteration:
                    done.add(sub)
                elif step == "state":
                    parked.add(sub)
                yield

    rwkv = rwkv_chunks()
    gla = _gla_steps(*gla_in, og_ref, st_ref, t_ref, d_ref, sx_ref)
    for _ in _round_robin([rwkv, gla, casts()]):
        pass


def _gla_rwkv(pg, gla_params, pr, rwkv_params, layer, expert_weights):
    nb, t, gcols = pg.shape
    rcols = pr.shape[2]
    tb = GLA_BLOCK
    n_steps = t // tb
    npt = GLA_WIDTH // LANES
    pair_rows = (GLA_CHUNK * (GLA_CHUNK + 1) // 2) * (tb // GLA_CHUNK)
    full2 = lambda a: pl.BlockSpec(a.shape, lambda i: (0, 0))
    blk = lambda c: pl.BlockSpec((nb, tb, c), lambda i: (0, i, 0))
    assert len(gla_params) + 1 == N_GLA_IN and len(rwkv_params) + 2 == N_RWKV_IN and len(expert_weights) == N_CAST
    cast_rows = [w.shape[1] // n_steps for w in expert_weights]
    return pl.pallas_call(
        _gla_rwkv_kernel,
        grid=(n_steps,),
        in_specs=[blk(gcols)] + [full2(p) for p in gla_params]
        + [blk(rcols), pl.BlockSpec((nb, SUBLANES, rcols),
                                    lambda i: (0, jnp.maximum(i * (tb // SUBLANES) - 1, 0), 0))]
        + [full2(p) for p in rwkv_params]
        + [pl.BlockSpec((None, r, w.shape[2]), lambda i: (layer, i, 0)) for r, w in zip(cast_rows, expert_weights)],
        out_specs=[blk(GLA_WIDTH), blk(RWKV_WIDTH)]
        + [pl.BlockSpec((r, w.shape[2]), lambda i: (i, 0)) for r, w in zip(cast_rows, expert_weights)],
        out_shape=[jax.ShapeDtypeStruct((nb, t, GLA_WIDTH), f32), jax.ShapeDtypeStruct((nb, t, RWKV_WIDTH), f32)]
        + [jax.ShapeDtypeStruct(w.shape[1:], bf16) for w in expert_weights],
        scratch_shapes=[
            pltpu.VMEM((nb, npt, LANES, LANES), f32),
            pltpu.VMEM((nb, 5, npt, tb, LANES), f32),
            pltpu.VMEM((nb, pair_rows, GLA_WIDTH), f32),
            pltpu.VMEM((nb, pair_rows, GLA_WIDTH), f32),
            pltpu.VMEM((nb, RWKV_WIDTH // LANES, LANES, LANES), f32),
        ],
        compiler_params=_cparams(("arbitrary",)),
        name="gla_rwkv_mixers",
    )(pg, *gla_params, pr, pr, *rwkv_params, *expert_weights)


def _route(logits, rb):
    tm = logits.shape[1]
    scores = jax.nn.sigmoid(logits)
    sel = scores + rb
    srow = [sel[e:e + 1, :] for e in range(N_EXPERTS)]
    crow = [scores[e:e + 1, :] for e in range(N_EXPERTS)]

    def top2_sum(v0, v1, v2, v3):
        m01, n01 = jnp.maximum(v0, v1), jnp.minimum(v0, v1)
        m23, n23 = jnp.maximum(v2, v3), jnp.minimum(v2, v3)
        return jnp.maximum(m01, m23) + jnp.maximum(jnp.minimum(m01, m23), jnp.maximum(n01, n23))

    gscore = [top2_sum(*srow[4 * gi:4 * gi + 4]) for gi in range(N_EXPERT_GROUPS)]
    best = jnp.zeros((1, tm), jnp.int32)
    best_v = gscore[0]
    for gi in range(1, N_EXPERT_GROUPS):
        better = gscore[gi] > best_v
        best = jnp.where(better, gi, best)
        best_v = jnp.where(better, gscore[gi], best_v)
    sv, cv = [], []
    for j in range(EXPERTS_PER_GROUP):
        s_j, c_j = srow[j], crow[j]
        for gi in range(1, N_EXPERT_GROUPS):
            s_j = jnp.where(best == gi, srow[4 * gi + j], s_j)
            c_j = jnp.where(best == gi, crow[4 * gi + j], c_j)
        sv.append(s_j)
        cv.append(c_j)
    j1 = jnp.zeros((1, tm), jnp.int32)
    v1 = sv[0]
    for j in range(1, EXPERTS_PER_GROUP):
        better = sv[j] > v1
        j1 = jnp.where(better, j, j1)
        v1 = jnp.where(better, sv[j], v1)
    j2 = jnp.full((1, tm), -1, jnp.int32)
    v2 = jnp.full((1, tm), -jnp.inf, f32)
    for j in range(EXPERTS_PER_GROUP):
        better = (j1 != j) & ((j2 < 0) | (sv[j] > v2))
        j2 = jnp.where(better, j, j2)
        v2 = jnp.where(better, sv[j], v2)
    w1 = cv[0]
    w2 = cv[0]
    for j in range(1, EXPERTS_PER_GROUP):
        w1 = jnp.where(j1 == j, cv[j], w1)
        w2 = jnp.where(j2 == j, cv[j], w2)
    wsum = w1 + w2
    g1, g2 = w1 / wsum, w2 / wsum
    rows = []
    for e in range(N_EXPERTS):
        gi, j = divmod(e, EXPERTS_PER_GROUP)
        in_g = best == gi
        rows.append(jnp.where(in_g & (j1 == j), g1, jnp.where(in_g & (j2 == j), g2, 0.0)))
    gate_t = jnp.concatenate(rows, axis=0)
    return gate_t, best


def _outproj_kernel(alpha, x_ref, og_ref, os_ref, or_ref, wo_ref, g_ref, b_ref, rwt_ref, rb_ref,
                    x1_ref, gate_ref, best_ref, cnt_ref):
    tm = x_ref.shape[0]
    ts = tm // cnt_ref.shape[0]
    parts = [slice(sb * ts, (sb + 1) * ts) for sb in range(cnt_ref.shape[0])]
    d = functools.partial(jnp.dot, preferred_element_type=f32)
    mix = [d(_bf(og_ref[s, :]), wo_ref[0:GLA_WIDTH, :])
           + d(_bf(os_ref[s, :]), wo_ref[GLA_WIDTH:GLA_WIDTH + SSD_WIDTH, :])
           + d(_bf(or_ref[s, :]), wo_ref[GLA_WIDTH + SSD_WIDTH:, :]) for s in parts]
    x1 = [_layer_norm(alpha * x_ref[s, :] + m, g_ref[...], b_ref[...], LN_EPS) for s, m in zip(parts, mix)]
    for s, v in zip(parts, x1):
        x1_ref[s, :] = v

    wh, wl = _split(rwt_ref[...], 2)
    nt = lambda a, b: lax.dot_general(a, b, (((1,), (1,)), ((), ())), preferred_element_type=f32)
    logits = []
    for v in x1:
        xh, xl = _split(v, 2)
        logits.append(nt(wh, xh) + nt(wl, xh) + nt(wh, xl))
    routed = [_route(lg, rb_ref[...]) for lg in logits]
    eye = (_iota((N_EXPERTS, LANES), 0) == _iota((N_EXPERTS, LANES), 1)).astype(bf16)
    lane = _iota((1, LANES), 1)
    for sb, (s, (gate_t, best)) in enumerate(zip(parts, routed)):
        gate = None
        for p in _split(gate_t, 3):
            t = lax.dot_general(p, eye, (((0,), (0,)), ((), ())), preferred_element_type=f32)
            gate = t if gate is None else gate + t
        gate_ref[s, :] = gate
        best_ref[:, s] = best
        cnt = jnp.zeros((1, LANES), jnp.int32)
        for gi in range(N_EXPERT_GROUPS):
            n_g = jnp.sum((best == gi).astype(jnp.int32), axis=-1, keepdims=True)
            cnt = jnp.where(lane == gi, n_g, cnt)
        cnt_ref[sb] = cnt


def _out_proj(alpha, x2, og, os_, or_, wo, g, b, rwt, rb):
    n, d = x2.shape
    tm = min(OUT_PROJ_ROWS, n)
    ts = min(MOE_SORT_ROWS, tm)
    full2 = lambda a: pl.BlockSpec(a.shape, lambda i: (0, 0))
    rows = lambda c: pl.BlockSpec((tm, c), lambda i: (i, 0))
    return pl.pallas_call(
        functools.partial(_outproj_kernel, alpha),
        grid=(n // tm,),
        in_specs=[rows(d), rows(og.shape[1]), rows(os_.shape[1]), rows(or_.shape[1]),
                  full2(wo), full2(g), full2(b), full2(rwt), full2(rb)],
        out_specs=[rows(d), rows(LANES), pl.BlockSpec((1, tm), lambda i: (0, i)),
                   pl.BlockSpec((tm // ts, 1, LANES), lambda i: (i, 0, 0))],
        out_shape=[jax.ShapeDtypeStruct((n, d), f32), jax.ShapeDtypeStruct((n, LANES), f32),
                   jax.ShapeDtypeStruct((1, n), jnp.int32), jax.ShapeDtypeStruct((n // ts, 1, LANES), jnp.int32)],
        compiler_params=_cparams(("arbitrary",)),
        name="out_proj_router",
    )(x2, og, os_, or_, wo, g, b, rwt, rb)


def _seg_copy(src, dst, sem, src_off, dst_off, rows, max_rows, wait):
    k = SEG_ALIGN
    sizes = []
    while k <= max_rows:
        sizes.append(k)
        k *= 2
    for k in reversed(sizes):
        shift = int(math.log2(k)) + 1

        @pl.when((rows & k) != 0)
        def _():
            done = (rows >> shift) << shift
            cp = pltpu.make_async_copy(
                src.at[pl.ds(pl.multiple_of(src_off + done, SEG_ALIGN), k)],
                dst.at[pl.ds(pl.multiple_of(dst_off + done, SEG_ALIGN), k)], sem)
            if wait:
                cp.wait()
            else:
                cp.start()


def _padded(count):
    return (count + (SEG_ALIGN - 1)) & (-SEG_ALIGN)


def _one_hot_rows(pos, n_rows):
    return jnp.where(_iota((n_rows, pos.shape[1]), 0) == pos, 1.0, 0.0).astype(bf16)


def _dispatch_kernel(cnt_sm, goff_sm, fill_sm, x_ref, gate_ref, best_ref, xs_out, gs_out, pos_ref,
                     xloc, gloc, zx, zg, sems):
    i = pl.program_id(0)
    tm = x_ref.shape[0]
    n_loc = xloc.shape[1]
    slot = i % 2

    def copies(blk, slot_, wait):
        loff = 0
        for gi in range(N_EXPERT_GROUPS):
            rows = _padded(cnt_sm[blk * N_EXPERT_GROUPS + gi])
            dst = goff_sm[blk * N_EXPERT_GROUPS + gi]
            _seg_copy(xloc.at[slot_], xs_out, sems.at[0, slot_], loff, dst, rows, tm, wait)
            _seg_copy(gloc.at[slot_], gs_out, sems.at[1, slot_], loff, dst, rows, tm, wait)
            loff = loff + rows

    best = best_ref[...]
    onehot = (_iota((SUBLANES, tm), 0) == best).astype(bf16)
    earlier = (_iota((tm, tm), 0) < _iota((tm, tm), 1)).astype(bf16)
    rank = jnp.dot(onehot, earlier, preferred_element_type=f32)
    pos = jnp.zeros((1, tm), f32)
    loff = 0
    for gi in range(N_EXPERT_GROUPS):
        pos = jnp.where(best == gi, rank[gi:gi + 1, :] + loff.astype(f32) if gi else rank[gi:gi + 1, :], pos)
        loff = loff + _padded(cnt_sm[i * N_EXPERT_GROUPS + gi])
    pos = pos.astype(jnp.int32)
    pos_ref[...] = pos
    perm = _one_hot_rows(pos, n_loc)
    xloc[slot] = jnp.dot(perm, _bf(x_ref[...]), preferred_element_type=f32)
    gloc[slot] = _dot_sel_lhs(perm, gate_ref[...], 2)
    copies(i, slot, False)

    @pl.when(i > 0)
    def _():
        copies(i - 1, 1 - slot, True)

    @pl.when(i == pl.num_programs(0) - 1)
    def _():
        copies(i, slot, True)
        zx[...] = jnp.zeros_like(zx)
        zg[...] = jnp.zeros_like(zg)
        tile = zx.shape[0]

        def fills(wait):
            for src, dst, sem in ((zx, xs_out, sems.at[2, 0]), (zg, gs_out, sems.at[2, 1])):
                for gi in range(N_EXPERT_GROUPS):
                    _seg_copy(src, dst, sem, 0, fill_sm[gi], fill_sm[N_EXPERT_GROUPS + gi], tile // 2, wait)

                def body(j, carry):
                    off = pl.multiple_of(fill_sm[2 * N_EXPERT_GROUPS] + j * tile, SEG_ALIGN)
                    cp = pltpu.make_async_copy(src, dst.at[pl.ds(off, tile)], sem)
                    if wait:
                        cp.wait()
                    else:
                        cp.start()
                    return carry

                lax.fori_loop(0, fill_sm[2 * N_EXPERT_GROUPS + 1], body, 0)

        fills(False)
        fills(True)


def _experts_kernel(tg_sm, tv_sm, x_ref, gate_ref, wg_ref, wu_ref, wd_ref, y_ref):
    t = pl.program_id(0)

    @pl.when(tv_sm[t] == 0)
    def _():
        y_ref[...] = jnp.zeros_like(y_ref)

    @pl.when(tv_sm[t] > 0)
    def _():
        first = tg_sm[t] * EXPERTS_PER_GROUP
        xb = _bf(x_ref[...])
        gate = gate_ref[...]
        lane = _iota(gate.shape, 1)
        hs = []
        for j in range(EXPERTS_PER_GROUP):
            hg = jnp.dot(xb, wg_ref[j], preferred_element_type=f32)
            hu = jnp.dot(xb, wu_ref[j], preferred_element_type=f32)
            gcol = jnp.sum(jnp.where(lane == first + j, gate, 0.0), axis=-1, keepdims=True)
            hs.append(_bf(_silu(hg) * hu * gcol))
        ff = wd_ref.shape[1]
        y_ref[...] = jnp.dot(jnp.concatenate(hs, axis=1), wd_ref[...].reshape(EXPERTS_PER_GROUP * ff, -1),
                             preferred_element_type=f32)


def _combine_kernel(alpha, cnt_sm, goff_sm, x_ref, pos_ref, g_ref, b_ref, ys_ref, o_ref, yloc, sems):
    i = pl.program_id(0)
    tm = x_ref.shape[0]
    n_loc = yloc.shape[1]
    slot = i % 2

    def copies(blk, slot_, wait):
        loff = 0
        for gi in range(N_EXPERT_GROUPS):
            rows = _padded(cnt_sm[blk * N_EXPERT_GROUPS + gi])
            src = goff_sm[blk * N_EXPERT_GROUPS + gi]
            _seg_copy(ys_ref, yloc.at[slot_], sems.at[slot_], src, loff, rows, tm, wait)
            loff = loff + rows

    @pl.when(i == 0)
    def _():
        yloc[...] = jnp.zeros_like(yloc)
        copies(0, 0, False)

    @pl.when(i + 1 < pl.num_programs(0))
    def _():
        copies(i + 1, 1 - slot, False)

    copies(i, slot, True)
    perm = _one_hot_rows(pos_ref[...], n_loc)
    y = lax.dot_general(perm, _bf(yloc[slot]), (((0,), (0,)), ((), ())), preferred_element_type=f32)
    o_ref[...] = _layer_norm(alpha * x_ref[...] + y, g_ref[...], b_ref[...], LN_EPS)


def _moe_tables(cnt, n_tiles, tile):
    pc = (cnt + (SEG_ALIGN - 1)) // SEG_ALIGN * SEG_ALIGN
    gtot = jnp.sum(pc, axis=0)
    gcap = (gtot + tile - 1) // tile * tile
    gend = jnp.cumsum(gcap)
    gstart = gend - gcap
    goff = gstart[None, :] + jnp.cumsum(pc, axis=0) - pc
    tstart = jnp.arange(n_tiles, dtype=jnp.int32) * tile
    tg = jnp.minimum(jnp.sum((tstart[:, None] >= gend[None, :]).astype(jnp.int32), axis=1), N_EXPERT_GROUPS - 1)
    tv = jnp.clip(gtot[tg] - (tstart - gstart[tg]), 0, tile)
    fill = jnp.concatenate([gstart + gtot, gcap - gtot, gend[-1:], n_tiles - gend[-1:] // tile])
    return goff.reshape(-1).astype(jnp.int32), tg.astype(jnp.int32), tv.astype(jnp.int32), fill.astype(jnp.int32)


def _moe(alpha, x1, gate, best, cnt, wg, wu, wd, g, b):
    n, d = x1.shape
    ff = wg.shape[2]
    nblk = cnt.shape[0]
    tm = n // nblk
    tile = MOE_TILE
    n_loc = tm + LANES
    n_sorted = -(-(n + nblk * N_EXPERT_GROUPS * SEG_ALIGN + N_EXPERT_GROUPS * tile) // tile) * tile
    n_tiles = n_sorted // tile
    cnt_flat = cnt[:, 0, :N_EXPERT_GROUPS].reshape(-1)
    goff, tg, tv, fill = _moe_tables(cnt[:, 0, :N_EXPERT_GROUPS], n_tiles, tile)
    any_spec = pl.BlockSpec(memory_space=pl.ANY)

    xs, gs, pos = pl.pallas_call(
        _dispatch_kernel,
        grid_spec=pltpu.PrefetchScalarGridSpec(
            num_scalar_prefetch=3,
            grid=(nblk,),
            in_specs=[
                pl.BlockSpec((tm, d), lambda i, *_: (i, 0)),
                pl.BlockSpec((tm, LANES), lambda i, *_: (i, 0)),
                pl.BlockSpec((1, tm), lambda i, *_: (0, i)),
            ],
            out_specs=[any_spec, any_spec, pl.BlockSpec((1, tm), lambda i, *_: (0, i))],
            scratch_shapes=[pltpu.VMEM((2, n_loc, d), f32), pltpu.VMEM((2, n_loc, LANES), f32),
                            pltpu.VMEM((tile, d), f32), pltpu.VMEM((tile, LANES), f32),
                            pltpu.SemaphoreType.DMA((3, 2))],
        ),
        out_shape=[jax.ShapeDtypeStruct((n_sorted, d), f32), jax.ShapeDtypeStruct((n_sorted, LANES), f32),
                   jax.ShapeDtypeStruct((1, n), jnp.int32)],
        compiler_params=_cparams(("arbitrary",)),
        name="moe_dispatch",
    )(cnt_flat, goff, fill, x1, gate, best)

    ys = pl.pallas_call(
        _experts_kernel,
        grid_spec=pltpu.PrefetchScalarGridSpec(
            num_scalar_prefetch=2,
            grid=(n_tiles,),
            in_specs=[
                pl.BlockSpec((tile, d), lambda t, *_: (t, 0)),
                pl.BlockSpec((tile, LANES), lambda t, *_: (t, 0)),
                pl.BlockSpec((EXPERTS_PER_GROUP, d, ff), lambda t, tg_, tv_: (tg_[t], 0, 0)),
                pl.BlockSpec((EXPERTS_PER_GROUP, d, ff), lambda t, tg_, tv_: (tg_[t], 0, 0)),
                pl.BlockSpec((EXPERTS_PER_GROUP, ff, d), lambda t, tg_, tv_: (tg_[t], 0, 0)),
            ],
            out_specs=pl.BlockSpec((tile, d), lambda t, *_: (t, 0)),
        ),
        out_shape=jax.ShapeDtypeStruct((n_sorted, d), f32),
        compiler_params=_cparams(("arbitrary",)),
        name="moe_experts",
    )(tg, tv, xs, gs, wg, wu, wd)

    return pl.pallas_call(
        functools.partial(_combine_kernel, alpha),
        grid_spec=pltpu.PrefetchScalarGridSpec(
            num_scalar_prefetch=2,
            grid=(nblk,),
            in_specs=[
                pl.BlockSpec((tm, d), lambda i, *_: (i, 0)),
                pl.BlockSpec((1, tm), lambda i, *_: (0, i)),
                pl.BlockSpec(g.shape, lambda i, *_: (0, 0)),
                pl.BlockSpec(b.shape, lambda i, *_: (0, 0)),
                any_spec,
            ],
            out_specs=pl.BlockSpec((tm, d), lambda i, *_: (i, 0)),
            scratch_shapes=[pltpu.VMEM((2, n_loc, d), f32), pltpu.SemaphoreType.DMA((2,))],
        ),
        out_shape=jax.ShapeDtypeStruct((n, d), f32),
        compiler_params=_cparams(("arbitrary",)),
        name="moe_combine",
    )(cnt_flat, goff, x1, pos, g, b, ys)


def _pad_cols(w, n):
    return jnp.pad(w, ((0, 0), (0, n - w.shape[1])))


def _row(v, n=None):
    v = v.reshape(1, -1)
    return v if n is None else _pad_cols(v, n)


def kernel(x, w_in, gla_w_alpha, gla_b_alpha, gla_norm_g, ssd_conv_w, ssd_conv_b, ssd_dt_bias, ssd_A_log, ssd_D, ssd_norm_g, rwkv_mu, rwkv_w0, rwkv_w2, rwkv_a0, rwkv_a2, rwkv_g2, rwkv_k_k, rwkv_k_a, rwkv_r_k, rwkv_ln_g, rwkv_ln_b, w_out, ln1_g, ln1_b, router_w, router_b, exp_w_gate, exp_w_up, exp_w_down, ln2_g, ln2_b):
    nb, t, d = x.shape
    depth = w_in.shape[0]
    alpha = float((2 * depth) ** 0.25)
    gla_cols = 4 * GLA_WIDTH + GLA_GATE_RANK
    ssd_cols = SSD_WIDTH + SSD_XBC + SSD_HEADS
    rwt = router_w.T
    rb = router_b.reshape(-1, 1)
    n_exp, _, ff = exp_w_gate.shape[1:]
    expert_views = (exp_w_gate.reshape(depth, n_exp * d, ff), exp_w_up.reshape(depth, n_exp * d, ff),
                    exp_w_down.reshape(depth, n_exp * ff, d))
    w_in_t =jnp.swapaxes(w_in, 1, 2)
    x2 = x.reshape(nb * t, d)
    for l in range(depth):
        wg, ws, wr = _regroup_w_in(w_in_t, l, (gla_cols, ssd_cols, RWKV_COLS), (GLA_COLS_PAD, SSD_COLS_PAD, RWKV_COLS))
        pg, pr, os_ = _in_proj_ssd(x2.reshape(nb, t, d), wg, ws, wr, ssd_conv_w[l], _row(ssd_conv_b[l]),
                                   _row(ssd_dt_bias[l], LANES), _row(ssd_A_log[l], LANES),
                                   _row(jnp.repeat(ssd_D[l], HEAD_DIM)), _row(ssd_norm_g[l]))
        gla_params = (jnp.pad(gla_w_alpha[l], ((0, LANES - GLA_GATE_RANK), (0, 0))), _row(gla_b_alpha[l]),
                      _row(jnp.tile(gla_norm_g[l], GLA_WIDTH // HEAD_DIM)))
        rwkv_params = (_row(rwkv_mu[l]), _row(rwkv_w0[l]), rwkv_w2[l], _row(rwkv_a0[l]), rwkv_a2[l], rwkv_g2[l],
                       _row(rwkv_k_k[l]), _row(rwkv_k_a[l]), _row(rwkv_r_k[l]), _row(rwkv_ln_g[l]),
                       _row(rwkv_ln_b[l]))
        og, or_, wg_b, wu_b, wd_b = _gla_rwkv(pg, gla_params, pr, rwkv_params, l, expert_views)
        x1, gate, best, cnt = _out_proj(alpha, x2, og.reshape(nb * t, -1), os_.reshape(nb * t, -1),
                                        or_.reshape(nb * t, -1), _bf(w_out[l]), _row(ln1_g[l]), _row(ln1_b[l]),
                                        rwt, rb)
        x2 = _moe(alpha, x1, gate, best, cnt, wg_b.reshape(n_exp, d, ff), wu_b.reshape(n_exp, d, ff),
                  wd_b.reshape(n_exp, ff, d), _row(ln2_g[l]), _row(ln2_b[l]))
    return x2.reshape(nb, t, d)
```

```python
import functools
import math

import jax
import jax.numpy as jnp
from jax import lax
from jax.experimental import pallas as pl
from jax.experimental.pallas import tpu as pltpu

f32 = jnp.float32
bf16 = jnp.bfloat16

LANES = 128
SUBLANES = 8
HEAD_DIM = 64
GLA_WIDTH = 256
GLA_GATE_RANK = 16
GLA_GATE_TAU = 16.0
GLA_CHUNK = 16
GLA_BLOCK = 128
GLA_COLS_PAD = 4 * GLA_WIDTH + LANES
SSD_WIDTH = 512
SSD_HEADS = 8
SSD_GROUPS = 2
SSD_STATE = 64
SSD_CONV = 4
SSD_XBC = SSD_WIDTH + 2 * SSD_GROUPS * SSD_STATE
SSD_BLOCK = 128
SSD_COLS_PAD = SSD_WIDTH + SSD_XBC + LANES
RWKV_WIDTH = 256
RWKV_CHUNK = 64
RWKV_GN_EPS = 64e-5
RWKV_COLS = 3 * RWKV_WIDTH + 64 + 64 + 128
N_EXPERTS = 16
N_EXPERT_GROUPS = 4
EXPERTS_PER_GROUP = 4
LN_EPS = 1e-5

PROJ_CHUNK = 512
PROJ_PACE = 8
OUT_PROJ_ROWS = 1024
MOE_SORT_ROWS = 512
MOE_TILE = 512
SEG_ALIGN = 8
VMEM_LIMIT = 56 * 1024 * 1024


def _bf(x):
    return x.astype(bf16)


def _dot(a, b):
    return jnp.dot(_bf(a), _bf(b), preferred_element_type=f32)


def _dot_nt(a, b):
    return lax.dot_general(_bf(a), _bf(b), (((1,), (1,)), ((), ())), preferred_element_type=f32)


def _dot_tn(a, b):
    return lax.dot_general(_bf(a), _bf(b), (((0,), (0,)), ((), ())), preferred_element_type=f32)


def _split(x, n):
    parts, r = [], x
    for _ in range(n):
        p = r.astype(bf16)
        parts.append(p)
        r = r - p.astype(f32)
    return parts


def _dot_sel_rhs(a, sel, n=3):
    out = None
    for p in _split(a, n):
        t = jnp.dot(p, sel, preferred_element_type=f32)
        out = t if out is None else out + t
    return out


def _dot_sel_lhs(sel, b, n=3):
    out = None
    for p in _split(b, n):
        t = jnp.dot(sel, p, preferred_element_type=f32)
        out = t if out is None else out + t
    return out


def _dot_hi(a, b):
    ah, al = _split(a, 2)
    bh, bl = _split(b, 2)
    d = functools.partial(jnp.dot, preferred_element_type=f32)
    return d(ah, bh) + d(al, bh) + d(ah, bl)


def _iota(shape, dim):
    return lax.broadcasted_iota(jnp.int32, shape, dim)


def _block_ones(n, blk):
    return (_iota((n, n), 0) // blk == _iota((n, n), 1) // blk)


def _silu(x):
    return x * jax.nn.sigmoid(x)


def _softplus(x):
    return jnp.maximum(x, 0.0) + jnp.log1p(jnp.exp(-jnp.abs(x)))


def _layer_norm(y, g, b, eps):
    mu = jnp.mean(y, axis=-1, keepdims=True)
    yc = y - mu
    var = jnp.mean(yc * yc, axis=-1, keepdims=True)
    return yc * lax.rsqrt(var + eps) * g + b


def _cparams(sem):
    return pltpu.CompilerParams(dimension_semantics=sem, vmem_limit_bytes=VMEM_LIMIT)


def _regroup_kernel(splits, w_ref, *o_refs):
    start = 0
    for width, o_ref in zip(splits, o_refs):
        part = w_ref[0, start:start + width, :]
        pad = o_ref.shape[1] - width
        if pad:
            part = jnp.concatenate([part, jnp.zeros((pad, part.shape[1]), part.dtype)], axis=0)
        o_ref[...] = _bf(part.T)
        start += width


def _regroup_w_in(w_in_t, layer, splits, padded):
    _, n_in, d = w_in_t.shape
    return pl.pallas_call(
        functools.partial(_regroup_kernel, splits),
        grid=(1,),
        in_specs=[pl.BlockSpec((1, n_in, d), lambda i: (layer, 0, 0))],
        out_specs=[pl.BlockSpec((d, p), lambda i: (0, 0)) for p in padded],
        out_shape=[jax.ShapeDtypeStruct((d, p), bf16) for p in padded],
        compiler_params=_cparams(("arbitrary",)),
        name="regroup_w_in",
    )(w_in_t)


def _gla_steps(p_ref, wa_ref, ba_ref, ng_ref, o_ref, st_ref, t_ref, d_ref, sx_ref):
    nb, tb, _ = p_ref.shape
    w, c = GLA_WIDTH, GLA_CHUNK
    nc = tb // c
    npt = w // LANES
    Q, K, V, CUM, OUT = range(5)

    @pl.when(pl.program_id(0) == 0)
    def _():
        st_ref[...] = jnp.zeros_like(st_ref)

    bones = _block_ones(w, HEAD_DIM).astype(bf16)
    pair_blk = _block_ones(LANES, HEAD_DIM)
    r_i, c_i = _iota((tb, tb), 0), _iota((tb, tb), 1)
    tri = ((r_i >= c_i) & (r_i // c == c_i // c)).astype(bf16)
    wa_h, wa_l = _split(wa_ref[...], 2)
    dd = functools.partial(jnp.dot, preferred_element_type=f32)

    for b in range(nb):
        a_h, a_l = _split(p_ref[b, :, 4 * w:4 * w + LANES], 2)
        z = dd(a_h, wa_h) + dd(a_l, wa_h) + dd(a_h, wa_l) + ba_ref[...]
        log_a = -_softplus(-z) * (1.0 / GLA_GATE_TAU)
        cum = _dot_sel_lhs(tri, log_a, 3)
        for lt in range(npt):
            lanes = slice(lt * LANES, (lt + 1) * LANES)
            t_ref[b, Q, lt] = p_ref[b, :, lanes] * (HEAD_DIM ** -0.5)
            t_ref[b, K, lt] = p_ref[b, :, w + lt * LANES:w + (lt + 1) * LANES]
            t_ref[b, V, lt] = p_ref[b, :, 2 * w + lt * LANES:2 * w + (lt + 1) * LANES]
            t_ref[b, CUM, lt] = cum[:, lanes]
        yield

    def ld(b, j, off):
        return jnp.concatenate([t_ref[b, j, lt, pl.ds(off, nc, stride=c), :] for lt in range(npt)], axis=1)

    for b in range(nb):
        qt = [ld(b, Q, o) for o in range(c)]
        kt = [ld(b, K, o) for o in range(c)]
        ct = [ld(b, CUM, o) for o in range(c)]
        pair = 0
        for t in range(c):
            for s in range(t + 1):
                qk = qt[t] * kt[s]
                d_ref[b, pair * nc:(pair + 1) * nc, :] = qk if s == t else qk * jnp.exp(ct[t] - ct[s])
                pair += 1
            if t % 2 == 1:
                yield
    for b in range(nb):
        sx_ref[b] = jnp.dot(_bf(d_ref[b]), bones, preferred_element_type=f32)
        yield
    for b in range(nb):
        vt = [ld(b, V, o) for o in range(c)]
        pair = 0
        for t in range(c):
            acc = None
            for s in range(t + 1):
                term = sx_ref[b, pair * nc:(pair + 1) * nc, :] * vt[s]
                acc = term if acc is None else acc + term
                pair += 1
            for lt in range(npt):
                t_ref[b, OUT, lt, pl.ds(t, nc, stride=c), :] = acc[:, lt * LANES:(lt + 1) * LANES]
            if t % 4 == 3:
                yield

    for ci in range(nc):
        sl = slice(ci * c, (ci + 1) * c)
        last = slice((ci + 1) * c - 1, (ci + 1) * c)
        for b in range(nb):
            for lt in range(npt):
                cum_c = t_ref[b, CUM, lt, sl, :]
                cum_last = t_ref[b, CUM, lt, last, :]
                st = st_ref[b, lt]
                t_ref[b, OUT, lt, sl, :] += _dot_nt(t_ref[b, Q, lt, sl, :] * jnp.exp(cum_c), st)
                upd = _dot_tn(t_ref[b, V, lt, sl, :], t_ref[b, K, lt, sl, :] * jnp.exp(cum_last - cum_c))
                st_ref[b, lt] = st * jnp.exp(cum_last) + jnp.where(pair_blk, upd, 0.0)
        yield

    o = jnp.concatenate(
        [jnp.concatenate([t_ref[b, OUT, lt] for lt in range(npt)], axis=1) for b in range(nb)], axis=0)
    ms = _dot_sel_rhs(o * o, bones, 2) * (1.0 / HEAD_DIM)
    o = o * lax.rsqrt(ms + LN_EPS) * ng_ref[...]
    for b in range(nb):
        o_ref[b] = o[b * tb:(b + 1) * tb] * _silu(p_ref[b, :, 3 * w:4 * w])


def _round_robin(gens):
    live = list(gens)
    while live:
        for gen in list(live):
            if next(gen, StopIteration) is StopIteration:
                live.remove(gen)
            else:
                yield


def _ssd_steps(b, p_ref, halo_ref, cw_ref, cb_ref, dtb_ref, alog_ref, dsk_ref, ng_ref, o_ref, st_ref):
    L = p_ref.shape[1]
    W, M, G = SSD_WIDTH, SSD_STATE, SSD_GROUPS
    gw = W // G

    @pl.when(pl.program_id(0) == 0)
    def _():
        st_ref[b] = jnp.zeros(st_ref.shape[1:], f32)
        halo_ref[b] = jnp.zeros(halo_ref.shape[1:], f32)

    z = p_ref[b, :, 0:W]
    xbc = p_ref[b, :, W:W + SSD_XBC]
    dt_raw = p_ref[b, :, W + SSD_XBC:W + SSD_XBC + LANES]
    xx = jnp.concatenate([halo_ref[b], xbc], axis=0)
    halo_ref[b] = xbc[L - SUBLANES:L]
    conv = cb_ref[...]
    for j in range(SSD_CONV):
        off = SUBLANES - (SSD_CONV - 1) + j
        conv = conv + cw_ref[j:j + 1, :] * xx[off:off + L]
    act = _silu(conv)
    xs = act[:, 0:W]
    bm = act[:, W:W + G * M]
    cm = act[:, W + G * M:W + 2 * G * M]
    yield

    dt = _softplus(dt_raw + dtb_ref[...])
    a_neg = jnp.where(_iota((1, LANES), 1) < SSD_HEADS, -jnp.exp(alog_ref[...]), 0.0)
    da = dt * a_neg
    r_i, c_i = _iota((L, L), 0), _iota((L, L), 1)
    causal = r_i >= c_i
    cs = _dot_sel_lhs(causal.astype(bf16), da, 3)
    expand = (_iota((LANES, W), 0) == _iota((LANES, W), 1) // HEAD_DIM).astype(bf16)
    dtx = _dot_sel_rhs(dt, expand, 2)
    csx = _dot_sel_rhs(cs, expand, 2)
    eye_h = (_iota((SSD_HEADS, LANES), 0) == _iota((SSD_HEADS, LANES), 1)).astype(bf16)
    cs_t = None
    for p in _split(cs, 2):
        t = lax.dot_general(eye_h, p, (((1,), (1,)), ((), ())), preferred_element_type=f32)
        cs_t = t if cs_t is None else cs_t + t
    xdt = xs * dtx
    yield

    st = st_ref[b]
    y_parts = []
    for gi in range(G):
        bm_g = bm[:, gi * M:(gi + 1) * M]
        cm_g = cm[:, gi * M:(gi + 1) * M]
        cb = _dot_nt(cm_g, bm_g)
        for e in range(SSD_HEADS // G):
            h = gi * (SSD_HEADS // G) + e
            seg = jnp.where(causal, jnp.exp(jnp.broadcast_to(cs[:, h:h + 1], (L, L)) - cs_t[h:h + 1, :]), 0.0)
            y_parts.append(_dot(cb * seg, xdt[:, h * HEAD_DIM:(h + 1) * HEAD_DIM]))
            if e % 2:
                yield
    y = jnp.concatenate(y_parts, axis=1)
    y_inter = jnp.concatenate(
        [_dot(cm[:, gi * M:(gi + 1) * M], st[:, gi * gw:(gi + 1) * gw]) for gi in range(G)], axis=1)
    y = y + y_inter * jnp.exp(csx)
    yield

    cs_last = csx[L - 1:L, :]
    xdtw = xdt * jnp.exp(cs_last - csx)
    d_state = jnp.concatenate(
        [_dot_tn(bm[:, gi * M:(gi + 1) * M], xdtw[:, gi * gw:(gi + 1) * gw]) for gi in range(G)], axis=1)
    st_ref[b] = st * jnp.exp(cs_last) + d_state
    yield

    y = (y + dsk_ref[...] * xs) * _silu(z)
    outs = []
    for gi in range(G):
        yg = y[:, gi * gw:(gi + 1) * gw]
        ms = jnp.mean(yg * yg, axis=-1, keepdims=True)
        outs.append(yg * lax.rsqrt(ms + LN_EPS))
    o_ref[b] = jnp.concatenate(outs, axis=1) * ng_ref[...]


def _inproj_ssd_kernel(x_ref, wg_ref, ws_ref, wr_ref, cw_ref, cb_ref, dtb_ref, alog_ref, dsk_ref, ng_ref,
                       pg_ref, pr_ref, os_ref, ps_ref, halo_ref, st_ref):
    nb, L, d = x_ref.shape
    xb = _bf(x_ref[...].reshape(nb * L, d))
    ps_ref[...] = jnp.dot(xb, ws_ref[...], preferred_element_type=f32).reshape(ps_ref.shape)

    def other_columns():
        for w_ref, o_ref in ((wg_ref, pg_ref), (wr_ref, pr_ref)):
            n = w_ref.shape[1]
            for c0 in range(0, n, PROJ_CHUNK):
                c1 = min(c0 + PROJ_CHUNK, n)
                o_ref[:, :, c0:c1] = jnp.dot(xb, w_ref[:, c0:c1], preferred_element_type=f32).reshape(nb, L, c1 - c0)
                for _ in range(PROJ_PACE):
                    yield

    ssd = _round_robin([_ssd_steps(b, ps_ref, halo_ref, cw_ref, cb_ref, dtb_ref, alog_ref, dsk_ref,
                                   ng_ref, os_ref, st_ref) for b in range(nb)])
    for _ in _round_robin([ssd, other_columns()]):
        pass


def _in_proj_ssd(x3, wg, ws, wr, cw, cb, dtb, alog, dsk, ng):
    nb, t, d = x3.shape
    L = SSD_BLOCK
    full2 = lambda a: pl.BlockSpec(a.shape, lambda i: (0, 0))
    blk = lambda c: pl.BlockSpec((nb, L, c), lambda i: (0, i, 0))
    return pl.pallas_call(
        _inproj_ssd_kernel,
        grid=(t // L,),
        in_specs=[blk(d)] + [full2(a) for a in (wg, ws, wr, cw, cb, dtb, alog, dsk, ng)],
        out_specs=[blk(wg.shape[1]), blk(wr.shape[1]), blk(SSD_WIDTH)],
        out_shape=[jax.ShapeDtypeStruct((nb, t, wg.shape[1]), f32), jax.ShapeDtypeStruct((nb, t, wr.shape[1]), f32),
                   jax.ShapeDtypeStruct((nb, t, SSD_WIDTH), f32)],
        scratch_shapes=[pltpu.VMEM((nb, L, ws.shape[1]), f32), pltpu.VMEM((nb, SUBLANES, SSD_XBC), f32),
                        pltpu.VMEM((nb, SSD_STATE, SSD_WIDTH), f32)],
        compiler_params=_cparams(("arbitrary",)),
        name="in_proj_ssd",
    )(x3, wg, ws, wr, cw, cb, dtb, alog, dsk, ng)


def _rwkv_steps(sub, p_ref, prev_ref, mu_ref, w0_ref, w2_ref, a0_ref, a2_ref, g2_ref, kk_ref, ka_ref, rk_ref,
                lng_ref, lnb_ref, o_ref, zt_ref):
    nb = p_ref.shape[0]
    C, W = RWKV_CHUNK, RWKV_WIDTH
    H = LANES // HEAD_DIM
    npair = W // LANES
    i = pl.program_id(0)
    r0 = sub * C

    if sub == 0:
        @pl.when(i == 0)
        def _():
            zt_ref[...] = jnp.zeros_like(zt_ref)

    bones = _block_ones(W, HEAD_DIM).astype(bf16)
    pair_blk = _block_ones(LANES, HEAD_DIM)
    n = H * C
    rr, cc = _iota((n, n), 0), _iota((n, n), 1)
    same = rr // C == cc // C
    strict = same & (rr > cc)
    lower = same & (rr >= cc)
    eye = jnp.where(rr == cc, 1.0, 0.0)

    def stack(t):
        return jnp.where(pair_blk, jnp.concatenate([t] * H, axis=0), 0.0)

    rows = range(nb)
    nr = nb * C
    cols = p_ref[:, r0:r0 + C, :].reshape(nr, p_ref.shape[2])

    def row_before(b):
        return p_ref[b, r0 - 1:r0, :] if sub else jnp.where(i > 0, prev_ref[b, SUBLANES - 1:SUBLANES, :], 0.0)

    shifted = jnp.concatenate(
        [piece for b in rows for piece in (row_before(b), p_ref[b, r0:r0 + C - 1, :])], axis=0)
    cols = cols + (shifted - cols) * mu_ref[...]
    r = cols[:, 0:W]
    k = cols[:, W:2 * W]
    v = cols[:, 2 * W:3 * W]
    w_lr = cols[:, 3 * W:3 * W + 64]
    a_lr = cols[:, 3 * W + 64:3 * W + 128]
    g_lr = cols[:, 3 * W + 128:3 * W + 256]

    wv = -_softplus(-(w0_ref[...] + _dot_hi(jnp.tanh(w_lr), w2_ref[...]))) - 0.5
    lw = -jnp.exp(wv)
    a = jax.nn.sigmoid(a0_ref[...] + _dot_hi(a_lr, a2_ref[...]))
    g = _dot_hi(jax.nn.sigmoid(g_lr), g2_ref[...])
    yield

    kk = k * kk_ref[...]
    k2 = k * (1.0 + (a - 1.0) * ka_ref[...])
    head_sums = _dot_sel_rhs(jnp.concatenate([kk * kk, r * k2 * rk_ref[...]], axis=0), bones, 2)
    kk = kk / jnp.maximum(jnp.sqrt(head_sums[:nr]), 1e-12)
    bonus_v = head_sums[nr:] * v
    bvec = kk * a
    yield

    rb_i, cb_i = _iota((nr, nr), 0), _iota((nr, nr), 1)
    tri = ((rb_i >= cb_i) & (rb_i // C == cb_i // C)).astype(bf16)
    cum = _dot_sel_lhs(tri, lw, 3)
    cum_last = [cum[(b + 1) * C - 1:(b + 1) * C, :] for b in rows]
    e_neg = jnp.exp(-cum)
    e_rem = jnp.exp(jnp.concatenate([jnp.broadcast_to(cl, (C, W)) for cl in cum_last], axis=0) - cum)
    per_token = dict(aw=-kk * jnp.exp(cum - lw), rw=r * jnp.exp(cum), bn=bvec * e_neg, kn=k2 * e_neg,
                     bt=bvec * e_rem, kt=k2 * e_rem, vs=v)
    chains = [(b, p) for b in rows for p in range(npair)]
    q = [dict({name: stack(t[b * C:(b + 1) * C, p * LANES:(p + 1) * LANES]) for name, t in per_token.items()},
              wc=jnp.exp(cum_last[b][:, p * LANES:(p + 1) * LANES])) for b, p in chains]
    ids = range(len(chains))
    yield

    a_ab = [jnp.where(strict, _dot_nt(q[c]["aw"], q[c]["bn"]), 0.0) for c in ids]
    a_ak = [jnp.where(strict, _dot_nt(q[c]["aw"], q[c]["kn"]), 0.0) for c in ids]
    yield
    a_rb = [jnp.where(lower, _dot_nt(q[c]["rw"], q[c]["bn"]), 0.0) for c in ids]
    a_rk = [jnp.where(lower, _dot_nt(q[c]["rw"], q[c]["kn"]), 0.0) for c in ids]
    yield

    pw = a_ab
    tinv = [eye + a_ab[c] for c in ids]
    for _ in range(int(math.log2(C)) - 1):
        pw = [_dot(pw[c], pw[c]) for c in ids]
        yield
        tinv = [tinv[c] + _dot(tinv[c], pw[c]) for c in ids]
        yield

    yield "state"
    zt = [zt_ref[b, p] for b, p in chains]
    rhs = [_dot_nt(q[c]["aw"], zt[c]) + _dot(a_ak[c], q[c]["vs"]) for c in ids]
    yield
    y0 = [_dot_nt(q[c]["rw"], zt[c]) + _dot(a_rk[c], q[c]["vs"]) for c in ids]
    yield
    u = [_dot(tinv[c], rhs[c]) for c in ids]
    yield
    y_st = [y0[c] + _dot(a_rb[c], u[c]) for c in ids]
    yield
    for c, (b, p) in enumerate(chains):
        zt_ref[b, p] = zt[c] * q[c]["wc"] + _dot_tn(u[c], q[c]["bt"]) + _dot_tn(q[c]["vs"], q[c]["kt"])
    yield

    inv_n = 1.0 / HEAD_DIM
    ys = []
    for b in rows:
        tiles = []
        for p in range(npair):
            y_c = y_st[b * npair + p]
            y = y_c[0:C]
            for h in range(1, H):
                y = y + y_c[h * C:(h + 1) * C]
            tiles.append(y)
        ys.append(jnp.concatenate(tiles, axis=1))
    y = jnp.concatenate(ys, axis=0)
    yc = y - _dot_sel_rhs(y, bones, 2) * inv_n
    var_h = _dot_sel_rhs(yc * yc, bones, 2) * inv_n
    y = yc * lax.rsqrt(var_h + RWKV_GN_EPS) * lng_ref[...] + lnb_ref[...]
    o_ref[:, r0:r0 + C, :] = ((y + bonus_v) * g).reshape(nb, C, W)


N_GLA_IN, N_RWKV_IN = 4, 13


N_CAST = 3


def _gla_rwkv_kernel(*refs):
    gla_in = refs[:N_GLA_IN]
    rwkv_in = refs[N_GLA_IN:N_GLA_IN + N_RWKV_IN]
    k = N_GLA_IN + N_RWKV_IN
    cast_in = refs[k:k + N_CAST]
    og_ref, or_ref = refs[k + N_CAST:k + N_CAST + 2]
    cast_out = refs[k + N_CAST + 2:k + 2 * N_CAST + 2]
    st_ref, t_ref, d_ref, sx_ref, zt_ref = refs[k + 2 * N_CAST + 2:]

    def casts():
        for src, dst in zip(cast_in, cast_out):
            dst[...] = _bf(src[...])
            yield

    n_sub = rwkv_in[0].shape[1] // RWKV_CHUNK

    def rwkv_chunks():
        gens = [_rwkv_steps(sub, *rwkv_in, or_ref, zt_ref) for sub in range(n_sub)]
        parked, done = set(), set()
        while len(done) < n_sub:
            for sub, gen in enumerate(gens):
                if sub in done or (sub in parked and any(s not in done for s in range(sub))):
                    continue
                step = next(gen, StopIteration)
                if step is StopIteration:
                    done.add(sub)
                elif step == "state":
                    parked.add(sub)
                yield

    rwkv = rwkv_chunks()
    gla = _gla_steps(*gla_in, og_ref, st_ref, t_ref, d_ref, sx_ref)
    for _ in _round_robin([rwkv, gla, casts()]):
        pass


def _gla_rwkv(pg, gla_params, pr, rwkv_params, layer, expert_weights):
    nb, t, gcols = pg.shape
    rcols = pr.shape[2]
    tb = GLA_BLOCK
    n_steps = t // tb
    npt = GLA_WIDTH // LANES
    pair_rows = (GLA_CHUNK * (GLA_CHUNK + 1) // 2) * (tb // GLA_CHUNK)
    full2 = lambda a: pl.BlockSpec(a.shape, lambda i: (0, 0))
    blk = lambda c: pl.BlockSpec((nb, tb, c), lambda i: (0, i, 0))
    assert len(gla_params) + 1 == N_GLA_IN and len(rwkv_params) + 2 == N_RWKV_IN and len(expert_weights) == N_CAST
    cast_rows = [w.shape[1] // n_steps for w in expert_weights]
    return pl.pallas_call(
        _gla_rwkv_kernel,
        grid=(n_steps,),
        in_specs=[blk(gcols)] + [full2(p) for p in gla_params]
        + [blk(rcols), pl.BlockSpec((nb, SUBLANES, rcols),
                                    lambda i: (0, jnp.maximum(i * (tb // SUBLANES) - 1, 0), 0))]
        + [full2(p) for p in rwkv_params]
        + [pl.BlockSpec((None, r, w.shape[2]), lambda i: (layer, i, 0)) for r, w in zip(cast_rows, expert_weights)],
        out_specs=[blk(GLA_WIDTH), blk(RWKV_WIDTH)]
        + [pl.BlockSpec((r, w.shape[2]), lambda i: (i, 0)) for r, w in zip(cast_rows, expert_weights)],
        out_shape=[jax.ShapeDtypeStruct((nb, t, GLA_WIDTH), f32), jax.ShapeDtypeStruct((nb, t, RWKV_WIDTH), f32)]
        + [jax.ShapeDtypeStruct(w.shape[1:], bf16) for w in expert_weights],
        scratch_shapes=[
            pltpu.VMEM((nb, npt, LANES, LANES), f32),
            pltpu.VMEM((nb, 5, npt, tb, LANES), f32),
            pltpu.VMEM((nb, pair_rows, GLA_WIDTH), f32),
            pltpu.VMEM((nb, pair_rows, GLA_WIDTH), f32),
            pltpu.VMEM((nb, RWKV_WIDTH // LANES, LANES, LANES), f32),
        ],
        compiler_params=_cparams(("arbitrary",)),
        name="gla_rwkv_mixers",
    )(pg, *gla_params, pr, pr, *rwkv_params, *expert_weights)


def _route(logits, rb):
    tm = logits.shape[1]
    scores = jax.nn.sigmoid(logits)
    sel = scores + rb
    srow = [sel[e:e + 1, :] for e in range(N_EXPERTS)]
    crow = [scores[e:e + 1, :] for e in range(N_EXPERTS)]

    def top2_sum(v0, v1, v2, v3):
        m01, n01 = jnp.maximum(v0, v1), jnp.minimum(v0, v1)
        m23, n23 = jnp.maximum(v2, v3), jnp.minimum(v2, v3)
        return jnp.maximum(m01, m23) + jnp.maximum(jnp.minimum(m01, m23), jnp.maximum(n01, n23))

    gscore = [top2_sum(*srow[4 * gi:4 * gi + 4]) for gi in range(N_EXPERT_GROUPS)]
    best = jnp.zeros((1, tm), jnp.int32)
    best_v = gscore[0]
    for gi in range(1, N_EXPERT_GROUPS):
        better = gscore[gi] > best_v
        best = jnp.where(better, gi, best)
        best_v = jnp.where(better, gscore[gi], best_v)
    sv, cv = [], []
    for j in range(EXPERTS_PER_GROUP):
        s_j, c_j = srow[j], crow[j]
        for gi in range(1, N_EXPERT_GROUPS):
            s_j = jnp.where(best == gi, srow[4 * gi + j], s_j)
            c_j = jnp.where(best == gi, crow[4 * gi + j], c_j)
        sv.append(s_j)
        cv.append(c_j)
    j1 = jnp.zeros((1, tm), jnp.int32)
    v1 = sv[0]
    for j in range(1, EXPERTS_PER_GROUP):
        better = sv[j] > v1
        j1 = jnp.where(better, j, j1)
        v1 = jnp.where(better, sv[j], v1)
    j2 = jnp.full((1, tm), -1, jnp.int32)
    v2 = jnp.full((1, tm), -jnp.inf, f32)
    for j in range(EXPERTS_PER_GROUP):
        better = (j1 != j) & ((j2 < 0) | (sv[j] > v2))
        j2 = jnp.where(better, j, j2)
        v2 = jnp.where(better, sv[j], v2)
    w1 = cv[0]
    w2 = cv[0]
    for j in range(1, EXPERTS_PER_GROUP):
        w1 = jnp.where(j1 == j, cv[j], w1)
        w2 = jnp.where(j2 == j, cv[j], w2)
    wsum = w1 + w2
    g1, g2 = w1 / wsum, w2 / wsum
    rows = []
    for e in range(N_EXPERTS):
        gi, j = divmod(e, EXPERTS_PER_GROUP)
        in_g = best == gi
        rows.append(jnp.where(in_g & (j1 == j), g1, jnp.where(in_g & (j2 == j), g2, 0.0)))
    gate_t = jnp.concatenate(rows, axis=0)
    return gate_t, best


def _outproj_kernel(alpha, x_ref, og_ref, os_ref, or_ref, wo_ref, g_ref, b_ref, rwt_ref, rb_ref,
                    x1_ref, gate_ref, best_ref, cnt_ref):
    tm = x_ref.shape[0]
    ts = tm // cnt_ref.shape[0]
    parts = [slice(sb * ts, (sb + 1) * ts) for sb in range(cnt_ref.shape[0])]
    d = functools.partial(jnp.dot, preferred_element_type=f32)
    mix = [d(_bf(og_ref[s, :]), wo_ref[0:GLA_WIDTH, :])
           + d(_bf(os_ref[s, :]), wo_ref[GLA_WIDTH:GLA_WIDTH + SSD_WIDTH, :])
           + d(_bf(or_ref[s, :]), wo_ref[GLA_WIDTH + SSD_WIDTH:, :]) for s in parts]
    x1 = [_layer_norm(alpha * x_ref[s, :] + m, g_ref[...], b_ref[...], LN_EPS) for s, m in zip(parts, mix)]
    for s, v in zip(parts, x1):
        x1_ref[s, :] = v

    wh, wl = _split(rwt_ref[...], 2)
    nt = lambda a, b: lax.dot_general(a, b, (((1,), (1,)), ((), ())), preferred_element_type=f32)
    logits = []
    for v in x1:
        xh, xl = _split(v, 2)
        logits.append(nt(wh, xh) + nt(wl, xh) + nt(wh, xl))
    routed = [_route(lg, rb_ref[...]) for lg in logits]
    eye = (_iota((N_EXPERTS, LANES), 0) == _iota((N_EXPERTS, LANES), 1)).astype(bf16)
    lane = _iota((1, LANES), 1)
    for sb, (s, (gate_t, best)) in enumerate(zip(parts, routed)):
        gate = None
        for p in _split(gate_t, 3):
            t = lax.dot_general(p, eye, (((0,), (0,)), ((), ())), preferred_element_type=f32)
            gate = t if gate is None else gate + t
        gate_ref[s, :] = gate
        best_ref[:, s] = best
        cnt = jnp.zeros((1, LANES), jnp.int32)
        for gi in range(N_EXPERT_GROUPS):
            n_g = jnp.sum((best == gi).astype(jnp.int32), axis=-1, keepdims=True)
            cnt = jnp.where(lane == gi, n_g, cnt)
        cnt_ref[sb] = cnt


def _out_proj(alpha, x2, og, os_, or_, wo, g, b, rwt, rb):
    n, d = x2.shape
    tm = min(OUT_PROJ_ROWS, n)
    ts = min(MOE_SORT_ROWS, tm)
    full2 = lambda a: pl.BlockSpec(a.shape, lambda i: (0, 0))
    rows = lambda c: pl.BlockSpec((tm, c), lambda i: (i, 0))
    return pl.pallas_call(
        functools.partial(_outproj_kernel, alpha),
        grid=(n // tm,),
        in_specs=[rows(d), rows(og.shape[1]), rows(os_.shape[1]), rows(or_.shape[1]),
                  full2(wo), full2(g), full2(b), full2(rwt), full2(rb)],
        out_specs=[rows(d), rows(LANES), pl.BlockSpec((1, tm), lambda i: (0, i)),
                   pl.BlockSpec((tm // ts, 1, LANES), lambda i: (i, 0, 0))],
        out_shape=[jax.ShapeDtypeStruct((n, d), f32), jax.ShapeDtypeStruct((n, LANES), f32),
                   jax.ShapeDtypeStruct((1, n), jnp.int32), jax.ShapeDtypeStruct((n // ts, 1, LANES), jnp.int32)],
        compiler_params=_cparams(("arbitrary",)),
        name="out_proj_router",
    )(x2, og, os_, or_, wo, g, b, rwt, rb)


def _seg_copy(src, dst, sem, src_off, dst_off, rows, max_rows, wait):
    k = SEG_ALIGN
    sizes = []
    while k <= max_rows:
        sizes.append(k)
        k *= 2
    for k in reversed(sizes):
        shift = int(math.log2(k)) + 1

        @pl.when((rows & k) != 0)
        def _():
            done = (rows >> shift) << shift
            cp = pltpu.make_async_copy(
                src.at[pl.ds(pl.multiple_of(src_off + done, SEG_ALIGN), k)],
                dst.at[pl.ds(pl.multiple_of(dst_off + done, SEG_ALIGN), k)], sem)
            if wait:
                cp.wait()
            else:
                cp.start()


def _padded(count):
    return (count + (SEG_ALIGN - 1)) & (-SEG_ALIGN)


def _one_hot_rows(pos, n_rows):
    return jnp.where(_iota((n_rows, pos.shape[1]), 0) == pos, 1.0, 0.0).astype(bf16)


def _dispatch_kernel(cnt_sm, goff_sm, fill_sm, x_ref, gate_ref, best_ref, xs_out, gs_out, pos_ref,
                     xloc, gloc, zx, zg, sems):
    i = pl.program_id(0)
    tm = x_ref.shape[0]
    n_loc = xloc.shape[1]
    slot = i % 2

    def copies(blk, slot_, wait):
        loff = 0
        for gi in range(N_EXPERT_GROUPS):
            rows = _padded(cnt_sm[blk * N_EXPERT_GROUPS + gi])
            dst = goff_sm[blk * N_EXPERT_GROUPS + gi]
            _seg_copy(xloc.at[slot_], xs_out, sems.at[0, slot_], loff, dst, rows, tm, wait)
            _seg_copy(gloc.at[slot_], gs_out, sems.at[1, slot_], loff, dst, rows, tm, wait)
            loff = loff + rows

    best = best_ref[...]
    onehot = (_iota((SUBLANES, tm), 0) == best).astype(bf16)
    earlier = (_iota((tm, tm), 0) < _iota((tm, tm), 1)).astype(bf16)
    rank = jnp.dot(onehot, earlier, preferred_element_type=f32)
    pos = jnp.zeros((1, tm), f32)
    loff = 0
    for gi in range(N_EXPERT_GROUPS):
        pos = jnp.where(best == gi, rank[gi:gi + 1, :] + loff.astype(f32) if gi else rank[gi:gi + 1, :], pos)
        loff = loff + _padded(cnt_sm[i * N_EXPERT_GROUPS + gi])
    pos = pos.astype(jnp.int32)
    pos_ref[...] = pos
    perm = _one_hot_rows(pos, n_loc)
    xloc[slot] = jnp.dot(perm, _bf(x_ref[...]), preferred_element_type=f32)
    gloc[slot] = _dot_sel_lhs(perm, gate_ref[...], 2)
    copies(i, slot, False)

    @pl.when(i > 0)
    def _():
        copies(i - 1, 1 - slot, True)

    @pl.when(i == pl.num_programs(0) - 1)
    def _():
        copies(i, slot, True)
        zx[...] = jnp.zeros_like(zx)
        zg[...] = jnp.zeros_like(zg)
        tile = zx.shape[0]

        def fills(wait):
            for src, dst, sem in ((zx, xs_out, sems.at[2, 0]), (zg, gs_out, sems.at[2, 1])):
                for gi in range(N_EXPERT_GROUPS):
                    _seg_copy(src, dst, sem, 0, fill_sm[gi], fill_sm[N_EXPERT_GROUPS + gi], tile // 2, wait)

                def body(j, carry):
                    off = pl.multiple_of(fill_sm[2 * N_EXPERT_GROUPS] + j * tile, SEG_ALIGN)
                    cp = pltpu.make_async_copy(src, dst.at[pl.ds(off, tile)], sem)
                    if wait:
                        cp.wait()
                    else:
                        cp.start()
                    return carry

                lax.fori_loop(0, fill_sm[2 * N_EXPERT_GROUPS + 1], body, 0)

        fills(False)
        fills(True)


def _experts_kernel(tg_sm, tv_sm, x_ref, gate_ref, wg_ref, wu_ref, wd_ref, y_ref):
    t = pl.program_id(0)

    @pl.when(tv_sm[t] == 0)
    def _():
        y_ref[...] = jnp.zeros_like(y_ref)

    @pl.when(tv_sm[t] > 0)
    def _():
        first = tg_sm[t] * EXPERTS_PER_GROUP
        xb = _bf(x_ref[...])
        gate = gate_ref[...]
        lane = _iota(gate.shape, 1)
        hs = []
        for j in range(EXPERTS_PER_GROUP):
            hg = jnp.dot(xb, wg_ref[j], preferred_element_type=f32)
            hu = jnp.dot(xb, wu_ref[j], preferred_element_type=f32)
            gcol = jnp.sum(jnp.where(lane == first + j, gate, 0.0), axis=-1, keepdims=True)
            hs.append(_bf(_silu(hg) * hu * gcol))
        ff = wd_ref.shape[1]
        y_ref[...] = jnp.dot(jnp.concatenate(hs, axis=1), wd_ref[...].reshape(EXPERTS_PER_GROUP * ff, -1),
                             preferred_element_type=f32)


def _combine_kernel(alpha, cnt_sm, goff_sm, x_ref, pos_ref, g_ref, b_ref, ys_ref, o_ref, yloc, sems):
    i = pl.program_id(0)
    tm = x_ref.shape[0]
    n_loc = yloc.shape[1]
    slot = i % 2

    def copies(blk, slot_, wait):
        loff = 0
        for gi in range(N_EXPERT_GROUPS):
            rows = _padded(cnt_sm[blk * N_EXPERT_GROUPS + gi])
            src = goff_sm[blk * N_EXPERT_GROUPS + gi]
            _seg_copy(ys_ref, yloc.at[slot_], sems.at[slot_], src, loff, rows, tm, wait)
            loff = loff + rows

    @pl.when(i == 0)
    def _():
        yloc[...] = jnp.zeros_like(yloc)
        copies(0, 0, False)

    @pl.when(i + 1 < pl.num_programs(0))
    def _():
        copies(i + 1, 1 - slot, False)

    copies(i, slot, True)
    perm = _one_hot_rows(pos_ref[...], n_loc)
    y = lax.dot_general(perm, _bf(yloc[slot]), (((0,), (0,)), ((), ())), preferred_element_type=f32)
    o_ref[...] = _layer_norm(alpha * x_ref[...] + y, g_ref[...], b_ref[...], LN_EPS)


def _moe_tables(cnt, n_tiles, tile):
    pc = (cnt + (SEG_ALIGN - 1)) // SEG_ALIGN * SEG_ALIGN
    gtot = jnp.sum(pc, axis=0)
    gcap = (gtot + tile - 1) // tile * tile
    gend = jnp.cumsum(gcap)
    gstart = gend - gcap
    goff = gstart[None, :] + jnp.cumsum(pc, axis=0) - pc
    tstart = jnp.arange(n_tiles, dtype=jnp.int32) * tile
    tg = jnp.minimum(jnp.sum((tstart[:, None] >= gend[None, :]).astype(jnp.int32), axis=1), N_EXPERT_GROUPS - 1)
    tv = jnp.clip(gtot[tg] - (tstart - gstart[tg]), 0, tile)
    fill = jnp.concatenate([gstart + gtot, gcap - gtot, gend[-1:], n_tiles - gend[-1:] // tile])
    return goff.reshape(-1).astype(jnp.int32), tg.astype(jnp.int32), tv.astype(jnp.int32), fill.astype(jnp.int32)


def _moe(alpha, x1, gate, best, cnt, wg, wu, wd, g, b):
    n, d = x1.shape
    ff = wg.shape[2]
    nblk = cnt.shape[0]
    tm = n // nblk
    tile = MOE_TILE
    n_loc = tm + LANES
    n_sorted = -(-(n + nblk * N_EXPERT_GROUPS * SEG_ALIGN + N_EXPERT_GROUPS * tile) // tile) * tile
    n_tiles = n_sorted // tile
    cnt_flat = cnt[:, 0, :N_EXPERT_GROUPS].reshape(-1)
    goff, tg, tv, fill = _moe_tables(cnt[:, 0, :N_EXPERT_GROUPS], n_tiles, tile)
    any_spec = pl.BlockSpec(memory_space=pl.ANY)

    xs, gs, pos = pl.pallas_call(
        _dispatch_kernel,
        grid_spec=pltpu.PrefetchScalarGridSpec(
            num_scalar_prefetch=3,
            grid=(nblk,),
            in_specs=[
                pl.BlockSpec((tm, d), lambda i, *_: (i, 0)),
                pl.BlockSpec((tm, LANES), lambda i, *_: (i, 0)),
                pl.BlockSpec((1, tm), lambda i, *_: (0, i)),
            ],
            out_specs=[any_spec, any_spec, pl.BlockSpec((1, tm), lambda i, *_: (0, i))],
            scratch_shapes=[pltpu.VMEM((2, n_loc, d), f32), pltpu.VMEM((2, n_loc, LANES), f32),
                            pltpu.VMEM((tile, d), f32), pltpu.VMEM((tile, LANES), f32),
                            pltpu.SemaphoreType.DMA((3, 2))],
        ),
        out_shape=[jax.ShapeDtypeStruct((n_sorted, d), f32), jax.ShapeDtypeStruct((n_sorted, LANES), f32),
                   jax.ShapeDtypeStruct((1, n), jnp.int32)],
        compiler_params=_cparams(("arbitrary",)),
        name="moe_dispatch",
    )(cnt_flat, goff, fill, x1, gate, best)

    ys = pl.pallas_call(
        _experts_kernel,
        grid_spec=pltpu.PrefetchScalarGridSpec(
            num_scalar_prefetch=2,
            grid=(n_tiles,),
            in_specs=[
                pl.BlockSpec((tile, d), lambda t, *_: (t, 0)),
                pl.BlockSpec((tile, LANES), lambda t, *_: (t, 0)),
                pl.BlockSpec((EXPERTS_PER_GROUP, d, ff), lambda t, tg_, tv_: (tg_[t], 0, 0)),
                pl.BlockSpec((EXPERTS_PER_GROUP, d, ff), lambda t, tg_, tv_: (tg_[t], 0, 0)),
                pl.BlockSpec((EXPERTS_PER_GROUP, ff, d), lambda t, tg_, tv_: (tg_[t], 0, 0)),
            ],
            out_specs=pl.BlockSpec((tile, d), lambda t, *_: (t, 0)),
        ),
        out_shape=jax.ShapeDtypeStruct((n_sorted, d), f32),
        compiler_params=_cparams(("arbitrary",)),
        name="moe_experts",
    )(tg, tv, xs, gs, wg, wu, wd)

    return pl.pallas_call(
        functools.partial(_combine_kernel, alpha),
        grid_spec=pltpu.PrefetchScalarGridSpec(
            num_scalar_prefetch=2,
            grid=(nblk,),
            in_specs=[
                pl.BlockSpec((tm, d), lambda i, *_: (i, 0)),
                pl.BlockSpec((1, tm), lambda i, *_: (0, i)),
                pl.BlockSpec(g.shape, lambda i, *_: (0, 0)),
                pl.BlockSpec(b.shape, lambda i, *_: (0, 0)),
                any_spec,
            ],
            out_specs=pl.BlockSpec((tm, d), lambda i, *_: (i, 0)),
            scratch_shapes=[pltpu.VMEM((2, n_loc, d), f32), pltpu.SemaphoreType.DMA((2,))],
        ),
        out_shape=jax.ShapeDtypeStruct((n, d), f32),
        compiler_params=_cparams(("arbitrary",)),
        name="moe_combine",
    )(cnt_flat, goff, x1, pos, g, b, ys)


def _pad_cols(w, n):
    return jnp.pad(w, ((0, 0), (0, n - w.shape[1])))


def _row(v, n=None):
    v = v.reshape(1, -1)
    return v if n is None else _pad_cols(v, n)


def kernel(x, w_in, gla_w_alpha, gla_b_alpha, gla_norm_g, ssd_conv_w, ssd_conv_b, ssd_dt_bias, ssd_A_log, ssd_D, ssd_norm_g, rwkv_mu, rwkv_w0, rwkv_w2, rwkv_a0, rwkv_a2, rwkv_g2, rwkv_k_k, rwkv_k_a, rwkv_r_k, rwkv_ln_g, rwkv_ln_b, w_out, ln1_g, ln1_b, router_w, router_b, exp_w_gate, exp_w_up, exp_w_down, ln2_g, ln2_b):
    nb, t, d = x.shape
    depth = w_in.shape[0]
    alpha = float((2 * depth) ** 0.25)
    gla_cols = 4 * GLA_WIDTH + GLA_GATE_RANK
    ssd_cols = SSD_WIDTH + SSD_XBC + SSD_HEADS
    rwt = router_w.T
    rb = router_b.reshape(-1, 1)
    n_exp, _, ff = exp_w_gate.shape[1:]
    expert_views = (exp_w_gate.reshape(depth, n_exp * d, ff), exp_w_up.reshape(depth, n_exp * d, ff),
                    exp_w_down.reshape(depth, n_exp * ff, d))
    w_in_t =jnp.swapaxes(w_in, 1, 2)
    x2 = x.reshape(nb * t, d)
    for l in range(depth):
        wg, ws, wr = _regroup_w_in(w_in_t, l, (gla_cols, ssd_cols, RWKV_COLS), (GLA_COLS_PAD, SSD_COLS_PAD, RWKV_COLS))
        pg, pr, os_ = _in_proj_ssd(x2.reshape(nb, t, d), wg, ws, wr, ssd_conv_w[l], _row(ssd_conv_b[l]),
                                   _row(ssd_dt_bias[l], LANES), _row(ssd_A_log[l], LANES),
                                   _row(jnp.repeat(ssd_D[l], HEAD_DIM)), _row(ssd_norm_g[l]))
        gla_params = (jnp.pad(gla_w_alpha[l], ((0, LANES - GLA_GATE_RANK), (0, 0))), _row(gla_b_alpha[l]),
                      _row(jnp.tile(gla_norm_g[l], GLA_WIDTH // HEAD_DIM)))
        rwkv_params = (_row(rwkv_mu[l]), _row(rwkv_w0[l]), rwkv_w2[l], _row(rwkv_a0[l]), rwkv_a2[l], rwkv_g2[l],
                       _row(rwkv_k_k[l]), _row(rwkv_k_a[l]), _row(rwkv_r_k[l]), _row(rwkv_ln_g[l]),
                       _row(rwkv_ln_b[l]))
        og, or_, wg_b, wu_b, wd_b = _gla_rwkv(pg, gla_params, pr, rwkv_params, l, expert_views)
        x1, gate, best, cnt = _out_proj(alpha, x2, og.reshape(nb * t, -1), os_.reshape(nb * t, -1),
                                        or_.reshape(nb * t, -1), _bf(w_out[l]), _row(ln1_g[l]), _row(ln1_b[l]),
                                        rwt, rb)
        x2 = _moe(alpha, x1, gate, best, cnt, wg_b.reshape(n_exp, d, ff), wu_b.reshape(n_exp, d, ff),
                  wd_b.reshape(n_exp, ff, d), _row(ln2_g[l]), _row(ln2_b[l]))
    return x2.reshape(nb, t, d)
```

```python
import functools
import math

import jax
import jax.numpy as jnp
from jax import lax
from jax.experimental import pallas as pl
from jax.experimental.pallas import tpu as pltpu

f32 = jnp.float32
bf16 = jnp.bfloat16

LANES = 128
SUBLANES = 8
HEAD_DIM = 64
GLA_WIDTH = 256
GLA_GATE_RANK = 16
GLA_GATE_TAU = 16.0
GLA_CHUNK = 16
GLA_BLOCK = 128
GLA_COLS_PAD = 4 * GLA_WIDTH + LANES
SSD_WIDTH = 512
SSD_HEADS = 8
SSD_GROUPS = 2
SSD_STATE = 64
SSD_CONV = 4
SSD_XBC = SSD_WIDTH + 2 * SSD_GROUPS * SSD_STATE
SSD_BLOCK = 128
SSD_COLS_PAD = SSD_WIDTH + SSD_XBC + LANES
RWKV_WIDTH = 256
RWKV_CHUNK = 64
RWKV_GN_EPS = 64e-5
RWKV_COLS = 3 * RWKV_WIDTH + 64 + 64 + 128
N_EXPERTS = 16
N_EXPERT_GROUPS = 4
EXPERTS_PER_GROUP = 4
LN_EPS = 1e-5

PROJ_CHUNK = 512
PROJ_PACE = 8
OUT_PROJ_ROWS = 1024
MOE_SORT_ROWS = 512
MOE_TILE = 512
SEG_ALIGN = 8
VMEM_LIMIT = 56 * 1024 * 1024


def _bf(x):
    return x.astype(bf16)


def _dot(a, b):
    return jnp.dot(_bf(a), _bf(b), preferred_element_type=f32)


def _dot_nt(a, b):
    return lax.dot_general(_bf(a), _bf(b), (((1,), (1,)), ((), ())), preferred_element_type=f32)


def _dot_tn(a, b):
    return lax.dot_general(_bf(a), _bf(b), (((0,), (0,)), ((), ())), preferred_element_type=f32)


def _split(x, n):
    parts, r = [], x
    for _ in range(n):
        p = r.astype(bf16)
        parts.append(p)
        r = r - p.astype(f32)
    return parts


def _dot_sel_rhs(a, sel, n=3):
    out = None
    for p in _split(a, n):
        t = jnp.dot(p, sel, preferred_element_type=f32)
        out = t if out is None else out + t
    return out


def _dot_sel_lhs(sel, b, n=3):
    out = None
    for p in _split(b, n):
        t = jnp.dot(sel, p, preferred_element_type=f32)
        out = t if out is None else out + t
    return out


def _dot_hi(a, b):
    ah, al = _split(a, 2)
    bh, bl = _split(b, 2)
    d = functools.partial(jnp.dot, preferred_element_type=f32)
    return d(ah, bh) + d(al, bh) + d(ah, bl)


def _iota(shape, dim):
    return lax.broadcasted_iota(jnp.int32, shape, dim)


def _block_ones(n, blk):
    return (_iota((n, n), 0) // blk == _iota((n, n), 1) // blk)


def _silu(x):
    return x * jax.nn.sigmoid(x)


def _softplus(x):
    return jnp.maximum(x, 0.0) + jnp.log1p(jnp.exp(-jnp.abs(x)))


def _layer_norm(y, g, b, eps):
    mu = jnp.mean(y, axis=-1, keepdims=True)
    yc = y - mu
    var = jnp.mean(yc * yc, axis=-1, keepdims=True)
    return yc * lax.rsqrt(var + eps) * g + b


def _cparams(sem):
    return pltpu.CompilerParams(dimension_semantics=sem, vmem_limit_bytes=VMEM_LIMIT)


def _regroup_kernel(splits, w_ref, *o_refs):
    start = 0
    for width, o_ref in zip(splits, o_refs):
        part = w_ref[0, start:start + width, :]
        pad = o_ref.shape[1] - width
        if pad:
            part = jnp.concatenate([part, jnp.zeros((pad, part.shape[1]), part.dtype)], axis=0)
        o_ref[...] = _bf(part.T)
        start += width


def _regroup_w_in(w_in_t, layer, splits, padded):
    _, n_in, d = w_in_t.shape
    return pl.pallas_call(
        functools.partial(_regroup_kernel, splits),
        grid=(1,),
        in_specs=[pl.BlockSpec((1, n_in, d), lambda i: (layer, 0, 0))],
        out_specs=[pl.BlockSpec((d, p), lambda i: (0, 0)) for p in padded],
        out_shape=[jax.ShapeDtypeStruct((d, p), bf16) for p in padded],
        compiler_params=_cparams(("arbitrary",)),
        name="regroup_w_in",
    )(w_in_t)


def _gla_steps(p_ref, wa_ref, ba_ref, ng_ref, o_ref, st_ref, t_ref, d_ref, sx_ref):
    nb, tb, _ = p_ref.shape
    w, c = GLA_WIDTH, GLA_CHUNK
    nc = tb // c
    npt = w // LANES
    Q, K, V, CUM, OUT = range(5)

    @pl.when(pl.program_id(0) == 0)
    def _():
        st_ref[...] = jnp.zeros_like(st_ref)

    bones = _block_ones(w, HEAD_DIM).astype(bf16)
    pair_blk = _block_ones(LANES, HEAD_DIM)
    r_i, c_i = _iota((tb, tb), 0), _iota((tb, tb), 1)
    tri = ((r_i >= c_i) & (r_i // c == c_i // c)).astype(bf16)
    wa_h, wa_l = _split(wa_ref[...], 2)
    dd = functools.partial(jnp.dot, preferred_element_type=f32)

    for b in range(nb):
        a_h, a_l = _split(p_ref[b, :, 4 * w:4 * w + LANES], 2)
        z = dd(a_h, wa_h) + dd(a_l, wa_h) + dd(a_h, wa_l) + ba_ref[...]
        log_a = -_softplus(-z) * (1.0 / GLA_GATE_TAU)
        cum = _dot_sel_lhs(tri, log_a, 3)
        for lt in range(npt):
            lanes = slice(lt * LANES, (lt + 1) * LANES)
            t_ref[b, Q, lt] = p_ref[b, :, lanes] * (HEAD_DIM ** -0.5)
            t_ref[b, K, lt] = p_ref[b, :, w + lt * LANES:w + (lt + 1) * LANES]
            t_ref[b, V, lt] = p_ref[b, :, 2 * w + lt * LANES:2 * w + (lt + 1) * LANES]
            t_ref[b, CUM, lt] = cum[:, lanes]
        yield

    def ld(b, j, off):
        return jnp.concatenate([t_ref[b, j, lt, pl.ds(off, nc, stride=c), :] for lt in range(npt)], axis=1)

    for b in range(nb):
        qt = [ld(b, Q, o) for o in range(c)]
        kt = [ld(b, K, o) for o in range(c)]
        ct = [ld(b, CUM, o) for o in range(c)]
        pair = 0
        for t in range(c):
            for s in range(t + 1):
                qk = qt[t] * kt[s]
                d_ref[b, pair * nc:(pair + 1) * nc, :] = qk if s == t else qk * jnp.exp(ct[t] - ct[s])
                pair += 1
            if t % 2 == 1:
                yield
    for b in range(nb):
        sx_ref[b] = jnp.dot(_bf(d_ref[b]), bones, preferred_element_type=f32)
        yield
    for b in range(nb):
        vt = [ld(b, V, o) for o in range(c)]
        pair = 0
        for t in range(c):
            acc = None
            for s in range(t + 1):
                term = sx_ref[b, pair * nc:(pair + 1) * nc, :] * vt[s]
                acc = term if acc is None else acc + term
                pair += 1
            for lt in range(npt):
                t_ref[b, OUT, lt, pl.ds(t, nc, stride=c), :] = acc[:, lt * LANES:(lt + 1) * LANES]
            if t % 4 == 3:
                yield

    for ci in range(nc):
        sl = slice(ci * c, (ci + 1) * c)
        last = slice((ci + 1) * c - 1, (ci + 1) * c)
        for b in range(nb):
            for lt in range(npt):
                cum_c = t_ref[b, CUM, lt, sl, :]
                cum_last = t_ref[b, CUM, lt, last, :]
                st = st_ref[b, lt]
                t_ref[b, OUT, lt, sl, :] += _dot_nt(t_ref[b, Q, lt, sl, :] * jnp.exp(cum_c), st)
                upd = _dot_tn(t_ref[b, V, lt, sl, :], t_ref[b, K, lt, sl, :] * jnp.exp(cum_last - cum_c))
                st_ref[b, lt] = st * jnp.exp(cum_last) + jnp.where(pair_blk, upd, 0.0)
        yield

    o = jnp.concatenate(
        [jnp.concatenate([t_ref[b, OUT, lt] for lt in range(npt)], axis=1) for b in range(nb)], axis=0)
    ms = _dot_sel_rhs(o * o, bones, 2) * (1.0 / HEAD_DIM)
    o = o * lax.rsqrt(ms + LN_EPS) * ng_ref[...]
    for b in range(nb):
        o_ref[b] = o[b * tb:(b + 1) * tb] * _silu(p_ref[b, :, 3 * w:4 * w])


def _round_robin(gens):
    live = list(gens)
    while live:
        for gen in list(live):
            if next(gen, StopIteration) is StopIteration:
                live.remove(gen)
            else:
                yield


def _ssd_steps(b, p_ref, halo_ref, cw_ref, cb_ref, dtb_ref, alog_ref, dsk_ref, ng_ref, o_ref, st_ref):
    L = p_ref.shape[1]
    W, M, G = SSD_WIDTH, SSD_STATE, SSD_GROUPS
    gw = W // G

    @pl.when(pl.program_id(0) == 0)
    def _():
        st_ref[b] = jnp.zeros(st_ref.shape[1:], f32)
        halo_ref[b] = jnp.zeros(halo_ref.shape[1:], f32)

    z = p_ref[b, :, 0:W]
    xbc = p_ref[b, :, W:W + SSD_XBC]
    dt_raw = p_ref[b, :, W + SSD_XBC:W + SSD_XBC + LANES]
    xx = jnp.concatenate([halo_ref[b], xbc], axis=0)
    halo_ref[b] = xbc[L - SUBLANES:L]
    conv = cb_ref[...]
    for j in range(SSD_CONV):
        off = SUBLANES - (SSD_CONV - 1) + j
        conv = conv + cw_ref[j:j + 1, :] * xx[off:off + L]
    act = _silu(conv)
    xs = act[:, 0:W]
    bm = act[:, W:W + G * M]
    cm = act[:, W + G * M:W + 2 * G * M]
    yield

    dt = _softplus(dt_raw + dtb_ref[...])
    a_neg = jnp.where(_iota((1, LANES), 1) < SSD_HEADS, -jnp.exp(alog_ref[...]), 0.0)
    da = dt * a_neg
    r_i, c_i = _iota((L, L), 0), _iota((L, L), 1)
    causal = r_i >= c_i
    cs = _dot_sel_lhs(causal.astype(bf16), da, 3)
    expand = (_iota((LANES, W), 0) == _iota((LANES, W), 1) // HEAD_DIM).astype(bf16)
    dtx = _dot_sel_rhs(dt, expand, 2)
    csx = _dot_sel_rhs(cs, expand, 2)
    eye_h = (_iota((SSD_HEADS, LANES), 0) == _iota((SSD_HEADS, LANES), 1)).astype(bf16)
    cs_t = None
    for p in _split(cs, 2):
        t = lax.dot_general(eye_h, p, (((1,), (1,)), ((), ())), preferred_element_type=f32)
        cs_t = t if cs_t is None else cs_t + t
    xdt = xs * dtx
    yield

    st = st_ref[b]
    y_parts = []
    for gi in range(G):
        bm_g = bm[:, gi * M:(gi + 1) * M]
        cm_g = cm[:, gi * M:(gi + 1) * M]
        cb = _dot_nt(cm_g, bm_g)
        for e in range(SSD_HEADS // G):
            h = gi * (SSD_HEADS // G) + e
            seg = jnp.where(causal, jnp.exp(jnp.broadcast_to(cs[:, h:h + 1], (L, L)) - cs_t[h:h + 1, :]), 0.0)
            y_parts.append(_dot(cb * seg, xdt[:, h * HEAD_DIM:(h + 1) * HEAD_DIM]))
            if e % 2:
                yield
    y = jnp.concatenate(y_parts, axis=1)
    y_inter = jnp.concatenate(
        [_dot(cm[:, gi * M:(gi + 1) * M], st[:, gi * gw:(gi + 1) * gw]) for gi in range(G)], axis=1)
    y = y + y_inter * jnp.exp(csx)
    yield

    cs_last = csx[L - 1:L, :]
    xdtw = xdt * jnp.exp(cs_last - csx)
    d_state = jnp.concatenate(
        [_dot_tn(bm[:, gi * M:(gi + 1) * M], xdtw[:, gi * gw:(gi + 1) * gw]) for gi in range(G)], axis=1)
    st_ref[b] = st * jnp.exp(cs_last) + d_state
    yield

    y = (y + dsk_ref[...] * xs) * _silu(z)
    outs = []
    for gi in range(G):
        yg = y[:, gi * gw:(gi + 1) * gw]
        ms = jnp.mean(yg * yg, axis=-1, keepdims=True)
        outs.append(yg * lax.rsqrt(ms + LN_EPS))
    o_ref[b] = jnp.concatenate(outs, axis=1) * ng_ref[...]


def _inproj_ssd_kernel(x_ref, wg_ref, ws_ref, wr_ref, cw_ref, cb_ref, dtb_ref, alog_ref, dsk_ref, ng_ref,
                       pg_ref, pr_ref, os_ref, ps_ref, halo_ref, st_ref):
    nb, L, d = x_ref.shape
    xb = _bf(x_ref[...].reshape(nb * L, d))
    ps_ref[...] = jnp.dot(xb, ws_ref[...], preferred_element_type=f32).reshape(ps_ref.shape)

    def other_columns():
        for w_ref, o_ref in ((wg_ref, pg_ref), (wr_ref, pr_ref)):
            n = w_ref.shape[1]
            for c0 in range(0, n, PROJ_CHUNK):
                c1 = min(c0 + PROJ_CHUNK, n)
                o_ref[:, :, c0:c1] = jnp.dot(xb, w_ref[:, c0:c1], preferred_element_type=f32).reshape(nb, L, c1 - c0)
                for _ in range(PROJ_PACE):
                    yield

    ssd = _round_robin([_ssd_steps(b, ps_ref, halo_ref, cw_ref, cb_ref, dtb_ref, alog_ref, dsk_ref,
                                   ng_ref, os_ref, st_ref) for b in range(nb)])
    for _ in _round_robin([ssd, other_columns()]):
        pass


def _in_proj_ssd(x3, wg, ws, wr, cw, cb, dtb, alog, dsk, ng):
    nb, t, d = x3.shape
    L = SSD_BLOCK
    full2 = lambda a: pl.BlockSpec(a.shape, lambda i: (0, 0))
    blk = lambda c: pl.BlockSpec((nb, L, c), lambda i: (0, i, 0))
    return pl.pallas_call(
        _inproj_ssd_kernel,
        grid=(t // L,),
        in_specs=[blk(d)] + [full2(a) for a in (wg, ws, wr, cw, cb, dtb, alog, dsk, ng)],
        out_specs=[blk(wg.shape[1]), blk(wr.shape[1]), blk(SSD_WIDTH)],
        out_shape=[jax.ShapeDtypeStruct((nb, t, wg.shape[1]), f32), jax.ShapeDtypeStruct((nb, t, wr.shape[1]), f32),
                   jax.ShapeDtypeStruct((nb, t, SSD_WIDTH), f32)],
        scratch_shapes=[pltpu.VMEM((nb, L, ws.shape[1]), f32), pltpu.VMEM((nb, SUBLANES, SSD_XBC), f32),
                        pltpu.VMEM((nb, SSD_STATE, SSD_WIDTH), f32)],
        compiler_params=_cparams(("arbitrary",)),
        name="in_proj_ssd",
    )(x3, wg, ws, wr, cw, cb, dtb, alog, dsk, ng)


def _rwkv_steps(sub, p_ref, prev_ref, mu_ref, w0_ref, w2_ref, a0_ref, a2_ref, g2_ref, kk_ref, ka_ref, rk_ref,
                lng_ref, lnb_ref, o_ref, zt_ref):
    nb = p_ref.shape[0]
    C, W = RWKV_CHUNK, RWKV_WIDTH
    H = LANES // HEAD_DIM
    npair = W // LANES
    i = pl.program_id(0)
    r0 = sub * C

    if sub == 0:
        @pl.when(i == 0)
        def _():
            zt_ref[...] = jnp.zeros_like(zt_ref)

    bones = _block_ones(W, HEAD_DIM).astype(bf16)
    pair_blk = _block_ones(LANES, HEAD_DIM)
    n = H * C
    rr, cc = _iota((n, n), 0), _iota((n, n), 1)
    same = rr // C == cc // C
    strict = same & (rr > cc)
    lower = same & (rr >= cc)
    eye = jnp.where(rr == cc, 1.0, 0.0)

    def stack(t):
        return jnp.where(pair_blk, jnp.concatenate([t] * H, axis=0), 0.0)

    rows = range(nb)
    nr = nb * C
    cols = p_ref[:, r0:r0 + C, :].reshape(nr, p_ref.shape[2])

    def row_before(b):
        return p_ref[b, r0 - 1:r0, :] if sub else jnp.where(i > 0, prev_ref[b, SUBLANES - 1:SUBLANES, :], 0.0)

    shifted = jnp.concatenate(
        [piece for b in rows for piece in (row_before(b), p_ref[b, r0:r0 + C - 1, :])], axis=0)
    cols = cols + (shifted - cols) * mu_ref[...]
    r = cols[:, 0:W]
    k = cols[:, W:2 * W]
    v = cols[:, 2 * W:3 * W]
    w_lr = cols[:, 3 * W:3 * W + 64]
    a_lr = cols[:, 3 * W + 64:3 * W + 128]
    g_lr = cols[:, 3 * W + 128:3 * W + 256]

    wv = -_softplus(-(w0_ref[...] + _dot_hi(jnp.tanh(w_lr), w2_ref[...]))) - 0.5
    lw = -jnp.exp(wv)
    a = jax.nn.sigmoid(a0_ref[...] + _dot_hi(a_lr, a2_ref[...]))
    g = _dot_hi(jax.nn.sigmoid(g_lr), g2_ref[...])
    yield

    kk = k * kk_ref[...]
    k2 = k * (1.0 + (a - 1.0) * ka_ref[...])
    head_sums = _dot_sel_rhs(jnp.concatenate([kk * kk, r * k2 * rk_ref[...]], axis=0), bones, 2)
    kk = kk / jnp.maximum(jnp.sqrt(head_sums[:nr]), 1e-12)
    bonus_v = head_sums[nr:] * v
    bvec = kk * a
    yield

    rb_i, cb_i = _iota((nr, nr), 0), _iota((nr, nr), 1)
    tri = ((rb_i >= cb_i) & (rb_i // C == cb_i // C)).astype(bf16)
    cum = _dot_sel_lhs(tri, lw, 3)
    cum_last = [cum[(b + 1) * C - 1:(b + 1) * C, :] for b in rows]
    e_neg = jnp.exp(-cum)
    e_rem = jnp.exp(jnp.concatenate([jnp.broadcast_to(cl, (C, W)) for cl in cum_last], axis=0) - cum)
    per_token = dict(aw=-kk * jnp.exp(cum - lw), rw=r * jnp.exp(cum), bn=bvec * e_neg, kn=k2 * e_neg,
                     bt=bvec * e_rem, kt=k2 * e_rem, vs=v)
    chains = [(b, p) for b in rows for p in range(npair)]
    q = [dict({name: stack(t[b * C:(b + 1) * C, p * LANES:(p + 1) * LANES]) for name, t in per_token.items()},
              wc=jnp.exp(cum_last[b][:, p * LANES:(p + 1) * LANES])) for b, p in chains]
    ids = range(len(chains))
    yield

    a_ab = [jnp.where(strict, _dot_nt(q[c]["aw"], q[c]["bn"]), 0.0) for c in ids]
    a_ak = [jnp.where(strict, _dot_nt(q[c]["aw"], q[c]["kn"]), 0.0) for c in ids]
    yield
    a_rb = [jnp.where(lower, _dot_nt(q[c]["rw"], q[c]["bn"]), 0.0) for c in ids]
    a_rk = [jnp.where(lower, _dot_nt(q[c]["rw"], q[c]["kn"]), 0.0) for c in ids]
    yield

    pw = a_ab
    tinv = [eye + a_ab[c] for c in ids]
    for _ in range(int(math.log2(C)) - 1):
        pw = [_dot(pw[c], pw[c]) for c in ids]
        yield
        tinv = [tinv[c] + _dot(tinv[c], pw[c]) for c in ids]
        yield

    yield "state"
    zt = [zt_ref[b, p] for b, p in chains]
    rhs = [_dot_nt(q[c]["aw"], zt[c]) + _dot(a_ak[c], q[c]["vs"]) for c in ids]
    yield
    y0 = [_dot_nt(q[c]["rw"], zt[c]) + _dot(a_rk[c], q[c]["vs"]) for c in ids]
    yield
    u = [_dot(tinv[c], rhs[c]) for c in ids]
    yield
    y_st = [y0[c] + _dot(a_rb[c], u[c]) for c in ids]
    yield
    for c, (b, p) in enumerate(chains):
        zt_ref[b, p] = zt[c] * q[c]["wc"] + _dot_tn(u[c], q[c]["bt"]) + _dot_tn(q[c]["vs"], q[c]["kt"])
    yield

    inv_n = 1.0 / HEAD_DIM
    ys = []
    for b in rows:
        tiles = []
        for p in range(npair):
            y_c = y_st[b * npair + p]
            y = y_c[0:C]
            for h in range(1, H):
                y = y + y_c[h * C:(h + 1) * C]
            tiles.append(y)
        ys.append(jnp.concatenate(tiles, axis=1))
    y = jnp.concatenate(ys, axis=0)
    yc = y - _dot_sel_rhs(y, bones, 2) * inv_n
    var_h = _dot_sel_rhs(yc * yc, bones, 2) * inv_n
    y = yc * lax.rsqrt(var_h + RWKV_GN_EPS) * lng_ref[...] + lnb_ref[...]
    o_ref[:, r0:r0 + C, :] = ((y + bonus_v) * g).reshape(nb, C, W)


N_GLA_IN, N_RWKV_IN = 4, 13


N_CAST = 3


def _gla_rwkv_kernel(*refs):
    gla_in = refs[:N_GLA_IN]
    rwkv_in = refs[N_GLA_IN:N_GLA_IN + N_RWKV_IN]
    k = N_GLA_IN + N_RWKV_IN
    cast_in = refs[k:k + N_CAST]
    og_ref, or_ref = refs[k + N_CAST:k + N_CAST + 2]
    cast_out = refs[k + N_CAST + 2:k + 2 * N_CAST + 2]
    st_ref, t_ref, d_ref, sx_ref, zt_ref = refs[k + 2 * N_CAST + 2:]

    def casts():
        for src, dst in zip(cast_in, cast_out):
            dst[...] = _bf(src[...])
            yield

    n_sub = rwkv_in[0].shape[1] // RWKV_CHUNK

    def rwkv_chunks():
        gens = [_rwkv_steps(sub, *rwkv_in, or_ref, zt_ref) for sub in range(n_sub)]
        parked, done = set(), set()
        while len(done) < n_sub:
            for sub, gen in enumerate(gens):
                if sub in done or (sub in parked and any(s not in done for s in range(sub))):
                    continue
                step = next(gen, StopIteration)
                if step is StopIteration:
                    done.add(sub)
                elif step == "state":
                    parked.add(sub)
                yield

    rwkv = rwkv_chunks()
    gla = _gla_steps(*gla_in, og_ref, st_ref, t_ref, d_ref, sx_ref)
    for _ in _round_robin([rwkv, gla, casts()]):
        pass


def _gla_rwkv(pg, gla_params, pr, rwkv_params, layer, expert_weights):
    nb, t, gcols = pg.shape
    rcols = pr.shape[2]
    tb = GLA_BLOCK
    n_steps = t // tb
    npt = GLA_WIDTH // LANES
    pair_rows = (GLA_CHUNK * (GLA_CHUNK + 1) // 2) * (tb // GLA_CHUNK)
    full2 = lambda a: pl.BlockSpec(a.shape, lambda i: (0, 0))
    blk = lambda c: pl.BlockSpec((nb, tb, c), lambda i: (0, i, 0))
    assert len(gla_params) + 1 == N_GLA_IN and len(rwkv_params) + 2 == N_RWKV_IN and len(expert_weights) == N_CAST
    cast_rows = [w.shape[1] // n_steps for w in expert_weights]
    return pl.pallas_call(
        _gla_rwkv_kernel,
        grid=(n_steps,),
        in_specs=[blk(gcols)] + [full2(p) for p in gla_params]
        + [blk(rcols), pl.BlockSpec((nb, SUBLANES, rcols),
                                    lambda i: (0, jnp.maximum(i * (tb // SUBLANES) - 1, 0), 0))]
        + [full2(p) for p in rwkv_params]
        + [pl.BlockSpec((None, r, w.shape[2]), lambda i: (layer, i, 0)) for r, w in zip(cast_rows, expert_weights)],
        out_specs=[blk(GLA_WIDTH), blk(RWKV_WIDTH)]
        + [pl.BlockSpec((r, w.shape[2]), lambda i: (i, 0)) for r, w in zip(cast_rows, expert_weights)],
        out_shape=[jax.ShapeDtypeStruct((nb, t, GLA_WIDTH), f32), jax.ShapeDtypeStruct((nb, t, RWKV_WIDTH), f32)]
        + [jax.ShapeDtypeStruct(w.shape[1:], bf16) for w in expert_weights],
        scratch_shapes=[
            pltpu.VMEM((nb, npt, LANES, LANES), f32),
            pltpu.VMEM((nb, 5, npt, tb, LANES), f32),
            pltpu.VMEM((nb, pair_rows, GLA_WIDTH), f32),
            pltpu.VMEM((nb, pair_rows, GLA_WIDTH), f32),
            pltpu.VMEM((nb, RWKV_WIDTH // LANES, LANES, LANES), f32),
        ],
        compiler_params=_cparams(("arbitrary",)),
        name="gla_rwkv_mixers",
    )(pg, *gla_params, pr, pr, *rwkv_params, *expert_weights)


def _route(logits, rb):
    tm = logits.shape[1]
    scores = jax.nn.sigmoid(logits)
    sel = scores + rb
    srow = [sel[e:e + 1, :] for e in range(N_EXPERTS)]
    crow = [scores[e:e + 1, :] for e in range(N_EXPERTS)]

    def top2_sum(v0, v1, v2, v3):
        m01, n01 = jnp.maximum(v0, v1), jnp.minimum(v0, v1)
        m23, n23 = jnp.maximum(v2, v3), jnp.minimum(v2, v3)
        return jnp.maximum(m01, m23) + jnp.maximum(jnp.minimum(m01, m23), jnp.maximum(n01, n23))

    gscore = [top2_sum(*srow[4 * gi:4 * gi + 4]) for gi in range(N_EXPERT_GROUPS)]
    best = jnp.zeros((1, tm), jnp.int32)
    best_v = gscore[0]
    for gi in range(1, N_EXPERT_GROUPS):
        better = gscore[gi] > best_v
        best = jnp.where(better, gi, best)
        best_v = jnp.where(better, gscore[gi], best_v)
    sv, cv = [], []
    for j in range(EXPERTS_PER_GROUP):
        s_j, c_j = srow[j], crow[j]
        for gi in range(1, N_EXPERT_GROUPS):
            s_j = jnp.where(best == gi, srow[4 * gi + j], s_j)
            c_j = jnp.where(best == gi, crow[4 * gi + j], c_j)
        sv.append(s_j)
        cv.append(c_j)
    j1 = jnp.zeros((1, tm), jnp.int32)
    v1 = sv[0]
    for j in range(1, EXPERTS_PER_GROUP):
        better = sv[j] > v1
        j1 = jnp.where(better, j, j1)
        v1 = jnp.where(better, sv[j], v1)
    j2 = jnp.full((1, tm), -1, jnp.int32)
    v2 = jnp.full((1, tm), -jnp.inf, f32)
    for j in range(EXPERTS_PER_GROUP):
        better = (j1 != j) & ((j2 < 0) | (sv[j] > v2))
        j2 = jnp.where(better, j, j2)
        v2 = jnp.where(better, sv[j], v2)
    w1 = cv[0]
    w2 = cv[0]
    for j in range(1, EXPERTS_PER_GROUP):
        w1 = jnp.where(j1 == j, cv[j], w1)
        w2 = jnp.where(j2 == j, cv[j], w2)
    wsum = w1 + w2
    g1, g2 = w1 / wsum, w2 / wsum
    rows = []
    for e in range(N_EXPERTS):
        gi, j = divmod(e, EXPERTS_PER_GROUP)
        in_g = best == gi
        rows.append(jnp.where(in_g & (j1 == j), g1, jnp.where(in_g & (j2 == j), g2, 0.0)))
    gate_t = jnp.concatenate(rows, axis=0)
    return gate_t, best


def _outproj_kernel(alpha, x_ref, og_ref, os_ref, or_ref, wo_ref, g_ref, b_ref, rwt_ref, rb_ref,
                    x1_ref, gate_ref, best_ref, cnt_ref):
    tm = x_ref.shape[0]
    ts = tm // cnt_ref.shape[0]
    parts = [slice(sb * ts, (sb + 1) * ts) for sb in range(cnt_ref.shape[0])]
    d = functools.partial(jnp.dot, preferred_element_type=f32)
    mix = [d(_bf(og_ref[s, :]), wo_ref[0:GLA_WIDTH, :])
           + d(_bf(os_ref[s, :]), wo_ref[GLA_WIDTH:GLA_WIDTH + SSD_WIDTH, :])
           + d(_bf(or_ref[s, :]), wo_ref[GLA_WIDTH + SSD_WIDTH:, :]) for s in parts]
    x1 = [_layer_norm(alpha * x_ref[s, :] + m, g_ref[...], b_ref[...], LN_EPS) for s, m in zip(parts, mix)]
    for s, v in zip(parts, x1):
        x1_ref[s, :] = v

    wh, wl = _split(rwt_ref[...], 2)
    nt = lambda a, b: lax.dot_general(a, b, (((1,), (1,)), ((), ())), preferred_element_type=f32)
    logits = []
    for v in x1:
        xh, xl = _split(v, 2)
        logits.append(nt(wh, xh) + nt(wl, xh) + nt(wh, xl))
    routed = [_route(lg, rb_ref[...]) for lg in logits]
    eye = (_iota((N_EXPERTS, LANES), 0) == _iota((N_EXPERTS, LANES), 1)).astype(bf16)
    lane = _iota((1, LANES), 1)
    for sb, (s, (gate_t, best)) in enumerate(zip(parts, routed)):
        gate = None
        for p in _split(gate_t, 3):
            t = lax.dot_general(p, eye, (((0,), (0,)), ((), ())), preferred_element_type=f32)
            gate = t if gate is None else gate + t
        gate_ref[s, :] = gate
        best_ref[:, s] = best
        cnt = jnp.zeros((1, LANES), jnp.int32)
        for gi in range(N_EXPERT_GROUPS):
            n_g = jnp.sum((best == gi).astype(jnp.int32), axis=-1, keepdims=True)
            cnt = jnp.where(lane == gi, n_g, cnt)
        cnt_ref[sb] = cnt


def _out_proj(alpha, x2, og, os_, or_, wo, g, b, rwt, rb):
    n, d = x2.shape
    tm = min(OUT_PROJ_ROWS, n)
    ts = min(MOE_SORT_ROWS, tm)
    full2 = lambda a: pl.BlockSpec(a.shape, lambda i: (0, 0))
    rows = lambda c: pl.BlockSpec((tm, c), lambda i: (i, 0))
    return pl.pallas_call(
        functools.partial(_outproj_kernel, alpha),
        grid=(n // tm,),
        in_specs=[rows(d), rows(og.shape[1]), rows(os_.shape[1]), rows(or_.shape[1]),
                  full2(wo), full2(g), full2(b), full2(rwt), full2(rb)],
        out_specs=[rows(d), rows(LANES), pl.BlockSpec((1, tm), lambda i: (0, i)),
                   pl.BlockSpec((tm // ts, 1, LANES), lambda i: (i, 0, 0))],
        out_shape=[jax.ShapeDtypeStruct((n, d), f32), jax.ShapeDtypeStruct((n, LANES), f32),
                   jax.ShapeDtypeStruct((1, n), jnp.int32), jax.ShapeDtypeStruct((n // ts, 1, LANES), jnp.int32)],
        compiler_params=_cparams(("arbitrary",)),
        name="out_proj_router",
    )(x2, og, os_, or_, wo, g, b, rwt, rb)


def _seg_copy(src, dst, sem, src_off, dst_off, rows, max_rows, wait):
    k = SEG_ALIGN
    sizes = []
    while k <= max_rows:
        sizes.append(k)
        k *= 2
    for k in reversed(sizes):
        shift = int(math.log2(k)) + 1

        @pl.when((rows & k) != 0)
        def _():
            done = (rows >> shift) << shift
            cp = pltpu.make_async_copy(
                src.at[pl.ds(pl.multiple_of(src_off + done, SEG_ALIGN), k)],
                dst.at[pl.ds(pl.multiple_of(dst_off + done, SEG_ALIGN), k)], sem)
            if wait:
                cp.wait()
            else:
                cp.start(priority=shift % 2)


def _padded(count):
    return (count + (SEG_ALIGN - 1)) & (-SEG_ALIGN)


def _one_hot_rows(pos, n_rows):
    return jnp.where(_iota((n_rows, pos.shape[1]), 0) == pos, 1.0, 0.0).astype(bf16)


def _dispatch_kernel(cnt_sm, goff_sm, fill_sm, x_ref, gate_ref, best_ref, xs_out, gs_out, pos_ref,
                     xloc, gloc, zx, zg, sems):
    i = pl.program_id(0)
    tm = x_ref.shape[0]
    n_loc = xloc.shape[1]
    slot = i % 2

    def copies(blk, slot_, wait):
        loff = 0
        for gi in range(N_EXPERT_GROUPS):
            rows = _padded(cnt_sm[blk * N_EXPERT_GROUPS + gi])
            dst = goff_sm[blk * N_EXPERT_GROUPS + gi]
            _seg_copy(xloc.at[slot_], xs_out, sems.at[0, slot_], loff, dst, rows, tm, wait)
            _seg_copy(gloc.at[slot_], gs_out, sems.at[1, slot_], loff, dst, rows, tm, wait)
            loff = loff + rows

    best = best_ref[...]
    onehot = (_iota((SUBLANES, tm), 0) == best).astype(bf16)
    earlier = (_iota((tm, tm), 0) < _iota((tm, tm), 1)).astype(bf16)
    rank = jnp.dot(onehot, earlier, preferred_element_type=f32)
    pos = jnp.zeros((1, tm), f32)
    loff = 0
    for gi in range(N_EXPERT_GROUPS):
        pos = jnp.where(best == gi, rank[gi:gi + 1, :] + loff.astype(f32) if gi else rank[gi:gi + 1, :], pos)
        loff = loff + _padded(cnt_sm[i * N_EXPERT_GROUPS + gi])
    pos = pos.astype(jnp.int32)
    pos_ref[...] = pos
    perm = _one_hot_rows(pos, n_loc)
    xloc[slot] = jnp.dot(perm, _bf(x_ref[...]), preferred_element_type=f32)
    gloc[slot] = _dot_sel_lhs(perm, gate_ref[...], 2)
    copies(i, slot, False)

    @pl.when(i > 0)
    def _():
        copies(i - 1, 1 - slot, True)

    @pl.when(i == pl.num_programs(0) - 1)
    def _():
        copies(i, slot, True)
        zx[...] = jnp.zeros_like(zx)
        zg[...] = jnp.zeros_like(zg)
        tile = zx.shape[0]

        def fills(wait):
            for src, dst, sem in ((zx, xs_out, sems.at[2, 0]), (zg, gs_out, sems.at[2, 1])):
                for gi in range(N_EXPERT_GROUPS):
                    _seg_copy(src, dst, sem, 0, fill_sm[gi], fill_sm[N_EXPERT_GROUPS + gi], tile // 2, wait)

                def body(j, carry):
                    off = pl.multiple_of(fill_sm[2 * N_EXPERT_GROUPS] + j * tile, SEG_ALIGN)
                    cp = pltpu.make_async_copy(src, dst.at[pl.ds(off, tile)], sem)
                    if wait:
                        cp.wait()
                    else:
                        cp.start()
                    return carry

                lax.fori_loop(0, fill_sm[2 * N_EXPERT_GROUPS + 1], body, 0)

        fills(False)
        fills(True)


def _experts_kernel(tg_sm, tv_sm, x_ref, gate_ref, wg_ref, wu_ref, wd_ref, y_ref):
    t = pl.program_id(0)

    @pl.when(tv_sm[t] == 0)
    def _():
        y_ref[...] = jnp.zeros_like(y_ref)

    @pl.when(tv_sm[t] > 0)
    def _():
        first = tg_sm[t] * EXPERTS_PER_GROUP
        xb = _bf(x_ref[...])
        gate = gate_ref[...]
        lane = _iota(gate.shape, 1)
        hs = []
        for j in range(EXPERTS_PER_GROUP):
            hg = jnp.dot(xb, wg_ref[j], preferred_element_type=f32)
            hu = jnp.dot(xb, wu_ref[j], preferred_element_type=f32)
            gcol = jnp.sum(jnp.where(lane == first + j, gate, 0.0), axis=-1, keepdims=True)
            hs.append(_bf(_silu(hg) * hu * gcol))
        ff = wd_ref.shape[1]
        y_ref[...] = jnp.dot(jnp.concatenate(hs, axis=1), wd_ref[...].reshape(EXPERTS_PER_GROUP * ff, -1),
                             preferred_element_type=f32)


def _combine_kernel(alpha, cnt_sm, goff_sm, x_ref, pos_ref, g_ref, b_ref, ys_ref, o_ref, yloc, sems):
    i = pl.program_id(0)
    tm = x_ref.shape[0]
    n_loc = yloc.shape[1]
    slot = i % 2

    def copies(blk, slot_, wait):
        loff = 0
        for gi in range(N_EXPERT_GROUPS):
            rows = _padded(cnt_sm[blk * N_EXPERT_GROUPS + gi])
            src = goff_sm[blk * N_EXPERT_GROUPS + gi]
            _seg_copy(ys_ref, yloc.at[slot_], sems.at[slot_], src, loff, rows, tm, wait)
            loff = loff + rows

    @pl.when(i == 0)
    def _():
        yloc[...] = jnp.zeros_like(yloc)
        copies(0, 0, False)

    @pl.when(i + 1 < pl.num_programs(0))
    def _():
        copies(i + 1, 1 - slot, False)

    copies(i, slot, True)
    perm = _one_hot_rows(pos_ref[...], n_loc)
    y = lax.dot_general(perm, _bf(yloc[slot]), (((0,), (0,)), ((), ())), preferred_element_type=f32)
    o_ref[...] = _layer_norm(alpha * x_ref[...] + y, g_ref[...], b_ref[...], LN_EPS)


def _moe_tables(cnt, n_tiles, tile):
    pc = (cnt + (SEG_ALIGN - 1)) // SEG_ALIGN * SEG_ALIGN
    gtot = jnp.sum(pc, axis=0)
    gcap = (gtot + tile - 1) // tile * tile
    gend = jnp.cumsum(gcap)
    gstart = gend - gcap
    goff = gstart[None, :] + jnp.cumsum(pc, axis=0) - pc
    tstart = jnp.arange(n_tiles, dtype=jnp.int32) * tile
    tg = jnp.minimum(jnp.sum((tstart[:, None] >= gend[None, :]).astype(jnp.int32), axis=1), N_EXPERT_GROUPS - 1)
    tv = jnp.clip(gtot[tg] - (tstart - gstart[tg]), 0, tile)
    fill = jnp.concatenate([gstart + gtot, gcap - gtot, gend[-1:], n_tiles - gend[-1:] // tile])
    return goff.reshape(-1).astype(jnp.int32), tg.astype(jnp.int32), tv.astype(jnp.int32), fill.astype(jnp.int32)


def _moe(alpha, x1, gate, best, cnt, wg, wu, wd, g, b):
    n, d = x1.shape
    ff = wg.shape[2]
    nblk = cnt.shape[0]
    tm = n // nblk
    tile = MOE_TILE
    n_loc = tm + LANES
    n_sorted = -(-(n + nblk * N_EXPERT_GROUPS * SEG_ALIGN + N_EXPERT_GROUPS * tile) // tile) * tile
    n_tiles = n_sorted // tile
    cnt_flat = cnt[:, 0, :N_EXPERT_GROUPS].reshape(-1)
    goff, tg, tv, fill = _moe_tables(cnt[:, 0, :N_EXPERT_GROUPS], n_tiles, tile)
    any_spec = pl.BlockSpec(memory_space=pl.ANY)

    xs, gs, pos = pl.pallas_call(
        _dispatch_kernel,
        grid_spec=pltpu.PrefetchScalarGridSpec(
            num_scalar_prefetch=3,
            grid=(nblk,),
            in_specs=[
                pl.BlockSpec((tm, d), lambda i, *_: (i, 0)),
                pl.BlockSpec((tm, LANES), lambda i, *_: (i, 0)),
                pl.BlockSpec((1, tm), lambda i, *_: (0, i)),
            ],
            out_specs=[any_spec, any_spec, pl.BlockSpec((1, tm), lambda i, *_: (0, i))],
            scratch_shapes=[pltpu.VMEM((2, n_loc, d), f32), pltpu.VMEM((2, n_loc, LANES), f32),
                            pltpu.VMEM((tile, d), f32), pltpu.VMEM((tile, LANES), f32),
                            pltpu.SemaphoreType.DMA((3, 2))],
        ),
        out_shape=[jax.ShapeDtypeStruct((n_sorted, d), f32), jax.ShapeDtypeStruct((n_sorted, LANES), f32),
                   jax.ShapeDtypeStruct((1, n), jnp.int32)],
        compiler_params=_cparams(("arbitrary",)),
        name="moe_dispatch",
    )(cnt_flat, goff, fill, x1, gate, best)

    ys = pl.pallas_call(
        _experts_kernel,
        grid_spec=pltpu.PrefetchScalarGridSpec(
            num_scalar_prefetch=2,
            grid=(n_tiles,),
            in_specs=[
                pl.BlockSpec((tile, d), lambda t, *_: (t, 0)),
                pl.BlockSpec((tile, LANES), lambda t, *_: (t, 0)),
                pl.BlockSpec((EXPERTS_PER_GROUP, d, ff), lambda t, tg_, tv_: (tg_[t], 0, 0)),
                pl.BlockSpec((EXPERTS_PER_GROUP, d, ff), lambda t, tg_, tv_: (tg_[t], 0, 0)),
                pl.BlockSpec((EXPERTS_PER_GROUP, ff, d), lambda t, tg_, tv_: (tg_[t], 0, 0)),
            ],
            out_specs=pl.BlockSpec((tile, d), lambda t, *_: (t, 0)),
        ),
        out_shape=jax.ShapeDtypeStruct((n_sorted, d), f32),
        compiler_params=_cparams(("arbitrary",)),
        name="moe_experts",
    )(tg, tv, xs, gs, wg, wu, wd)

    return pl.pallas_call(
        functools.partial(_combine_kernel, alpha),
        grid_spec=pltpu.PrefetchScalarGridSpec(
            num_scalar_prefetch=2,
            grid=(nblk,),
            in_specs=[
                pl.BlockSpec((tm, d), lambda i, *_: (i, 0)),
                pl.BlockSpec((1, tm), lambda i, *_: (0, i)),
                pl.BlockSpec(g.shape, lambda i, *_: (0, 0)),
                pl.BlockSpec(b.shape, lambda i, *_: (0, 0)),
                any_spec,
            ],
            out_specs=pl.BlockSpec((tm, d), lambda i, *_: (i, 0)),
            scratch_shapes=[pltpu.VMEM((2, n_loc, d), f32), pltpu.SemaphoreType.DMA((2,))],
        ),
        out_shape=jax.ShapeDtypeStruct((n, d), f32),
        compiler_params=_cparams(("arbitrary",)),
        name="moe_combine",
    )(cnt_flat, goff, x1, pos, g, b, ys)


def _pad_cols(w, n):
    return jnp.pad(w, ((0, 0), (0, n - w.shape[1])))


def _row(v, n=None):
    v = v.reshape(1, -1)
    return v if n is None else _pad_cols(v, n)


def kernel(x, w_in, gla_w_alpha, gla_b_alpha, gla_norm_g, ssd_conv_w, ssd_conv_b, ssd_dt_bias, ssd_A_log, ssd_D, ssd_norm_g, rwkv_mu, rwkv_w0, rwkv_w2, rwkv_a0, rwkv_a2, rwkv_g2, rwkv_k_k, rwkv_k_a, rwkv_r_k, rwkv_ln_g, rwkv_ln_b, w_out, ln1_g, ln1_b, router_w, router_b, exp_w_gate, exp_w_up, exp_w_down, ln2_g, ln2_b):
    nb, t, d = x.shape
    depth = w_in.shape[0]
    alpha = float((2 * depth) ** 0.25)
    gla_cols = 4 * GLA_WIDTH + GLA_GATE_RANK
    ssd_cols = SSD_WIDTH + SSD_XBC + SSD_HEADS
    rwt = router_w.T
    rb = router_b.reshape(-1, 1)
    n_exp, _, ff = exp_w_gate.shape[1:]
    expert_views = (exp_w_gate.reshape(depth, n_exp * d, ff), exp_w_up.reshape(depth, n_exp * d, ff),
                    exp_w_down.reshape(depth, n_exp * ff, d))
    w_in_t =jnp.swapaxes(w_in, 1, 2)
    x2 = x.reshape(nb * t, d)
    for l in range(depth):
        wg, ws, wr = _regroup_w_in(w_in_t, l, (gla_cols, ssd_cols, RWKV_COLS), (GLA_COLS_PAD, SSD_COLS_PAD, RWKV_COLS))
        pg, pr, os_ = _in_proj_ssd(x2.reshape(nb, t, d), wg, ws, wr, ssd_conv_w[l], _row(ssd_conv_b[l]),
                                   _row(ssd_dt_bias[l], LANES), _row(ssd_A_log[l], LANES),
                                   _row(jnp.repeat(ssd_D[l], HEAD_DIM)), _row(ssd_norm_g[l]))
        gla_params = (jnp.pad(gla_w_alpha[l], ((0, LANES - GLA_GATE_RANK), (0, 0))), _row(gla_b_alpha[l]),
                      _row(jnp.tile(gla_norm_g[l], GLA_WIDTH // HEAD_DIM)))
        rwkv_params = (_row(rwkv_mu[l]), _row(rwkv_w0[l]), rwkv_w2[l], _row(rwkv_a0[l]), rwkv_a2[l], rwkv_g2[l],
                       _row(rwkv_k_k[l]), _row(rwkv_k_a[l]), _row(rwkv_r_k[l]), _row(rwkv_ln_g[l]),
                       _row(rwkv_ln_b[l]))
        og, or_, wg_b, wu_b, wd_b = _gla_rwkv(pg, gla_params, pr, rwkv_params, l, expert_views)
        x1, gate, best, cnt = _out_proj(alpha, x2, og.reshape(nb * t, -1), os_.reshape(nb * t, -1),
                                        or_.reshape(nb * t, -1), _bf(w_out[l]), _row(ln1_g[l]), _row(ln1_b[l]),
                                        rwt, rb)
        x2 = _moe(alpha, x1, gate, best, cnt, wg_b.reshape(n_exp, d, ff), wu_b.reshape(n_exp, d, ff),
                  wd_b.reshape(n_exp, ff, d), _row(ln2_g[l]), _row(ln2_b[l]))
    return x2.reshape(nb, t, d)
```
